```python
import math
import jax, jax.numpy as jnp
from jax import lax
import numpy as np


D_MODEL = 1024
BATCH = 8
SEQ = 4096
DEPTH = 2

MIX_W = 512
ATTN_HEADS = 8
ATTN_HEAD_DIM = 64
IDX_HEADS = 8
IDX_DIM = 64
TOPK_MAX = 256
Q_BLOCK = 128
N_BUCKETS = 32
MAX_DISTANCE = 128
RWKV_HEADS = 8
RWKV_HEAD_DIM = 64
LORA_DECAY = 64
LORA_ICLR = 64
LORA_GATE = 128
CONV_CH = 512
CONV_W = 3
D_FF = 2816
N_BRANCH = 3
NORM_EPS = 1e-6
GN_EPS = 64e-5
NEG_INF = -1e30

ATTN_COLS = 3 * MIX_W + IDX_HEADS * IDX_DIM + IDX_DIM + IDX_HEADS
RWKV_COLS = 3 * MIX_W + LORA_DECAY + LORA_ICLR + LORA_GATE
CONV_COLS = 3 * CONV_CH
GATE_COLS = N_BRANCH * D_MODEL
IN_COLS = ATTN_COLS + RWKV_COLS + CONV_COLS + GATE_COLS

kernel_name = 'hybrid_dsa_rwkv7_shortconv_block'


def split_cols(z, sizes):
    out, start = [], 0
    for n in sizes:
        out.append(z[..., start:start + n])
        start += n
    return out


def rmsnorm(x, g):
    xf = x.astype(jnp.float32)
    y = xf * lax.rsqrt(jnp.mean(xf * xf, axis=-1, keepdims=True) + NORM_EPS)
    return (y * g.astype(jnp.float32)).astype(x.dtype)


def token_shift(z):
    return jnp.pad(z, ((0, 0), (1, 0), (0, 0)))[:, :-1]


def causal_dwconv(z, w):
    s = z.shape[1]
    zp = jnp.pad(z, ((0, 0), (CONV_W - 1, 0), (0, 0)))
    return sum(zp[:, j:j + s] * w[:, j] for j in range(CONV_W))


def t5_bucket(dist):
    n = jnp.maximum(dist, 0)
    max_exact = N_BUCKETS // 2
    nf = jnp.maximum(n, 1).astype(jnp.float32)
    large = max_exact + (jnp.log(nf / max_exact) / math.log(MAX_DISTANCE / max_exact)
                         * (N_BUCKETS - max_exact)).astype(jnp.int32)
    large = jnp.minimum(large, N_BUCKETS - 1)
    return jnp.where(n < max_exact, n, large)


def dsa_attention(z_attn, positions, rel_bias):
    b, s, _ = z_attn.shape
    q, k, v, qi, ki, wi = split_cols(z_attn, (MIX_W, MIX_W, MIX_W, IDX_HEADS * IDX_DIM, IDX_DIM, IDX_HEADS))
    q = q.reshape(b, s, ATTN_HEADS, ATTN_HEAD_DIM)
    k = k.reshape(b, s, ATTN_HEADS, ATTN_HEAD_DIM)
    v = v.reshape(b, s, ATTN_HEADS, ATTN_HEAD_DIM)
    qi = qi.reshape(b, s, IDX_HEADS, IDX_DIM)
    wi = wi * IDX_HEADS ** -0.5
    n_keep = min(TOPK_MAX, s // 4)
    nb = s // Q_BLOCK
    s_idx = jnp.arange(s, dtype=jnp.int32)
    t_blocks = s_idx.reshape(nb, Q_BLOCK)
    gather = jax.vmap(lambda a, i: a[i])

    def to_blocks(a):
        return jnp.moveaxis(a.reshape((b, nb, Q_BLOCK) + a.shape[2:]), 1, 0)

    def one_block(args):
        q_b, qi_b, wi_b, pos_b, t_b = args
        sc = jnp.einsum('bqhd,bsd->bqsh', qi_b, ki) * IDX_DIM ** -0.5
        score = jnp.einsum('bqsh,bqh->bqs', jax.nn.relu(sc), wi_b).astype(jnp.float32)
        score = jnp.where(s_idx[None, None, :] <= t_b[None, :, None], score, NEG_INF)
        _, sel = lax.top_k(score, n_keep)
        k_sel = gather(k, sel)
        v_sel = gather(v, sel)
        logits = jnp.einsum('bqhd,bqkhd->bqhk', q_b, k_sel).astype(jnp.float32) * ATTN_HEAD_DIM ** -0.5
        dist = pos_b[:, :, None] - gather(positions, sel)
        bias = rel_bias[t5_bucket(dist)].astype(jnp.float32)
        logits = logits + jnp.moveaxis(bias, -1, 2)
        valid = (sel <= t_b[None, :, None])[:, :, None, :]
        p = jax.nn.softmax(jnp.where(valid, logits, NEG_INF), axis=-1)
        o = jnp.einsum('bqhk,bqkhd->bqhd', p.astype(v.dtype), v_sel)
        return o.reshape(b, Q_BLOCK, MIX_W)

    out = lax.map(one_block, (to_blocks(q), to_blocks(qi), to_blocks(wi), to_blocks(positions), t_blocks))
    return jnp.moveaxis(out, 0, 1).reshape(b, s, MIX_W)


def rwkv7_mix(z, mu, w0, w_up, a0, a_up, g_up, k_k, k_a, r_k, ln_w, ln_b):
    b, s, _ = z.shape
    f32 = jnp.float32
    z = z + (token_shift(z) - z) * mu
    r, k, v, wd, ad, gd = split_cols(z, (MIX_W, MIX_W, MIX_W, LORA_DECAY, LORA_ICLR, LORA_GATE))
    log_w = -jax.nn.softplus(-(w0 + jnp.tanh(wd) @ w_up).astype(f32)) - 0.5
    decay = jnp.exp(-jnp.exp(log_w))
    a_f = jax.nn.sigmoid((a0 + ad @ a_up).astype(f32))
    g = jax.nn.sigmoid(gd) @ g_up
    heads = lambda t: t.astype(f32).reshape(b, s, RWKV_HEADS, RWKV_HEAD_DIM)
    kk = heads(k * k_k)
    kk = kk / jnp.maximum(jnp.sqrt(jnp.sum(kk * kk, axis=-1, keepdims=True)), 1e-12)
    k_mod = k.astype(f32) * (1 + (a_f - 1) * k_a)
    r_h, w_h, k_h, v_h, a_h = heads(r), heads(decay), heads(k_mod), heads(v), heads(a_f)

    def step(state, inp):
        r_t, w_t, k_t, v_t, ka_t, kb_t = inp
        sa = jnp.einsum('bhij,bhj->bhi', state, ka_t)
        state = state * w_t[:, :, None, :] + sa[..., None] * kb_t[:, :, None, :] + v_t[..., None] * k_t[:, :, None, :]
        return state, jnp.einsum('bhij,bhj->bhi', state, r_t)

    tm = lambda t: jnp.moveaxis(t, 1, 0)
    s0 = jnp.zeros((b, RWKV_HEADS, RWKV_HEAD_DIM, RWKV_HEAD_DIM), f32)
    _, y = lax.scan(step, s0, (tm(r_h), tm(w_h), tm(k_h), tm(v_h), tm(-kk), tm(kk * a_h)))
    y = jnp.moveaxis(y, 0, 1)
    mean = jnp.mean(y, axis=-1, keepdims=True)
    var = jnp.mean(jnp.square(y - mean), axis=-1, keepdims=True)
    hn = (RWKV_HEADS, RWKV_HEAD_DIM)
    y = (y - mean) * lax.rsqrt(var + GN_EPS) * ln_w.astype(f32).reshape(hn) + ln_b.astype(f32).reshape(hn)
    y = y + jnp.sum(r_h * k_h * r_k.astype(f32).reshape(hn), axis=-1, keepdims=True) * v_h
    return (y.reshape(b, s, MIX_W) * g.astype(f32)).astype(z.dtype)


def conv_glu_ffn(h, w_up, conv_w, w_down):
    a, g = jnp.split(h @ w_up, 2, axis=-1)
    return (jax.nn.silu(causal_dwconv(a, conv_w)) * g) @ w_down


def setup_inputs(seed: int = 0) -> dict:
    key = jax.random.key(seed)
    ks = jax.random.split(key, 32)
    nrm = lambda k, shape, scale: jax.random.normal(k, shape, jnp.float32) * scale
    L = DEPTH
    return {
        'x': nrm(ks[0], (BATCH, SEQ, D_MODEL), 1.0),
        'c': nrm(ks[1], (BATCH, D_MODEL), 1.0),
        'positions': jnp.broadcast_to(jnp.arange(SEQ, dtype=jnp.int32), (BATCH, SEQ)),
        'rel_bias': nrm(ks[2], (N_BUCKETS, ATTN_HEADS), 0.5),
        'final_norm': 1.0 + nrm(ks[3], (D_MODEL,), 0.02),
        'ada_w': nrm(ks[4], (L, D_MODEL, 6 * D_MODEL), 0.5 * D_MODEL ** -0.5),
        'ada_b': nrm(ks[5], (L, 6 * D_MODEL), 0.02),
        'norm_mix': 1.0 + nrm(ks[6], (L, D_MODEL), 0.02),
        'w_in': nrm(ks[7], (L, D_MODEL, IN_COLS), D_MODEL ** -0.5),
        'rwkv_mu': jax.random.uniform(ks[8], (L, RWKV_COLS), jnp.float32),
        'rwkv_w0': jax.random.uniform(ks[9], (L, MIX_W), jnp.float32, -4.0, 0.0),
        'rwkv_w_up': nrm(ks[10], (L, LORA_DECAY, MIX_W), 0.1),
        'rwkv_a0': nrm(ks[11], (L, MIX_W), 0.1),
        'rwkv_a_up': nrm(ks[12], (L, LORA_ICLR, MIX_W), LORA_ICLR ** -0.5),
        'rwkv_g_up': nrm(ks[13], (L, LORA_GATE, MIX_W), LORA_GATE ** -0.5),
        'rwkv_k_k': 0.85 + nrm(ks[14], (L, MIX_W), 0.1),
        'rwkv_k_a': 1.0 + nrm(ks[15], (L, MIX_W), 0.1),
        'rwkv_r_k': nrm(ks[16], (L, MIX_W), 0.1),
        'rwkv_ln_w': 1.0 + nrm(ks[17], (L, MIX_W), 0.02),
        'rwkv_ln_b': nrm(ks[18], (L, MIX_W), 0.02),
        'sc_conv_w': nrm(ks[19], (L, CONV_CH, CONV_W), CONV_W ** -0.5),
        'w_branch': nrm(ks[20], (L, N_BRANCH, MIX_W, D_MODEL), MIX_W ** -0.5),
        'w_o': nrm(ks[21], (L, D_MODEL, D_MODEL), D_MODEL ** -0.5),
        'norm_ffn': 1.0 + nrm(ks[22], (L, D_MODEL), 0.02),
        'ffn_w_up': nrm(ks[23], (L, D_MODEL, 2 * D_FF), D_MODEL ** -0.5),
        'ffn_conv_w': nrm(ks[24], (L, D_FF, CONV_W), CONV_W ** -0.5),
        'ffn_w_down': nrm(ks[25], (L, D_FF, D_MODEL), D_FF ** -0.5),
    }


def reference(x, c, positions, rel_bias, final_norm, ada_w, ada_b, norm_mix, w_in,
              rwkv_mu, rwkv_w0, rwkv_w_up, rwkv_a0, rwkv_a_up, rwkv_g_up, rwkv_k_k, rwkv_k_a,
              rwkv_r_k, rwkv_ln_w, rwkv_ln_b, sc_conv_w, w_branch, w_o, norm_ffn,
              ffn_w_up, ffn_conv_w, ffn_w_down):
    b, s, _ = x.shape
    for l in range(DEPTH):
        mod = (c @ ada_w[l] + ada_b[l])[:, None, :]
        sh1, sc1, g1, sh2, sc2, g2 = jnp.split(mod, 6, axis=-1)

        h = rmsnorm(x, norm_mix[l]) * (1 + sc1) + sh1
        z = h @ w_in[l]
        z_attn, z_rwkv, z_conv, z_gate = split_cols(z, (ATTN_COLS, RWKV_COLS, CONV_COLS, GATE_COLS))
        o_attn = dsa_attention(z_attn, positions, rel_bias)
        o_rwkv = rwkv7_mix(z_rwkv, rwkv_mu[l], rwkv_w0[l], rwkv_w_up[l], rwkv_a0[l], rwkv_a_up[l],
                           rwkv_g_up[l], rwkv_k_k[l], rwkv_k_a[l], rwkv_r_k[l], rwkv_ln_w[l], rwkv_ln_b[l])
        c_b, c_c, c_x = split_cols(z_conv, (CONV_CH, CONV_CH, CONV_CH))
        o_conv = c_b * causal_dwconv(c_c * c_x, sc_conv_w[l])
        gates = jax.nn.sigmoid(z_gate).reshape(b, s, N_BRANCH, D_MODEL)
        merged = sum(gates[:, :, i] * (o @ w_branch[l, i]) for i, o in enumerate((o_attn, o_rwkv, o_conv)))
        x = x + g1 * (merged @ w_o[l])

        h = rmsnorm(x, norm_ffn[l]) * (1 + sc2) + sh2
        x = x + g2 * conv_glu_ffn(h, ffn_w_up[l], ffn_conv_w[l], ffn_w_down[l])
    return rmsnorm(x, final_norm)
```

```python
import functools
import math

import jax
import jax.numpy as jnp
from jax import lax
from jax.experimental import pallas as pl
from jax.experimental.pallas import tpu as pltpu

F32 = jnp.float32
BF16 = jnp.bfloat16
HIGHEST = lax.Precision.HIGHEST

MIX_W = 512
HEADS = 8
HEAD_DIM = 64
TOPK_MAX = 256
N_BUCKETS = 32
MAX_DISTANCE = 128
LORA_DECAY = 64
LORA_ICLR = 64
LORA_GATE = 128
CONV_W = 3
NORM_EPS = 1e-6
GN_EPS = 64e-5
NEG_INF = -1e30

LANES = 128
SUBLANES = 8
VMEM_LIMIT = 56 * 1024 * 1024

TQ = 256
CHUNK = 64
RW_TM = 256
QUAD = 4 * HEAD_DIM

ZR_R, ZR_K, ZR_V, ZR_CB, ZR_CC, ZR_CX = 0, 1, 2, 3, 4, 5
ZR_SMALL0 = 6 * MIX_W // LANES
ZR_KI, ZR_WI, ZR_S2, ZR_S3 = ZR_SMALL0, ZR_SMALL0 + 1, ZR_SMALL0 + 2, ZR_SMALL0 + 3
ZR_COLS = 6 * MIX_W + 4 * LANES


def _cparams(sem):
    return pltpu.CompilerParams(dimension_semantics=sem, vmem_limit_bytes=VMEM_LIMIT)


def _nt(a, b, precision=None):
    return lax.dot_general(a, b, (((1,), (1,)), ((), ())), precision=precision,
                           preferred_element_type=F32)


def _tn(a, b, precision=None):
    return lax.dot_general(a, b, (((0,), (0,)), ((), ())), precision=precision,
                           preferred_element_type=F32)


def _mm(a, b, precision=None):
    return jnp.dot(a, b, precision=precision, preferred_element_type=F32)


def _mod_kernel(c_ref, w_ref, b_ref, o_ref):
    o_ref[...] = _mm(c_ref[...], w_ref[...], HIGHEST) + b_ref[...]


def _ada_mod(c, ada_w, ada_b):
    depth, d, d6 = ada_w.shape
    b = c.shape[0]
    out = pl.pallas_call(
        _mod_kernel,
        grid=(depth, d6 // d),
        in_specs=[pl.BlockSpec((b, d), lambda l, j: (0, 0)),
                  pl.BlockSpec((None, d, d), lambda l, j: (l, 0, j)),
                  pl.BlockSpec((None, 1, d), lambda l, j: (l, 0, j))],
        out_specs=pl.BlockSpec((None, b, d), lambda l, j: (l, 0, j)),
        out_shape=jax.ShapeDtypeStruct((depth, b, d6), F32),
        compiler_params=_cparams(("parallel", "parallel")),
        name="ada_mod",
    )(c, ada_w, ada_b.reshape(depth, 1, d6))
    return out.reshape(depth, b, d6 // d, 1, d)


def _mod_spec(layer, which, ngrid):
    if ngrid == 2:
        return pl.BlockSpec((None, None, None, 1, None), lambda b, i: (layer, b, which, 0, 0))
    return pl.BlockSpec((None, None, None, 1, None), lambda b, i, j: (layer, b, which, 0, 0))


def _norm_mod(x, gain, scale, shift):
    y = x * lax.rsqrt(jnp.mean(x * x, axis=-1, keepdims=True) + NORM_EPS) * gain
    return y * (1.0 + scale) + shift


def _inproj_kernel(x_ref, sh_ref, sc_ref, g_ref, w_ref, o_ref, h_ref):
    @pl.when(pl.program_id(2) == 0)
    def _():
        h_ref[...] = _norm_mod(x_ref[...], g_ref[...], sc_ref[...], sh_ref[...]).astype(h_ref.dtype)

    o_ref[...] = _mm(h_ref[...], w_ref[...]).astype(o_ref.dtype)


def _full_mod_spec(mod5, layer, which, ngrid):
    d = mod5.shape[-1]
    if ngrid == 2:
        return pl.BlockSpec((None, None, None, 1, d), lambda b, i: (layer, b, which, 0, 0))
    return pl.BlockSpec((None, None, None, 1, d), lambda b, i, j: (layer, b, which, 0, 0))


def _inproj(x, mod5, layer, gain, w, out_dtype, tm=512, tn=512):
    b, s, d = x.shape
    zc = w.shape[1]
    tm = min(tm, s)
    return pl.pallas_call(
        _inproj_kernel,
        grid=(b, s // tm, zc // tn),
        in_specs=[pl.BlockSpec((None, tm, d), lambda bi, i, j: (bi, i, 0)),
                  _full_mod_spec(mod5, layer, 0, 3),
                  _full_mod_spec(mod5, layer, 1, 3),
                  pl.BlockSpec((1, d), lambda bi, i, j: (0, 0)),
                  pl.BlockSpec((d, tn), lambda bi, i, j: (0, j))],
        out_specs=pl.BlockSpec((None, tm, tn), lambda bi, i, j: (bi, i, j)),
        out_shape=jax.ShapeDtypeStruct((b, s, zc), out_dtype),
        scratch_shapes=[pltpu.VMEM((tm, d), BF16)],
        compiler_params=_cparams(("parallel", "parallel", "arbitrary")),
        name="inproj",
    )(x, mod5, mod5, gain.reshape(1, d), w)


def _bias_kernel(rb_ref, o_ref):
    which = pl.program_id(0)
    h = pl.program_id(1)
    ri = lax.broadcasted_iota(jnp.int32, (TQ, TQ), 0)
    ci = lax.broadcasted_iota(jnp.int32, (TQ, TQ), 1)
    dist = ri - ci + (1 - which) * TQ
    n = jnp.maximum(dist, 0)
    max_exact = N_BUCKETS // 2
    nf = jnp.maximum(n, 1).astype(F32)
    large = max_exact + (jnp.log(nf / max_exact) / math.log(MAX_DISTANCE / max_exact)
                         * (N_BUCKETS - max_exact)).astype(jnp.int32)
    large = jnp.minimum(large, N_BUCKETS - 1)
    bucket = jnp.where(n < max_exact, n, large)
    far = rb_ref[N_BUCKETS - 1, h]
    acc = jnp.zeros((TQ, TQ), F32)
    for bkt in range(N_BUCKETS - 1):
        acc = jnp.where(bucket == bkt, rb_ref[bkt, h] - far, acc)
    o_ref[...] = acc


def _bias_tiles(rel_bias):
    assert TQ >= MAX_DISTANCE
    return pl.pallas_call(
        _bias_kernel,
        grid=(2, HEADS),
        in_specs=[pl.BlockSpec(memory_space=pltpu.SMEM)],
        out_specs=pl.BlockSpec((None, None, TQ, TQ), lambda w, h: (w, h, 0, 0)),
        out_shape=jax.ShapeDtypeStruct((2, HEADS, TQ, TQ), F32),
        compiler_params=_cparams(("parallel", "parallel")),
        name="bias_tiles",
    )(rel_bias)


BISECT_MAX_IT = 300


def _dsa_kernel(q_ref, k_ref, v_ref, qi_ref, ki_ref, wi_ref, bias_ref, o_ref,
                sc_ref, qm_ref, qim_ref, wb_ref, *, n_keep, seq):
    j = pl.program_id(1)
    nt = j + 1
    n_lane_blk = TQ // LANES
    kf = float(n_keep)
    lane = lax.broadcasted_iota(jnp.int32, (TQ, LANES), 1)
    att_scale = HEAD_DIM ** -0.5
    w_scale = (HEADS ** -0.5) * (HEAD_DIM ** -0.5)

    for h in range(HEADS):
        p, odd = divmod(h, 2)
        hm = (lane >= HEAD_DIM) if odd else (lane < HEAD_DIM)
        qs = q_ref[:, p * LANES:(p + 1) * LANES]
        qm_ref[h] = jnp.where(hm, qs, jnp.zeros_like(qs)) * att_scale
        qis = qi_ref[:, p * LANES:(p + 1) * LANES]
        qim_ref[h] = jnp.where(hm, qis, jnp.zeros_like(qis))
        wb_ref[h] = jnp.broadcast_to(wi_ref[:, h:h + 1] * w_scale, (TQ, LANES))

    row = lax.broadcasted_iota(jnp.int32, (TQ, TQ), 0) + j * TQ
    col = lax.broadcasted_iota(jnp.int32, (TQ, TQ), 1)

    def idx_tile(kt, carry):
        ki_t = ki_ref[pl.ds(pl.multiple_of(kt * TQ, TQ), TQ), :].astype(BF16)
        acc = jnp.zeros((TQ, TQ), F32)
        for h in range(HEADS):
            sc = _nt(qim_ref[h], ki_t)
            wfull = jnp.concatenate([wb_ref[h]] * n_lane_blk, axis=1)
            acc = acc + jnp.maximum(sc, 0.0) * wfull
        sc_ref[kt] = jnp.where(col + kt * TQ <= row, acc, NEG_INF)
        return carry

    lax.fori_loop(0, nt, idx_tile, 0)

    def lane_blocks(t):
        return [t[:, c * LANES:(c + 1) * LANES] for c in range(n_lane_blk)]

    def minmax_tile(kt, c):
        mx, mn = c
        for tc in lane_blocks(sc_ref[kt]):
            mx = jnp.maximum(mx, tc)
            mn = jnp.minimum(mn, jnp.where(tc > 0.5 * NEG_INF, tc, -NEG_INF))
        return mx, mn

    mx, mn = lax.fori_loop(0, nt, minmax_tile,
                           (jnp.full((TQ, LANES), NEG_INF, F32), jnp.full((TQ, LANES), -NEG_INF, F32)))
    rmax = jnp.max(mx, axis=1, keepdims=True)
    rmin = jnp.min(mn, axis=1, keepdims=True)
    nvalid = (lax.broadcasted_iota(jnp.int32, (TQ, 1), 0) + j * TQ + 1).astype(F32)
    small = nvalid <= kf

    def count_ge(thr):
        def body(kt, cp):
            for tc in lane_blocks(sc_ref[kt]):
                cp = cp + jnp.where(tc >= thr, 1.0, 0.0)
            return cp
        cp = lax.fori_loop(0, nt, body, jnp.zeros((TQ, LANES), F32))
        return jnp.sum(cp, axis=1, keepdims=True)

    lo0 = jnp.where(small, 0.5 * NEG_INF, rmin)
    hi0 = jnp.where(small, -NEG_INF, rmax + jnp.maximum(jnp.abs(rmax) * 1e-6, 1e-30))
    done0 = jnp.where(small, 1.0, 0.0)

    def bis_cond(c):
        return jnp.logical_and(c[0] < BISECT_MAX_IT, c[6] > 0.0)

    def bis_body(c):
        it, lo, hi, clo, chi, done, _ = c
        mid = lo + 0.5 * (hi - lo)
        stalled = jnp.logical_or(mid <= lo, mid >= hi)
        cnt = count_ge(mid)
        act = done < 0.5
        ge = cnt >= kf
        up_lo = jnp.logical_and(act, ge)
        up_hi = jnp.logical_and(act, jnp.logical_not(ge))
        lo = jnp.where(up_lo, mid, lo)
        clo = jnp.where(up_lo, cnt, clo)
        hi = jnp.where(up_hi, mid, hi)
        chi = jnp.where(up_hi, cnt, chi)
        fin = jnp.logical_or(cnt == kf, stalled)
        done = jnp.where(fin, 1.0, done)
        return it + 1, lo, hi, clo, chi, done, jnp.sum(1.0 - done)

    _, lo, hi, clo, chi, _, _ = lax.while_loop(
        bis_cond, bis_body,
        (jnp.int32(0), lo0, hi0, nvalid, jnp.zeros((TQ, 1), F32), done0, jnp.sum(1.0 - done0)))

    tie = jnp.logical_and(clo > kf, jnp.logical_not(small))
    n_tie_iter = int(math.ceil(math.log2(seq))) + 1

    def tie_fn():
        need = kf - chi

        def count_tie_le(m):
            def body(kt, cp):
                t = sc_ref[kt]
                s_abs = (col + kt * TQ).astype(F32)
                inside = jnp.where(t >= lo, jnp.where(t < hi, jnp.where(s_abs <= m, 1.0, 0.0), 0.0), 0.0)
                for tc in lane_blocks(inside):
                    cp = cp + tc
                return cp
            cp = lax.fori_loop(0, nt, body, jnp.zeros((TQ, LANES), F32))
            return jnp.sum(cp, axis=1, keepdims=True)

        def tb(_, c):
            loi, hii = c
            midi = jnp.floor((loi + hii) * 0.5)
            ge = count_tie_le(midi) >= need
            return jnp.where(ge, loi, midi), jnp.where(ge, midi, hii)

        _, hii = lax.fori_loop(0, n_tie_iter, tb,
                               (jnp.full((TQ, 1), -1.0, F32), jnp.full((TQ, 1), seq - 1.0, F32)))
        return jnp.where(tie, hii, float(seq))

    mstar = lax.cond(jnp.sum(jnp.where(tie, 1.0, 0.0)) > 0.0, tie_fn,
                     lambda: jnp.full((TQ, 1), float(seq), F32))

    def mask_tile(kt, carry):
        t = sc_ref[kt]
        s_abs = (col + kt * TQ).astype(F32)
        keep_tie = jnp.where(s_abs <= mstar, 0.0, NEG_INF)
        sc_ref[kt] = jnp.where(t >= lo, jnp.where(t >= hi, 0.0, keep_tie), NEG_INF)
        return carry

    lax.fori_loop(0, nt, mask_tile, 0)

    def attn_tile(kt, carry, h, p, bias_idx):
        m, l, acc = carry
        r0 = pl.multiple_of(kt * TQ, TQ)
        ks = k_ref[pl.ds(r0, TQ), p * LANES:(p + 1) * LANES]
        vs = v_ref[pl.ds(r0, TQ), p * LANES:(p + 1) * LANES]
        s = _nt(qm_ref[h], ks) + sc_ref[kt]
        if bias_idx is not None:
            s = s + bias_ref[bias_idx(kt), h]
        m_new = jnp.maximum(m, jnp.max(s, axis=1, keepdims=True))
        alpha = jnp.exp(m - m_new)
        pe = jnp.exp(s - m_new)
        l = alpha * l + jnp.sum(pe, axis=1, keepdims=True)
        acc = alpha * acc + _mm(pe.astype(BF16), vs)
        return m_new, l, acc

    n_far = jnp.maximum(j - 1, 0)
    outs = []
    for h in range(HEADS):
        p = h // 2
        init = (jnp.full((TQ, 1), NEG_INF, F32), jnp.zeros((TQ, 1), F32), jnp.zeros((TQ, LANES), F32))
        c = lax.fori_loop(0, n_far, functools.partial(attn_tile, h=h, p=p, bias_idx=None), init)
        c = lax.fori_loop(n_far, nt, functools.partial(attn_tile, h=h, p=p, bias_idx=lambda kt: kt - j + 1), c)
        outs.append(c[2] / c[1])
        if h % 2 == 1:
            o_ref[:, p * LANES:(p + 1) * LANES] = jnp.where(lane < HEAD_DIM, outs[h - 1], outs[h]).astype(o_ref.dtype)


def _dsa(z_a, z_r, bias):
    b, s, _ = z_a.shape
    n_keep = min(TOPK_MAX, s // 4)
    kern = functools.partial(_dsa_kernel, n_keep=n_keep, seq=s)
    return pl.pallas_call(
        kern,
        grid=(b, s // TQ),
        in_specs=[pl.BlockSpec((None, TQ, MIX_W), lambda bi, j: (bi, j, 0)),
                  pl.BlockSpec((None, s, MIX_W), lambda bi, j: (bi, 0, 1)),
                  pl.BlockSpec((None, s, MIX_W), lambda bi, j: (bi, 0, 2)),
                  pl.BlockSpec((None, TQ, MIX_W), lambda bi, j: (bi, j, 3)),
                  pl.BlockSpec((None, s, LANES), lambda bi, j: (bi, 0, ZR_KI)),
                  pl.BlockSpec((None, TQ, LANES), lambda bi, j: (bi, j, ZR_WI)),
                  pl.BlockSpec((2, HEADS, TQ, TQ), lambda bi, j: (0, 0, 0, 0))],
        out_specs=pl.BlockSpec((None, TQ, MIX_W), lambda bi, j: (bi, j, 0)),
        out_shape=jax.ShapeDtypeStruct((b, s, MIX_W), BF16),
        scratch_shapes=[pltpu.VMEM((s // TQ, TQ, TQ), F32),
                        pltpu.VMEM((HEADS, TQ, LANES), BF16),
                        pltpu.VMEM((HEADS, TQ, LANES), BF16),
                        pltpu.VMEM((HEADS, TQ, LANES), F32)],
        compiler_params=_cparams(("parallel", "arbitrary")),
        name="dsa",
    )(z_a, z_a, z_a, z_a, z_r, z_r, bias)


def _halo_spec(width, blk, tm):
    step = tm // SUBLANES
    return pl.BlockSpec((None, SUBLANES, width), lambda bi, i: (bi, jnp.maximum(i * step - 1, 0), blk))


def _shift_lerp(cur, halo_ref, mu, first):
    prev_last = jnp.where(first, 0.0, halo_ref[SUBLANES - 1:SUBLANES, :])
    rolled = pltpu.roll(cur, 1, axis=0)
    rowid = lax.broadcasted_iota(jnp.int32, cur.shape, 0)
    sh = jnp.where(rowid == 0, prev_last, rolled)
    return cur + (sh - cur) * mu


def _rwkv_kernel(r_ref, k_ref, v_ref, s2_ref, s3_ref, rh_ref, kh_ref, vh_ref, s2h_ref, s3h_ref,
                 mu_r, mu_k, mu_v, mu_s2, mu_s3, w0_ref, wup_ref, a0_ref, aup_ref, gup_ref,
                 kk_ref, ka_ref, rk_ref, bones_ref, tri_ref,
                 y_ref, g_ref, bonus_ref,
                 st_ref, rs, ls, ks, vs, kks, kbs):
    first = pl.program_id(1) == 0

    @pl.when(first)
    def _():
        st_ref[...] = jnp.zeros_like(st_ref)

    r = _shift_lerp(r_ref[...], rh_ref, mu_r[...], first)
    k = _shift_lerp(k_ref[...], kh_ref, mu_k[...], first)
    v = _shift_lerp(v_ref[...], vh_ref, mu_v[...], first)
    s2 = _shift_lerp(s2_ref[...], s2h_ref, mu_s2[...], first)
    s3 = _shift_lerp(s3_ref[...], s3h_ref, mu_s3[...], first)
    xw = w0_ref[...] + _mm(jnp.tanh(s2), wup_ref[...], HIGHEST)
    softplus = jnp.maximum(-xw, 0.0) + jnp.log1p(jnp.exp(-jnp.abs(xw)))
    ld = -jnp.exp(-softplus - 0.5)
    af = jax.nn.sigmoid(a0_ref[...] + _mm(s2, aup_ref[...], HIGHEST))
    g_ref[...] = _mm(jax.nn.sigmoid(s3), gup_ref[...], HIGHEST)
    kkr = k * kk_ref[...]
    ss = _mm(kkr * kkr, bones_ref[...], HIGHEST)
    kkn = kkr / jnp.maximum(jnp.sqrt(ss), 1e-12)
    kmod = k * (1.0 + (af - 1.0) * ka_ref[...])
    bonus_ref[...] = _mm(r * kmod * rk_ref[...], bones_ref[...], HIGHEST) * v
    rs[...] = r
    ls[...] = ld
    ks[...] = kmod
    vs[...] = v
    kks[...] = kkn
    kbs[...] = kkn * af

    ri = lax.broadcasted_iota(jnp.int32, (QUAD, QUAD), 0)
    ci = lax.broadcasted_iota(jnp.int32, (QUAD, QUAD), 1)
    same_head = (ri // HEAD_DIM) == (ci // HEAD_DIM)
    strict = jnp.logical_and(same_head, (ri % CHUNK) > (ci % CHUNK))
    incl = jnp.logical_and(same_head, (ri % CHUNK) >= (ci % CHUNK))
    eye = ri == ci

    def stack(x):
        return jnp.where(same_head, jnp.concatenate([x] * 4, axis=0), 0.0)

    def tile4(x):
        return jnp.concatenate([x] * 4, axis=0)

    def unstack(x):
        return (x[0:CHUNK] + x[CHUNK:2 * CHUNK]) + (x[2 * CHUNK:3 * CHUNK] + x[3 * CHUNK:4 * CHUNK])

    def bmm(a, b):
        return _mm(a.astype(BF16), b.astype(BF16))

    def chunk(c, carry):
        r0 = pl.multiple_of(c * CHUNK, CHUNK)
        rows = pl.ds(r0, CHUNK)
        ldc = ls[rows, :]
        cl = _mm(tri_ref[...], ldc, HIGHEST)
        cl_end = cl[CHUNK - 1:CHUNK, :]
        e_in = jnp.exp(cl)
        e_out = jnp.exp(-cl)
        rt_all = rs[rows, :] * e_in
        at_all = -kks[rows, :] * jnp.exp(cl - ldc)
        bt_all = kbs[rows, :] * e_out
        kt_all = ks[rows, :] * e_out
        e_end = jnp.exp(cl_end - cl)
        bg_all = kbs[rows, :] * e_end
        kg_all = ks[rows, :] * e_end
        v_all = vs[rows, :]
        gam_all = jnp.exp(cl_end)
        for q in range(MIX_W // QUAD):
            sl = slice(q * QUAD, (q + 1) * QUAD)
            rt, at, bt, kt, vv = rt_all[:, sl], at_all[:, sl], bt_all[:, sl], kt_all[:, sl], v_all[:, sl]
            a4, r4, v4 = stack(at), stack(rt), stack(vv)
            bt4, kt4 = tile4(bt).astype(BF16), tile4(kt).astype(BF16)
            a4b, r4b = a4.astype(BF16), r4.astype(BF16)
            m_ab = jnp.where(strict, _nt(a4b, bt4), 0.0)
            m_ak = jnp.where(strict, _nt(a4b, kt4), 0.0)
            m_rb = jnp.where(incl, _nt(r4b, bt4), 0.0)
            m_rk = jnp.where(incl, _nt(r4b, kt4), 0.0)
            pw = m_ab
            inv = jnp.where(eye, 1.0, 0.0) + m_ab
            for _ in range(int(math.log2(CHUNK)) - 1):
                pw = bmm(pw, pw)
                inv = inv + bmm(inv, pw)
            mv4 = bmm(m_ak, v4)
            ah4 = bmm(inv, a4)
            uh4 = bmm(inv, mv4)
            ah, uh = unstack(ah4), unstack(uh4)
            ry = rt + unstack(bmm(m_rb, ah4))
            y0 = unstack(bmm(m_rb, uh4) + bmm(m_rk, v4))
            bg, kg = bg_all[:, sl].astype(BF16), kg_all[:, sl].astype(BF16)
            gam = jnp.broadcast_to(gam_all[:, sl], (QUAD, QUAD))
            g_mat = jnp.where(same_head, _tn(bg, ah.astype(BF16)), 0.0) + jnp.where(eye, gam, 0.0)
            h_mat = jnp.where(same_head, _tn(bg, uh.astype(BF16)) + _tn(kg, vv.astype(BF16)), 0.0)
            st = st_ref[q]
            y_ref[rows, sl] = _mm(ry, st, HIGHEST) + y0
            st_ref[q] = _mm(g_mat, st, HIGHEST) + h_mat
        return carry

    lax.fori_loop(0, r_ref.shape[0] // CHUNK, chunk, 0)


def _rwkv(z_r, prm):
    b, s, _ = z_r.shape
    tm = min(RW_TM, s)
    tok = lambda blk: pl.BlockSpec((None, tm, MIX_W), lambda bi, i: (bi, i, blk))
    tok128 = lambda blk: pl.BlockSpec((None, tm, LANES), lambda bi, i: (bi, i, blk))
    const = lambda a: pl.BlockSpec(a.shape, lambda bi, i: (0,) * a.ndim)
    consts = [prm[n] for n in ("mu_r", "mu_k", "mu_v", "mu_s2", "mu_s3", "w0", "w_up", "a0", "a_up", "g_up",
                               "k_k", "k_a", "r_k", "bones", "tri")]
    out_spec = pl.BlockSpec((None, tm, MIX_W), lambda bi, i: (bi, i, 0))
    out_sds = jax.ShapeDtypeStruct((b, s, MIX_W), F32)
    return pl.pallas_call(
        _rwkv_kernel,
        grid=(b, s // tm),
        in_specs=[tok(ZR_R), tok(ZR_K), tok(ZR_V), tok128(ZR_S2), tok128(ZR_S3),
                  _halo_spec(MIX_W, ZR_R, tm), _halo_spec(MIX_W, ZR_K, tm), _halo_spec(MIX_W, ZR_V, tm),
                  _halo_spec(LANES, ZR_S2, tm), _halo_spec(LANES, ZR_S3, tm)] + [const(a) for a in consts],
        out_specs=[out_spec, out_spec, out_spec],
        out_shape=[out_sds, out_sds, out_sds],
        scratch_shapes=[pltpu.VMEM((MIX_W // QUAD, QUAD, QUAD), F32)] + [pltpu.VMEM((tm, MIX_W), F32)] * 6,
        compiler_params=_cparams(("parallel", "arbitrary")),
        name="rwkv",
    )(z_r, z_r, z_r, z_r, z_r, z_r, z_r, z_r, z_r, z_r, *consts)


def _merge_kernel(x_ref, oa_ref, y_ref, g_ref, bonus_ref, cb_ref, cc_ref, cx_ref, cch_ref, cxh_ref, gate_ref,
                  g1_ref, lnw_ref, lnb_ref, bavg_ref, cw_ref, wb_ref, wo_ref, o_ref):
    first = pl.program_id(1) == 0
    y = y_ref[...]
    mean = _mm(y, bavg_ref[...], HIGHEST)
    yc = y - mean
    var = _mm(yc * yc, bavg_ref[...], HIGHEST)
    o_rwkv = (yc * lax.rsqrt(var + GN_EPS) * lnw_ref[...] + lnb_ref[...] + bonus_ref[...]) * g_ref[...]
    p = cc_ref[...] * cx_ref[...]
    ph = jnp.where(first, 0.0, cch_ref[...] * cxh_ref[...])
    rowid = lax.broadcasted_iota(jnp.int32, p.shape, 0)
    conv = p * cw_ref[CONV_W - 1:CONV_W, :]
    for back in range(1, CONV_W):
        rolled = pltpu.roll(p, back, axis=0)
        for rr in range(back):
            rolled = jnp.where(rowid == rr, ph[SUBLANES - back + rr:SUBLANES - back + rr + 1, :], rolled)
        conv = conv + rolled * cw_ref[CONV_W - 1 - back:CONV_W - back, :]
    o_conv = cb_ref[...] * conv
    d = x_ref.shape[-1]
    merged = jnp.zeros(x_ref.shape, F32)
    for bi, o in enumerate((oa_ref[...], o_rwkv, o_conv)):
        gate = jax.nn.sigmoid(gate_ref[:, bi * d:(bi + 1) * d].astype(F32))
        merged = merged + gate * _mm(o.astype(BF16), wb_ref[bi])
    o_ref[...] = x_ref[...] + g1_ref[...] * _mm(merged.astype(BF16), wo_ref[...])


def _merge(x, o_attn, y, g, bonus, z_r, z_g, mod5, layer, prm, tm=256):
    b, s, d = x.shape
    tm = min(tm, s)
    tok = lambda w, blk: pl.BlockSpec((None, tm, w), lambda bi, i: (bi, i, blk))
    const = lambda a: pl.BlockSpec(a.shape, lambda bi, i: (0,) * a.ndim)
    consts = [prm[n] for n in ("ln_w", "ln_b", "bavg", "conv_w", "w_branch", "w_o")]
    return pl.pallas_call(
        _merge_kernel,
        grid=(b, s // tm),
        in_specs=[tok(d, 0), tok(MIX_W, 0), tok(MIX_W, 0), tok(MIX_W, 0), tok(MIX_W, 0),
                  tok(MIX_W, ZR_CB), tok(MIX_W, ZR_CC), tok(MIX_W, ZR_CX),
                  _halo_spec(MIX_W, ZR_CC, tm), _halo_spec(MIX_W, ZR_CX, tm),
                  tok(3 * d, 0), _full_mod_spec(mod5, layer, 2, 2)] + [const(a) for a in consts],
        out_specs=tok(d, 0),
        out_shape=jax.ShapeDtypeStruct((b, s, d), F32),
        compiler_params=_cparams(("parallel", "parallel")),
        name="merge",
    )(x, o_attn, y, g, bonus, z_r, z_r, z_r, z_r, z_r, z_g, mod5, *consts)


def _ffn_kernel(x_ref, xh_ref, sh_ref, sc_ref, g2_ref, gain_ref, wa_ref, wg_ref, cw_ref, wd_ref, o_ref,
                h_ref, hh_ref, acc_ref):
    jf = pl.program_id(2)
    first = pl.program_id(1) == 0

    @pl.when(jf == 0)
    def _():
        h_ref[...] = _norm_mod(x_ref[...], gain_ref[...], sc_ref[...], sh_ref[...]).astype(h_ref.dtype)
        hh_ref[...] = _norm_mod(xh_ref[...], gain_ref[...], sc_ref[...], sh_ref[...]).astype(hh_ref.dtype)
        acc_ref[...] = jnp.zeros_like(acc_ref)

    a = _mm(h_ref[...], wa_ref[...])
    ah = jnp.where(first, 0.0, _mm(hh_ref[...], wa_ref[...]))
    gl = _mm(h_ref[...], wg_ref[...])
    rowid = lax.broadcasted_iota(jnp.int32, a.shape, 0)
    conv = a * cw_ref[CONV_W - 1:CONV_W, :]
    for back in range(1, CONV_W):
        rolled = pltpu.roll(a, back, axis=0)
        for rr in range(back):
            rolled = jnp.where(rowid == rr, ah[SUBLANES - back + rr:SUBLANES - back + rr + 1, :], rolled)
        conv = conv + rolled * cw_ref[CONV_W - 1 - back:CONV_W - back, :]
    u = conv * jax.nn.sigmoid(conv) * gl
    acc_ref[...] += _mm(u.astype(BF16), wd_ref[...])

    @pl.when(jf == pl.num_programs(2) - 1)
    def _():
        o_ref[...] = x_ref[...] + g2_ref[...] * acc_ref[...]


def _ffn(x, mod5, layer, gain, w_up, conv_w, w_down, tm=512, tf=256):
    b, s, d = x.shape
    d_ff = w_down.shape[0]
    nf = d_ff // tf
    tm = min(tm, s)
    step = tm // SUBLANES
    return pl.pallas_call(
        _ffn_kernel,
        grid=(b, s // tm, nf),
        in_specs=[pl.BlockSpec((None, tm, d), lambda bi, i, j: (bi, i, 0)),
                  pl.BlockSpec((None, SUBLANES, d), lambda bi, i, j: (bi, jnp.maximum(i * step - 1, 0), 0)),
                  _full_mod_spec(mod5, layer, 3, 3), _full_mod_spec(mod5, layer, 4, 3),
                  _full_mod_spec(mod5, layer, 5, 3),
                  pl.BlockSpec((1, d), lambda bi, i, j: (0, 0)),
                  pl.BlockSpec((d, tf), lambda bi, i, j: (0, j)),
                  pl.BlockSpec((d, tf), lambda bi, i, j: (0, j + nf)),
                  pl.BlockSpec((CONV_W, tf), lambda bi, i, j: (0, j)),
                  pl.BlockSpec((tf, d), lambda bi, i, j: (j, 0))],
        out_specs=pl.BlockSpec((None, tm, d), lambda bi, i, j: (bi, i, 0)),
        out_shape=jax.ShapeDtypeStruct((b, s, d), F32),
        scratch_shapes=[pltpu.VMEM((tm, d), BF16), pltpu.VMEM((SUBLANES, d), BF16), pltpu.VMEM((tm, d), F32)],
        compiler_params=_cparams(("parallel", "parallel", "arbitrary")),
        name="ffn",
    )(x, x, mod5, mod5, mod5, gain.reshape(1, d), w_up, w_up, conv_w, w_down)


def _final_norm_kernel(x_ref, g_ref, o_ref):
    x = x_ref[...]
    o_ref[...] = x * lax.rsqrt(jnp.mean(x * x, axis=-1, keepdims=True) + NORM_EPS) * g_ref[...]


def _final_norm(x, gain, tm=1024):
    b, s, d = x.shape
    tm = min(tm, s)
    return pl.pallas_call(
        _final_norm_kernel,
        grid=(b, s // tm),
        in_specs=[pl.BlockSpec((None, tm, d), lambda bi, i: (bi, i, 0)),
                  pl.BlockSpec((1, d), lambda bi, i: (0, 0))],
        out_specs=pl.BlockSpec((None, tm, d), lambda bi, i: (bi, i, 0)),
        out_shape=jax.ShapeDtypeStruct((b, s, d), F32),
        compiler_params=_cparams(("parallel", "parallel")),
        name="final_norm",
    )(x, gain.reshape(1, d))


def _split_w_in(w):
    sizes = (MIX_W, MIX_W, MIX_W, HEADS * HEAD_DIM, HEAD_DIM, HEADS,
             MIX_W, MIX_W, MIX_W, LORA_DECAY, LORA_ICLR, LORA_GATE,
             MIX_W, MIX_W, MIX_W)
    out, o = [], 0
    for n in sizes:
        out.append(w[:, o:o + n])
        o += n
    out.append(w[:, o:])
    return out


def _layer_params(l, w_in, rwkv_mu, rwkv_w0, rwkv_w_up, rwkv_a0, rwkv_a_up, rwkv_g_up, rwkv_k_k, rwkv_k_a,
                  rwkv_r_k, rwkv_ln_w, rwkv_ln_b, sc_conv_w, w_branch, w_o):
    d = w_in.shape[1]
    q, k, v, qi, ki, wi, rr, rk, rv, wd, ad, gd, cb, cc, cx, gates = _split_w_in(w_in[l])
    pad = lambda n: jnp.zeros((d, n), F32)
    w_a = jnp.concatenate([q, k, v, qi], axis=1).astype(BF16)
    w_r = jnp.concatenate([rr, rk, rv, cb, cc, cx, ki, ki, wi, pad(LANES - HEADS), wd, ad, gd],
                          axis=1).astype(BF16)
    assert w_r.shape[1] == ZR_COLS
    mu = rwkv_mu[l]
    row = lambda a: a.reshape(1, -1)
    head_id = jnp.arange(MIX_W) // HEAD_DIM
    bones = (head_id[:, None] == head_id[None, :]).astype(F32)
    tri = (jnp.arange(CHUNK)[:, None] >= jnp.arange(CHUNK)[None, :]).astype(F32)
    zl = lambda n: jnp.zeros((n, MIX_W), F32)
    return dict(
        w_a=w_a, w_g=gates.astype(BF16), w_r=w_r,
        mu_r=row(mu[:MIX_W]), mu_k=row(mu[MIX_W:2 * MIX_W]), mu_v=row(mu[2 * MIX_W:3 * MIX_W]),
        mu_s2=row(mu[3 * MIX_W:3 * MIX_W + LORA_DECAY + LORA_ICLR]), mu_s3=row(mu[3 * MIX_W + LORA_DECAY + LORA_ICLR:]),
        w0=row(rwkv_w0[l]), w_up=jnp.concatenate([rwkv_w_up[l], zl(LORA_ICLR)], axis=0),
        a0=row(rwkv_a0[l]), a_up=jnp.concatenate([zl(LORA_DECAY), rwkv_a_up[l]], axis=0),
        g_up=rwkv_g_up[l], k_k=row(rwkv_k_k[l]), k_a=row(rwkv_k_a[l]), r_k=row(rwkv_r_k[l]),
        bones=bones, tri=tri, bavg=bones / HEAD_DIM,
        ln_w=row(rwkv_ln_w[l]), ln_b=row(rwkv_ln_b[l]), conv_w=sc_conv_w[l].T,
        w_branch=w_branch[l].astype(BF16), w_o=w_o[l].astype(BF16),
    )


def kernel(x, c, positions, rel_bias, final_norm, ada_w, ada_b, norm_mix, w_in, rwkv_mu, rwkv_w0, rwkv_w_up,
           rwkv_a0, rwkv_a_up, rwkv_g_up, rwkv_k_k, rwkv_k_a, rwkv_r_k, rwkv_ln_w, rwkv_ln_b, sc_conv_w,
           w_branch, w_o, norm_ffn, ffn_w_up, ffn_conv_w, ffn_w_down):
    depth = ada_w.shape[0]
    mod5 = _ada_mod(c, ada_w, ada_b)
    bias = _bias_tiles(rel_bias)
    for l in range(depth):
        prm = _layer_params(l, w_in, rwkv_mu, rwkv_w0, rwkv_w_up, rwkv_a0, rwkv_a_up, rwkv_g_up, rwkv_k_k,
                            rwkv_k_a, rwkv_r_k, rwkv_ln_w, rwkv_ln_b, sc_conv_w, w_branch, w_o)
        z_a = _inproj(x, mod5, l, norm_mix[l], prm["w_a"], BF16)
        z_g = _inproj(x, mod5, l, norm_mix[l], prm["w_g"], BF16)
        z_r = _inproj(x, mod5, l, norm_mix[l], prm["w_r"], F32)
        o_attn = _dsa(z_a, z_r, bias)
        y, g, bonus = _rwkv(z_r, prm)
        x = _merge(x, o_attn, y, g, bonus, z_r, z_g, mod5, l, prm)
        x = _ffn(x, mod5, l, norm_ffn[l], ffn_w_up[l].astype(BF16), ffn_conv_w[l].T, ffn_w_down[l].astype(BF16))
    return _final_norm(x, final_norm)
```

```python
import functools
import math

import jax
import jax.numpy as jnp
from jax import lax
from jax.experimental import pallas as pl
from jax.experimental.pallas import tpu as pltpu

F32 = jnp.float32
BF16 = jnp.bfloat16
HIGHEST = lax.Precision.HIGHEST

MIX_W = 512
HEADS = 8
HEAD_DIM = 64
TOPK_MAX = 256
N_BUCKETS = 32
MAX_DISTANCE = 128
LORA_DECAY = 64
LORA_ICLR = 64
LORA_GATE = 128
CONV_W = 3
NORM_EPS = 1e-6
GN_EPS = 64e-5
NEG_INF = -1e30

LANES = 128
SUBLANES = 8
VMEM_LIMIT = 56 * 1024 * 1024

TQ = 256
CHUNK = 64
RW_TM = 256
QUAD = 4 * HEAD_DIM

ZR_R, ZR_K, ZR_V, ZR_CB, ZR_CC, ZR_CX = 0, 1, 2, 3, 4, 5
ZR_SMALL0 = 6 * MIX_W // LANES
ZR_KI, ZR_WI, ZR_S2, ZR_S3 = ZR_SMALL0, ZR_SMALL0 + 1, ZR_SMALL0 + 2, ZR_SMALL0 + 3
ZR_COLS = 6 * MIX_W + 4 * LANES


def _cparams(sem):
    return pltpu.CompilerParams(dimension_semantics=sem, vmem_limit_bytes=VMEM_LIMIT)


def _nt(a, b, precision=None):
    return lax.dot_general(a, b, (((1,), (1,)), ((), ())), precision=precision,
                           preferred_element_type=F32)


def _tn(a, b, precision=None):
    return lax.dot_general(a, b, (((0,), (0,)), ((), ())), precision=precision,
                           preferred_element_type=F32)


def _mm(a, b, precision=None):
    return jnp.dot(a, b, precision=precision, preferred_element_type=F32)


def _mod_kernel(c_ref, w_ref, b_ref, o_ref):
    o_ref[...] = _mm(c_ref[...], w_ref[...], HIGHEST) + b_ref[...]


def _ada_mod(c, ada_w, ada_b):
    depth, d, d6 = ada_w.shape
    b = c.shape[0]
    out = pl.pallas_call(
        _mod_kernel,
        grid=(depth, d6 // d),
        in_specs=[pl.BlockSpec((b, d), lambda l, j: (0, 0)),
                  pl.BlockSpec((None, d, d), lambda l, j: (l, 0, j)),
                  pl.BlockSpec((None, 1, d), lambda l, j: (l, 0, j))],
        out_specs=pl.BlockSpec((None, b, d), lambda l, j: (l, 0, j)),
        out_shape=jax.ShapeDtypeStruct((depth, b, d6), F32),
        compiler_params=_cparams(("parallel", "parallel")),
        name="ada_mod",
    )(c, ada_w, ada_b.reshape(depth, 1, d6))
    return out.reshape(depth, b, d6 // d, 1, d)


def _mod_spec(layer, which, ngrid):
    if ngrid == 2:
        return pl.BlockSpec((None, None, None, 1, None), lambda b, i: (layer, b, which, 0, 0))
    return pl.BlockSpec((None, None, None, 1, None), lambda b, i, j: (layer, b, which, 0, 0))


def _norm_mod(x, gain, scale, shift):
    y = x * lax.rsqrt(jnp.mean(x * x, axis=-1, keepdims=True) + NORM_EPS) * gain
    return y * (1.0 + scale) + shift


def _inproj_kernel(x_ref, sh_ref, sc_ref, g_ref, w_ref, o_ref, h_ref):
    @pl.when(pl.program_id(2) == 0)
    def _():
        h_ref[...] = _norm_mod(x_ref[...], g_ref[...], sc_ref[...], sh_ref[...]).astype(h_ref.dtype)

    o_ref[...] = _mm(h_ref[...], w_ref[...]).astype(o_ref.dtype)


def _full_mod_spec(mod5, layer, which, ngrid):
    d = mod5.shape[-1]
    if ngrid == 2:
        return pl.BlockSpec((None, None, None, 1, d), lambda b, i: (layer, b, which, 0, 0))
    return pl.BlockSpec((None, None, None, 1, d), lambda b, i, j: (layer, b, which, 0, 0))


def _inproj(x, mod5, layer, gain, w, out_dtype, tm=512, tn=512):
    b, s, d = x.shape
    zc = w.shape[1]
    tm = min(tm, s)
    return pl.pallas_call(
        _inproj_kernel,
        grid=(b, s // tm, zc // tn),
        in_specs=[pl.BlockSpec((None, tm, d), lambda bi, i, j: (bi, i, 0)),
                  _full_mod_spec(mod5, layer, 0, 3),
                  _full_mod_spec(mod5, layer, 1, 3),
                  pl.BlockSpec((1, d), lambda bi, i, j: (0, 0)),
                  pl.BlockSpec((d, tn), lambda bi, i, j: (0, j))],
        out_specs=pl.BlockSpec((None, tm, tn), lambda bi, i, j: (bi, i, j)),
        out_shape=jax.ShapeDtypeStruct((b, s, zc), out_dtype),
        scratch_shapes=[pltpu.VMEM((tm, d), BF16)],
        compiler_params=_cparams(("parallel", "parallel", "arbitrary")),
        name="inproj",
    )(x, mod5, mod5, gain.reshape(1, d), w)


def _bias_kernel(rb_ref, o_ref):
    which = pl.program_id(0)
    h = pl.program_id(1)
    ri = lax.broadcasted_iota(jnp.int32, (TQ, TQ), 0)
    ci = lax.broadcasted_iota(jnp.int32, (TQ, TQ), 1)
    dist = ci - ri + (1 - which) * TQ
    n = jnp.maximum(dist, 0)
    max_exact = N_BUCKETS // 2
    nf = jnp.maximum(n, 1).astype(F32)
    large = max_exact + (jnp.log(nf / max_exact) / math.log(MAX_DISTANCE / max_exact)
                         * (N_BUCKETS - max_exact)).astype(jnp.int32)
    large = jnp.minimum(large, N_BUCKETS - 1)
    bucket = jnp.where(n < max_exact, n, large)
    far = rb_ref[N_BUCKETS - 1, h]
    acc = jnp.zeros((TQ, TQ), F32)
    for bkt in range(N_BUCKETS - 1):
        acc = jnp.where(bucket == bkt, rb_ref[bkt, h] - far, acc)
    o_ref[...] = acc


def _bias_tiles(rel_bias):
    assert TQ >= MAX_DISTANCE
    return pl.pallas_call(
        _bias_kernel,
        grid=(2, HEADS),
        in_specs=[pl.BlockSpec(memory_space=pltpu.SMEM)],
        out_specs=pl.BlockSpec((None, None, TQ, TQ), lambda w, h: (w, h, 0, 0)),
        out_shape=jax.ShapeDtypeStruct((2, HEADS, TQ, TQ), F32),
        compiler_params=_cparams(("parallel", "parallel")),
        name="bias_tiles",
    )(rel_bias)


BISECT_MAX_IT = 300


def _row_groups(t):
    return [t[r * SUBLANES:(r + 1) * SUBLANES, :] for r in range(t.shape[0] // SUBLANES)]


def _dsa_kernel(q_ref, k_ref, v_ref, qi_ref, ki_ref, wi_ref, bias_ref, o_ref,
                sc_ref, qm_ref, qim_ref, wt_ref, vt_ref, m_ref, l_ref, acc_ref, s_ref, *, n_keep, seq):
    j = pl.program_id(1)
    nt = j + 1
    kf = float(n_keep)
    lane = lax.broadcasted_iota(jnp.int32, (TQ, LANES), 1)
    att_scale = HEAD_DIM ** -0.5
    w_scale = (HEADS ** -0.5) * (HEAD_DIM ** -0.5)

    for h in range(HEADS):
        p, odd = divmod(h, 2)
        hm = (lane >= HEAD_DIM) if odd else (lane < HEAD_DIM)
        qs = q_ref[:, p * LANES:(p + 1) * LANES]
        qm_ref[h] = jnp.where(hm, qs, jnp.zeros_like(qs)) * att_scale
        qis = qi_ref[:, p * LANES:(p + 1) * LANES]
        qim_ref[h] = jnp.where(hm, qis, jnp.zeros_like(qis))
    wt_ref[...] = wi_ref[...].T * w_scale
    vt_ref[j] = v_ref[...].T

    key = lax.broadcasted_iota(jnp.int32, (TQ, TQ), 0)
    qry = lax.broadcasted_iota(jnp.int32, (TQ, TQ), 1) + j * TQ

    def idx_tile(kt, carry):
        ki_t = ki_ref[pl.ds(pl.multiple_of(kt * TQ, TQ), TQ), :].astype(BF16)
        acc = jnp.zeros((TQ, TQ), F32)
        for h in range(HEADS):
            acc = acc + jnp.maximum(_nt(ki_t, qim_ref[h]), 0.0) * wt_ref[h:h + 1, :]
        sc_ref[kt] = jnp.where(key + kt * TQ <= qry, acc, NEG_INF)
        return carry

    lax.fori_loop(0, nt, idx_tile, 0)

    def stats_tile(kt, c):
        mx, mn, mp, cp, cn = c
        for g in _row_groups(sc_ref[kt]):
            pos = g > 0.0
            mx = jnp.maximum(mx, g)
            mn = jnp.minimum(mn, jnp.where(g > 0.5 * NEG_INF, g, -NEG_INF))
            mp = jnp.minimum(mp, jnp.where(pos, g, -NEG_INF))
            cp = cp + jnp.where(pos, 1.0, 0.0)
            cn = cn + jnp.where(g >= 0.0, 1.0, 0.0)
        return mx, mn, mp, cp, cn

    part = lambda v: jnp.full((SUBLANES, TQ), v, F32)
    mx, mn, mp, cp, cn = lax.fori_loop(0, nt, stats_tile,
                                       (part(NEG_INF), part(-NEG_INF), part(-NEG_INF), part(0.0), part(0.0)))
    rmax = jnp.max(mx, axis=0, keepdims=True)
    rmin = jnp.min(mn, axis=0, keepdims=True)
    minpos = jnp.min(mp, axis=0, keepdims=True)
    cpos = jnp.sum(cp, axis=0, keepdims=True)
    cnn = jnp.sum(cn, axis=0, keepdims=True)
    nvalid = (lax.broadcasted_iota(jnp.int32, (1, TQ), 1) + j * TQ + 1).astype(F32)
    small = nvalid <= kf

    def count_ge(thr):
        def body(kt, acc):
            for g in _row_groups(sc_ref[kt]):
                acc = acc + jnp.where(g >= thr, 1.0, 0.0)
            return acc
        acc = lax.fori_loop(0, nt, body, jnp.zeros((SUBLANES, TQ), F32))
        return jnp.sum(acc, axis=0, keepdims=True)

    ztie = jnp.logical_and(cpos < kf, cnn >= kf)
    pos_side = cpos >= kf
    hi_top = rmax + jnp.maximum(jnp.abs(rmax) * 1e-6, 1e-30)
    lo0 = jnp.where(small, 0.5 * NEG_INF, jnp.where(ztie, 0.0, jnp.where(pos_side, minpos, rmin)))
    hi0 = jnp.where(small, -NEG_INF, jnp.where(ztie, minpos, jnp.where(pos_side, hi_top, 0.0)))
    clo0 = jnp.where(jnp.logical_or(small, ztie), jnp.where(small, nvalid, cnn), jnp.where(pos_side, cpos, nvalid))
    chi0 = jnp.where(small, 0.0, jnp.where(ztie, cpos, jnp.where(pos_side, 0.0, cnn)))
    done0 = jnp.where(jnp.logical_or(jnp.logical_or(small, ztie), clo0 == kf), 1.0, 0.0)

    def bis_cond(c):
        return jnp.logical_and(c[0] < BISECT_MAX_IT, c[6] > 0.0)

    def bis_body(c):
        it, lo, hi, clo, chi, done, _ = c
        mid = lo + 0.5 * (hi - lo)
        stalled = jnp.logical_or(mid <= lo, mid >= hi)
        cnt = count_ge(mid)
        act = done < 0.5
        ge = cnt >= kf
        up_lo = jnp.logical_and(act, ge)
        up_hi = jnp.logical_and(act, jnp.logical_not(ge))
        lo = jnp.where(up_lo, mid, lo)
        clo = jnp.where(up_lo, cnt, clo)
        hi = jnp.where(up_hi, mid, hi)
        chi = jnp.where(up_hi, cnt, chi)
        fin = jnp.logical_or(cnt == kf, stalled)
        done = jnp.where(fin, 1.0, done)
        return it + 1, lo, hi, clo, chi, done, jnp.sum(1.0 - done)

    _, lo, hi, clo, chi, _, _ = lax.while_loop(
        bis_cond, bis_body,
        (jnp.int32(0), lo0, hi0, clo0, chi0, done0, jnp.sum(1.0 - done0)))

    tie = jnp.logical_and(clo > kf, jnp.logical_not(small))
    n_tie_iter = int(math.ceil(math.log2(seq))) + 1
    key_f = key.astype(F32)

    def tie_fn():
        need = kf - chi

        def count_tie_le(m):
            def body(kt, acc):
                t = sc_ref[kt]
                s_abs = key_f + (kt * TQ).astype(F32)
                inside = jnp.where(t >= lo, jnp.where(t < hi, jnp.where(s_abs <= m, 1.0, 0.0), 0.0), 0.0)
                for g in _row_groups(inside):
                    acc = acc + g
                return acc
            acc = lax.fori_loop(0, nt, body, jnp.zeros((SUBLANES, TQ), F32))
            return jnp.sum(acc, axis=0, keepdims=True)

        def tb(_, c):
            loi, hii = c
            midi = jnp.floor((loi + hii) * 0.5)
            ge = count_tie_le(midi) >= need
            return jnp.where(ge, loi, midi), jnp.where(ge, midi, hii)

        _, hii = lax.fori_loop(0, n_tie_iter, tb,
                               (jnp.full((1, TQ), -1.0, F32), jnp.full((1, TQ), seq - 1.0, F32)))
        return jnp.where(tie, hii, float(seq))

    mstar = lax.cond(jnp.sum(jnp.where(tie, 1.0, 0.0)) > 0.0, tie_fn,
                     lambda: jnp.full((1, TQ), float(seq), F32))

    def mask_tile(kt, carry):
        t = sc_ref[kt]
        s_abs = key_f + (kt * TQ).astype(F32)
        keep_tie = jnp.where(s_abs <= mstar, 0.0, NEG_INF)
        sc_ref[kt] = jnp.where(t >= lo, jnp.where(t >= hi, 0.0, keep_tie), NEG_INF)
        return carry

    lax.fori_loop(0, nt, mask_tile, 0)

    m_ref[...] = jnp.full(m_ref.shape, NEG_INF, F32)
    l_ref[...] = jnp.zeros(l_ref.shape, F32)
    acc_ref[...] = jnp.zeros(acc_ref.shape, F32)

    def attn_tile(kt, carry, near):
        rows = pl.ds(pl.multiple_of(kt * TQ, TQ), TQ)
        mask_add = sc_ref[kt]
        for h in range(HEADS):
            p = h // 2
            s = _nt(k_ref[rows, p * LANES:(p + 1) * LANES], qm_ref[h]) + mask_add
            if near:
                s = s + bias_ref[kt - j + 1, h]
            s_ref[h] = s
        for h in range(HEADS):
            p = h // 2
            m_old = m_ref[h]
            m_new = jnp.maximum(m_old, jnp.max(s_ref[h], axis=0, keepdims=True))
            alpha = jnp.exp(m_old - m_new)
            pe = jnp.exp(s_ref[h] - m_new)
            l_ref[h] = alpha * l_ref[h] + jnp.sum(pe, axis=0, keepdims=True)
            m_ref[h] = m_new
            acc_ref[h] = alpha * acc_ref[h] + _mm(vt_ref[kt, p * LANES:(p + 1) * LANES, :], pe.astype(BF16))
        return carry

    n_far = jnp.maximum(j - 1, 0)
    lax.fori_loop(0, n_far, functools.partial(attn_tile, near=False), 0)
    lax.fori_loop(n_far, nt, functools.partial(attn_tile, near=True), 0)

    chan = lax.broadcasted_iota(jnp.int32, (LANES, TQ), 0)
    for p in range(HEADS // 2):
        even = acc_ref[2 * p] / l_ref[2 * p]
        odd = acc_ref[2 * p + 1] / l_ref[2 * p + 1]
        o_ref[:, p * LANES:(p + 1) * LANES] = jnp.where(chan < HEAD_DIM, even, odd).T.astype(o_ref.dtype)


def _dsa(z_a, z_r, bias):
    b, s, _ = z_a.shape
    n_keep = min(TOPK_MAX, s // 4)
    kern = functools.partial(_dsa_kernel, n_keep=n_keep, seq=s)
    return pl.pallas_call(
        kern,
        grid=(b, s // TQ),
        in_specs=[pl.BlockSpec((None, TQ, MIX_W), lambda bi, j: (bi, j, 0)),
                  pl.BlockSpec((None, s, MIX_W), lambda bi, j: (bi, 0, 1)),
                  pl.BlockSpec((None, TQ, MIX_W), lambda bi, j: (bi, j, 2)),
                  pl.BlockSpec((None, TQ, MIX_W), lambda bi, j: (bi, j, 3)),
                  pl.BlockSpec((None, s, LANES), lambda bi, j: (bi, 0, ZR_KI)),
                  pl.BlockSpec((None, TQ, LANES), lambda bi, j: (bi, j, ZR_WI)),
                  pl.BlockSpec((2, HEADS, TQ, TQ), lambda bi, j: (0, 0, 0, 0))],
        out_specs=pl.BlockSpec((None, TQ, MIX_W), lambda bi, j: (bi, j, 0)),
        out_shape=jax.ShapeDtypeStruct((b, s, MIX_W), BF16),
        scratch_shapes=[pltpu.VMEM((s // TQ, TQ, TQ), F32),
                        pltpu.VMEM((HEADS, TQ, LANES), BF16),
                        pltpu.VMEM((HEADS, TQ, LANES), BF16),
                        pltpu.VMEM((LANES, TQ), F32),
                        pltpu.VMEM((s // TQ, MIX_W, TQ), BF16),
                        pltpu.VMEM((HEADS, 1, TQ), F32),
                        pltpu.VMEM((HEADS, 1, TQ), F32),
                        pltpu.VMEM((HEADS, LANES, TQ), F32),
                        pltpu.VMEM((HEADS, TQ, TQ), F32)],
        compiler_params=_cparams(("parallel", "arbitrary")),
        name="dsa",
    )(z_a, z_a, z_a, z_a, z_r, z_r, bias)


def _halo_spec(width, blk, tm):
    step = tm // SUBLANES
    return pl.BlockSpec((None, SUBLANES, width), lambda bi, i: (bi, jnp.maximum(i * step - 1, 0), blk))


def _shift_lerp(cur, halo_ref, mu, first):
    prev_last = jnp.where(first, 0.0, halo_ref[SUBLANES - 1:SUBLANES, :])
    rolled = pltpu.roll(cur, 1, axis=0)
    rowid = lax.broadcasted_iota(jnp.int32, cur.shape, 0)
    sh = jnp.where(rowid == 0, prev_last, rolled)
    return cur + (sh - cur) * mu


def _rwkv_kernel(r_ref, k_ref, v_ref, s2_ref, s3_ref, rh_ref, kh_ref, vh_ref, s2h_ref, s3h_ref,
                 mu_r, mu_k, mu_v, mu_s2, mu_s3, w0_ref, wup_ref, a0_ref, aup_ref, gup_ref,
                 kk_ref, ka_ref, rk_ref, bones_ref, tri_ref,
                 y_ref, g_ref, bonus_ref,
                 st_ref, rs, ls, ks, vs, kks, kbs):
    first = pl.program_id(1) == 0

    @pl.when(first)
    def _():
        st_ref[...] = jnp.zeros_like(st_ref)

    r = _shift_lerp(r_ref[...], rh_ref, mu_r[...], first)
    k = _shift_lerp(k_ref[...], kh_ref, mu_k[...], first)
    v = _shift_lerp(v_ref[...], vh_ref, mu_v[...], first)
    s2 = _shift_lerp(s2_ref[...], s2h_ref, mu_s2[...], first)
    s3 = _shift_lerp(s3_ref[...], s3h_ref, mu_s3[...], first)
    xw = w0_ref[...] + _mm(jnp.tanh(s2), wup_ref[...], HIGHEST)
    softplus = jnp.maximum(-xw, 0.0) + jnp.log1p(jnp.exp(-jnp.abs(xw)))
    ld = -jnp.exp(-softplus - 0.5)
    af = jax.nn.sigmoid(a0_ref[...] + _mm(s2, aup_ref[...], HIGHEST))
    g_ref[...] = _mm(jax.nn.sigmoid(s3), gup_ref[...], HIGHEST)
    kkr = k * kk_ref[...]
    ss = _mm(kkr * kkr, bones_ref[...], HIGHEST)
    kkn = kkr / jnp.maximum(jnp.sqrt(ss), 1e-12)
    kmod = k * (1.0 + (af - 1.0) * ka_ref[...])
    bonus_ref[...] = _mm(r * kmod * rk_ref[...], bones_ref[...], HIGHEST) * v
    rs[...] = r
    ls[...] = ld
    ks[...] = kmod
    vs[...] = v
    kks[...] = kkn
    kbs[...] = kkn * af

    ri = lax.broadcasted_iota(jnp.int32, (QUAD, QUAD), 0)
    ci = lax.broadcasted_iota(jnp.int32, (QUAD, QUAD), 1)
    same_head = (ri // HEAD_DIM) == (ci // HEAD_DIM)
    strict = jnp.logical_and(same_head, (ri % CHUNK) > (ci % CHUNK))
    incl = jnp.logical_and(same_head, (ri % CHUNK) >= (ci % CHUNK))
    eye = ri == ci

    def stack(x):
        return jnp.where(same_head, jnp.concatenate([x] * 4, axis=0), 0.0)

    def tile4(x):
        return jnp.concatenate([x] * 4, axis=0)

    def unstack(x):
        return (x[0:CHUNK] + x[CHUNK:2 * CHUNK]) + (x[2 * CHUNK:3 * CHUNK] + x[3 * CHUNK:4 * CHUNK])

    def bmm(a, b):
        return _mm(a.astype(BF16), b.astype(BF16))

    def chunk(c, carry):
        r0 = pl.multiple_of(c * CHUNK, CHUNK)
        rows = pl.ds(r0, CHUNK)
        ldc = ls[rows, :]
        cl = _mm(tri_ref[...], ldc, HIGHEST)
        cl_end = cl[CHUNK - 1:CHUNK, :]
        e_in = jnp.exp(cl)
        e_out = jnp.exp(-cl)
        rt_all = rs[rows, :] * e_in
        at_all = -kks[rows, :] * jnp.exp(cl - ldc)
        bt_all = kbs[rows, :] * e_out
        kt_all = ks[rows, :] * e_out
        e_end = jnp.exp(cl_end - cl)
        bg_all = kbs[rows, :] * e_end
        kg_all = ks[rows, :] * e_end
        v_all = vs[rows, :]
        gam_all = jnp.exp(cl_end)
        for q in range(MIX_W // QUAD):
            sl = slice(q * QUAD, (q + 1) * QUAD)
            rt, at, bt, kt, vv = rt_all[:, sl], at_all[:, sl], bt_all[:, sl], kt_all[:, sl], v_all[:, sl]
            a4, r4, v4 = stack(at), stack(rt), stack(vv)
            bt4, kt4 = tile4(bt).astype(BF16), tile4(kt).astype(BF16)
            a4b, r4b = a4.astype(BF16), r4.astype(BF16)
            m_ab = jnp.where(strict, _nt(a4b, bt4), 0.0)
            m_ak = jnp.where(strict, _nt(a4b, kt4), 0.0)
            m_rb = jnp.where(incl, _nt(r4b, bt4), 0.0)
            m_rk = jnp.where(incl, _nt(r4b, kt4), 0.0)
            pw = m_ab
            inv = jnp.where(eye, 1.0, 0.0) + m_ab
            for _ in range(int(math.log2(CHUNK)) - 1):
                pw = bmm(pw, pw)
                inv = inv + bmm(inv, pw)
            mv4 = bmm(m_ak, v4)
            ah4 = bmm(inv, a4)
            uh4 = bmm(inv, mv4)
            ah, uh = unstack(ah4), unstack(uh4)
            ry = rt + unstack(bmm(m_rb, ah4))
            y0 = unstack(bmm(m_rb, uh4) + bmm(m_rk, v4))
            bg, kg = bg_all[:, sl].astype(BF16), kg_all[:, sl].astype(BF16)
            gam = jnp.broadcast_to(gam_all[:, sl], (QUAD, QUAD))
            g_mat = jnp.where(same_head, _tn(bg, ah.astype(BF16)), 0.0) + jnp.where(eye, gam, 0.0)
            h_mat = jnp.where(same_head, _tn(bg, uh.astype(BF16)) + _tn(kg, vv.astype(BF16)), 0.0)
            st = st_ref[q]
            y_ref[rows, sl] = _mm(ry, st, HIGHEST) + y0
            st_ref[q] = _mm(g_mat, st, HIGHEST) + h_mat
        return carry

    lax.fori_loop(0, r_ref.shape[0] // CHUNK, chunk, 0)


def _rwkv(z_r, prm):
    b, s, _ = z_r.shape
    tm = min(RW_TM, s)
    tok = lambda blk: pl.BlockSpec((None, tm, MIX_W), lambda bi, i: (bi, i, blk))
    tok128 = lambda blk: pl.BlockSpec((None, tm, LANES), lambda bi, i: (bi, i, blk))
    const = lambda a: pl.BlockSpec(a.shape, lambda bi, i: (0,) * a.ndim)
    consts = [prm[n] for n in ("mu_r", "mu_k", "mu_v", "mu_s2", "mu_s3", "w0", "w_up", "a0", "a_up", "g_up",
                               "k_k", "k_a", "r_k", "bones", "tri")]
    out_spec = pl.BlockSpec((None, tm, MIX_W), lambda bi, i: (bi, i, 0))
    out_sds = jax.ShapeDtypeStruct((b, s, MIX_W), F32)
    return pl.pallas_call(
        _rwkv_kernel,
        grid=(b, s // tm),
        in_specs=[tok(ZR_R), tok(ZR_K), tok(ZR_V), tok128(ZR_S2), tok128(ZR_S3),
                  _halo_spec(MIX_W, ZR_R, tm), _halo_spec(MIX_W, ZR_K, tm), _halo_spec(MIX_W, ZR_V, tm),
                  _halo_spec(LANES, ZR_S2, tm), _halo_spec(LANES, ZR_S3, tm)] + [const(a) for a in consts],
        out_specs=[out_spec, out_spec, out_spec],
        out_shape=[out_sds, out_sds, out_sds],
        scratch_shapes=[pltpu.VMEM((MIX_W // QUAD, QUAD, QUAD), F32)] + [pltpu.VMEM((tm, MIX_W), F32)] * 6,
        compiler_params=_cparams(("parallel", "arbitrary")),
        name="rwkv",
    )(z_r, z_r, z_r, z_r, z_r, z_r, z_r, z_r, z_r, z_r, *consts)


def _merge_kernel(x_ref, oa_ref, y_ref, g_ref, bonus_ref, cb_ref, cc_ref, cx_ref, cch_ref, cxh_ref, gate_ref,
                  g1_ref, lnw_ref, lnb_ref, bavg_ref, cw_ref, wb_ref, wo_ref, o_ref):
    first = pl.program_id(1) == 0
    y = y_ref[...]
    mean = _mm(y, bavg_ref[...], HIGHEST)
    yc = y - mean
    var = _mm(yc * yc, bavg_ref[...], HIGHEST)
    o_rwkv = (yc * lax.rsqrt(var + GN_EPS) * lnw_ref[...] + lnb_ref[...] + bonus_ref[...]) * g_ref[...]
    p = cc_ref[...] * cx_ref[...]
    ph = jnp.where(first, 0.0, cch_ref[...] * cxh_ref[...])
    rowid = lax.broadcasted_iota(jnp.int32, p.shape, 0)
    conv = p * cw_ref[CONV_W - 1:CONV_W, :]
    for back in range(1, CONV_W):
        rolled = pltpu.roll(p, back, axis=0)
        for rr in range(back):
            rolled = jnp.where(rowid == rr, ph[SUBLANES - back + rr:SUBLANES - back + rr + 1, :], rolled)
        conv = conv + rolled * cw_ref[CONV_W - 1 - back:CONV_W - back, :]
    o_conv = cb_ref[...] * conv
    d = x_ref.shape[-1]
    merged = jnp.zeros(x_ref.shape, F32)
    for bi, o in enumerate((oa_ref[...], o_rwkv, o_conv)):
        gate = jax.nn.sigmoid(gate_ref[:, bi * d:(bi + 1) * d].astype(F32))
        merged = merged + gate * _mm(o.astype(BF16), wb_ref[bi])
    o_ref[...] = x_ref[...] + g1_ref[...] * _mm(merged.astype(BF16), wo_ref[...])


def _merge(x, o_attn, y, g, bonus, z_r, z_g, mod5, layer, prm, tm=256):
    b, s, d = x.shape
    tm = min(tm, s)
    tok = lambda w, blk: pl.BlockSpec((None, tm, w), lambda bi, i: (bi, i, blk))
    const = lambda a: pl.BlockSpec(a.shape, lambda bi, i: (0,) * a.ndim)
    consts = [prm[n] for n in ("ln_w", "ln_b", "bavg", "conv_w", "w_branch", "w_o")]
    return pl.pallas_call(
        _merge_kernel,
        grid=(b, s // tm),
        in_specs=[tok(d, 0), tok(MIX_W, 0), tok(MIX_W, 0), tok(MIX_W, 0), tok(MIX_W, 0),
                  tok(MIX_W, ZR_CB), tok(MIX_W, ZR_CC), tok(MIX_W, ZR_CX),
                  _halo_spec(MIX_W, ZR_CC, tm), _halo_spec(MIX_W, ZR_CX, tm),
                  tok(3 * d, 0), _full_mod_spec(mod5, layer, 2, 2)] + [const(a) for a in consts],
        out_specs=tok(d, 0),
        out_shape=jax.ShapeDtypeStruct((b, s, d), F32),
        compiler_params=_cparams(("parallel", "parallel")),
        name="merge",
    )(x, o_attn, y, g, bonus, z_r, z_r, z_r, z_r, z_r, z_g, mod5, *consts)


def _ffn_kernel(x_ref, xh_ref, sh_ref, sc_ref, g2_ref, gain_ref, wa_ref, wg_ref, cw_ref, wd_ref, o_ref,
                h_ref, hh_ref, acc_ref):
    jf = pl.program_id(2)
    first = pl.program_id(1) == 0

    @pl.when(jf == 0)
    def _():
        h_ref[...] = _norm_mod(x_ref[...], gain_ref[...], sc_ref[...], sh_ref[...]).astype(h_ref.dtype)
        hh_ref[...] = _norm_mod(xh_ref[...], gain_ref[...], sc_ref[...], sh_ref[...]).astype(hh_ref.dtype)
        acc_ref[...] = jnp.zeros_like(acc_ref)

    a = _mm(h_ref[...], wa_ref[...])
    ah = jnp.where(first, 0.0, _mm(hh_ref[...], wa_ref[...]))
    gl = _mm(h_ref[...], wg_ref[...])
    rowid = lax.broadcasted_iota(jnp.int32, a.shape, 0)
    conv = a * cw_ref[CONV_W - 1:CONV_W, :]
    for back in range(1, CONV_W):
        rolled = pltpu.roll(a, back, axis=0)
        for rr in range(back):
            rolled = jnp.where(rowid == rr, ah[SUBLANES - back + rr:SUBLANES - back + rr + 1, :], rolled)
        conv = conv + rolled * cw_ref[CONV_W - 1 - back:CONV_W - back, :]
    u = conv * jax.nn.sigmoid(conv) * gl
    acc_ref[...] += _mm(u.astype(BF16), wd_ref[...])

    @pl.when(jf == pl.num_programs(2) - 1)
    def _():
        o_ref[...] = x_ref[...] + g2_ref[...] * acc_ref[...]


def _ffn(x, mod5, layer, gain, w_up, conv_w, w_down, tm=512, tf=256):
    b, s, d = x.shape
    d_ff = w_down.shape[0]
    nf = d_ff // tf
    tm = min(tm, s)
    step = tm // SUBLANES
    return pl.pallas_call(
        _ffn_kernel,
        grid=(b, s // tm, nf),
        in_specs=[pl.BlockSpec((None, tm, d), lambda bi, i, j: (bi, i, 0)),
                  pl.BlockSpec((None, SUBLANES, d), lambda bi, i, j: (bi, jnp.maximum(i * step - 1, 0), 0)),
                  _full_mod_spec(mod5, layer, 3, 3), _full_mod_spec(mod5, layer, 4, 3),
                  _full_mod_spec(mod5, layer, 5, 3),
                  pl.BlockSpec((1, d), lambda bi, i, j: (0, 0)),
                  pl.BlockSpec((d, tf), lambda bi, i, j: (0, j)),
                  pl.BlockSpec((d, tf), lambda bi, i, j: (0, j + nf)),
                  pl.BlockSpec((CONV_W, tf), lambda bi, i, j: (0, j)),
                  pl.BlockSpec((tf, d), lambda bi, i, j: (j, 0))],
        out_specs=pl.BlockSpec((None, tm, d), lambda bi, i, j: (bi, i, 0)),
        out_shape=jax.ShapeDtypeStruct((b, s, d), F32),
        scratch_shapes=[pltpu.VMEM((tm, d), BF16), pltpu.VMEM((SUBLANES, d), BF16), pltpu.VMEM((tm, d), F32)],
        compiler_params=_cparams(("parallel", "parallel", "arbitrary")),
        name="ffn",
    )(x, x, mod5, mod5, mod5, gain.reshape(1, d), w_up, w_up, conv_w, w_down)


def _final_norm_kernel(x_ref, g_ref, o_ref):
    x = x_ref[...]
    o_ref[...] = x * lax.rsqrt(jnp.mean(x * x, axis=-1, keepdims=True) + NORM_EPS) * g_ref[...]


def _final_norm(x, gain, tm=1024):
    b, s, d = x.shape
    tm = min(tm, s)
    return pl.pallas_call(
        _final_norm_kernel,
        grid=(b, s // tm),
        in_specs=[pl.BlockSpec((None, tm, d), lambda bi, i: (bi, i, 0)),
                  pl.BlockSpec((1, d), lambda bi, i: (0, 0))],
        out_specs=pl.BlockSpec((None, tm, d), lambda bi, i: (bi, i, 0)),
        out_shape=jax.ShapeDtypeStruct((b, s, d), F32),
        compiler_params=_cparams(("parallel", "parallel")),
        name="final_norm",
    )(x, gain.reshape(1, d))


def _split_w_in(w):
    sizes = (MIX_W, MIX_W, MIX_W, HEADS * HEAD_DIM, HEAD_DIM, HEADS,
             MIX_W, MIX_W, MIX_W, LORA_DECAY, LORA_ICLR, LORA_GATE,
             MIX_W, MIX_W, MIX_W)
    out, o = [], 0
    for n in sizes:
        out.append(w[:, o:o + n])
        o += n
    out.append(w[:, o:])
    return out


def _layer_params(l, w_in, rwkv_mu, rwkv_w0, rwkv_w_up, rwkv_a0, rwkv_a_up, rwkv_g_up, rwkv_k_k, rwkv_k_a,
                  rwkv_r_k, rwkv_ln_w, rwkv_ln_b, sc_conv_w, w_branch, w_o):
    d = w_in.shape[1]
    q, k, v, qi, ki, wi, rr, rk, rv, wd, ad, gd, cb, cc, cx, gates = _split_w_in(w_in[l])
    pad = lambda n: jnp.zeros((d, n), F32)
    w_a = jnp.concatenate([q, k, v, qi], axis=1).astype(BF16)
    w_r = jnp.concatenate([rr, rk, rv, cb, cc, cx, ki, ki, wi, pad(LANES - HEADS), wd, ad, gd],
                          axis=1).astype(BF16)
    assert w_r.shape[1] == ZR_COLS
    mu = rwkv_mu[l]
    row = lambda a: a.reshape(1, -1)
    head_id = jnp.arange(MIX_W) // HEAD_DIM
    bones = (head_id[:, None] == head_id[None, :]).astype(F32)
    tri = (jnp.arange(CHUNK)[:, None] >= jnp.arange(CHUNK)[None, :]).astype(F32)
    zl = lambda n: jnp.zeros((n, MIX_W), F32)
    return dict(
        w_a=w_a, w_g=gates.astype(BF16), w_r=w_r,
        mu_r=row(mu[:MIX_W]), mu_k=row(mu[MIX_W:2 * MIX_W]), mu_v=row(mu[2 * MIX_W:3 * MIX_W]),
        mu_s2=row(mu[3 * MIX_W:3 * MIX_W + LORA_DECAY + LORA_ICLR]), mu_s3=row(mu[3 * MIX_W + LORA_DECAY + LORA_ICLR:]),
        w0=row(rwkv_w0[l]), w_up=jnp.concatenate([rwkv_w_up[l], zl(LORA_ICLR)], axis=0),
        a0=row(rwkv_a0[l]), a_up=jnp.concatenate([zl(LORA_DECAY), rwkv_a_up[l]], axis=0),
        g_up=rwkv_g_up[l], k_k=row(rwkv_k_k[l]), k_a=row(rwkv_k_a[l]), r_k=row(rwkv_r_k[l]),
        bones=bones, tri=tri, bavg=bones / HEAD_DIM,
        ln_w=row(rwkv_ln_w[l]), ln_b=row(rwkv_ln_b[l]), conv_w=sc_conv_w[l].T,
        w_branch=w_branch[l].astype(BF16), w_o=w_o[l].astype(BF16),
    )


def kernel(x, c, positions, rel_bias, final_norm, ada_w, ada_b, norm_mix, w_in, rwkv_mu, rwkv_w0, rwkv_w_up,
           rwkv_a0, rwkv_a_up, rwkv_g_up, rwkv_k_k, rwkv_k_a, rwkv_r_k, rwkv_ln_w, rwkv_ln_b, sc_conv_w,
           w_branch, w_o, norm_ffn, ffn_w_up, ffn_conv_w, ffn_w_down):
    depth = ada_w.shape[0]
    mod5 = _ada_mod(c, ada_w, ada_b)
    bias = _bias_tiles(rel_bias)
    for l in range(depth):
        prm = _layer_params(l, w_in, rwkv_mu, rwkv_w0, rwkv_w_up, rwkv_a0, rwkv_a_up, rwkv_g_up, rwkv_k_k,
                            rwkv_k_a, rwkv_r_k, rwkv_ln_w, rwkv_ln_b, sc_conv_w, w_branch, w_o)
        z_a = _inproj(x, mod5, l, norm_mix[l], prm["w_a"], BF16)
        z_g = _inproj(x, mod5, l, norm_mix[l], prm["w_g"], BF16)
        z_r = _inproj(x, mod5, l, norm_mix[l], prm["w_r"], F32)
        o_attn = _dsa(z_a, z_r, bias)
        y, g, bonus = _rwkv(z_r, prm)
        x = _merge(x, o_attn, y, g, bonus, z_r, z_g, mod5, l, prm)
        x = _ffn(x, mod5, l, norm_ffn[l], ffn_w_up[l].astype(BF16), ffn_conv_w[l].T, ffn_w_down[l].astype(BF16))
    return _final_norm(x, final_norm)
```

```python
import functools
import math

import jax
import jax.numpy as jnp
from jax import lax
from jax.experimental import pallas as pl
from jax.experimental.pallas import tpu as pltpu

F32 = jnp.float32
BF16 = jnp.bfloat16
HIGHEST = lax.Precision.HIGHEST

MIX_W = 512
HEADS = 8
HEAD_DIM = 64
TOPK_MAX = 256
N_BUCKETS = 32
MAX_DISTANCE = 128
LORA_DECAY = 64
LORA_ICLR = 64
LORA_GATE = 128
CONV_W = 3
NORM_EPS = 1e-6
GN_EPS = 64e-5
NEG_INF = -1e30

LANES = 128
SUBLANES = 8
BF16_ROWS = 16
VMEM_LIMIT = 56 * 1024 * 1024

TQ = 256
CHUNK = 64
RW_TM = 256
QUAD = 4 * HEAD_DIM
assert CHUNK == HEAD_DIM

Z_Q, Z_K, Z_V, Z_QI, Z_RR, Z_RK, Z_RV, Z_CB, Z_CC, Z_CX = range(10)
Z_GATE_COL = 10 * MIX_W


def _z_small_block(d_model, i):
    return (Z_GATE_COL + 3 * d_model) // LANES + i


def _cparams(sem):
    return pltpu.CompilerParams(dimension_semantics=sem, vmem_limit_bytes=VMEM_LIMIT)


def _nt(a, b, precision=None):
    return lax.dot_general(a, b, (((1,), (1,)), ((), ())), precision=precision,
                           preferred_element_type=F32)


def _tn(a, b, precision=None):
    return lax.dot_general(a, b, (((0,), (0,)), ((), ())), precision=precision,
                           preferred_element_type=F32)


def _mm(a, b, precision=None):
    return jnp.dot(a, b, precision=precision, preferred_element_type=F32)


def _bmm(a, b):
    return _mm(a.astype(BF16), b.astype(BF16))


def _mod_kernel(c_ref, w_ref, b_ref, o_ref):
    o_ref[...] = _mm(c_ref[...], w_ref[...], HIGHEST) + b_ref[...]


def _ada_mod(c, ada_w, ada_b):
    depth, d, d6 = ada_w.shape
    b = c.shape[0]
    out = pl.pallas_call(
        _mod_kernel,
        grid=(depth, d6 // d),
        in_specs=[pl.BlockSpec((b, d), lambda l, j: (0, 0)),
                  pl.BlockSpec((None, d, d), lambda l, j: (l, 0, j)),
                  pl.BlockSpec((None, 1, d), lambda l, j: (l, 0, j))],
        out_specs=pl.BlockSpec((None, b, d), lambda l, j: (l, 0, j)),
        out_shape=jax.ShapeDtypeStruct((depth, b, d6), F32),
        compiler_params=_cparams(("parallel", "parallel")),
        name="ada_mod",
    )(c, ada_w, ada_b.reshape(depth, 1, d6))
    return out.reshape(depth, b, d6 // d, 1, d)


def _mod_spec(mod5, layer, which, ngrid):
    d = mod5.shape[-1]
    if ngrid == 2:
        return pl.BlockSpec((None, None, None, 1, d), lambda b, i: (layer, b, which, 0, 0))
    return pl.BlockSpec((None, None, None, 1, d), lambda b, i, j: (layer, b, which, 0, 0))


def _norm_mod(x, gain, scale, shift):
    y = x * lax.rsqrt(jnp.mean(x * x, axis=-1, keepdims=True) + NORM_EPS) * gain
    return y * (1.0 + scale) + shift


def _inproj_kernel(x_ref, sh_ref, sc_ref, g_ref, w_ref, o_ref, h_ref):
    @pl.when(pl.program_id(2) == 0)
    def _():
        h_ref[...] = _norm_mod(x_ref[...], g_ref[...], sc_ref[...], sh_ref[...]).astype(h_ref.dtype)

    o_ref[...] = _mm(h_ref[...], w_ref[...]).astype(o_ref.dtype)


def _inproj(x, mod5, layer, gain, w, tm=1024, tn=512):
    b, s, d = x.shape
    zc = w.shape[1]
    tm = min(tm, s)
    return pl.pallas_call(
        _inproj_kernel,
        grid=(b, s // tm, zc // tn),
        in_specs=[pl.BlockSpec((None, tm, d), lambda bi, i, j: (bi, i, 0)),
                  _mod_spec(mod5, layer, 0, 3),
                  _mod_spec(mod5, layer, 1, 3),
                  pl.BlockSpec((1, d), lambda bi, i, j: (0, 0)),
                  pl.BlockSpec((d, tn), lambda bi, i, j: (0, j))],
        out_specs=pl.BlockSpec((None, tm, tn), lambda bi, i, j: (bi, i, j)),
        out_shape=jax.ShapeDtypeStruct((b, s, zc), BF16),
        scratch_shapes=[pltpu.VMEM((tm, d), BF16)],
        compiler_params=_cparams(("parallel", "parallel", "arbitrary")),
        name="inproj",
    )(x, mod5, mod5, gain.reshape(1, d), w)


def _bias_kernel(rb_ref, o_ref):
    which = pl.program_id(0)
    h = pl.program_id(1)
    ri = lax.broadcasted_iota(jnp.int32, (TQ, TQ), 0)
    ci = lax.broadcasted_iota(jnp.int32, (TQ, TQ), 1)
    dist = ci - ri + (1 - which) * TQ
    n = jnp.maximum(dist, 0)
    max_exact = N_BUCKETS // 2
    nf = jnp.maximum(n, 1).astype(F32)
    large = max_exact + (jnp.log(nf / max_exact) / math.log(MAX_DISTANCE / max_exact)
                         * (N_BUCKETS - max_exact)).astype(jnp.int32)
    large = jnp.minimum(large, N_BUCKETS - 1)
    bucket = jnp.where(n < max_exact, n, large)
    far = rb_ref[N_BUCKETS - 1, h]
    acc = jnp.zeros((TQ, TQ), F32)
    for bkt in range(N_BUCKETS - 1):
        acc = jnp.where(bucket == bkt, rb_ref[bkt, h] - far, acc)
    o_ref[...] = acc


def _bias_tiles(rel_bias):
    assert TQ >= MAX_DISTANCE
    return pl.pallas_call(
        _bias_kernel,
        grid=(2, HEADS),
        in_specs=[pl.BlockSpec(memory_space=pltpu.SMEM)],
        out_specs=pl.BlockSpec((None, None, TQ, TQ), lambda w, h: (w, h, 0, 0)),
        out_shape=jax.ShapeDtypeStruct((2, HEADS, TQ, TQ), F32),
        compiler_params=_cparams(("parallel", "parallel")),
        name="bias_tiles",
    )(rel_bias)


BISECT_MAX_IT = 300


def _row_groups(t):
    return [t[r * SUBLANES:(r + 1) * SUBLANES, :] for r in range(t.shape[0] // SUBLANES)]


def _dsa_kernel(q_ref, k_ref, v_ref, qi_ref, ki_ref, wi_ref, bias_ref, o_ref,
                sc_ref, qm_ref, qim_ref, wt_ref, vt_ref, m_ref, l_ref, acc_ref, s_ref, *, n_keep, seq):
    j = pl.program_id(1)
    nt = j + 1
    kf = float(n_keep)
    lane = lax.broadcasted_iota(jnp.int32, (TQ, LANES), 1)
    att_scale = HEAD_DIM ** -0.5
    w_scale = (HEADS ** -0.5) * (HEAD_DIM ** -0.5)

    for h in range(HEADS):
        p, odd = divmod(h, 2)
        hm = (lane >= HEAD_DIM) if odd else (lane < HEAD_DIM)
        qs = q_ref[:, p * LANES:(p + 1) * LANES]
        qm_ref[h] = jnp.where(hm, qs, jnp.zeros_like(qs)) * att_scale
        qis = qi_ref[:, p * LANES:(p + 1) * LANES]
        qim_ref[h] = jnp.where(hm, qis, jnp.zeros_like(qis))
    wt_ref[...] = wi_ref[...].astype(F32).T * w_scale
    vt_ref[j] = v_ref[...].T

    key = lax.broadcasted_iota(jnp.int32, (TQ, TQ), 0)
    qry = lax.broadcasted_iota(jnp.int32, (TQ, TQ), 1) + j * TQ

    def idx_tile(kt, carry):
        ki_t = ki_ref[pl.ds(pl.multiple_of(kt * TQ, TQ), TQ), :]
        acc = jnp.zeros((TQ, TQ), F32)
        for h in range(HEADS):
            acc = acc + jnp.maximum(_nt(ki_t, qim_ref[h]), 0.0) * wt_ref[h:h + 1, :]
        sc_ref[kt] = jnp.where(key + kt * TQ <= qry, acc, NEG_INF)
        return carry

    lax.fori_loop(0, nt, idx_tile, 0)

    def stats_tile(kt, c):
        mx, mn, mp, cp, cn = c
        for g in _row_groups(sc_ref[kt]):
            pos = g > 0.0
            mx = jnp.maximum(mx, g)
            mn = jnp.minimum(mn, jnp.where(g > 0.5 * NEG_INF, g, -NEG_INF))
            mp = jnp.minimum(mp, jnp.where(pos, g, -NEG_INF))
            cp = cp + jnp.where(pos, 1.0, 0.0)
            cn = cn + jnp.where(g >= 0.0, 1.0, 0.0)
        return mx, mn, mp, cp, cn

    part = lambda v: jnp.full((SUBLANES, TQ), v, F32)
    mx, mn, mp, cp, cn = lax.fori_loop(0, nt, stats_tile,
                                       (part(NEG_INF), part(-NEG_INF), part(-NEG_INF), part(0.0), part(0.0)))
    rmax = jnp.max(mx, axis=0, keepdims=True)
    rmin = jnp.min(mn, axis=0, keepdims=True)
    minpos = jnp.min(mp, axis=0, keepdims=True)
    cpos = jnp.sum(cp, axis=0, keepdims=True)
    cnn = jnp.sum(cn, axis=0, keepdims=True)
    nvalid = (lax.broadcasted_iota(jnp.int32, (1, TQ), 1) + j * TQ + 1).astype(F32)
    small = nvalid <= kf

    def count_ge(thr):
        def body(kt, acc):
            for g in _row_groups(sc_ref[kt]):
                acc = acc + jnp.where(g >= thr, 1.0, 0.0)
            return acc
        acc = lax.fori_loop(0, nt, body, jnp.zeros((SUBLANES, TQ), F32))
        return jnp.sum(acc, axis=0, keepdims=True)

    ztie = jnp.logical_and(cpos < kf, cnn >= kf)
    pos_side = cpos >= kf
    hi_top = rmax + jnp.maximum(jnp.abs(rmax) * 1e-6, 1e-30)
    lo0 = jnp.where(small, 0.5 * NEG_INF, jnp.where(ztie, 0.0, jnp.where(pos_side, minpos, rmin)))
    hi0 = jnp.where(small, -NEG_INF, jnp.where(ztie, minpos, jnp.where(pos_side, hi_top, 0.0)))
    clo0 = jnp.where(jnp.logical_or(small, ztie), jnp.where(small, nvalid, cnn), jnp.where(pos_side, cpos, nvalid))
    chi0 = jnp.where(small, 0.0, jnp.where(ztie, cpos, jnp.where(pos_side, 0.0, cnn)))
    done0 = jnp.where(jnp.logical_or(jnp.logical_or(small, ztie), clo0 == kf), 1.0, 0.0)

    def bis_cond(c):
        return jnp.logical_and(c[0] < BISECT_MAX_IT, c[6] > 0.0)

    def bis_body(c):
        it, lo, hi, clo, chi, done, _ = c
        mid = lo + 0.5 * (hi - lo)
        stalled = jnp.logical_or(mid <= lo, mid >= hi)
        cnt = count_ge(mid)
        act = done < 0.5
        ge = cnt >= kf
        up_lo = jnp.logical_and(act, ge)
        up_hi = jnp.logical_and(act, jnp.logical_not(ge))
        lo = jnp.where(up_lo, mid, lo)
        clo = jnp.where(up_lo, cnt, clo)
        hi = jnp.where(up_hi, mid, hi)
        chi = jnp.where(up_hi, cnt, chi)
        fin = jnp.logical_or(cnt == kf, stalled)
        done = jnp.where(fin, 1.0, done)
        return it + 1, lo, hi, clo, chi, done, jnp.sum(1.0 - done)

    _, lo, hi, clo, chi, _, _ = lax.while_loop(
        bis_cond, bis_body,
        (jnp.int32(0), lo0, hi0, clo0, chi0, done0, jnp.sum(1.0 - done0)))

    tie = jnp.logical_and(clo > kf, jnp.logical_not(small))
    n_tie_iter = int(math.ceil(math.log2(seq))) + 1
    key_f = key.astype(F32)

    def tie_fn():
        need = kf - chi

        def count_tie_le(m):
            def body(kt, acc):
                t = sc_ref[kt]
                s_abs = key_f + (kt * TQ).astype(F32)
                inside = jnp.where(t >= lo, jnp.where(t < hi, jnp.where(s_abs <= m, 1.0, 0.0), 0.0), 0.0)
                for g in _row_groups(inside):
                    acc = acc + g
                return acc
            acc = lax.fori_loop(0, nt, body, jnp.zeros((SUBLANES, TQ), F32))
            return jnp.sum(acc, axis=0, keepdims=True)

        def tb(_, c):
            loi, hii = c
            midi = jnp.floor((loi + hii) * 0.5)
            ge = count_tie_le(midi) >= need
            return jnp.where(ge, loi, midi), jnp.where(ge, midi, hii)

        _, hii = lax.fori_loop(0, n_tie_iter, tb,
                               (jnp.full((1, TQ), -1.0, F32), jnp.full((1, TQ), seq - 1.0, F32)))
        return jnp.where(tie, hii, float(seq))

    mstar = lax.cond(jnp.sum(jnp.where(tie, 1.0, 0.0)) > 0.0, tie_fn,
                     lambda: jnp.full((1, TQ), float(seq), F32))

    def mask_tile(kt, carry):
        t = sc_ref[kt]
        s_abs = key_f + (kt * TQ).astype(F32)
        keep_tie = jnp.where(s_abs <= mstar, 0.0, NEG_INF)
        sc_ref[kt] = jnp.where(t >= lo, jnp.where(t >= hi, 0.0, keep_tie), NEG_INF)
        return carry

    lax.fori_loop(0, nt, mask_tile, 0)

    m_ref[...] = jnp.full(m_ref.shape, NEG_INF, F32)
    l_ref[...] = jnp.zeros(l_ref.shape, F32)
    acc_ref[...] = jnp.zeros(acc_ref.shape, F32)

    def attn_tile(kt, carry, near):
        rows = pl.ds(pl.multiple_of(kt * TQ, TQ), TQ)
        mask_add = sc_ref[kt]
        for h in range(HEADS):
            p = h // 2
            s = _nt(k_ref[rows, p * LANES:(p + 1) * LANES], qm_ref[h]) + mask_add
            if near:
                s = s + bias_ref[kt - j + 1, h]
            s_ref[h] = s
        for h in range(HEADS):
            p = h // 2
            m_old = m_ref[h]
            m_new = jnp.maximum(m_old, jnp.max(s_ref[h], axis=0, keepdims=True))
            alpha = jnp.exp(m_old - m_new)
            pe = jnp.exp(s_ref[h] - m_new)
            l_ref[h] = alpha * l_ref[h] + jnp.sum(pe, axis=0, keepdims=True)
            m_ref[h] = m_new
            acc_ref[h] = alpha * acc_ref[h] + _mm(vt_ref[kt, p * LANES:(p + 1) * LANES, :], pe.astype(BF16))
        return carry

    n_far = jnp.maximum(j - 1, 0)
    lax.fori_loop(0, n_far, functools.partial(attn_tile, near=False), 0)
    lax.fori_loop(n_far, nt, functools.partial(attn_tile, near=True), 0)

    chan = lax.broadcasted_iota(jnp.int32, (LANES, TQ), 0)
    for p in range(HEADS // 2):
        even = acc_ref[2 * p] / l_ref[2 * p]
        odd = acc_ref[2 * p + 1] / l_ref[2 * p + 1]
        o_ref[:, p * LANES:(p + 1) * LANES] = jnp.where(chan < HEAD_DIM, even, odd).T.astype(o_ref.dtype)


def _dsa(z, bias, d_model):
    b, s, _ = z.shape
    n_keep = min(TOPK_MAX, s // 4)
    kern = functools.partial(_dsa_kernel, n_keep=n_keep, seq=s)
    ki_blk, wi_blk = _z_small_block(d_model, 0), _z_small_block(d_model, 1)
    return pl.pallas_call(
        kern,
        grid=(b, s // TQ),
        in_specs=[pl.BlockSpec((None, TQ, MIX_W), lambda bi, j: (bi, j, Z_Q)),
                  pl.BlockSpec((None, s, MIX_W), lambda bi, j: (bi, 0, Z_K)),
                  pl.BlockSpec((None, TQ, MIX_W), lambda bi, j: (bi, j, Z_V)),
                  pl.BlockSpec((None, TQ, MIX_W), lambda bi, j: (bi, j, Z_QI)),
                  pl.BlockSpec((None, s, LANES), lambda bi, j: (bi, 0, ki_blk)),
                  pl.BlockSpec((None, TQ, LANES), lambda bi, j: (bi, j, wi_blk)),
                  pl.BlockSpec((2, HEADS, TQ, TQ), lambda bi, j: (0, 0, 0, 0))],
        out_specs=pl.BlockSpec((None, TQ, MIX_W), lambda bi, j: (bi, j, 0)),
        out_shape=jax.ShapeDtypeStruct((b, s, MIX_W), BF16),
        scratch_shapes=[pltpu.VMEM((s // TQ, TQ, TQ), F32),
                        pltpu.VMEM((HEADS, TQ, LANES), BF16),
                        pltpu.VMEM((HEADS, TQ, LANES), BF16),
                        pltpu.VMEM((LANES, TQ), F32),
                        pltpu.VMEM((s // TQ, MIX_W, TQ), BF16),
                        pltpu.VMEM((HEADS, 1, TQ), F32),
                        pltpu.VMEM((HEADS, 1, TQ), F32),
                        pltpu.VMEM((HEADS, LANES, TQ), F32),
                        pltpu.VMEM((HEADS, TQ, TQ), F32)],
        compiler_params=_cparams(("parallel", "arbitrary")),
        name="dsa",
    )(z, z, z, z, z, z, bias)


def _halo_spec(width, blk, tm, rows):
    step = tm // rows
    return pl.BlockSpec((None, rows, width), lambda bi, i: (bi, jnp.maximum(i * step - 1, 0), blk))


def _shift_lerp(cur_ref, halo_ref, mu, first):
    cur = cur_ref[...].astype(F32)
    nh = halo_ref.shape[0]
    prev_last = jnp.where(first, 0.0, halo_ref[nh - 1:nh, :].astype(F32))
    rolled = pltpu.roll(cur, 1, axis=0)
    rowid = lax.broadcasted_iota(jnp.int32, cur.shape, 0)
    sh = jnp.where(rowid == 0, prev_last, rolled)
    return cur + (sh - cur) * mu


def _head_sum(x, bq_ref):
    xb = x.astype(BF16)
    return jnp.concatenate([_mm(xb[:, q * QUAD:(q + 1) * QUAD], bq_ref[...]) for q in range(MIX_W // QUAD)], axis=1)


def _rwkv_kernel(r_ref, k_ref, v_ref, s2_ref, s3_ref, rh_ref, kh_ref, vh_ref, s2h_ref, s3h_ref,
                 mu_r, mu_k, mu_v, mu_s2, mu_s3, w0_ref, wup_ref, a0_ref, aup_ref, gup_ref,
                 kk_ref, ka_ref, rk_ref, bq_ref, tri_ref,
                 y_ref, g_ref, bonus_ref,
                 st_ref, rs, ls, ks, vs, kks, kbs):
    first = pl.program_id(1) == 0

    @pl.when(first)
    def _():
        st_ref[...] = jnp.zeros_like(st_ref)

    r = _shift_lerp(r_ref, rh_ref, mu_r[...], first)
    k = _shift_lerp(k_ref, kh_ref, mu_k[...], first)
    v = _shift_lerp(v_ref, vh_ref, mu_v[...], first)
    s2 = _shift_lerp(s2_ref, s2h_ref, mu_s2[...], first)
    s3 = _shift_lerp(s3_ref, s3h_ref, mu_s3[...], first)
    xw = w0_ref[...] + _bmm(jnp.tanh(s2), wup_ref[...])
    softplus = jnp.maximum(-xw, 0.0) + jnp.log1p(jnp.exp(-jnp.abs(xw)))
    ld = -jnp.exp(-softplus - 0.5)
    af = jax.nn.sigmoid(a0_ref[...] + _bmm(s2, aup_ref[...]))
    g_ref[...] = _bmm(jax.nn.sigmoid(s3), gup_ref[...]).astype(g_ref.dtype)
    kkr = k * kk_ref[...]
    kkn = kkr / jnp.maximum(jnp.sqrt(_head_sum(kkr * kkr, bq_ref)), 1e-12)
    kmod = k * (1.0 + (af - 1.0) * ka_ref[...])
    bonus_ref[...] = (_head_sum(r * kmod * rk_ref[...], bq_ref) * v).astype(bonus_ref.dtype)
    rs[...] = r
    ls[...] = ld
    ks[...] = kmod
    vs[...] = v
    kks[...] = kkn
    kbs[...] = kkn * af

    ri = lax.broadcasted_iota(jnp.int32, (QUAD, QUAD), 0)
    ci = lax.broadcasted_iota(jnp.int32, (QUAD, QUAD), 1)
    same_head = (ri // HEAD_DIM) == (ci // HEAD_DIM)
    strict = jnp.logical_and(same_head, (ri % CHUNK) > (ci % CHUNK))
    incl = jnp.logical_and(same_head, (ri % CHUNK) >= (ci % CHUNK))
    eye = jnp.where(ri == ci, 1.0, 0.0)

    def stack(x):
        return jnp.where(same_head, jnp.concatenate([x] * 4, axis=0), 0.0).astype(BF16)

    def tile4(x):
        return jnp.concatenate([x] * 4, axis=0).astype(BF16)

    def unstack(x):
        return (x[0:CHUNK] + x[CHUNK:2 * CHUNK]) + (x[2 * CHUNK:3 * CHUNK] + x[3 * CHUNK:4 * CHUNK])

    def chunk_operands(c):
        rows = pl.ds(pl.multiple_of(c * CHUNK, CHUNK), CHUNK)
        ldc = ls[rows, :]
        p1 = ldc.astype(BF16)
        e1 = ldc - p1.astype(F32)
        p2 = e1.astype(BF16)
        p3 = (e1 - p2.astype(F32)).astype(BF16)
        tri = tri_ref[...]
        cl = (_mm(tri, p1) + _mm(tri, p2)) + _mm(tri, p3)
        cl_end = cl[CHUNK - 1:CHUNK, :]
        e_in = jnp.exp(cl)
        e_out = jnp.exp(-cl)
        e_end = jnp.exp(cl_end - cl)
        rt_all = rs[rows, :] * e_in
        at_all = -kks[rows, :] * jnp.exp(cl - ldc)
        bt_all = kbs[rows, :] * e_out
        kt_all = ks[rows, :] * e_out
        bg_all = (kbs[rows, :] * e_end).astype(BF16)
        kg_all = (ks[rows, :] * e_end).astype(BF16)
        v_all = vs[rows, :]
        gam_all = jnp.exp(cl_end)
        units = []
        for q in range(MIX_W // QUAD):
            sl = slice(q * QUAD, (q + 1) * QUAD)
            units.append(dict(rows=rows, sl=sl, q=q, rt=rt_all[:, sl], vv=v_all[:, sl].astype(BF16),
                              a4=stack(at_all[:, sl]), r4=stack(rt_all[:, sl]), v4=stack(v_all[:, sl]),
                              bt4=tile4(bt_all[:, sl]), kt4=tile4(kt_all[:, sl]),
                              bg=bg_all[:, sl], kg=kg_all[:, sl], gam=gam_all[:, sl]))
        return units

    n_doublings = int(math.log2(CHUNK)) - 1

    def chunk_pair(cp, carry):
        us = chunk_operands(2 * cp) + chunk_operands(2 * cp + 1)
        for u in us:
            u["m_ab"] = jnp.where(strict, _nt(u["a4"], u["bt4"]), 0.0)
        for u in us:
            u["m_ak"] = jnp.where(strict, _nt(u["a4"], u["kt4"]), 0.0)
        for u in us:
            u["m_rb"] = jnp.where(incl, _nt(u["r4"], u["bt4"]), 0.0).astype(BF16)
        for u in us:
            u["m_rk"] = jnp.where(incl, _nt(u["r4"], u["kt4"]), 0.0).astype(BF16)
        for u in us:
            u["pw"] = u["m_ab"]
            u["inv"] = eye + u["m_ab"]
        for _ in range(n_doublings):
            for u in us:
                u["pw"] = _bmm(u["pw"], u["pw"])
            for u in us:
                u["inv"] = u["inv"] + _bmm(u["inv"], u["pw"])
        for u in us:
            u["inv"] = u["inv"].astype(BF16)
            u["mv4"] = _bmm(u["m_ak"], u["v4"])
        for u in us:
            u["ah4"] = _mm(u["inv"], u["a4"])
        for u in us:
            u["uh4"] = _bmm(u["inv"], u["mv4"])
        for u in us:
            u["ry"] = (u["rt"] + unstack(_bmm(u["m_rb"], u["ah4"]))).astype(BF16)
        for u in us:
            u["y0"] = unstack(_bmm(u["m_rb"], u["uh4"]) + _mm(u["m_rk"], u["v4"]))
        for u in us:
            ah, uh = unstack(u["ah4"]).astype(BF16), unstack(u["uh4"]).astype(BF16)
            u["g_low"] = jnp.where(same_head, _tn(u["bg"], ah), 0.0).astype(BF16)
            u["h_t"] = jnp.where(same_head, _tn(uh, u["bg"]) + _tn(u["vv"], u["kg"]), 0.0)
        for u in us:
            st = st_ref[u["q"]]
            stb = st.astype(BF16)
            y_ref[u["rows"], u["sl"]] = _nt(u["ry"], stb) + u["y0"]
            st_ref[u["q"]] = st * u["gam"] + _nt(stb, u["g_low"]) + u["h_t"]
        return carry

    lax.fori_loop(0, r_ref.shape[0] // (2 * CHUNK), chunk_pair, 0)


def _rwkv(z, prm, d_model):
    b, s, _ = z.shape
    tm = min(RW_TM, s)
    s2_blk, s3_blk = _z_small_block(d_model, 2), _z_small_block(d_model, 3)
    tok = lambda blk: pl.BlockSpec((None, tm, MIX_W), lambda bi, i: (bi, i, blk))
    tok128 = lambda blk: pl.BlockSpec((None, tm, LANES), lambda bi, i: (bi, i, blk))
    const = lambda a: pl.BlockSpec(a.shape, lambda bi, i: (0,) * a.ndim)
    consts = [prm[n] for n in ("mu_r", "mu_k", "mu_v", "mu_s2", "mu_s3", "w0", "w_up", "a0", "a_up", "g_up",
                               "k_k", "k_a", "r_k", "bones_q", "tri")]
    out_spec = pl.BlockSpec((None, tm, MIX_W), lambda bi, i: (bi, i, 0))
    return pl.pallas_call(
        _rwkv_kernel,
        grid=(b, s // tm),
        in_specs=[tok(Z_RR), tok(Z_RK), tok(Z_RV), tok128(s2_blk), tok128(s3_blk),
                  _halo_spec(MIX_W, Z_RR, tm, BF16_ROWS), _halo_spec(MIX_W, Z_RK, tm, BF16_ROWS),
                  _halo_spec(MIX_W, Z_RV, tm, BF16_ROWS), _halo_spec(LANES, s2_blk, tm, BF16_ROWS),
                  _halo_spec(LANES, s3_blk, tm, BF16_ROWS)] + [const(a) for a in consts],
        out_specs=[out_spec, out_spec, out_spec],
        out_shape=[jax.ShapeDtypeStruct((b, s, MIX_W), F32), jax.ShapeDtypeStruct((b, s, MIX_W), BF16),
                   jax.ShapeDtypeStruct((b, s, MIX_W), BF16)],
        scratch_shapes=[pltpu.VMEM((MIX_W // QUAD, QUAD, QUAD), F32)] + [pltpu.VMEM((tm, MIX_W), F32)] * 6,
        compiler_params=_cparams(("parallel", "arbitrary")),
        name="rwkv",
    )(z, z, z, z, z, z, z, z, z, z, *consts)


def _causal_conv(p, halo, cw_ref, rowid):
    nh = halo.shape[0]
    conv = p * cw_ref[CONV_W - 1:CONV_W, :]
    for back in range(1, CONV_W):
        rolled = pltpu.roll(p, back, axis=0)
        for rr in range(back):
            rolled = jnp.where(rowid == rr, halo[nh - back + rr:nh - back + rr + 1, :], rolled)
        conv = conv + rolled * cw_ref[CONV_W - 1 - back:CONV_W - back, :]
    return conv


def _merge_kernel(x_ref, oa_ref, y_ref, g_ref, bonus_ref, cb_ref, cc_ref, cx_ref, cch_ref, cxh_ref,
                  gate0_ref, gate1_ref, gate2_ref, g1_ref, lnw_ref, lnb_ref, bq_ref, cw_ref, wb_ref, wo_ref, o_ref):
    first = pl.program_id(1) == 0
    y = y_ref[...]
    inv_n = 1.0 / HEAD_DIM
    p1 = y.astype(BF16)
    mean = (_head_sum(p1, bq_ref) + _head_sum(y - p1.astype(F32), bq_ref)) * inv_n
    yc = y - mean
    var = _head_sum(yc * yc, bq_ref) * inv_n
    o_rwkv = ((yc * lax.rsqrt(var + GN_EPS) * lnw_ref[...] + lnb_ref[...] + bonus_ref[...].astype(F32))
              * g_ref[...].astype(F32))
    p = cc_ref[...].astype(F32) * cx_ref[...].astype(F32)
    ph = jnp.where(first, 0.0, cch_ref[...].astype(F32) * cxh_ref[...].astype(F32))
    rowid = lax.broadcasted_iota(jnp.int32, p.shape, 0)
    o_conv = cb_ref[...].astype(F32) * _causal_conv(p, ph, cw_ref, rowid)
    merged = jnp.zeros(x_ref.shape, F32)
    for bi, (o, gate_ref) in enumerate(((oa_ref[...], gate0_ref), (o_rwkv, gate1_ref), (o_conv, gate2_ref))):
        merged = merged + jax.nn.sigmoid(gate_ref[...].astype(F32)) * _mm(o.astype(BF16), wb_ref[bi])
    o_ref[...] = x_ref[...] + g1_ref[...] * _mm(merged.astype(BF16), wo_ref[...])


def _merge(x, o_attn, y, g, bonus, z, mod5, layer, prm, tm=256):
    b, s, d = x.shape
    tm = min(tm, s)
    tok = lambda w, blk: pl.BlockSpec((None, tm, w), lambda bi, i: (bi, i, blk))
    const = lambda a: pl.BlockSpec(a.shape, lambda bi, i: (0,) * a.ndim)
    consts = [prm[n] for n in ("ln_w", "ln_b", "bones_q", "conv_w", "w_branch", "w_o")]
    gate_blk = Z_GATE_COL // d
    return pl.pallas_call(
        _merge_kernel,
        grid=(b, s // tm),
        in_specs=[tok(d, 0), tok(MIX_W, 0), tok(MIX_W, 0), tok(MIX_W, 0), tok(MIX_W, 0),
                  tok(MIX_W, Z_CB), tok(MIX_W, Z_CC), tok(MIX_W, Z_CX),
                  _halo_spec(MIX_W, Z_CC, tm, BF16_ROWS), _halo_spec(MIX_W, Z_CX, tm, BF16_ROWS),
                  tok(d, gate_blk), tok(d, gate_blk + 1), tok(d, gate_blk + 2),
                  _mod_spec(mod5, layer, 2, 2)] + [const(a) for a in consts],
        out_specs=tok(d, 0),
        out_shape=jax.ShapeDtypeStruct((b, s, d), F32),
        compiler_params=_cparams(("parallel", "parallel")),
        name="merge",
    )(x, o_attn, y, g, bonus, z, z, z, z, z, z, z, z, mod5, *consts)


def _ffn_kernel(x_ref, xh_ref, sh_ref, sc_ref, g2_ref, gain_ref, wa_ref, wg_ref, cw_ref, wd_ref, o_ref,
                h_ref, hh_ref, acc_ref):
    jf = pl.program_id(2)
    first = pl.program_id(1) == 0

    @pl.when(jf == 0)
    def _():
        h_ref[...] = _norm_mod(x_ref[...], gain_ref[...], sc_ref[...], sh_ref[...]).astype(h_ref.dtype)
        hh_ref[...] = _norm_mod(xh_ref[...], gain_ref[...], sc_ref[...], sh_ref[...]).astype(hh_ref.dtype)
        acc_ref[...] = jnp.zeros_like(acc_ref)

    a = _mm(h_ref[...], wa_ref[...])
    ah = jnp.where(first, 0.0, _mm(hh_ref[...], wa_ref[...]))
    gl = _mm(h_ref[...], wg_ref[...])
    rowid = lax.broadcasted_iota(jnp.int32, a.shape, 0)
    conv = _causal_conv(a, ah, cw_ref, rowid)
    u = conv * jax.nn.sigmoid(conv) * gl
    acc_ref[...] += _mm(u.astype(BF16), wd_ref[...])

    @pl.when(jf == pl.num_programs(2) - 1)
    def _():
        o_ref[...] = x_ref[...] + g2_ref[...] * acc_ref[...]


def _ffn(x, mod5, layer, gain, w_up, conv_w, w_down, tm=1024, tf=256):
    b, s, d = x.shape
    d_ff = w_down.shape[0]
    nf = d_ff // tf
    tm = min(tm, s)
    step = tm // SUBLANES
    return pl.pallas_call(
        _ffn_kernel,
        grid=(b, s // tm, nf),
        in_specs=[pl.BlockSpec((None, tm, d), lambda bi, i, j: (bi, i, 0)),
                  pl.BlockSpec((None, SUBLANES, d), lambda bi, i, j: (bi, jnp.maximum(i * step - 1, 0), 0)),
                  _mod_spec(mod5, layer, 3, 3), _mod_spec(mod5, layer, 4, 3), _mod_spec(mod5, layer, 5, 3),
                  pl.BlockSpec((1, d), lambda bi, i, j: (0, 0)),
                  pl.BlockSpec((d, tf), lambda bi, i, j: (0, j)),
                  pl.BlockSpec((d, tf), lambda bi, i, j: (0, j + nf)),
                  pl.BlockSpec((CONV_W, tf), lambda bi, i, j: (0, j)),
                  pl.BlockSpec((tf, d), lambda bi, i, j: (j, 0))],
        out_specs=pl.BlockSpec((None, tm, d), lambda bi, i, j: (bi, i, 0)),
        out_shape=jax.ShapeDtypeStruct((b, s, d), F32),
        scratch_shapes=[pltpu.VMEM((tm, d), BF16), pltpu.VMEM((SUBLANES, d), BF16), pltpu.VMEM((tm, d), F32)],
        compiler_params=_cparams(("parallel", "parallel", "arbitrary")),
        name="ffn",
    )(x, x, mod5, mod5, mod5, gain.reshape(1, d), w_up, w_up, conv_w, w_down)


def _final_norm_kernel(x_ref, g_ref, o_ref):
    x = x_ref[...]
    o_ref[...] = x * lax.rsqrt(jnp.mean(x * x, axis=-1, keepdims=True) + NORM_EPS) * g_ref[...]


def _final_norm(x, gain, tm=1024):
    b, s, d = x.shape
    tm = min(tm, s)
    return pl.pallas_call(
        _final_norm_kernel,
        grid=(b, s // tm),
        in_specs=[pl.BlockSpec((None, tm, d), lambda bi, i: (bi, i, 0)),
                  pl.BlockSpec((1, d), lambda bi, i: (0, 0))],
        out_specs=pl.BlockSpec((None, tm, d), lambda bi, i: (bi, i, 0)),
        out_shape=jax.ShapeDtypeStruct((b, s, d), F32),
        compiler_params=_cparams(("parallel", "parallel")),
        name="final_norm",
    )(x, gain.reshape(1, d))


def _split_w_in(w):
    sizes = (MIX_W, MIX_W, MIX_W, HEADS * HEAD_DIM, HEAD_DIM, HEADS,
             MIX_W, MIX_W, MIX_W, LORA_DECAY, LORA_ICLR, LORA_GATE,
             MIX_W, MIX_W, MIX_W)
    out, o = [], 0
    for n in sizes:
        out.append(w[:, o:o + n])
        o += n
    out.append(w[:, o:])
    return out


def _layer_params(l, w_in, rwkv_mu, rwkv_w0, rwkv_w_up, rwkv_a0, rwkv_a_up, rwkv_g_up, rwkv_k_k, rwkv_k_a,
                  rwkv_r_k, rwkv_ln_w, rwkv_ln_b, sc_conv_w, w_branch, w_o):
    d = w_in.shape[1]
    q, k, v, qi, ki, wi, rr, rk, rv, wd, ad, gd, cb, cc, cx, gates = _split_w_in(w_in[l])
    assert Z_GATE_COL % d == 0 and gates.shape[1] == 3 * d
    w_z = jnp.concatenate([q, k, v, qi, rr, rk, rv, cb, cc, cx, gates,
                           ki, ki, wi, jnp.zeros((d, LANES - HEADS), F32), wd, ad, gd], axis=1).astype(BF16)
    mu = rwkv_mu[l]
    row = lambda a: a.reshape(1, -1)
    head_id = jnp.arange(QUAD) // HEAD_DIM
    bones_q = (head_id[:, None] == head_id[None, :]).astype(BF16)
    tri = (jnp.arange(CHUNK)[:, None] >= jnp.arange(CHUNK)[None, :]).astype(BF16)
    zl = lambda n: jnp.zeros((n, MIX_W), F32)
    return dict(
        w_z=w_z,
        mu_r=row(mu[:MIX_W]), mu_k=row(mu[MIX_W:2 * MIX_W]), mu_v=row(mu[2 * MIX_W:3 * MIX_W]),
        mu_s2=row(mu[3 * MIX_W:3 * MIX_W + LORA_DECAY + LORA_ICLR]), mu_s3=row(mu[3 * MIX_W + LORA_DECAY + LORA_ICLR:]),
        w0=row(rwkv_w0[l]), w_up=jnp.concatenate([rwkv_w_up[l], zl(LORA_ICLR)], axis=0).astype(BF16),
        a0=row(rwkv_a0[l]), a_up=jnp.concatenate([zl(LORA_DECAY), rwkv_a_up[l]], axis=0).astype(BF16),
        g_up=rwkv_g_up[l].astype(BF16), k_k=row(rwkv_k_k[l]), k_a=row(rwkv_k_a[l]), r_k=row(rwkv_r_k[l]),
        bones_q=bones_q, tri=tri,
        ln_w=row(rwkv_ln_w[l]), ln_b=row(rwkv_ln_b[l]), conv_w=sc_conv_w[l].T,
        w_branch=w_branch[l].astype(BF16), w_o=w_o[l].astype(BF16),
    )


def kernel(x, c, positions, rel_bias, final_norm, ada_w, ada_b, norm_mix, w_in, rwkv_mu, rwkv_w0, rwkv_w_up,
           rwkv_a0, rwkv_a_up, rwkv_g_up, rwkv_k_k, rwkv_k_a, rwkv_r_k, rwkv_ln_w, rwkv_ln_b, sc_conv_w,
           w_branch, w_o, norm_ffn, ffn_w_up, ffn_conv_w, ffn_w_down):
    depth, d = ada_w.shape[0], x.shape[-1]
    mod5 = _ada_mod(c, ada_w, ada_b)
    bias = _bias_tiles(rel_bias)
    for l in range(depth):
        prm = _layer_params(l, w_in, rwkv_mu, rwkv_w0, rwkv_w_up, rwkv_a0, rwkv_a_up, rwkv_g_up, rwkv_k_k,
                            rwkv_k_a, rwkv_r_k, rwkv_ln_w, rwkv_ln_b, sc_conv_w, w_branch, w_o)
        z = _inproj(x, mod5, l, norm_mix[l], prm["w_z"])
        o_attn = _dsa(z, bias, d)
        y, g, bonus = _rwkv(z, prm, d)
        x = _merge(x, o_attn, y, g, bonus, z, mod5, l, prm)
        x = _ffn(x, mod5, l, norm_ffn[l], ffn_w_up[l].astype(BF16), ffn_conv_w[l].T, ffn_w_down[l].astype(BF16))
    return _final_norm(x, final_norm)
```

```python
import functools
import math

import jax
import jax.numpy as jnp
from jax import lax
from jax.experimental import pallas as pl
from jax.experimental.pallas import tpu as pltpu

F32 = jnp.float32
BF16 = jnp.bfloat16
HIGHEST = lax.Precision.HIGHEST

MIX_W = 512
HEADS = 8
HEAD_DIM = 64
TOPK_MAX = 256
N_BUCKETS = 32
MAX_DISTANCE = 128
LORA_DECAY = 64
LORA_ICLR = 64
LORA_GATE = 128
CONV_W = 3
NORM_EPS = 1e-6
GN_EPS = 64e-5
NEG_INF = -1e30

LANES = 128
SUBLANES = 8
BF16_ROWS = 16
VMEM_LIMIT = 56 * 1024 * 1024

TQ = 256
CHUNK = 64
RW_TM = 256
QUAD = 4 * HEAD_DIM
assert CHUNK == HEAD_DIM

Z_Q, Z_K, Z_V, Z_QI, Z_RR, Z_RK, Z_RV, Z_CB, Z_CC, Z_CX = range(10)
Z_GATE_COL = 10 * MIX_W


def _z_small_block(d_model, i):
    return (Z_GATE_COL + 3 * d_model) // LANES + i


def _cparams(sem):
    return pltpu.CompilerParams(dimension_semantics=sem, vmem_limit_bytes=VMEM_LIMIT)


def _nt(a, b, precision=None):
    return lax.dot_general(a, b, (((1,), (1,)), ((), ())), precision=precision,
                           preferred_element_type=F32)


def _tn(a, b, precision=None):
    return lax.dot_general(a, b, (((0,), (0,)), ((), ())), precision=precision,
                           preferred_element_type=F32)


def _mm(a, b, precision=None):
    return jnp.dot(a, b, precision=precision, preferred_element_type=F32)


def _bmm(a, b):
    return _mm(a.astype(BF16), b.astype(BF16))


def _mod_kernel(c_ref, w_ref, b_ref, o_ref):
    o_ref[...] = _mm(c_ref[...], w_ref[...], HIGHEST) + b_ref[...]


def _ada_mod(c, ada_w, ada_b):
    depth, d, d6 = ada_w.shape
    b = c.shape[0]
    out = pl.pallas_call(
        _mod_kernel,
        grid=(depth, d6 // d),
        in_specs=[pl.BlockSpec((b, d), lambda l, j: (0, 0)),
                  pl.BlockSpec((None, d, d), lambda l, j: (l, 0, j)),
                  pl.BlockSpec((None, 1, d), lambda l, j: (l, 0, j))],
        out_specs=pl.BlockSpec((None, b, d), lambda l, j: (l, 0, j)),
        out_shape=jax.ShapeDtypeStruct((depth, b, d6), F32),
        compiler_params=_cparams(("parallel", "parallel")),
        name="ada_mod",
    )(c, ada_w, ada_b.reshape(depth, 1, d6))
    return out.reshape(depth, b, d6 // d, 1, d)


def _mod_spec(mod5, layer, which, ngrid):
    d = mod5.shape[-1]
    if ngrid == 2:
        return pl.BlockSpec((None, None, None, 1, d), lambda b, i: (layer, b, which, 0, 0))
    return pl.BlockSpec((None, None, None, 1, d), lambda b, i, j: (layer, b, which, 0, 0))


def _norm_mod(x, gain, scale, shift):
    y = x * lax.rsqrt(jnp.mean(x * x, axis=-1, keepdims=True) + NORM_EPS) * gain
    return y * (1.0 + scale) + shift


def _inproj_kernel(x_ref, sh_ref, sc_ref, g_ref, w_ref, o_ref, h_ref):
    @pl.when(pl.program_id(2) == 0)
    def _():
        h_ref[...] = _norm_mod(x_ref[...], g_ref[...], sc_ref[...], sh_ref[...]).astype(h_ref.dtype)

    o_ref[...] = _mm(h_ref[...], w_ref[...]).astype(o_ref.dtype)


def _inproj(x, mod5, layer, gain, w, tm=1024, tn=512):
    b, s, d = x.shape
    zc = w.shape[1]
    tm = min(tm, s)
    return pl.pallas_call(
        _inproj_kernel,
        grid=(b, s // tm, zc // tn),
        in_specs=[pl.BlockSpec((None, tm, d), lambda bi, i, j: (bi, i, 0)),
                  _mod_spec(mod5, layer, 0, 3),
                  _mod_spec(mod5, layer, 1, 3),
                  pl.BlockSpec((1, d), lambda bi, i, j: (0, 0)),
                  pl.BlockSpec((d, tn), lambda bi, i, j: (0, j))],
        out_specs=pl.BlockSpec((None, tm, tn), lambda bi, i, j: (bi, i, j)),
        out_shape=jax.ShapeDtypeStruct((b, s, zc), BF16),
        scratch_shapes=[pltpu.VMEM((tm, d), BF16)],
        compiler_params=_cparams(("parallel", "parallel", "arbitrary")),
        name="inproj",
    )(x, mod5, mod5, gain.reshape(1, d), w)


def _bias_kernel(rb_ref, o_ref):
    which = pl.program_id(0)
    h = pl.program_id(1)
    ri = lax.broadcasted_iota(jnp.int32, (TQ, TQ), 0)
    ci = lax.broadcasted_iota(jnp.int32, (TQ, TQ), 1)
    dist = ci - ri + (1 - which) * TQ
    n = jnp.maximum(dist, 0)
    max_exact = N_BUCKETS // 2
    nf = jnp.maximum(n, 1).astype(F32)
    large = max_exact + (jnp.log(nf / max_exact) / math.log(MAX_DISTANCE / max_exact)
                         * (N_BUCKETS - max_exact)).astype(jnp.int32)
    large = jnp.minimum(large, N_BUCKETS - 1)
    bucket = jnp.where(n < max_exact, n, large)
    far = rb_ref[N_BUCKETS - 1, h]
    acc = jnp.zeros((TQ, TQ), F32)
    for bkt in range(N_BUCKETS - 1):
        acc = jnp.where(bucket == bkt, rb_ref[bkt, h] - far, acc)
    o_ref[...] = acc * LOG2E


def _bias_tiles(rel_bias):
    assert TQ >= MAX_DISTANCE
    return pl.pallas_call(
        _bias_kernel,
        grid=(2, HEADS),
        in_specs=[pl.BlockSpec(memory_space=pltpu.SMEM)],
        out_specs=pl.BlockSpec((None, None, TQ, TQ), lambda w, h: (w, h, 0, 0)),
        out_shape=jax.ShapeDtypeStruct((2, HEADS, TQ, TQ), F32),
        compiler_params=_cparams(("parallel", "parallel")),
        name="bias_tiles",
    )(rel_bias)


BISECT_MAX_IT = 300
LOG2E = math.log2(math.e)
ONES_ROWS = BF16_ROWS


def _row_groups(t):
    return [t[r * SUBLANES:(r + 1) * SUBLANES, :] for r in range(t.shape[0] // SUBLANES)]


def _dsa_kernel(q_ref, k_ref, v_ref, qi_ref, ki_ref, wi_ref, bias_ref, o_ref,
                sc_ref, qm_ref, qim_ref, wt_ref, vt_ref, m_ref, l_ref, acc_ref, s_ref, *, n_keep, seq):
    j = pl.program_id(1)
    nt = j + 1
    kf = float(n_keep)
    lane = lax.broadcasted_iota(jnp.int32, (TQ, LANES), 1)
    att_scale = HEAD_DIM ** -0.5 * LOG2E
    w_scale = (HEADS ** -0.5) * (HEAD_DIM ** -0.5)

    for h in range(HEADS):
        p, odd = divmod(h, 2)
        hm = (lane >= HEAD_DIM) if odd else (lane < HEAD_DIM)
        qs = q_ref[:, p * LANES:(p + 1) * LANES]
        qm_ref[h] = jnp.where(hm, qs, jnp.zeros_like(qs)) * att_scale
        qis = qi_ref[:, p * LANES:(p + 1) * LANES]
        qim_ref[h] = jnp.where(hm, qis, jnp.zeros_like(qis))
    wt_ref[...] = wi_ref[...].astype(F32).T * w_scale
    v_t = v_ref[...].T
    for h in range(HEADS):
        vt_ref[j, h, 0:HEAD_DIM, :] = v_t[h * HEAD_DIM:(h + 1) * HEAD_DIM, :]
        vt_ref[j, h, HEAD_DIM:HEAD_DIM + ONES_ROWS, :] = jnp.ones((ONES_ROWS, TQ), BF16)

    key = lax.broadcasted_iota(jnp.int32, (TQ, TQ), 0)
    qry = lax.broadcasted_iota(jnp.int32, (TQ, TQ), 1) + j * TQ

    def idx_tile(kt, carry):
        ki_t = ki_ref[pl.ds(pl.multiple_of(kt * TQ, TQ), TQ), :]
        acc = jnp.zeros((TQ, TQ), F32)
        for h in range(HEADS):
            acc = acc + jnp.maximum(_nt(ki_t, qim_ref[h]), 0.0) * wt_ref[h:h + 1, :]
        sc_ref[kt] = jnp.where(key + kt * TQ <= qry, acc, NEG_INF)
        return carry

    lax.fori_loop(0, nt, idx_tile, 0)

    def stats_tile(kt, c):
        mx, mn, mp, cp, cn = c
        for g in _row_groups(sc_ref[kt]):
            pos = g > 0.0
            mx = jnp.maximum(mx, g)
            mn = jnp.minimum(mn, jnp.where(g > 0.5 * NEG_INF, g, -NEG_INF))
            mp = jnp.minimum(mp, jnp.where(pos, g, -NEG_INF))
            cp = cp + jnp.where(pos, 1.0, 0.0)
            cn = cn + jnp.where(g >= 0.0, 1.0, 0.0)
        return mx, mn, mp, cp, cn

    part = lambda v: jnp.full((SUBLANES, TQ), v, F32)
    mx, mn, mp, cp, cn = lax.fori_loop(0, nt, stats_tile,
                                       (part(NEG_INF), part(-NEG_INF), part(-NEG_INF), part(0.0), part(0.0)))
    rmax = jnp.max(mx, axis=0, keepdims=True)
    rmin = jnp.min(mn, axis=0, keepdims=True)
    minpos = jnp.min(mp, axis=0, keepdims=True)
    cpos = jnp.sum(cp, axis=0, keepdims=True)
    cnn = jnp.sum(cn, axis=0, keepdims=True)
    nvalid = (lax.broadcasted_iota(jnp.int32, (1, TQ), 1) + j * TQ + 1).astype(F32)
    small = nvalid <= kf

    def count_ge(thr):
        def body(kt, acc):
            for g in _row_groups(sc_ref[kt]):
                acc = acc + jnp.where(g >= thr, 1.0, 0.0)
            return acc
        acc = lax.fori_loop(0, nt, body, jnp.zeros((SUBLANES, TQ), F32))
        return jnp.sum(acc, axis=0, keepdims=True)

    ztie = jnp.logical_and(cpos < kf, cnn >= kf)
    pos_side = cpos >= kf
    hi_top = rmax + jnp.maximum(jnp.abs(rmax) * 1e-6, 1e-30)
    lo0 = jnp.where(small, 0.5 * NEG_INF, jnp.where(ztie, 0.0, jnp.where(pos_side, minpos, rmin)))
    hi0 = jnp.where(small, -NEG_INF, jnp.where(ztie, minpos, jnp.where(pos_side, hi_top, 0.0)))
    clo0 = jnp.where(jnp.logical_or(small, ztie), jnp.where(small, nvalid, cnn), jnp.where(pos_side, cpos, nvalid))
    chi0 = jnp.where(small, 0.0, jnp.where(ztie, cpos, jnp.where(pos_side, 0.0, cnn)))
    done0 = jnp.where(jnp.logical_or(jnp.logical_or(small, ztie), clo0 == kf), 1.0, 0.0)

    def bis_cond(c):
        return jnp.logical_and(c[0] < BISECT_MAX_IT, c[6] > 0.0)

    def bis_body(c):
        it, lo, hi, clo, chi, done, _ = c
        mid = lo + 0.5 * (hi - lo)
        stalled = jnp.logical_or(mid <= lo, mid >= hi)
        cnt = count_ge(mid)
        act = done < 0.5
        ge = cnt >= kf
        up_lo = jnp.logical_and(act, ge)
        up_hi = jnp.logical_and(act, jnp.logical_not(ge))
        lo = jnp.where(up_lo, mid, lo)
        clo = jnp.where(up_lo, cnt, clo)
        hi = jnp.where(up_hi, mid, hi)
        chi = jnp.where(up_hi, cnt, chi)
        fin = jnp.logical_or(cnt == kf, stalled)
        done = jnp.where(fin, 1.0, done)
        return it + 1, lo, hi, clo, chi, done, jnp.sum(1.0 - done)

    _, lo, hi, clo, chi, _, _ = lax.while_loop(
        bis_cond, bis_body,
        (jnp.int32(0), lo0, hi0, clo0, chi0, done0, jnp.sum(1.0 - done0)))

    tie = jnp.logical_and(clo > kf, jnp.logical_not(small))
    n_tie_iter = int(math.ceil(math.log2(seq))) + 1
    key_f = key.astype(F32)

    def tie_fn():
        need = kf - chi

        def count_tie_le(m):
            def body(kt, acc):
                t = sc_ref[kt]
                s_abs = key_f + (kt * TQ).astype(F32)
                inside = jnp.where(t >= lo, jnp.where(t < hi, jnp.where(s_abs <= m, 1.0, 0.0), 0.0), 0.0)
                for g in _row_groups(inside):
                    acc = acc + g
                return acc
            acc = lax.fori_loop(0, nt, body, jnp.zeros((SUBLANES, TQ), F32))
            return jnp.sum(acc, axis=0, keepdims=True)

        def tb(_, c):
            loi, hii = c
            midi = jnp.floor((loi + hii) * 0.5)
            ge = count_tie_le(midi) >= need
            return jnp.where(ge, loi, midi), jnp.where(ge, midi, hii)

        _, hii = lax.fori_loop(0, n_tie_iter, tb,
                               (jnp.full((1, TQ), -1.0, F32), jnp.full((1, TQ), seq - 1.0, F32)))
        return jnp.where(tie, hii, float(seq))

    mstar = lax.cond(jnp.sum(jnp.where(tie, 1.0, 0.0)) > 0.0, tie_fn,
                     lambda: jnp.full((1, TQ), float(seq), F32))

    def mask_tile(kt, carry):
        t = sc_ref[kt]
        s_abs = key_f + (kt * TQ).astype(F32)
        keep_tie = jnp.where(s_abs <= mstar, 0.0, NEG_INF)
        sc_ref[kt] = jnp.where(t >= lo, jnp.where(t >= hi, 0.0, keep_tie), NEG_INF)
        return carry

    lax.fori_loop(0, nt, mask_tile, 0)

    m_ref[...] = jnp.full(m_ref.shape, NEG_INF, F32)
    l_ref[...] = jnp.zeros(l_ref.shape, F32)
    acc_ref[...] = jnp.zeros(acc_ref.shape, F32)

    def attn_tiles(kts, near):
        tile_max = [None] * HEADS
        for i, kt in enumerate(kts):
            rows = pl.ds(pl.multiple_of(kt * TQ, TQ), TQ)
            mask_add = sc_ref[kt]
            for h in range(HEADS):
                p = h // 2
                s = _nt(k_ref[rows, p * LANES:(p + 1) * LANES], qm_ref[h]) + mask_add
                if near:
                    s = s + bias_ref[kt - j + 1, h]
                s_ref[i, h] = s
                mx = jnp.max(s, axis=0, keepdims=True)
                tile_max[h] = mx if i == 0 else jnp.maximum(tile_max[h], mx)
        for h in range(HEADS):
            m_old = m_ref[h]
            m_new = jnp.maximum(m_old, tile_max[h])
            alpha = jnp.exp2(m_old - m_new)
            pv = None
            for i, kt in enumerate(kts):
                part = _mm(vt_ref[kt, h], jnp.exp2(s_ref[i, h] - m_new).astype(BF16))
                pv = part if i == 0 else pv + part
            m_ref[h] = m_new
            l_ref[h] = alpha * l_ref[h] + pv[HEAD_DIM:HEAD_DIM + 1, :]
            acc_ref[h] = alpha * acc_ref[h] + pv[0:HEAD_DIM, :]

    n_far = jnp.maximum(j - 1, 0)
    n_far_pairs = lax.shift_right_logical(n_far, 1)

    def far_pair(i, carry):
        attn_tiles([2 * i, 2 * i + 1], near=False)
        return carry

    def far_single(kt, carry):
        attn_tiles([kt], near=False)
        return carry

    lax.fori_loop(0, n_far_pairs, far_pair, 0)
    lax.fori_loop(2 * n_far_pairs, n_far, far_single, 0)

    @pl.when(j >= 1)
    def _():
        attn_tiles([j - 1, j], near=True)

    @pl.when(j == 0)
    def _():
        attn_tiles([j], near=True)

    for p in range(HEADS // 2):
        pair = jnp.concatenate([acc_ref[2 * p] / l_ref[2 * p], acc_ref[2 * p + 1] / l_ref[2 * p + 1]], axis=0)
        o_ref[:, p * LANES:(p + 1) * LANES] = pair.T.astype(o_ref.dtype)


def _dsa(z, bias, d_model):
    b, s, _ = z.shape
    n_keep = min(TOPK_MAX, s // 4)
    kern = functools.partial(_dsa_kernel, n_keep=n_keep, seq=s)
    ki_blk, wi_blk = _z_small_block(d_model, 0), _z_small_block(d_model, 1)
    return pl.pallas_call(
        kern,
        grid=(b, s // TQ),
        in_specs=[pl.BlockSpec((None, TQ, MIX_W), lambda bi, j: (bi, j, Z_Q)),
                  pl.BlockSpec((None, s, MIX_W), lambda bi, j: (bi, 0, Z_K)),
                  pl.BlockSpec((None, TQ, MIX_W), lambda bi, j: (bi, j, Z_V)),
                  pl.BlockSpec((None, TQ, MIX_W), lambda bi, j: (bi, j, Z_QI)),
                  pl.BlockSpec((None, s, LANES), lambda bi, j: (bi, 0, ki_blk)),
                  pl.BlockSpec((None, TQ, LANES), lambda bi, j: (bi, j, wi_blk)),
                  pl.BlockSpec((2, HEADS, TQ, TQ), lambda bi, j: (0, 0, 0, 0))],
        out_specs=pl.BlockSpec((None, TQ, MIX_W), lambda bi, j: (bi, j, 0)),
        out_shape=jax.ShapeDtypeStruct((b, s, MIX_W), BF16),
        scratch_shapes=[pltpu.VMEM((s // TQ, TQ, TQ), F32),
                        pltpu.VMEM((HEADS, TQ, LANES), BF16),
                        pltpu.VMEM((HEADS, TQ, LANES), BF16),
                        pltpu.VMEM((LANES, TQ), F32),
                        pltpu.VMEM((s // TQ, HEADS, HEAD_DIM + ONES_ROWS, TQ), BF16),
                        pltpu.VMEM((HEADS, 1, TQ), F32),
                        pltpu.VMEM((HEADS, 1, TQ), F32),
                        pltpu.VMEM((HEADS, HEAD_DIM, TQ), F32),
                        pltpu.VMEM((2, HEADS, TQ, TQ), F32)],
        compiler_params=_cparams(("parallel", "arbitrary")),
        name="dsa",
    )(z, z, z, z, z, z, bias)


def _halo_spec(width, blk, tm, rows):
    step = tm // rows
    return pl.BlockSpec((None, rows, width), lambda bi, i: (bi, jnp.maximum(i * step - 1, 0), blk))


def _shift_lerp(cur_ref, halo_ref, mu, first):
    cur = cur_ref[...].astype(F32)
    nh = halo_ref.shape[0]
    prev_last = jnp.where(first, 0.0, halo_ref[nh - 1:nh, :].astype(F32))
    rolled = pltpu.roll(cur, 1, axis=0)
    rowid = lax.broadcasted_iota(jnp.int32, cur.shape, 0)
    sh = jnp.where(rowid == 0, prev_last, rolled)
    return cur + (sh - cur) * mu


def _head_sum(x, bq_ref):
    xb = x.astype(BF16)
    return jnp.concatenate([_mm(xb[:, q * QUAD:(q + 1) * QUAD], bq_ref[...]) for q in range(MIX_W // QUAD)], axis=1)


def _rwkv_kernel(r_ref, k_ref, v_ref, s2_ref, s3_ref, rh_ref, kh_ref, vh_ref, s2h_ref, s3h_ref,
                 mu_r, mu_k, mu_v, mu_s2, mu_s3, w0_ref, wup_ref, a0_ref, aup_ref, gup_ref,
                 kk_ref, ka_ref, rk_ref, bq_ref, tri_ref,
                 y_ref, g_ref, bonus_ref,
                 st_ref, rs, ls, ks, vs, kks, kbs):
    first = pl.program_id(1) == 0

    @pl.when(first)
    def _():
        st_ref[...] = jnp.zeros_like(st_ref)

    r = _shift_lerp(r_ref, rh_ref, mu_r[...], first)
    k = _shift_lerp(k_ref, kh_ref, mu_k[...], first)
    v = _shift_lerp(v_ref, vh_ref, mu_v[...], first)
    s2 = _shift_lerp(s2_ref, s2h_ref, mu_s2[...], first)
    s3 = _shift_lerp(s3_ref, s3h_ref, mu_s3[...], first)
    xw = w0_ref[...] + _bmm(jnp.tanh(s2), wup_ref[...])
    softplus = jnp.maximum(-xw, 0.0) + jnp.log1p(jnp.exp(-jnp.abs(xw)))
    ld = -jnp.exp(-softplus - 0.5)
    af = jax.nn.sigmoid(a0_ref[...] + _bmm(s2, aup_ref[...]))
    g_ref[...] = _bmm(jax.nn.sigmoid(s3), gup_ref[...]).astype(g_ref.dtype)
    kkr = k * kk_ref[...]
    kkn = kkr / jnp.maximum(jnp.sqrt(_head_sum(kkr * kkr, bq_ref)), 1e-12)
    kmod = k * (1.0 + (af - 1.0) * ka_ref[...])
    bonus_ref[...] = (_head_sum(r * kmod * rk_ref[...], bq_ref) * v).astype(bonus_ref.dtype)
    rs[...] = r
    ls[...] = ld
    ks[...] = kmod
    vs[...] = v
    kks[...] = kkn
    kbs[...] = kkn * af

    ri = lax.broadcasted_iota(jnp.int32, (QUAD, QUAD), 0)
    ci = lax.broadcasted_iota(jnp.int32, (QUAD, QUAD), 1)
    same_head = (ri // HEAD_DIM) == (ci // HEAD_DIM)
    strict = jnp.logical_and(same_head, (ri % CHUNK) > (ci % CHUNK))
    incl = jnp.logical_and(same_head, (ri % CHUNK) >= (ci % CHUNK))
    eye = jnp.where(ri == ci, 1.0, 0.0)

    def stack(x):
        return jnp.where(same_head, jnp.concatenate([x] * 4, axis=0), 0.0).astype(BF16)

    def tile4(x):
        return jnp.concatenate([x] * 4, axis=0).astype(BF16)

    def unstack(x):
        return (x[0:CHUNK] + x[CHUNK:2 * CHUNK]) + (x[2 * CHUNK:3 * CHUNK] + x[3 * CHUNK:4 * CHUNK])

    def chunk_operands(c):
        rows = pl.ds(pl.multiple_of(c * CHUNK, CHUNK), CHUNK)
        ldc = ls[rows, :]
        p1 = ldc.astype(BF16)
        e1 = ldc - p1.astype(F32)
        p2 = e1.astype(BF16)
        p3 = (e1 - p2.astype(F32)).astype(BF16)
        tri = tri_ref[...]
        cl = (_mm(tri, p1) + _mm(tri, p2)) + _mm(tri, p3)
        cl_end = cl[CHUNK - 1:CHUNK, :]
        e_in = jnp.exp(cl)
        e_out = jnp.exp(-cl)
        e_end = jnp.exp(cl_end - cl)
        rt_all = rs[rows, :] * e_in
        at_all = -kks[rows, :] * jnp.exp(cl - ldc)
        bt_all = kbs[rows, :] * e_out
        kt_all = ks[rows, :] * e_out
        bg_all = (kbs[rows, :] * e_end).astype(BF16)
        kg_all = (ks[rows, :] * e_end).astype(BF16)
        v_all = vs[rows, :]
        gam_all = jnp.exp(cl_end)
        units = []
        for q in range(MIX_W // QUAD):
            sl = slice(q * QUAD, (q + 1) * QUAD)
            units.append(dict(rows=rows, sl=sl, q=q, rt=rt_all[:, sl], vv=v_all[:, sl].astype(BF16),
                              a4=stack(at_all[:, sl]), r4=stack(rt_all[:, sl]), v4=stack(v_all[:, sl]),
                              bt4=tile4(bt_all[:, sl]), kt4=tile4(kt_all[:, sl]),
                              bg=bg_all[:, sl], kg=kg_all[:, sl], gam=gam_all[:, sl]))
        return units

    n_doublings = int(math.log2(CHUNK)) - 1

    def chunk_pair(cp, carry):
        us = chunk_operands(2 * cp) + chunk_operands(2 * cp + 1)
        for u in us:
            u["m_ab"] = jnp.where(strict, _nt(u["a4"], u["bt4"]), 0.0)
        for u in us:
            u["m_ak"] = jnp.where(strict, _nt(u["a4"], u["kt4"]), 0.0)
        for u in us:
            u["m_rb"] = jnp.where(incl, _nt(u["r4"], u["bt4"]), 0.0).astype(BF16)
        for u in us:
            u["m_rk"] = jnp.where(incl, _nt(u["r4"], u["kt4"]), 0.0).astype(BF16)
        for u in us:
            u["pw"] = u["m_ab"]
            u["inv"] = eye + u["m_ab"]
        for _ in range(n_doublings):
            for u in us:
                u["pw"] = _bmm(u["pw"], u["pw"])
            for u in us:
                u["inv"] = u["inv"] + _bmm(u["inv"], u["pw"])
        for u in us:
            u["inv"] = u["inv"].astype(BF16)
            u["mv4"] = _bmm(u["m_ak"], u["v4"])
        for u in us:
            u["ah4"] = _mm(u["inv"], u["a4"])
        for u in us:
            u["uh4"] = _bmm(u["inv"], u["mv4"])
        for u in us:
            u["ry"] = (u["rt"] + unstack(_bmm(u["m_rb"], u["ah4"]))).astype(BF16)
        for u in us:
            u["y0"] = unstack(_bmm(u["m_rb"], u["uh4"]) + _mm(u["m_rk"], u["v4"]))
        for u in us:
            ah, uh = unstack(u["ah4"]).astype(BF16), unstack(u["uh4"]).astype(BF16)
            u["g_low"] = jnp.where(same_head, _tn(u["bg"], ah), 0.0).astype(BF16)
            u["h_t"] = jnp.where(same_head, _tn(uh, u["bg"]) + _tn(u["vv"], u["kg"]), 0.0)
        for u in us:
            st = st_ref[u["q"]]
            stb = st.astype(BF16)
            y_ref[u["rows"], u["sl"]] = _nt(u["ry"], stb) + u["y0"]
            st_ref[u["q"]] = st * u["gam"] + _nt(stb, u["g_low"]) + u["h_t"]
        return carry

    lax.fori_loop(0, r_ref.shape[0] // (2 * CHUNK), chunk_pair, 0)


def _rwkv(z, prm, d_model):
    b, s, _ = z.shape
    tm = min(RW_TM, s)
    s2_blk, s3_blk = _z_small_block(d_model, 2), _z_small_block(d_model, 3)
    tok = lambda blk: pl.BlockSpec((None, tm, MIX_W), lambda bi, i: (bi, i, blk))
    tok128 = lambda blk: pl.BlockSpec((None, tm, LANES), lambda bi, i: (bi, i, blk))
    const = lambda a: pl.BlockSpec(a.shape, lambda bi, i: (0,) * a.ndim)
    consts = [prm[n] for n in ("mu_r", "mu_k", "mu_v", "mu_s2", "mu_s3", "w0", "w_up", "a0", "a_up", "g_up",
                               "k_k", "k_a", "r_k", "bones_q", "tri")]
    out_spec = pl.BlockSpec((None, tm, MIX_W), lambda bi, i: (bi, i, 0))
    return pl.pallas_call(
        _rwkv_kernel,
        grid=(b, s // tm),
        in_specs=[tok(Z_RR), tok(Z_RK), tok(Z_RV), tok128(s2_blk), tok128(s3_blk),
                  _halo_spec(MIX_W, Z_RR, tm, BF16_ROWS), _halo_spec(MIX_W, Z_RK, tm, BF16_ROWS),
                  _halo_spec(MIX_W, Z_RV, tm, BF16_ROWS), _halo_spec(LANES, s2_blk, tm, BF16_ROWS),
                  _halo_spec(LANES, s3_blk, tm, BF16_ROWS)] + [const(a) for a in consts],
        out_specs=[out_spec, out_spec, out_spec],
        out_shape=[jax.ShapeDtypeStruct((b, s, MIX_W), F32), jax.ShapeDtypeStruct((b, s, MIX_W), BF16),
                   jax.ShapeDtypeStruct((b, s, MIX_W), BF16)],
        scratch_shapes=[pltpu.VMEM((MIX_W // QUAD, QUAD, QUAD), F32)] + [pltpu.VMEM((tm, MIX_W), F32)] * 6,
        compiler_params=_cparams(("parallel", "arbitrary")),
        name="rwkv",
    )(z, z, z, z, z, z, z, z, z, z, *consts)


def _causal_conv(p, halo, cw_ref, rowid):
    nh = halo.shape[0]
    conv = p * cw_ref[CONV_W - 1:CONV_W, :]
    for back in range(1, CONV_W):
        rolled = pltpu.roll(p, back, axis=0)
        for rr in range(back):
            rolled = jnp.where(rowid == rr, halo[nh - back + rr:nh - back + rr + 1, :], rolled)
        conv = conv + rolled * cw_ref[CONV_W - 1 - back:CONV_W - back, :]
    return conv


def _merge_kernel(x_ref, oa_ref, y_ref, g_ref, bonus_ref, cb_ref, cc_ref, cx_ref, cch_ref, cxh_ref,
                  gate0_ref, gate1_ref, gate2_ref, g1_ref, lnw_ref, lnb_ref, bq_ref, cw_ref, wb_ref, wo_ref, o_ref):
    first = pl.program_id(1) == 0
    y = y_ref[...]
    inv_n = 1.0 / HEAD_DIM
    p1 = y.astype(BF16)
    mean = (_head_sum(p1, bq_ref) + _head_sum(y - p1.astype(F32), bq_ref)) * inv_n
    yc = y - mean
    var = _head_sum(yc * yc, bq_ref) * inv_n
    o_rwkv = ((yc * lax.rsqrt(var + GN_EPS) * lnw_ref[...] + lnb_ref[...] + bonus_ref[...].astype(F32))
              * g_ref[...].astype(F32))
    p = cc_ref[...].astype(F32) * cx_ref[...].astype(F32)
    ph = jnp.where(first, 0.0, cch_ref[...].astype(F32) * cxh_ref[...].astype(F32))
    rowid = lax.broadcasted_iota(jnp.int32, p.shape, 0)
    o_conv = cb_ref[...].astype(F32) * _causal_conv(p, ph, cw_ref, rowid)
    merged = jnp.zeros(x_ref.shape, F32)
    for bi, (o, gate_ref) in enumerate(((oa_ref[...], gate0_ref), (o_rwkv, gate1_ref), (o_conv, gate2_ref))):
        merged = merged + jax.nn.sigmoid(gate_ref[...].astype(F32)) * _mm(o.astype(BF16), wb_ref[bi])
    o_ref[...] = x_ref[...] + g1_ref[...] * _mm(merged.astype(BF16), wo_ref[...])


def _merge(x, o_attn, y, g, bonus, z, mod5, layer, prm, tm=256):
    b, s, d = x.shape
    tm = min(tm, s)
    tok = lambda w, blk: pl.BlockSpec((None, tm, w), lambda bi, i: (bi, i, blk))
    const = lambda a: pl.BlockSpec(a.shape, lambda bi, i: (0,) * a.ndim)
    consts = [prm[n] for n in ("ln_w", "ln_b", "bones_q", "conv_w", "w_branch", "w_o")]
    gate_blk = Z_GATE_COL // d
    return pl.pallas_call(
        _merge_kernel,
        grid=(b, s // tm),
        in_specs=[tok(d, 0), tok(MIX_W, 0), tok(MIX_W, 0), tok(MIX_W, 0), tok(MIX_W, 0),
                  tok(MIX_W, Z_CB), tok(MIX_W, Z_CC), tok(MIX_W, Z_CX),
                  _halo_spec(MIX_W, Z_CC, tm, BF16_ROWS), _halo_spec(MIX_W, Z_CX, tm, BF16_ROWS),
                  tok(d, gate_blk), tok(d, gate_blk + 1), tok(d, gate_blk + 2),
                  _mod_spec(mod5, layer, 2, 2)] + [const(a) for a in consts],
        out_specs=tok(d, 0),
        out_shape=jax.ShapeDtypeStruct((b, s, d), F32),
        compiler_params=_cparams(("parallel", "parallel")),
        name="merge",
    )(x, o_attn, y, g, bonus, z, z, z, z, z, z, z, z, mod5, *consts)


def _ffn_kernel(x_ref, xh_ref, sh_ref, sc_ref, g2_ref, gain_ref, wa_ref, wg_ref, cw_ref, wd_ref, o_ref,
                h_ref, hh_ref, acc_ref):
    jf = pl.program_id(2)
    first = pl.program_id(1) == 0

    @pl.when(jf == 0)
    def _():
        h_ref[...] = _norm_mod(x_ref[...], gain_ref[...], sc_ref[...], sh_ref[...]).astype(h_ref.dtype)
        hh_ref[...] = _norm_mod(xh_ref[...], gain_ref[...], sc_ref[...], sh_ref[...]).astype(hh_ref.dtype)
        acc_ref[...] = jnp.zeros_like(acc_ref)

    a = _mm(h_ref[...], wa_ref[...])
    ah = jnp.where(first, 0.0, _mm(hh_ref[...], wa_ref[...]))
    gl = _mm(h_ref[...], wg_ref[...])
    rowid = lax.broadcasted_iota(jnp.int32, a.shape, 0)
    conv = _causal_conv(a, ah, cw_ref, rowid)
    u = conv * jax.nn.sigmoid(conv) * gl
    acc_ref[...] += _mm(u.astype(BF16), wd_ref[...])

    @pl.when(jf == pl.num_programs(2) - 1)
    def _():
        o_ref[...] = x_ref[...] + g2_ref[...] * acc_ref[...]


def _ffn(x, mod5, layer, gain, w_up, conv_w, w_down, tm=1024, tf=256):
    b, s, d = x.shape
    d_ff = w_down.shape[0]
    nf = d_ff // tf
    tm = min(tm, s)
    step = tm // SUBLANES
    return pl.pallas_call(
        _ffn_kernel,
        grid=(b, s // tm, nf),
        in_specs=[pl.BlockSpec((None, tm, d), lambda bi, i, j: (bi, i, 0)),
                  pl.BlockSpec((None, SUBLANES, d), lambda bi, i, j: (bi, jnp.maximum(i * step - 1, 0), 0)),
                  _mod_spec(mod5, layer, 3, 3), _mod_spec(mod5, layer, 4, 3), _mod_spec(mod5, layer, 5, 3),
                  pl.BlockSpec((1, d), lambda bi, i, j: (0, 0)),
                  pl.BlockSpec((d, tf), lambda bi, i, j: (0, j)),
                  pl.BlockSpec((d, tf), lambda bi, i, j: (0, j + nf)),
                  pl.BlockSpec((CONV_W, tf), lambda bi, i, j: (0, j)),
                  pl.BlockSpec((tf, d), lambda bi, i, j: (j, 0))],
        out_specs=pl.BlockSpec((None, tm, d), lambda bi, i, j: (bi, i, 0)),
        out_shape=jax.ShapeDtypeStruct((b, s, d), F32),
        scratch_shapes=[pltpu.VMEM((tm, d), BF16), pltpu.VMEM((SUBLANES, d), BF16), pltpu.VMEM((tm, d), F32)],
        compiler_params=_cparams(("parallel", "parallel", "arbitrary")),
        name="ffn",
    )(x, x, mod5, mod5, mod5, gain.reshape(1, d), w_up, w_up, conv_w, w_down)


def _final_norm_kernel(x_ref, g_ref, o_ref):
    x = x_ref[...]
    o_ref[...] = x * lax.rsqrt(jnp.mean(x * x, axis=-1, keepdims=True) + NORM_EPS) * g_ref[...]


def _final_norm(x, gain, tm=1024):
    b, s, d = x.shape
    tm = min(tm, s)
    return pl.pallas_call(
        _final_norm_kernel,
        grid=(b, s // tm),
        in_specs=[pl.BlockSpec((None, tm, d), lambda bi, i: (bi, i, 0)),
                  pl.BlockSpec((1, d), lambda bi, i: (0, 0))],
        out_specs=pl.BlockSpec((None, tm, d), lambda bi, i: (bi, i, 0)),
        out_shape=jax.ShapeDtypeStruct((b, s, d), F32),
        compiler_params=_cparams(("parallel", "parallel")),
        name="final_norm",
    )(x, gain.reshape(1, d))


def _split_w_in(w):
    sizes = (MIX_W, MIX_W, MIX_W, HEADS * HEAD_DIM, HEAD_DIM, HEADS,
             MIX_W, MIX_W, MIX_W, LORA_DECAY, LORA_ICLR, LORA_GATE,
             MIX_W, MIX_W, MIX_W)
    out, o = [], 0
    for n in sizes:
        out.append(w[:, o:o + n])
        o += n
    out.append(w[:, o:])
    return out


def _layer_params(l, w_in, rwkv_mu, rwkv_w0, rwkv_w_up, rwkv_a0, rwkv_a_up, rwkv_g_up, rwkv_k_k, rwkv_k_a,
                  rwkv_r_k, rwkv_ln_w, rwkv_ln_b, sc_conv_w, w_branch, w_o):
    d = w_in.shape[1]
    q, k, v, qi, ki, wi, rr, rk, rv, wd, ad, gd, cb, cc, cx, gates = _split_w_in(w_in[l])
    assert Z_GATE_COL % d == 0 and gates.shape[1] == 3 * d
    w_z = jnp.concatenate([q, k, v, qi, rr, rk, rv, cb, cc, cx, gates,
                           ki, ki, wi, jnp.zeros((d, LANES - HEADS), F32), wd, ad, gd], axis=1).astype(BF16)
    mu = rwkv_mu[l]
    row = lambda a: a.reshape(1, -1)
    head_id = jnp.arange(QUAD) // HEAD_DIM
    bones_q = (head_id[:, None] == head_id[None, :]).astype(BF16)
    tri = (jnp.arange(CHUNK)[:, None] >= jnp.arange(CHUNK)[None, :]).astype(BF16)
    zl = lambda n: jnp.zeros((n, MIX_W), F32)
    return dict(
        w_z=w_z,
        mu_r=row(mu[:MIX_W]), mu_k=row(mu[MIX_W:2 * MIX_W]), mu_v=row(mu[2 * MIX_W:3 * MIX_W]),
        mu_s2=row(mu[3 * MIX_W:3 * MIX_W + LORA_DECAY + LORA_ICLR]), mu_s3=row(mu[3 * MIX_W + LORA_DECAY + LORA_ICLR:]),
        w0=row(rwkv_w0[l]), w_up=jnp.concatenate([rwkv_w_up[l], zl(LORA_ICLR)], axis=0).astype(BF16),
        a0=row(rwkv_a0[l]), a_up=jnp.concatenate([zl(LORA_DECAY), rwkv_a_up[l]], axis=0).astype(BF16),
        g_up=rwkv_g_up[l].astype(BF16), k_k=row(rwkv_k_k[l]), k_a=row(rwkv_k_a[l]), r_k=row(rwkv_r_k[l]),
        bones_q=bones_q, tri=tri,
        ln_w=row(rwkv_ln_w[l]), ln_b=row(rwkv_ln_b[l]), conv_w=sc_conv_w[l].T,
        w_branch=w_branch[l].astype(BF16), w_o=w_o[l].astype(BF16),
    )


def kernel(x, c, positions, rel_bias, final_norm, ada_w, ada_b, norm_mix, w_in, rwkv_mu, rwkv_w0, rwkv_w_up,
           rwkv_a0, rwkv_a_up, rwkv_g_up, rwkv_k_k, rwkv_k_a, rwkv_r_k, rwkv_ln_w, rwkv_ln_b, sc_conv_w,
           w_branch, w_o, norm_ffn, ffn_w_up, ffn_conv_w, ffn_w_down):
    depth, d = ada_w.shape[0], x.shape[-1]
    mod5 = _ada_mod(c, ada_w, ada_b)
    bias = _bias_tiles(rel_bias)
    for l in range(depth):
        prm = _layer_params(l, w_in, rwkv_mu, rwkv_w0, rwkv_w_up, rwkv_a0, rwkv_a_up, rwkv_g_up, rwkv_k_k,
                            rwkv_k_a, rwkv_r_k, rwkv_ln_w, rwkv_ln_b, sc_conv_w, w_branch, w_o)
        z = _inproj(x, mod5, l, norm_mix[l], prm["w_z"])
        o_attn = _dsa(z, bias, d)
        y, g, bonus = _rwkv(z, prm, d)
        x = _merge(x, o_attn, y, g, bonus, z, mod5, l, prm)
        x = _ffn(x, mod5, l, norm_ffn[l], ffn_w_up[l].astype(BF16), ffn_conv_w[l].T, ffn_w_down[l].astype(BF16))
    return _final_norm(x, final_norm)
```

```python
import functools
import math

import jax
import jax.numpy as jnp
from jax import lax
from jax.experimental import pallas as pl
from jax.experimental.pallas import tpu as pltpu

F32 = jnp.float32
BF16 = jnp.bfloat16
HIGHEST = lax.Precision.HIGHEST

MIX_W = 512
HEADS = 8
HEAD_DIM = 64
TOPK_MAX = 256
N_BUCKETS = 32
MAX_DISTANCE = 128
LORA_DECAY = 64
LORA_ICLR = 64
LORA_GATE = 128
CONV_W = 3
NORM_EPS = 1e-6
GN_EPS = 64e-5
NEG_INF = -1e30

LANES = 128
SUBLANES = 8
BF16_ROWS = 16
VMEM_LIMIT = 56 * 1024 * 1024

TQ = 256
CHUNK = 64
RW_TM = 256
QUAD = 4 * HEAD_DIM
assert CHUNK == HEAD_DIM

Z_Q, Z_K, Z_V, Z_QI, Z_RR, Z_RK, Z_RV, Z_CB, Z_CC, Z_CX = range(10)
Z_GATE_COL = 10 * MIX_W


def _z_small_block(d_model, i):
    return (Z_GATE_COL + 3 * d_model) // LANES + i


def _cparams(sem):
    return pltpu.CompilerParams(dimension_semantics=sem, vmem_limit_bytes=VMEM_LIMIT)


def _nt(a, b, precision=None):
    return lax.dot_general(a, b, (((1,), (1,)), ((), ())), precision=precision,
                           preferred_element_type=F32)


def _tn(a, b, precision=None):
    return lax.dot_general(a, b, (((0,), (0,)), ((), ())), precision=precision,
                           preferred_element_type=F32)


def _mm(a, b, precision=None):
    return jnp.dot(a, b, precision=precision, preferred_element_type=F32)


def _bmm(a, b):
    return _mm(a.astype(BF16), b.astype(BF16))


def _mod_kernel(c_ref, w_ref, b_ref, o_ref):
    o_ref[...] = _mm(c_ref[...], w_ref[...], HIGHEST) + b_ref[...]


def _ada_mod(c, ada_w, ada_b):
    depth, d, d6 = ada_w.shape
    b = c.shape[0]
    out = pl.pallas_call(
        _mod_kernel,
        grid=(depth, d6 // d),
        in_specs=[pl.BlockSpec((b, d), lambda l, j: (0, 0)),
                  pl.BlockSpec((None, d, d), lambda l, j: (l, 0, j)),
                  pl.BlockSpec((None, 1, d), lambda l, j: (l, 0, j))],
        out_specs=pl.BlockSpec((None, b, d), lambda l, j: (l, 0, j)),
        out_shape=jax.ShapeDtypeStruct((depth, b, d6), F32),
        compiler_params=_cparams(("parallel", "parallel")),
        name="ada_mod",
    )(c, ada_w, ada_b.reshape(depth, 1, d6))
    return out.reshape(depth, b, d6 // d, 1, d)


def _mod_spec(mod5, layer, which, ngrid):
    d = mod5.shape[-1]
    if ngrid == 2:
        return pl.BlockSpec((None, None, None, 1, d), lambda b, i: (layer, b, which, 0, 0))
    return pl.BlockSpec((None, None, None, 1, d), lambda b, i, j: (layer, b, which, 0, 0))


def _norm_mod(x, gain, scale, shift):
    y = x * lax.rsqrt(jnp.mean(x * x, axis=-1, keepdims=True) + NORM_EPS) * gain
    return y * (1.0 + scale) + shift


def _inproj_kernel(x_ref, sh_ref, sc_ref, g_ref, w_ref, o_ref, h_ref):
    @pl.when(pl.program_id(2) == 0)
    def _():
        h_ref[...] = _norm_mod(x_ref[...], g_ref[...], sc_ref[...], sh_ref[...]).astype(h_ref.dtype)

    o_ref[...] = _mm(h_ref[...], w_ref[...]).astype(o_ref.dtype)


def _inproj(x, mod5, layer, gain, w, tm=1024, n_col_tiles=4):
    b, s, d = x.shape
    zc = w.shape[1]
    tm = min(tm, s)
    tn = zc // n_col_tiles
    assert tn * n_col_tiles == zc and tn % LANES == 0
    return pl.pallas_call(
        _inproj_kernel,
        grid=(b, s // tm, zc // tn),
        in_specs=[pl.BlockSpec((None, tm, d), lambda bi, i, j: (bi, i, 0)),
                  _mod_spec(mod5, layer, 0, 3),
                  _mod_spec(mod5, layer, 1, 3),
                  pl.BlockSpec((1, d), lambda bi, i, j: (0, 0)),
                  pl.BlockSpec((d, tn), lambda bi, i, j: (0, j))],
        out_specs=pl.BlockSpec((None, tm, tn), lambda bi, i, j: (bi, i, j)),
        out_shape=jax.ShapeDtypeStruct((b, s, zc), BF16),
        scratch_shapes=[pltpu.VMEM((tm, d), BF16)],
        compiler_params=_cparams(("parallel", "parallel", "arbitrary")),
        name="inproj",
    )(x, mod5, mod5, gain.reshape(1, d), w)


def _bias_kernel(rb_ref, o_ref):
    which = pl.program_id(0)
    h = pl.program_id(1)
    ri = lax.broadcasted_iota(jnp.int32, (TQ, TQ), 0)
    ci = lax.broadcasted_iota(jnp.int32, (TQ, TQ), 1)
    dist = ci - ri + (1 - which) * TQ
    n = jnp.maximum(dist, 0)
    max_exact = N_BUCKETS // 2
    nf = jnp.maximum(n, 1).astype(F32)
    large = max_exact + (jnp.log(nf / max_exact) / math.log(MAX_DISTANCE / max_exact)
                         * (N_BUCKETS - max_exact)).astype(jnp.int32)
    large = jnp.minimum(large, N_BUCKETS - 1)
    bucket = jnp.where(n < max_exact, n, large)
    far = rb_ref[N_BUCKETS - 1, h]
    acc = jnp.zeros((TQ, TQ), F32)
    for bkt in range(N_BUCKETS - 1):
        acc = jnp.where(bucket == bkt, rb_ref[bkt, h] - far, acc)
    o_ref[...] = acc * LOG2E


def _bias_tiles(rel_bias):
    assert TQ >= MAX_DISTANCE
    return pl.pallas_call(
        _bias_kernel,
        grid=(2, HEADS),
        in_specs=[pl.BlockSpec(memory_space=pltpu.SMEM)],
        out_specs=pl.BlockSpec((None, None, TQ, TQ), lambda w, h: (w, h, 0, 0)),
        out_shape=jax.ShapeDtypeStruct((2, HEADS, TQ, TQ), F32),
        compiler_params=_cparams(("parallel", "parallel")),
        name="bias_tiles",
    )(rel_bias)


BISECT_MAX_IT = 300
LOG2E = math.log2(math.e)
ONES_ROWS = BF16_ROWS


def _row_groups(t):
    return [t[r * SUBLANES:(r + 1) * SUBLANES, :] for r in range(t.shape[0] // SUBLANES)]


def _dsa_kernel(q_ref, k_ref, v_ref, qi_ref, ki_ref, wi_ref, bias_ref, o_ref,
                sc_ref, qm_ref, qim_ref, wt_ref, vt_ref, m_ref, l_ref, acc_ref, s_ref, *, n_keep, seq):
    j = pl.program_id(1)
    nt = j + 1
    kf = float(n_keep)
    lane = lax.broadcasted_iota(jnp.int32, (TQ, LANES), 1)
    att_scale = HEAD_DIM ** -0.5 * LOG2E
    w_scale = (HEADS ** -0.5) * (HEAD_DIM ** -0.5)

    for h in range(HEADS):
        p, odd = divmod(h, 2)
        hm = (lane >= HEAD_DIM) if odd else (lane < HEAD_DIM)
        qs = q_ref[:, p * LANES:(p + 1) * LANES]
        qm_ref[h] = jnp.where(hm, qs, jnp.zeros_like(qs)) * att_scale
        qis = qi_ref[:, p * LANES:(p + 1) * LANES]
        qim_ref[h] = jnp.where(hm, qis, jnp.zeros_like(qis))
    wt_ref[...] = wi_ref[...].astype(F32).T * w_scale
    v_t = v_ref[...].T
    for h in range(HEADS):
        vt_ref[j, h, 0:HEAD_DIM, :] = v_t[h * HEAD_DIM:(h + 1) * HEAD_DIM, :]
        vt_ref[j, h, HEAD_DIM:HEAD_DIM + ONES_ROWS, :] = jnp.ones((ONES_ROWS, TQ), BF16)

    key = lax.broadcasted_iota(jnp.int32, (TQ, TQ), 0)
    qry = lax.broadcasted_iota(jnp.int32, (TQ, TQ), 1) + j * TQ

    def idx_tile(kt, carry):
        ki_t = ki_ref[pl.ds(pl.multiple_of(kt * TQ, TQ), TQ), :]
        acc = jnp.zeros((TQ, TQ), F32)
        for h in range(HEADS):
            acc = acc + jnp.maximum(_nt(ki_t, qim_ref[h]), 0.0) * wt_ref[h:h + 1, :]
        sc_ref[kt] = jnp.where(key + kt * TQ <= qry, acc, NEG_INF)
        return carry

    lax.fori_loop(0, nt, idx_tile, 0)

    def stats_tile(kt, c):
        mx, mn, mp, cp, cn = c
        for g in _row_groups(sc_ref[kt]):
            pos = g > 0.0
            mx = jnp.maximum(mx, g)
            mn = jnp.minimum(mn, jnp.where(g > 0.5 * NEG_INF, g, -NEG_INF))
            mp = jnp.minimum(mp, jnp.where(pos, g, -NEG_INF))
            cp = cp + jnp.where(pos, 1.0, 0.0)
            cn = cn + jnp.where(g >= 0.0, 1.0, 0.0)
        return mx, mn, mp, cp, cn

    part = lambda v: jnp.full((SUBLANES, TQ), v, F32)
    mx, mn, mp, cp, cn = lax.fori_loop(0, nt, stats_tile,
                                       (part(NEG_INF), part(-NEG_INF), part(-NEG_INF), part(0.0), part(0.0)))
    rmax = jnp.max(mx, axis=0, keepdims=True)
    rmin = jnp.min(mn, axis=0, keepdims=True)
    minpos = jnp.min(mp, axis=0, keepdims=True)
    cpos = jnp.sum(cp, axis=0, keepdims=True)
    cnn = jnp.sum(cn, axis=0, keepdims=True)
    nvalid = (lax.broadcasted_iota(jnp.int32, (1, TQ), 1) + j * TQ + 1).astype(F32)
    small = nvalid <= kf

    def count_ge(thr):
        def body(kt, acc):
            for g in _row_groups(sc_ref[kt]):
                acc = acc + jnp.where(g >= thr, 1.0, 0.0)
            return acc
        acc = lax.fori_loop(0, nt, body, jnp.zeros((SUBLANES, TQ), F32))
        return jnp.sum(acc, axis=0, keepdims=True)

    ztie = jnp.logical_and(cpos < kf, cnn >= kf)
    pos_side = cpos >= kf
    hi_top = rmax + jnp.maximum(jnp.abs(rmax) * 1e-6, 1e-30)
    lo0 = jnp.where(small, 0.5 * NEG_INF, jnp.where(ztie, 0.0, jnp.where(pos_side, minpos, rmin)))
    hi0 = jnp.where(small, -NEG_INF, jnp.where(ztie, minpos, jnp.where(pos_side, hi_top, 0.0)))
    clo0 = jnp.where(jnp.logical_or(small, ztie), jnp.where(small, nvalid, cnn), jnp.where(pos_side, cpos, nvalid))
    chi0 = jnp.where(small, 0.0, jnp.where(ztie, cpos, jnp.where(pos_side, 0.0, cnn)))
    done0 = jnp.where(jnp.logical_or(jnp.logical_or(small, ztie), clo0 == kf), 1.0, 0.0)

    def bis_cond(c):
        return jnp.logical_and(c[0] < BISECT_MAX_IT, c[6] > 0.0)

    def bis_body(c):
        it, lo, hi, clo, chi, done, _ = c
        mid = lo + 0.5 * (hi - lo)
        stalled = jnp.logical_or(mid <= lo, mid >= hi)
        cnt = count_ge(mid)
        act = done < 0.5
        ge = cnt >= kf
        up_lo = jnp.logical_and(act, ge)
        up_hi = jnp.logical_and(act, jnp.logical_not(ge))
        lo = jnp.where(up_lo, mid, lo)
        clo = jnp.where(up_lo, cnt, clo)
        hi = jnp.where(up_hi, mid, hi)
        chi = jnp.where(up_hi, cnt, chi)
        fin = jnp.logical_or(cnt == kf, stalled)
        done = jnp.where(fin, 1.0, done)
        return it + 1, lo, hi, clo, chi, done, jnp.sum(1.0 - done)

    _, lo, hi, clo, chi, _, _ = lax.while_loop(
        bis_cond, bis_body,
        (jnp.int32(0), lo0, hi0, clo0, chi0, done0, jnp.sum(1.0 - done0)))

    tie = jnp.logical_and(clo > kf, jnp.logical_not(small))
    n_tie_iter = int(math.ceil(math.log2(seq))) + 1
    key_f = key.astype(F32)

    def tie_fn():
        need = kf - chi

        def count_tie_le(m):
            def body(kt, acc):
                t = sc_ref[kt]
                s_abs = key_f + (kt * TQ).astype(F32)
                inside = jnp.where(t >= lo, jnp.where(t < hi, jnp.where(s_abs <= m, 1.0, 0.0), 0.0), 0.0)
                for g in _row_groups(inside):
                    acc = acc + g
                return acc
            acc = lax.fori_loop(0, nt, body, jnp.zeros((SUBLANES, TQ), F32))
            return jnp.sum(acc, axis=0, keepdims=True)

        def tb(_, c):
            loi, hii = c
            midi = jnp.floor((loi + hii) * 0.5)
            ge = count_tie_le(midi) >= need
            return jnp.where(ge, loi, midi), jnp.where(ge, midi, hii)

        _, hii = lax.fori_loop(0, n_tie_iter, tb,
                               (jnp.full((1, TQ), -1.0, F32), jnp.full((1, TQ), seq - 1.0, F32)))
        return jnp.where(tie, hii, float(seq))

    mstar = lax.cond(jnp.sum(jnp.where(tie, 1.0, 0.0)) > 0.0, tie_fn,
                     lambda: jnp.full((1, TQ), float(seq), F32))

    def mask_tile(kt, carry):
        t = sc_ref[kt]
        s_abs = key_f + (kt * TQ).astype(F32)
        keep_tie = jnp.where(s_abs <= mstar, 0.0, NEG_INF)
        sc_ref[kt] = jnp.where(t >= lo, jnp.where(t >= hi, 0.0, keep_tie), NEG_INF)
        return carry

    lax.fori_loop(0, nt, mask_tile, 0)

    m_ref[...] = jnp.full(m_ref.shape, NEG_INF, F32)
    l_ref[...] = jnp.zeros(l_ref.shape, F32)
    acc_ref[...] = jnp.zeros(acc_ref.shape, F32)

    def attn_tiles(kts, near):
        tile_max = [None] * HEADS
        for i, kt in enumerate(kts):
            rows = pl.ds(pl.multiple_of(kt * TQ, TQ), TQ)
            mask_add = sc_ref[kt]
            for h in range(HEADS):
                p = h // 2
                s = _nt(k_ref[rows, p * LANES:(p + 1) * LANES], qm_ref[h]) + mask_add
                if near:
                    s = s + bias_ref[kt - j + 1, h]
                s_ref[i, h] = s
                mx = jnp.max(s, axis=0, keepdims=True)
                tile_max[h] = mx if i == 0 else jnp.maximum(tile_max[h], mx)
        for h in range(HEADS):
            m_old = m_ref[h]
            m_new = jnp.maximum(m_old, tile_max[h])
            alpha = jnp.exp2(m_old - m_new)
            pv = None
            for i, kt in enumerate(kts):
                part = _mm(vt_ref[kt, h], jnp.exp2(s_ref[i, h] - m_new).astype(BF16))
                pv = part if i == 0 else pv + part
            m_ref[h] = m_new
            l_ref[h] = alpha * l_ref[h] + pv[HEAD_DIM:HEAD_DIM + 1, :]
            acc_ref[h] = alpha * acc_ref[h] + pv[0:HEAD_DIM, :]

    n_far = jnp.maximum(j - 1, 0)
    n_far_pairs = lax.shift_right_logical(n_far, 1)

    def far_pair(i, carry):
        attn_tiles([2 * i, 2 * i + 1], near=False)
        return carry

    def far_single(kt, carry):
        attn_tiles([kt], near=False)
        return carry

    lax.fori_loop(0, n_far_pairs, far_pair, 0)
    lax.fori_loop(2 * n_far_pairs, n_far, far_single, 0)

    @pl.when(j >= 1)
    def _():
        attn_tiles([j - 1, j], near=True)

    @pl.when(j == 0)
    def _():
        attn_tiles([j], near=True)

    for p in range(HEADS // 2):
        pair = jnp.concatenate([acc_ref[2 * p] / l_ref[2 * p], acc_ref[2 * p + 1] / l_ref[2 * p + 1]], axis=0)
        o_ref[:, p * LANES:(p + 1) * LANES] = pair.T.astype(o_ref.dtype)


def _dsa(z, bias, d_model):
    b, s, _ = z.shape
    n_keep = min(TOPK_MAX, s // 4)
    kern = functools.partial(_dsa_kernel, n_keep=n_keep, seq=s)
    ki_blk, wi_blk = _z_small_block(d_model, 0), _z_small_block(d_model, 1)
    return pl.pallas_call(
        kern,
        grid=(b, s // TQ),
        in_specs=[pl.BlockSpec((None, TQ, MIX_W), lambda bi, j: (bi, j, Z_Q)),
                  pl.BlockSpec((None, s, MIX_W), lambda bi, j: (bi, 0, Z_K)),
                  pl.BlockSpec((None, TQ, MIX_W), lambda bi, j: (bi, j, Z_V)),
                  pl.BlockSpec((None, TQ, MIX_W), lambda bi, j: (bi, j, Z_QI)),
                  pl.BlockSpec((None, s, LANES), lambda bi, j: (bi, 0, ki_blk)),
                  pl.BlockSpec((None, TQ, LANES), lambda bi, j: (bi, j, wi_blk)),
                  pl.BlockSpec((2, HEADS, TQ, TQ), lambda bi, j: (0, 0, 0, 0))],
        out_specs=pl.BlockSpec((None, TQ, MIX_W), lambda bi, j: (bi, j, 0)),
        out_shape=jax.ShapeDtypeStruct((b, s, MIX_W), BF16),
        scratch_shapes=[pltpu.VMEM((s // TQ, TQ, TQ), F32),
                        pltpu.VMEM((HEADS, TQ, LANES), BF16),
                        pltpu.VMEM((HEADS, TQ, LANES), BF16),
                        pltpu.VMEM((LANES, TQ), F32),
                        pltpu.VMEM((s // TQ, HEADS, HEAD_DIM + ONES_ROWS, TQ), BF16),
                        pltpu.VMEM((HEADS, 1, TQ), F32),
                        pltpu.VMEM((HEADS, 1, TQ), F32),
                        pltpu.VMEM((HEADS, HEAD_DIM, TQ), F32),
                        pltpu.VMEM((2, HEADS, TQ, TQ), F32)],
        compiler_params=_cparams(("parallel", "arbitrary")),
        name="dsa",
    )(z, z, z, z, z, z, bias)


def _halo_spec(width, blk, tm, rows):
    step = tm // rows
    return pl.BlockSpec((None, rows, width), lambda bi, i: (bi, jnp.maximum(i * step - 1, 0), blk))


def _shift_lerp(cur_ref, halo_ref, mu, first):
    cur = cur_ref[...].astype(F32)
    nh = halo_ref.shape[0]
    prev_last = jnp.where(first, 0.0, halo_ref[nh - 1:nh, :].astype(F32))
    rolled = pltpu.roll(cur, 1, axis=0)
    rowid = lax.broadcasted_iota(jnp.int32, cur.shape, 0)
    sh = jnp.where(rowid == 0, prev_last, rolled)
    return cur + (sh - cur) * mu


def _head_sum(x, bq_ref):
    xb = x.astype(BF16)
    return jnp.concatenate([_mm(xb[:, q * QUAD:(q + 1) * QUAD], bq_ref[...]) for q in range(MIX_W // QUAD)], axis=1)


def _rwkv_kernel(r_ref, k_ref, v_ref, s2_ref, s3_ref, rh_ref, kh_ref, vh_ref, s2h_ref, s3h_ref,
                 mu_r, mu_k, mu_v, mu_s2, mu_s3, w0_ref, wup_ref, a0_ref, aup_ref, gup_ref,
                 kk_ref, ka_ref, rk_ref, bq_ref, tri_ref,
                 y_ref, g_ref, bonus_ref,
                 st_ref, rs, ls, ks, vs, kks, kbs):
    first = pl.program_id(1) == 0

    @pl.when(first)
    def _():
        st_ref[...] = jnp.zeros_like(st_ref)

    r = _shift_lerp(r_ref, rh_ref, mu_r[...], first)
    k = _shift_lerp(k_ref, kh_ref, mu_k[...], first)
    v = _shift_lerp(v_ref, vh_ref, mu_v[...], first)
    s2 = _shift_lerp(s2_ref, s2h_ref, mu_s2[...], first)
    s3 = _shift_lerp(s3_ref, s3h_ref, mu_s3[...], first)
    xw = w0_ref[...] + _bmm(jnp.tanh(s2), wup_ref[...])
    softplus = jnp.maximum(-xw, 0.0) + jnp.log1p(jnp.exp(-jnp.abs(xw)))
    ld = -jnp.exp(-softplus - 0.5)
    af = jax.nn.sigmoid(a0_ref[...] + _bmm(s2, aup_ref[...]))
    g_ref[...] = _bmm(jax.nn.sigmoid(s3), gup_ref[...]).astype(g_ref.dtype)
    kkr = k * kk_ref[...]
    kkn = kkr / jnp.maximum(jnp.sqrt(_head_sum(kkr * kkr, bq_ref)), 1e-12)
    kmod = k * (1.0 + (af - 1.0) * ka_ref[...])
    bonus_ref[...] = (_head_sum(r * kmod * rk_ref[...], bq_ref) * v).astype(bonus_ref.dtype)
    rs[...] = r
    ls[...] = ld
    ks[...] = kmod
    vs[...] = v
    kks[...] = kkn
    kbs[...] = kkn * af

    ri = lax.broadcasted_iota(jnp.int32, (QUAD, QUAD), 0)
    ci = lax.broadcasted_iota(jnp.int32, (QUAD, QUAD), 1)
    same_head = (ri // HEAD_DIM) == (ci // HEAD_DIM)
    strict = jnp.logical_and(same_head, (ri % CHUNK) > (ci % CHUNK))
    incl = jnp.logical_and(same_head, (ri % CHUNK) >= (ci % CHUNK))
    eye = jnp.where(ri == ci, 1.0, 0.0)

    def stack(x):
        return jnp.where(same_head, jnp.concatenate([x] * 4, axis=0), 0.0).astype(BF16)

    def tile4(x):
        return jnp.concatenate([x] * 4, axis=0).astype(BF16)

    def unstack(x):
        return (x[0:CHUNK] + x[CHUNK:2 * CHUNK]) + (x[2 * CHUNK:3 * CHUNK] + x[3 * CHUNK:4 * CHUNK])

    def chunk_operands(c):
        rows = pl.ds(pl.multiple_of(c * CHUNK, CHUNK), CHUNK)
        ldc = ls[rows, :]
        p1 = ldc.astype(BF16)
        e1 = ldc - p1.astype(F32)
        p2 = e1.astype(BF16)
        p3 = (e1 - p2.astype(F32)).astype(BF16)
        tri = tri_ref[...]
        cl = (_mm(tri, p1) + _mm(tri, p2)) + _mm(tri, p3)
        cl_end = cl[CHUNK - 1:CHUNK, :]
        e_in = jnp.exp(cl)
        e_out = jnp.exp(-cl)
        e_end = jnp.exp(cl_end - cl)
        rt_all = rs[rows, :] * e_in
        at_all = -kks[rows, :] * jnp.exp(cl - ldc)
        bt_all = kbs[rows, :] * e_out
        kt_all = ks[rows, :] * e_out
        bg_all = (kbs[rows, :] * e_end).astype(BF16)
        kg_all = (ks[rows, :] * e_end).astype(BF16)
        v_all = vs[rows, :]
        gam_all = jnp.exp(cl_end)
        units = []
        for q in range(MIX_W // QUAD):
            sl = slice(q * QUAD, (q + 1) * QUAD)
            units.append(dict(rows=rows, sl=sl, q=q, rt=rt_all[:, sl], vv=v_all[:, sl].astype(BF16),
                              a4=stack(at_all[:, sl]), r4=stack(rt_all[:, sl]), v4=stack(v_all[:, sl]),
                              bt4=tile4(bt_all[:, sl]), kt4=tile4(kt_all[:, sl]),
                              bg=bg_all[:, sl], kg=kg_all[:, sl], gam=gam_all[:, sl]))
        return units

    n_doublings = int(math.log2(CHUNK)) - 1

    def chunk_pair(cp, carry):
        us = chunk_operands(2 * cp) + chunk_operands(2 * cp + 1)
        for u in us:
            u["m_ab"] = jnp.where(strict, _nt(u["a4"], u["bt4"]), 0.0)
        for u in us:
            u["m_ak"] = jnp.where(strict, _nt(u["a4"], u["kt4"]), 0.0)
        for u in us:
            u["m_rb"] = jnp.where(incl, _nt(u["r4"], u["bt4"]), 0.0).astype(BF16)
        for u in us:
            u["m_rk"] = jnp.where(incl, _nt(u["r4"], u["kt4"]), 0.0).astype(BF16)
        for u in us:
            u["pw"] = u["m_ab"]
            u["inv"] = eye + u["m_ab"]
        for _ in range(n_doublings):
            for u in us:
                u["pw"] = _bmm(u["pw"], u["pw"])
            for u in us:
                u["inv"] = u["inv"] + _bmm(u["inv"], u["pw"])
        for u in us:
            u["inv"] = u["inv"].astype(BF16)
            u["mv4"] = _bmm(u["m_ak"], u["v4"])
        for u in us:
            u["ah4"] = _mm(u["inv"], u["a4"])
        for u in us:
            u["uh4"] = _bmm(u["inv"], u["mv4"])
        for u in us:
            u["ry"] = (u["rt"] + unstack(_bmm(u["m_rb"], u["ah4"]))).astype(BF16)
        for u in us:
            u["y0"] = unstack(_bmm(u["m_rb"], u["uh4"]) + _mm(u["m_rk"], u["v4"]))
        for u in us:
            ah, uh = unstack(u["ah4"]).astype(BF16), unstack(u["uh4"]).astype(BF16)
            u["g_low"] = jnp.where(same_head, _tn(u["bg"], ah), 0.0).astype(BF16)
            u["h_t"] = jnp.where(same_head, _tn(uh, u["bg"]) + _tn(u["vv"], u["kg"]), 0.0)
        for u in us:
            st = st_ref[u["q"]]
            stb = st.astype(BF16)
            y_ref[u["rows"], u["sl"]] = _nt(u["ry"], stb) + u["y0"]
            st_ref[u["q"]] = st * u["gam"] + _nt(stb, u["g_low"]) + u["h_t"]
        return carry

    lax.fori_loop(0, r_ref.shape[0] // (2 * CHUNK), chunk_pair, 0)


def _rwkv(z, prm, d_model):
    b, s, _ = z.shape
    tm = min(RW_TM, s)
    s2_blk, s3_blk = _z_small_block(d_model, 2), _z_small_block(d_model, 3)
    tok = lambda blk: pl.BlockSpec((None, tm, MIX_W), lambda bi, i: (bi, i, blk))
    tok128 = lambda blk: pl.BlockSpec((None, tm, LANES), lambda bi, i: (bi, i, blk))
    const = lambda a: pl.BlockSpec(a.shape, lambda bi, i: (0,) * a.ndim)
    consts = [prm[n] for n in ("mu_r", "mu_k", "mu_v", "mu_s2", "mu_s3", "w0", "w_up", "a0", "a_up", "g_up",
                               "k_k", "k_a", "r_k", "bones_q", "tri")]
    out_spec = pl.BlockSpec((None, tm, MIX_W), lambda bi, i: (bi, i, 0))
    return pl.pallas_call(
        _rwkv_kernel,
        grid=(b, s // tm),
        in_specs=[tok(Z_RR), tok(Z_RK), tok(Z_RV), tok128(s2_blk), tok128(s3_blk),
                  _halo_spec(MIX_W, Z_RR, tm, BF16_ROWS), _halo_spec(MIX_W, Z_RK, tm, BF16_ROWS),
                  _halo_spec(MIX_W, Z_RV, tm, BF16_ROWS), _halo_spec(LANES, s2_blk, tm, BF16_ROWS),
                  _halo_spec(LANES, s3_blk, tm, BF16_ROWS)] + [const(a) for a in consts],
        out_specs=[out_spec, out_spec, out_spec],
        out_shape=[jax.ShapeDtypeStruct((b, s, MIX_W), F32), jax.ShapeDtypeStruct((b, s, MIX_W), BF16),
                   jax.ShapeDtypeStruct((b, s, MIX_W), BF16)],
        scratch_shapes=[pltpu.VMEM((MIX_W // QUAD, QUAD, QUAD), F32)] + [pltpu.VMEM((tm, MIX_W), F32)] * 6,
        compiler_params=_cparams(("parallel", "arbitrary")),
        name="rwkv",
    )(z, z, z, z, z, z, z, z, z, z, *consts)


def _causal_conv(p, halo, cw_ref):
    nh = halo.shape[0]
    rowid = lax.broadcasted_iota(jnp.int32, (SUBLANES, p.shape[1]), 0)
    conv = p * cw_ref[CONV_W - 1:CONV_W, :]
    for back in range(1, CONV_W):
        rolled = pltpu.roll(p, back, axis=0)
        head = rolled[0:SUBLANES]
        for rr in range(back):
            head = jnp.where(rowid == rr, halo[nh - back + rr:nh - back + rr + 1, :], head)
        rolled = jnp.concatenate([head, rolled[SUBLANES:]], axis=0)
        conv = conv + rolled * cw_ref[CONV_W - 1 - back:CONV_W - back, :]
    return conv


def _merge_kernel(x_ref, oa_ref, y_ref, g_ref, bonus_ref, cb_ref, cc_ref, cx_ref, cch_ref, cxh_ref,
                  gate0_ref, gate1_ref, gate2_ref, g1_ref, lnw_ref, lnb_ref, bq_ref, cw_ref, wb_ref, wo_ref, o_ref):
    first = pl.program_id(1) == 0
    y = y_ref[...]
    inv_n = 1.0 / HEAD_DIM
    p1 = y.astype(BF16)
    mean = (_head_sum(p1, bq_ref) + _head_sum(y - p1.astype(F32), bq_ref)) * inv_n
    yc = y - mean
    var = _head_sum(yc * yc, bq_ref) * inv_n
    o_rwkv = ((yc * lax.rsqrt(var + GN_EPS) * lnw_ref[...] + lnb_ref[...] + bonus_ref[...].astype(F32))
              * g_ref[...].astype(F32))
    p = cc_ref[...].astype(F32) * cx_ref[...].astype(F32)
    ph = jnp.where(first, 0.0, cch_ref[...].astype(F32) * cxh_ref[...].astype(F32))
    o_conv = cb_ref[...].astype(F32) * _causal_conv(p, ph, cw_ref)
    merged = jnp.zeros(x_ref.shape, F32)
    for bi, (o, gate_ref) in enumerate(((oa_ref[...], gate0_ref), (o_rwkv, gate1_ref), (o_conv, gate2_ref))):
        merged = merged + jax.nn.sigmoid(gate_ref[...].astype(F32)) * _mm(o.astype(BF16), wb_ref[bi])
    o_ref[...] = x_ref[...] + g1_ref[...] * _mm(merged.astype(BF16), wo_ref[...])


def _merge(x, o_attn, y, g, bonus, z, mod5, layer, prm, tm=256):
    b, s, d = x.shape
    tm = min(tm, s)
    tok = lambda w, blk: pl.BlockSpec((None, tm, w), lambda bi, i: (bi, i, blk))
    const = lambda a: pl.BlockSpec(a.shape, lambda bi, i: (0,) * a.ndim)
    consts = [prm[n] for n in ("ln_w", "ln_b", "bones_q", "conv_w", "w_branch", "w_o")]
    gate_blk = Z_GATE_COL // d
    return pl.pallas_call(
        _merge_kernel,
        grid=(b, s // tm),
        in_specs=[tok(d, 0), tok(MIX_W, 0), tok(MIX_W, 0), tok(MIX_W, 0), tok(MIX_W, 0),
                  tok(MIX_W, Z_CB), tok(MIX_W, Z_CC), tok(MIX_W, Z_CX),
                  _halo_spec(MIX_W, Z_CC, tm, BF16_ROWS), _halo_spec(MIX_W, Z_CX, tm, BF16_ROWS),
                  tok(d, gate_blk), tok(d, gate_blk + 1), tok(d, gate_blk + 2),
                  _mod_spec(mod5, layer, 2, 2)] + [const(a) for a in consts],
        out_specs=tok(d, 0),
        out_shape=jax.ShapeDtypeStruct((b, s, d), F32),
        compiler_params=_cparams(("parallel", "parallel")),
        name="merge",
    )(x, o_attn, y, g, bonus, z, z, z, z, z, z, z, z, mod5, *consts)


def _ffn_kernel(x_ref, xh_ref, sh_ref, sc_ref, g2_ref, gain_ref, wa_ref, wg_ref, cw_ref, wd_ref, fin_ref, o_ref,
                h_ref, hh_ref, acc_ref, *, final_norm):
    jf = pl.program_id(2)
    first = pl.program_id(1) == 0

    @pl.when(jf == 0)
    def _():
        h_ref[...] = _norm_mod(x_ref[...], gain_ref[...], sc_ref[...], sh_ref[...]).astype(h_ref.dtype)
        hh_ref[...] = _norm_mod(xh_ref[...], gain_ref[...], sc_ref[...], sh_ref[...]).astype(hh_ref.dtype)
        acc_ref[...] = jnp.zeros_like(acc_ref)

    a = _mm(h_ref[...], wa_ref[...])
    ah = jnp.where(first, 0.0, _mm(hh_ref[...], wa_ref[...]))
    gl = _mm(h_ref[...], wg_ref[...])
    conv = _causal_conv(a, ah, cw_ref)
    u = conv * jax.nn.sigmoid(conv) * gl
    acc_ref[...] += _mm(u.astype(BF16), wd_ref[...])

    @pl.when(jf == pl.num_programs(2) - 1)
    def _():
        out = x_ref[...] + g2_ref[...] * acc_ref[...]
        if final_norm:
            out = out * lax.rsqrt(jnp.mean(out * out, axis=-1, keepdims=True) + NORM_EPS) * fin_ref[...]
        o_ref[...] = out


def _ffn(x, mod5, layer, gain, w_up, conv_w, w_down, final_gain, final_norm, tm=512, tf=1408):
    b, s, d = x.shape
    d_ff = w_down.shape[0]
    nf = d_ff // tf
    tm = min(tm, s)
    step = tm // SUBLANES
    return pl.pallas_call(
        functools.partial(_ffn_kernel, final_norm=final_norm),
        grid=(b, s // tm, nf),
        in_specs=[pl.BlockSpec((None, tm, d), lambda bi, i, j: (bi, i, 0)),
                  pl.BlockSpec((None, SUBLANES, d), lambda bi, i, j: (bi, jnp.maximum(i * step - 1, 0), 0)),
                  _mod_spec(mod5, layer, 3, 3), _mod_spec(mod5, layer, 4, 3), _mod_spec(mod5, layer, 5, 3),
                  pl.BlockSpec((1, d), lambda bi, i, j: (0, 0)),
                  pl.BlockSpec((d, tf), lambda bi, i, j: (0, j)),
                  pl.BlockSpec((d, tf), lambda bi, i, j: (0, j + nf)),
                  pl.BlockSpec((CONV_W, tf), lambda bi, i, j: (0, j)),
                  pl.BlockSpec((tf, d), lambda bi, i, j: (j, 0)),
                  pl.BlockSpec((1, d), lambda bi, i, j: (0, 0))],
        out_specs=pl.BlockSpec((None, tm, d), lambda bi, i, j: (bi, i, 0)),
        out_shape=jax.ShapeDtypeStruct((b, s, d), F32),
        scratch_shapes=[pltpu.VMEM((tm, d), BF16), pltpu.VMEM((SUBLANES, d), BF16), pltpu.VMEM((tm, d), F32)],
        compiler_params=_cparams(("parallel", "parallel", "arbitrary")),
        name="ffn",
    )(x, x, mod5, mod5, mod5, gain.reshape(1, d), w_up, w_up, conv_w, w_down, final_gain.reshape(1, d))


def _split_w_in(w):
    sizes = (MIX_W, MIX_W, MIX_W, HEADS * HEAD_DIM, HEAD_DIM, HEADS,
             MIX_W, MIX_W, MIX_W, LORA_DECAY, LORA_ICLR, LORA_GATE,
             MIX_W, MIX_W, MIX_W)
    out, o = [], 0
    for n in sizes:
        out.append(w[:, o:o + n])
        o += n
    out.append(w[:, o:])
    return out


def _layer_params(l, w_in, rwkv_mu, rwkv_w0, rwkv_w_up, rwkv_a0, rwkv_a_up, rwkv_g_up, rwkv_k_k, rwkv_k_a,
                  rwkv_r_k, rwkv_ln_w, rwkv_ln_b, sc_conv_w, w_branch, w_o):
    d = w_in.shape[1]
    q, k, v, qi, ki, wi, rr, rk, rv, wd, ad, gd, cb, cc, cx, gates = _split_w_in(w_in[l])
    assert Z_GATE_COL % d == 0 and gates.shape[1] == 3 * d
    w_z = jnp.concatenate([q, k, v, qi, rr, rk, rv, cb, cc, cx, gates,
                           ki, ki, wi, jnp.zeros((d, LANES - HEADS), F32), wd, ad, gd], axis=1).astype(BF16)
    mu = rwkv_mu[l]
    row = lambda a: a.reshape(1, -1)
    head_id = jnp.arange(QUAD) // HEAD_DIM
    bones_q = (head_id[:, None] == head_id[None, :]).astype(BF16)
    tri = (jnp.arange(CHUNK)[:, None] >= jnp.arange(CHUNK)[None, :]).astype(BF16)
    zl = lambda n: jnp.zeros((n, MIX_W), F32)
    return dict(
        w_z=w_z,
        mu_r=row(mu[:MIX_W]), mu_k=row(mu[MIX_W:2 * MIX_W]), mu_v=row(mu[2 * MIX_W:3 * MIX_W]),
        mu_s2=row(mu[3 * MIX_W:3 * MIX_W + LORA_DECAY + LORA_ICLR]), mu_s3=row(mu[3 * MIX_W + LORA_DECAY + LORA_ICLR:]),
        w0=row(rwkv_w0[l]), w_up=jnp.concatenate([rwkv_w_up[l], zl(LORA_ICLR)], axis=0).astype(BF16),
        a0=row(rwkv_a0[l]), a_up=jnp.concatenate([zl(LORA_DECAY), rwkv_a_up[l]], axis=0).astype(BF16),
        g_up=rwkv_g_up[l].astype(BF16), k_k=row(rwkv_k_k[l]), k_a=row(rwkv_k_a[l]), r_k=row(rwkv_r_k[l]),
        bones_q=bones_q, tri=tri,
        ln_w=row(rwkv_ln_w[l]), ln_b=row(rwkv_ln_b[l]), conv_w=sc_conv_w[l].T,
        w_branch=w_branch[l].astype(BF16), w_o=w_o[l].astype(BF16),
    )


def kernel(x, c, positions, rel_bias, final_norm, ada_w, ada_b, norm_mix, w_in, rwkv_mu, rwkv_w0, rwkv_w_up,
           rwkv_a0, rwkv_a_up, rwkv_g_up, rwkv_k_k, rwkv_k_a, rwkv_r_k, rwkv_ln_w, rwkv_ln_b, sc_conv_w,
           w_branch, w_o, norm_ffn, ffn_w_up, ffn_conv_w, ffn_w_down):
    depth, d = ada_w.shape[0], x.shape[-1]
    mod5 = _ada_mod(c, ada_w, ada_b)
    bias = _bias_tiles(rel_bias)
    for l in range(depth):
        prm = _layer_params(l, w_in, rwkv_mu, rwkv_w0, rwkv_w_up, rwkv_a0, rwkv_a_up, rwkv_g_up, rwkv_k_k,
                            rwkv_k_a, rwkv_r_k, rwkv_ln_w, rwkv_ln_b, sc_conv_w, w_branch, w_o)
        z = _inproj(x, mod5, l, norm_mix[l], prm["w_z"])
        o_attn = _dsa(z, bias, d)
        y, g, bonus = _rwkv(z, prm, d)
        x = _merge(x, o_attn, y, g, bonus, z, mod5, l, prm)
        x = _ffn(x, mod5, l, norm_ffn[l], ffn_w_up[l].astype(BF16), ffn_conv_w[l].T, ffn_w_down[l].astype(BF16),
                 final_norm, final_norm=(l == depth - 1))
    return x
```

```python
import functools
import math

import jax
import jax.numpy as jnp
from jax import lax
from jax.experimental import pallas as pl
from jax.experimental.pallas import tpu as pltpu

F32 = jnp.float32
BF16 = jnp.bfloat16
HIGHEST = lax.Precision.HIGHEST

MIX_W = 512
HEADS = 8
HEAD_DIM = 64
TOPK_MAX = 256
N_BUCKETS = 32
MAX_DISTANCE = 128
LORA_DECAY = 64
LORA_ICLR = 64
LORA_GATE = 128
CONV_W = 3
NORM_EPS = 1e-6
GN_EPS = 64e-5
NEG_INF = -1e30

LANES = 128
SUBLANES = 8
BF16_ROWS = 16
VMEM_LIMIT = 56 * 1024 * 1024

TQ = 256
CHUNK = 64
RW_TM = 256
QUAD = 4 * HEAD_DIM
assert CHUNK == HEAD_DIM

Z_Q, Z_K, Z_V, Z_QI, Z_RR, Z_RK, Z_RV, Z_CB, Z_CC, Z_CX = range(10)
Z_GATE_COL = 10 * MIX_W


def _z_small_block(d_model, i):
    return (Z_GATE_COL + 3 * d_model) // LANES + i


def _cparams(sem):
    return pltpu.CompilerParams(dimension_semantics=sem, vmem_limit_bytes=VMEM_LIMIT)


def _nt(a, b, precision=None):
    return lax.dot_general(a, b, (((1,), (1,)), ((), ())), precision=precision,
                           preferred_element_type=F32)


def _tn(a, b, precision=None):
    return lax.dot_general(a, b, (((0,), (0,)), ((), ())), precision=precision,
                           preferred_element_type=F32)


def _mm(a, b, precision=None):
    return jnp.dot(a, b, precision=precision, preferred_element_type=F32)


def _bmm(a, b):
    return _mm(a.astype(BF16), b.astype(BF16))


def _mod_kernel(c_ref, w_ref, b_ref, o_ref):
    o_ref[...] = _mm(c_ref[...], w_ref[...], HIGHEST) + b_ref[...]


def _ada_mod(c, ada_w, ada_b):
    depth, d, d6 = ada_w.shape
    b = c.shape[0]
    out = pl.pallas_call(
        _mod_kernel,
        grid=(depth, d6 // d),
        in_specs=[pl.BlockSpec((b, d), lambda l, j: (0, 0)),
                  pl.BlockSpec((None, d, d), lambda l, j: (l, 0, j)),
                  pl.BlockSpec((None, 1, d), lambda l, j: (l, 0, j))],
        out_specs=pl.BlockSpec((None, b, d), lambda l, j: (l, 0, j)),
        out_shape=jax.ShapeDtypeStruct((depth, b, d6), F32),
        compiler_params=_cparams(("parallel", "parallel")),
        name="ada_mod",
    )(c, ada_w, ada_b.reshape(depth, 1, d6))
    return out.reshape(depth, b, d6 // d, 1, d)


def _mod_spec(mod5, layer, which, ngrid):
    d = mod5.shape[-1]
    if ngrid == 2:
        return pl.BlockSpec((None, None, None, 1, d), lambda b, i: (layer, b, which, 0, 0))
    return pl.BlockSpec((None, None, None, 1, d), lambda b, i, j: (layer, b, which, 0, 0))


def _norm_mod(x, gain, scale, shift):
    y = x * lax.rsqrt(jnp.mean(x * x, axis=-1, keepdims=True) + NORM_EPS) * gain
    return y * (1.0 + scale) + shift


def _inproj_kernel(x_ref, sh_ref, sc_ref, g_ref, w_ref, o_ref, h_ref):
    @pl.when(pl.program_id(2) == 0)
    def _():
        h_ref[...] = _norm_mod(x_ref[...], g_ref[...], sc_ref[...], sh_ref[...]).astype(h_ref.dtype)

    o_ref[...] = _mm(h_ref[...], w_ref[...]).astype(o_ref.dtype)


def _inproj(x, mod5, layer, gain, w, tm=1024, n_col_tiles=4):
    b, s, d = x.shape
    zc = w.shape[1]
    tm = min(tm, s)
    tn = zc // n_col_tiles
    assert tn * n_col_tiles == zc and tn % LANES == 0
    return pl.pallas_call(
        _inproj_kernel,
        grid=(b, s // tm, zc // tn),
        in_specs=[pl.BlockSpec((None, tm, d), lambda bi, i, j: (bi, i, 0)),
                  _mod_spec(mod5, layer, 0, 3),
                  _mod_spec(mod5, layer, 1, 3),
                  pl.BlockSpec((1, d), lambda bi, i, j: (0, 0)),
                  pl.BlockSpec((d, tn), lambda bi, i, j: (0, j))],
        out_specs=pl.BlockSpec((None, tm, tn), lambda bi, i, j: (bi, i, j)),
        out_shape=jax.ShapeDtypeStruct((b, s, zc), BF16),
        scratch_shapes=[pltpu.VMEM((tm, d), BF16)],
        compiler_params=_cparams(("parallel", "parallel", "arbitrary")),
        name="inproj",
    )(x, mod5, mod5, gain.reshape(1, d), w)


def _bias_kernel(rb_ref, o_ref):
    which = pl.program_id(0)
    h = pl.program_id(1)
    ri = lax.broadcasted_iota(jnp.int32, (TQ, TQ), 0)
    ci = lax.broadcasted_iota(jnp.int32, (TQ, TQ), 1)
    dist = ci - ri + (1 - which) * TQ
    n = jnp.maximum(dist, 0)
    max_exact = N_BUCKETS // 2
    nf = jnp.maximum(n, 1).astype(F32)
    large = max_exact + (jnp.log(nf / max_exact) / math.log(MAX_DISTANCE / max_exact)
                         * (N_BUCKETS - max_exact)).astype(jnp.int32)
    large = jnp.minimum(large, N_BUCKETS - 1)
    bucket = jnp.where(n < max_exact, n, large)
    far = rb_ref[N_BUCKETS - 1, h]
    acc = jnp.zeros((TQ, TQ), F32)
    for bkt in range(N_BUCKETS - 1):
        acc = jnp.where(bucket == bkt, rb_ref[bkt, h] - far, acc)
    o_ref[...] = acc * LOG2E


def _bias_tiles(rel_bias):
    assert TQ >= MAX_DISTANCE
    return pl.pallas_call(
        _bias_kernel,
        grid=(2, HEADS),
        in_specs=[pl.BlockSpec(memory_space=pltpu.SMEM)],
        out_specs=pl.BlockSpec((None, None, TQ, TQ), lambda w, h: (w, h, 0, 0)),
        out_shape=jax.ShapeDtypeStruct((2, HEADS, TQ, TQ), F32),
        compiler_params=_cparams(("parallel", "parallel")),
        name="bias_tiles",
    )(rel_bias)


BISECT_MAX_IT = 300
COARSE_STEPS = 10
FINE_STEPS_PER_CHECK = 2
LOG2E = math.log2(math.e)
ONES_ROWS = BF16_ROWS


def _row_groups(t):
    return [t[r * SUBLANES:(r + 1) * SUBLANES, :] for r in range(t.shape[0] // SUBLANES)]


def _dsa_kernel(q_ref, k_ref, v_ref, qi_ref, ki_ref, wi_ref, bias_ref, o_ref,
                sc_ref, scb_ref, qm_ref, qim_ref, wt_ref, vt_ref, m_ref, l_ref, acc_ref, s_ref, *, n_keep, seq):
    j = pl.program_id(1)
    nt = j + 1
    kf = float(n_keep)
    lane = lax.broadcasted_iota(jnp.int32, (TQ, LANES), 1)
    att_scale = HEAD_DIM ** -0.5 * LOG2E
    w_scale = (HEADS ** -0.5) * (HEAD_DIM ** -0.5)

    for h in range(HEADS):
        p, odd = divmod(h, 2)
        hm = (lane >= HEAD_DIM) if odd else (lane < HEAD_DIM)
        qs = q_ref[:, p * LANES:(p + 1) * LANES]
        qm_ref[h] = jnp.where(hm, qs, jnp.zeros_like(qs)) * att_scale
        qis = qi_ref[:, p * LANES:(p + 1) * LANES]
        qim_ref[h] = jnp.where(hm, qis, jnp.zeros_like(qis))
    wt_ref[...] = wi_ref[...].astype(F32).T * w_scale
    v_t = v_ref[...].T
    for h in range(HEADS):
        vt_ref[j, h, 0:HEAD_DIM, :] = v_t[h * HEAD_DIM:(h + 1) * HEAD_DIM, :]
        vt_ref[j, h, HEAD_DIM:HEAD_DIM + ONES_ROWS, :] = jnp.ones((ONES_ROWS, TQ), BF16)

    key = lax.broadcasted_iota(jnp.int32, (TQ, TQ), 0)
    qry = lax.broadcasted_iota(jnp.int32, (TQ, TQ), 1) + j * TQ

    def idx_tile(kt, carry):
        ki_t = ki_ref[pl.ds(pl.multiple_of(kt * TQ, TQ), TQ), :]
        acc = jnp.zeros((TQ, TQ), F32)
        for h in range(HEADS):
            acc = acc + jnp.maximum(_nt(ki_t, qim_ref[h]), 0.0) * wt_ref[h:h + 1, :]
        sc_ref[kt] = jnp.where(key + kt * TQ <= qry, acc, NEG_INF)
        return carry

    lax.fori_loop(0, nt, idx_tile, 0)

    def floor_bf16(x):
        xi = lax.bitcast_convert_type(x, jnp.int32)
        xi = jnp.where(xi < 0, xi + jnp.int32(0xFFFF), xi) & jnp.int32(-0x10000)
        return lax.bitcast_convert_type(xi, F32).astype(BF16)

    def stats_tile(kt, c):
        mx, mn, mp, cp, cn = c
        scb_ref[kt] = floor_bf16(sc_ref[kt])
        for g in _row_groups(sc_ref[kt]):
            pos = g > 0.0
            mx = jnp.maximum(mx, g)
            mn = jnp.minimum(mn, jnp.where(g > 0.5 * NEG_INF, g, -NEG_INF))
            mp = jnp.minimum(mp, jnp.where(pos, g, -NEG_INF))
            cp = cp + jnp.where(pos, 1.0, 0.0)
            cn = cn + jnp.where(g >= 0.0, 1.0, 0.0)
        return mx, mn, mp, cp, cn

    part = lambda v: jnp.full((SUBLANES, TQ), v, F32)
    mx, mn, mp, cp, cn = lax.fori_loop(0, nt, stats_tile,
                                       (part(NEG_INF), part(-NEG_INF), part(-NEG_INF), part(0.0), part(0.0)))
    rmax = jnp.max(mx, axis=0, keepdims=True)
    rmin = jnp.min(mn, axis=0, keepdims=True)
    minpos = jnp.min(mp, axis=0, keepdims=True)
    cpos = jnp.sum(cp, axis=0, keepdims=True)
    cnn = jnp.sum(cn, axis=0, keepdims=True)
    nvalid = (lax.broadcasted_iota(jnp.int32, (1, TQ), 1) + j * TQ + 1).astype(F32)
    small = nvalid <= kf

    def count_ge(thr):
        def body(kt, acc):
            for g in _row_groups(sc_ref[kt]):
                acc = acc + jnp.where(g >= thr, 1.0, 0.0)
            return acc
        acc = lax.fori_loop(0, nt, body, jnp.zeros((SUBLANES, TQ), F32))
        return jnp.sum(acc, axis=0, keepdims=True)

    ztie = jnp.logical_and(cpos < kf, cnn >= kf)
    pos_side = cpos >= kf
    hi_top = rmax + jnp.maximum(jnp.abs(rmax) * 1e-6, 1e-30)
    lo0 = jnp.where(small, 0.5 * NEG_INF, jnp.where(ztie, 0.0, jnp.where(pos_side, minpos, rmin)))
    hi0 = jnp.where(small, -NEG_INF, jnp.where(ztie, minpos, jnp.where(pos_side, hi_top, 0.0)))
    clo0 = jnp.where(jnp.logical_or(small, ztie), jnp.where(small, nvalid, cnn), jnp.where(pos_side, cpos, nvalid))
    chi0 = jnp.where(small, 0.0, jnp.where(ztie, cpos, jnp.where(pos_side, 0.0, cnn)))
    done0 = jnp.where(jnp.logical_or(jnp.logical_or(small, ztie), clo0 == kf), 1.0, 0.0)

    def bisect_step(state, mid, cnt, usable):
        lo, hi, clo, chi, done = state
        act = jnp.logical_and(done < 0.5, usable)
        ge = cnt >= kf
        up_lo = jnp.logical_and(act, ge)
        up_hi = jnp.logical_and(act, jnp.logical_not(ge))
        lo = jnp.where(up_lo, mid, lo)
        clo = jnp.where(up_lo, cnt, clo)
        hi = jnp.where(up_hi, mid, hi)
        chi = jnp.where(up_hi, cnt, chi)
        done = jnp.where(jnp.logical_and(act, cnt == kf), 1.0, done)
        return lo, hi, clo, chi, done

    def count_ge_bf16(thr):
        one, zero = jnp.ones((), BF16), jnp.zeros((), BF16)

        def body(kt, acc):
            t = scb_ref[kt]
            part = jnp.zeros((BF16_ROWS, TQ), BF16)
            for r in range(TQ // BF16_ROWS):
                part = part + jnp.where(t[r * BF16_ROWS:(r + 1) * BF16_ROWS, :] >= thr, one, zero)
            return acc + part.astype(F32)
        acc = lax.fori_loop(0, nt, body, jnp.zeros((BF16_ROWS, TQ), F32))
        return jnp.sum(acc, axis=0, keepdims=True)

    def coarse_step(_, state):
        lo, hi = state[0], state[1]
        mid_b = (lo + 0.5 * (hi - lo)).astype(BF16)
        mid = mid_b.astype(F32)
        usable = jnp.logical_and(mid > lo, mid < hi)
        return bisect_step(state, mid, count_ge_bf16(mid_b), usable)

    state = lax.fori_loop(0, COARSE_STEPS, coarse_step, (lo0, hi0, clo0, chi0, done0))

    def bis_cond(c):
        return jnp.logical_and(c[0] < BISECT_MAX_IT, c[2] > 0.0)

    def bis_body(c):
        it, state, _ = c
        for _ in range(FINE_STEPS_PER_CHECK):
            lo, hi = state[0], state[1]
            mid = lo + 0.5 * (hi - lo)
            stalled = jnp.logical_or(mid <= lo, mid >= hi)
            state = bisect_step(state, mid, count_ge(mid), jnp.logical_not(stalled))
            state = state[:4] + (jnp.where(stalled, 1.0, state[4]),)
        return it + FINE_STEPS_PER_CHECK, state, jnp.sum(1.0 - state[4])

    _, (lo, hi, clo, chi, _), _ = lax.while_loop(
        bis_cond, bis_body, (jnp.int32(0), state, jnp.sum(1.0 - state[4])))

    tie = jnp.logical_and(clo > kf, jnp.logical_not(small))
    n_tie_iter = int(math.ceil(math.log2(seq))) + 1
    key_f = key.astype(F32)

    def tie_fn():
        need = kf - chi

        def count_tie_le(m):
            def body(kt, acc):
                t = sc_ref[kt]
                s_abs = key_f + (kt * TQ).astype(F32)
                inside = jnp.where(t >= lo, jnp.where(t < hi, jnp.where(s_abs <= m, 1.0, 0.0), 0.0), 0.0)
                for g in _row_groups(inside):
                    acc = acc + g
                return acc
            acc = lax.fori_loop(0, nt, body, jnp.zeros((SUBLANES, TQ), F32))
            return jnp.sum(acc, axis=0, keepdims=True)

        def tb(_, c):
            loi, hii = c
            midi = jnp.floor((loi + hii) * 0.5)
            ge = count_tie_le(midi) >= need
            return jnp.where(ge, loi, midi), jnp.where(ge, midi, hii)

        _, hii = lax.fori_loop(0, n_tie_iter, tb,
                               (jnp.full((1, TQ), -1.0, F32), jnp.full((1, TQ), seq - 1.0, F32)))
        return jnp.where(tie, hii, float(seq))

    mstar = lax.cond(jnp.sum(jnp.where(tie, 1.0, 0.0)) > 0.0, tie_fn,
                     lambda: jnp.full((1, TQ), float(seq), F32))

    def mask_tile(kt, carry):
        t = sc_ref[kt]
        s_abs = key_f + (kt * TQ).astype(F32)
        keep_tie = jnp.where(s_abs <= mstar, 0.0, NEG_INF)
        sc_ref[kt] = jnp.where(t >= lo, jnp.where(t >= hi, 0.0, keep_tie), NEG_INF)
        return carry

    lax.fori_loop(0, nt, mask_tile, 0)

    m_ref[...] = jnp.full(m_ref.shape, NEG_INF, F32)
    l_ref[...] = jnp.zeros(l_ref.shape, F32)
    acc_ref[...] = jnp.zeros(acc_ref.shape, F32)

    def attn_tiles(kts, near):
        tile_max = [None] * HEADS
        for i, kt in enumerate(kts):
            rows = pl.ds(pl.multiple_of(kt * TQ, TQ), TQ)
            mask_add = sc_ref[kt]
            for h in range(HEADS):
                p = h // 2
                s = _nt(k_ref[rows, p * LANES:(p + 1) * LANES], qm_ref[h]) + mask_add
                if near:
                    s = s + bias_ref[kt - j + 1, h]
                s_ref[i, h] = s
                mx = jnp.max(s, axis=0, keepdims=True)
                tile_max[h] = mx if i == 0 else jnp.maximum(tile_max[h], mx)
        for h in range(HEADS):
            m_old = m_ref[h]
            m_new = jnp.maximum(m_old, tile_max[h])
            alpha = jnp.exp2(m_old - m_new)
            pv = None
            for i, kt in enumerate(kts):
                part = _mm(vt_ref[kt, h], jnp.exp2(s_ref[i, h] - m_new).astype(BF16))
                pv = part if i == 0 else pv + part
            m_ref[h] = m_new
            l_ref[h] = alpha * l_ref[h] + pv[HEAD_DIM:HEAD_DIM + 1, :]
            acc_ref[h] = alpha * acc_ref[h] + pv[0:HEAD_DIM, :]

    n_far = jnp.maximum(j - 1, 0)
    n_far_pairs = lax.shift_right_logical(n_far, 1)

    def far_pair(i, carry):
        attn_tiles([2 * i, 2 * i + 1], near=False)
        return carry

    def far_single(kt, carry):
        attn_tiles([kt], near=False)
        return carry

    lax.fori_loop(0, n_far_pairs, far_pair, 0)
    lax.fori_loop(2 * n_far_pairs, n_far, far_single, 0)

    @pl.when(j >= 1)
    def _():
        attn_tiles([j - 1, j], near=True)

    @pl.when(j == 0)
    def _():
        attn_tiles([j], near=True)

    for p in range(HEADS // 2):
        pair = jnp.concatenate([acc_ref[2 * p] / l_ref[2 * p], acc_ref[2 * p + 1] / l_ref[2 * p + 1]], axis=0)
        o_ref[:, p * LANES:(p + 1) * LANES] = pair.T.astype(o_ref.dtype)


def _dsa(z, bias, d_model):
    b, s, _ = z.shape
    n_keep = min(TOPK_MAX, s // 4)
    kern = functools.partial(_dsa_kernel, n_keep=n_keep, seq=s)
    ki_blk, wi_blk = _z_small_block(d_model, 0), _z_small_block(d_model, 1)
    return pl.pallas_call(
        kern,
        grid=(b, s // TQ),
        in_specs=[pl.BlockSpec((None, TQ, MIX_W), lambda bi, j: (bi, j, Z_Q)),
                  pl.BlockSpec((None, s, MIX_W), lambda bi, j: (bi, 0, Z_K)),
                  pl.BlockSpec((None, TQ, MIX_W), lambda bi, j: (bi, j, Z_V)),
                  pl.BlockSpec((None, TQ, MIX_W), lambda bi, j: (bi, j, Z_QI)),
                  pl.BlockSpec((None, s, LANES), lambda bi, j: (bi, 0, ki_blk)),
                  pl.BlockSpec((None, TQ, LANES), lambda bi, j: (bi, j, wi_blk)),
                  pl.BlockSpec((2, HEADS, TQ, TQ), lambda bi, j: (0, 0, 0, 0))],
        out_specs=pl.BlockSpec((None, TQ, MIX_W), lambda bi, j: (bi, j, 0)),
        out_shape=jax.ShapeDtypeStruct((b, s, MIX_W), BF16),
        scratch_shapes=[pltpu.VMEM((s // TQ, TQ, TQ), F32),
                        pltpu.VMEM((s // TQ, TQ, TQ), BF16),
                        pltpu.VMEM((HEADS, TQ, LANES), BF16),
                        pltpu.VMEM((HEADS, TQ, LANES), BF16),
                        pltpu.VMEM((LANES, TQ), F32),
                        pltpu.VMEM((s // TQ, HEADS, HEAD_DIM + ONES_ROWS, TQ), BF16),
                        pltpu.VMEM((HEADS, 1, TQ), F32),
                        pltpu.VMEM((HEADS, 1, TQ), F32),
                        pltpu.VMEM((HEADS, HEAD_DIM, TQ), F32),
                        pltpu.VMEM((2, HEADS, TQ, TQ), F32)],
        compiler_params=_cparams(("parallel", "arbitrary")),
        name="dsa",
    )(z, z, z, z, z, z, bias)


def _halo_spec(width, blk, tm, rows):
    step = tm // rows
    return pl.BlockSpec((None, rows, width), lambda bi, i: (bi, jnp.maximum(i * step - 1, 0), blk))


def _shift_lerp(cur_ref, halo_ref, mu, first):
    cur = cur_ref[...].astype(F32)
    nh = halo_ref.shape[0]
    prev_last = jnp.where(first, 0.0, halo_ref[nh - 1:nh, :].astype(F32))
    rolled = pltpu.roll(cur, 1, axis=0)
    rowid = lax.broadcasted_iota(jnp.int32, cur.shape, 0)
    sh = jnp.where(rowid == 0, prev_last, rolled)
    return cur + (sh - cur) * mu


def _head_sum(x, bq_ref):
    xb = x.astype(BF16)
    return jnp.concatenate([_mm(xb[:, q * QUAD:(q + 1) * QUAD], bq_ref[...]) for q in range(MIX_W // QUAD)], axis=1)


def _rwkv_kernel(r_ref, k_ref, v_ref, s2_ref, s3_ref, rh_ref, kh_ref, vh_ref, s2h_ref, s3h_ref,
                 mu_r, mu_k, mu_v, mu_s2, mu_s3, w0_ref, wup_ref, a0_ref, aup_ref, gup_ref,
                 kk_ref, ka_ref, rk_ref, bq_ref, tri_ref,
                 y_ref, g_ref, bonus_ref,
                 st_ref, rs, ls, ks, vs, kks, kbs):
    first = pl.program_id(1) == 0

    @pl.when(first)
    def _():
        st_ref[...] = jnp.zeros_like(st_ref)

    r = _shift_lerp(r_ref, rh_ref, mu_r[...], first)
    k = _shift_lerp(k_ref, kh_ref, mu_k[...], first)
    v = _shift_lerp(v_ref, vh_ref, mu_v[...], first)
    s2 = _shift_lerp(s2_ref, s2h_ref, mu_s2[...], first)
    s3 = _shift_lerp(s3_ref, s3h_ref, mu_s3[...], first)
    xw = w0_ref[...] + _bmm(jnp.tanh(s2), wup_ref[...])
    softplus = jnp.maximum(-xw, 0.0) + jnp.log1p(jnp.exp(-jnp.abs(xw)))
    ld = -jnp.exp(-softplus - 0.5)
    af = jax.nn.sigmoid(a0_ref[...] + _bmm(s2, aup_ref[...]))
    g_ref[...] = _bmm(jax.nn.sigmoid(s3), gup_ref[...]).astype(g_ref.dtype)
    kkr = k * kk_ref[...]
    kkn = kkr / jnp.maximum(jnp.sqrt(_head_sum(kkr * kkr, bq_ref)), 1e-12)
    kmod = k * (1.0 + (af - 1.0) * ka_ref[...])
    bonus_ref[...] = (_head_sum(r * kmod * rk_ref[...], bq_ref) * v).astype(bonus_ref.dtype)
    rs[...] = r
    ls[...] = ld
    ks[...] = kmod
    vs[...] = v
    kks[...] = kkn
    kbs[...] = kkn * af

    ri = lax.broadcasted_iota(jnp.int32, (QUAD, QUAD), 0)
    ci = lax.broadcasted_iota(jnp.int32, (QUAD, QUAD), 1)
    same_head = (ri // HEAD_DIM) == (ci // HEAD_DIM)
    strict = jnp.logical_and(same_head, (ri % CHUNK) > (ci % CHUNK))
    incl = jnp.logical_and(same_head, (ri % CHUNK) >= (ci % CHUNK))
    eye = jnp.where(ri == ci, 1.0, 0.0)

    def stack(x):
        return jnp.where(same_head, jnp.concatenate([x] * 4, axis=0), 0.0).astype(BF16)

    def tile4(x):
        return jnp.concatenate([x] * 4, axis=0).astype(BF16)

    def unstack(x):
        return (x[0:CHUNK] + x[CHUNK:2 * CHUNK]) + (x[2 * CHUNK:3 * CHUNK] + x[3 * CHUNK:4 * CHUNK])

    def chunk_operands(c):
        rows = pl.ds(pl.multiple_of(c * CHUNK, CHUNK), CHUNK)
        ldc = ls[rows, :]
        p1 = ldc.astype(BF16)
        e1 = ldc - p1.astype(F32)
        p2 = e1.astype(BF16)
        p3 = (e1 - p2.astype(F32)).astype(BF16)
        tri = tri_ref[...]
        cl = (_mm(tri, p1) + _mm(tri, p2)) + _mm(tri, p3)
        cl_end = cl[CHUNK - 1:CHUNK, :]
        e_in = jnp.exp(cl)
        e_out = jnp.exp(-cl)
        e_end = jnp.exp(cl_end - cl)
        rt_all = rs[rows, :] * e_in
        at_all = -kks[rows, :] * jnp.exp(cl - ldc)
        bt_all = kbs[rows, :] * e_out
        kt_all = ks[rows, :] * e_out
        bg_all = (kbs[rows, :] * e_end).astype(BF16)
        kg_all = (ks[rows, :] * e_end).astype(BF16)
        v_all = vs[rows, :]
        gam_all = jnp.exp(cl_end)
        units = []
        for q in range(MIX_W // QUAD):
            sl = slice(q * QUAD, (q + 1) * QUAD)
            units.append(dict(rows=rows, sl=sl, q=q, rt=rt_all[:, sl], vv=v_all[:, sl].astype(BF16),
                              a4=stack(at_all[:, sl]), r4=stack(rt_all[:, sl]), v4=stack(v_all[:, sl]),
                              bt4=tile4(bt_all[:, sl]), kt4=tile4(kt_all[:, sl]),
                              bg=bg_all[:, sl], kg=kg_all[:, sl], gam=gam_all[:, sl]))
        return units

    n_doublings = int(math.log2(CHUNK)) - 1

    def chunk_pair(cp, carry):
        us = chunk_operands(2 * cp) + chunk_operands(2 * cp + 1)
        for u in us:
            u["m_ab"] = jnp.where(strict, _nt(u["a4"], u["bt4"]), 0.0)
        for u in us:
            u["m_ak"] = jnp.where(strict, _nt(u["a4"], u["kt4"]), 0.0)
        for u in us:
            u["m_rb"] = jnp.where(incl, _nt(u["r4"], u["bt4"]), 0.0).astype(BF16)
        for u in us:
            u["m_rk"] = jnp.where(incl, _nt(u["r4"], u["kt4"]), 0.0).astype(BF16)
        for u in us:
            u["pw"] = u["m_ab"]
            u["inv"] = eye + u["m_ab"]
        for _ in range(n_doublings):
            for u in us:
                u["pw"] = _bmm(u["pw"], u["pw"])
            for u in us:
                u["inv"] = u["inv"] + _bmm(u["inv"], u["pw"])
        for u in us:
            u["inv"] = u["inv"].astype(BF16)
            u["mv4"] = _bmm(u["m_ak"], u["v4"])
        for u in us:
            u["ah4"] = _mm(u["inv"], u["a4"])
        for u in us:
            u["uh4"] = _bmm(u["inv"], u["mv4"])
        for u in us:
            u["ry"] = (u["rt"] + unstack(_bmm(u["m_rb"], u["ah4"]))).astype(BF16)
        for u in us:
            u["y0"] = unstack(_bmm(u["m_rb"], u["uh4"]) + _mm(u["m_rk"], u["v4"]))
        for u in us:
            ah, uh = unstack(u["ah4"]).astype(BF16), unstack(u["uh4"]).astype(BF16)
            u["g_low"] = jnp.where(same_head, _tn(u["bg"], ah), 0.0).astype(BF16)
            u["h_t"] = jnp.where(same_head, _tn(uh, u["bg"]) + _tn(u["vv"], u["kg"]), 0.0)
        for u in us:
            st = st_ref[u["q"]]
            stb = st.astype(BF16)
            y_ref[u["rows"], u["sl"]] = _nt(u["ry"], stb) + u["y0"]
            st_ref[u["q"]] = st * u["gam"] + _nt(stb, u["g_low"]) + u["h_t"]
        return carry

    lax.fori_loop(0, r_ref.shape[0] // (2 * CHUNK), chunk_pair, 0)


def _rwkv(z, prm, d_model):
    b, s, _ = z.shape
    tm = min(RW_TM, s)
    s2_blk, s3_blk = _z_small_block(d_model, 2), _z_small_block(d_model, 3)
    tok = lambda blk: pl.BlockSpec((None, tm, MIX_W), lambda bi, i: (bi, i, blk))
    tok128 = lambda blk: pl.BlockSpec((None, tm, LANES), lambda bi, i: (bi, i, blk))
    const = lambda a: pl.BlockSpec(a.shape, lambda bi, i: (0,) * a.ndim)
    consts = [prm[n] for n in ("mu_r", "mu_k", "mu_v", "mu_s2", "mu_s3", "w0", "w_up", "a0", "a_up", "g_up",
                               "k_k", "k_a", "r_k", "bones_q", "tri")]
    out_spec = pl.BlockSpec((None, tm, MIX_W), lambda bi, i: (bi, i, 0))
    return pl.pallas_call(
        _rwkv_kernel,
        grid=(b, s // tm),
        in_specs=[tok(Z_RR), tok(Z_RK), tok(Z_RV), tok128(s2_blk), tok128(s3_blk),
                  _halo_spec(MIX_W, Z_RR, tm, BF16_ROWS), _halo_spec(MIX_W, Z_RK, tm, BF16_ROWS),
                  _halo_spec(MIX_W, Z_RV, tm, BF16_ROWS), _halo_spec(LANES, s2_blk, tm, BF16_ROWS),
                  _halo_spec(LANES, s3_blk, tm, BF16_ROWS)] + [const(a) for a in consts],
        out_specs=[out_spec, out_spec, out_spec],
        out_shape=[jax.ShapeDtypeStruct((b, s, MIX_W), F32), jax.ShapeDtypeStruct((b, s, MIX_W), BF16),
                   jax.ShapeDtypeStruct((b, s, MIX_W), BF16)],
        scratch_shapes=[pltpu.VMEM((MIX_W // QUAD, QUAD, QUAD), F32)] + [pltpu.VMEM((tm, MIX_W), F32)] * 6,
        compiler_params=_cparams(("parallel", "arbitrary")),
        name="rwkv",
    )(z, z, z, z, z, z, z, z, z, z, *consts)


def _causal_conv(p, halo, cw_ref):
    nh = halo.shape[0]
    rowid = lax.broadcasted_iota(jnp.int32, (SUBLANES, p.shape[1]), 0)
    conv = p * cw_ref[CONV_W - 1:CONV_W, :]
    for back in range(1, CONV_W):
        rolled = pltpu.roll(p, back, axis=0)
        head = rolled[0:SUBLANES]
        for rr in range(back):
            head = jnp.where(rowid == rr, halo[nh - back + rr:nh - back + rr + 1, :], head)
        rolled = jnp.concatenate([head, rolled[SUBLANES:]], axis=0)
        conv = conv + rolled * cw_ref[CONV_W - 1 - back:CONV_W - back, :]
    return conv


def _merge_kernel(x_ref, oa_ref, y_ref, g_ref, bonus_ref, cb_ref, cc_ref, cx_ref, cch_ref, cxh_ref,
                  gate0_ref, gate1_ref, gate2_ref, g1_ref, lnw_ref, lnb_ref, bq_ref, cw_ref, wb_ref, wo_ref, o_ref):
    first = pl.program_id(1) == 0
    y = y_ref[...]
    inv_n = 1.0 / HEAD_DIM
    p1 = y.astype(BF16)
    mean = (_head_sum(p1, bq_ref) + _head_sum(y - p1.astype(F32), bq_ref)) * inv_n
    yc = y - mean
    var = _head_sum(yc * yc, bq_ref) * inv_n
    o_rwkv = ((yc * lax.rsqrt(var + GN_EPS) * lnw_ref[...] + lnb_ref[...] + bonus_ref[...].astype(F32))
              * g_ref[...].astype(F32))
    p = cc_ref[...].astype(F32) * cx_ref[...].astype(F32)
    ph = jnp.where(first, 0.0, cch_ref[...].astype(F32) * cxh_ref[...].astype(F32))
    o_conv = cb_ref[...].astype(F32) * _causal_conv(p, ph, cw_ref)
    merged = jnp.zeros(x_ref.shape, F32)
    for bi, (o, gate_ref) in enumerate(((oa_ref[...], gate0_ref), (o_rwkv, gate1_ref), (o_conv, gate2_ref))):
        merged = merged + jax.nn.sigmoid(gate_ref[...].astype(F32)) * _mm(o.astype(BF16), wb_ref[bi])
    o_ref[...] = x_ref[...] + g1_ref[...] * _mm(merged.astype(BF16), wo_ref[...])


def _merge(x, o_attn, y, g, bonus, z, mod5, layer, prm, tm=256):
    b, s, d = x.shape
    tm = min(tm, s)
    tok = lambda w, blk: pl.BlockSpec((None, tm, w), lambda bi, i: (bi, i, blk))
    const = lambda a: pl.BlockSpec(a.shape, lambda bi, i: (0,) * a.ndim)
    consts = [prm[n] for n in ("ln_w", "ln_b", "bones_q", "conv_w", "w_branch", "w_o")]
    gate_blk = Z_GATE_COL // d
    return pl.pallas_call(
        _merge_kernel,
        grid=(b, s // tm),
        in_specs=[tok(d, 0), tok(MIX_W, 0), tok(MIX_W, 0), tok(MIX_W, 0), tok(MIX_W, 0),
                  tok(MIX_W, Z_CB), tok(MIX_W, Z_CC), tok(MIX_W, Z_CX),
                  _halo_spec(MIX_W, Z_CC, tm, BF16_ROWS), _halo_spec(MIX_W, Z_CX, tm, BF16_ROWS),
                  tok(d, gate_blk), tok(d, gate_blk + 1), tok(d, gate_blk + 2),
                  _mod_spec(mod5, layer, 2, 2)] + [const(a) for a in consts],
        out_specs=tok(d, 0),
        out_shape=jax.ShapeDtypeStruct((b, s, d), F32),
        compiler_params=_cparams(("parallel", "parallel")),
        name="merge",
    )(x, o_attn, y, g, bonus, z, z, z, z, z, z, z, z, mod5, *consts)


def _ffn_kernel(x_ref, xh_ref, sh_ref, sc_ref, g2_ref, gain_ref, wa_ref, wg_ref, cw_ref, wd_ref, fin_ref, o_ref,
                h_ref, hh_ref, acc_ref, *, final_norm):
    jf = pl.program_id(2)
    first = pl.program_id(1) == 0

    @pl.when(jf == 0)
    def _():
        h_ref[...] = _norm_mod(x_ref[...], gain_ref[...], sc_ref[...], sh_ref[...]).astype(h_ref.dtype)
        hh_ref[...] = _norm_mod(xh_ref[...], gain_ref[...], sc_ref[...], sh_ref[...]).astype(hh_ref.dtype)
        acc_ref[...] = jnp.zeros_like(acc_ref)

    a = _mm(h_ref[...], wa_ref[...])
    ah = jnp.where(first, 0.0, _mm(hh_ref[...], wa_ref[...]))
    gl = _mm(h_ref[...], wg_ref[...])
    conv = _causal_conv(a, ah, cw_ref)
    u = conv * jax.nn.sigmoid(conv) * gl
    acc_ref[...] += _mm(u.astype(BF16), wd_ref[...])

    @pl.when(jf == pl.num_programs(2) - 1)
    def _():
        out = x_ref[...] + g2_ref[...] * acc_ref[...]
        if final_norm:
            out = out * lax.rsqrt(jnp.mean(out * out, axis=-1, keepdims=True) + NORM_EPS) * fin_ref[...]
        o_ref[...] = out


def _ffn(x, mod5, layer, gain, w_up, conv_w, w_down, final_gain, final_norm, tm=512, tf=1408):
    b, s, d = x.shape
    d_ff = w_down.shape[0]
    nf = d_ff // tf
    tm = min(tm, s)
    step = tm // SUBLANES
    return pl.pallas_call(
        functools.partial(_ffn_kernel, final_norm=final_norm),
        grid=(b, s // tm, nf),
        in_specs=[pl.BlockSpec((None, tm, d), lambda bi, i, j: (bi, i, 0)),
                  pl.BlockSpec((None, SUBLANES, d), lambda bi, i, j: (bi, jnp.maximum(i * step - 1, 0), 0)),
                  _mod_spec(mod5, layer, 3, 3), _mod_spec(mod5, layer, 4, 3), _mod_spec(mod5, layer, 5, 3),
                  pl.BlockSpec((1, d), lambda bi, i, j: (0, 0)),
                  pl.BlockSpec((d, tf), lambda bi, i, j: (0, j)),
                  pl.BlockSpec((d, tf), lambda bi, i, j: (0, j + nf)),
                  pl.BlockSpec((CONV_W, tf), lambda bi, i, j: (0, j)),
                  pl.BlockSpec((tf, d), lambda bi, i, j: (j, 0)),
                  pl.BlockSpec((1, d), lambda bi, i, j: (0, 0))],
        out_specs=pl.BlockSpec((None, tm, d), lambda bi, i, j: (bi, i, 0)),
        out_shape=jax.ShapeDtypeStruct((b, s, d), F32),
        scratch_shapes=[pltpu.VMEM((tm, d), BF16), pltpu.VMEM((SUBLANES, d), BF16), pltpu.VMEM((tm, d), F32)],
        compiler_params=_cparams(("parallel", "parallel", "arbitrary")),
        name="ffn",
    )(x, x, mod5, mod5, mod5, gain.reshape(1, d), w_up, w_up, conv_w, w_down, final_gain.reshape(1, d))


def _split_w_in(w):
    sizes = (MIX_W, MIX_W, MIX_W, HEADS * HEAD_DIM, HEAD_DIM, HEADS,
             MIX_W, MIX_W, MIX_W, LORA_DECAY, LORA_ICLR, LORA_GATE,
             MIX_W, MIX_W, MIX_W)
    out, o = [], 0
    for n in sizes:
        out.append(w[:, o:o + n])
        o += n
    out.append(w[:, o:])
    return out


def _layer_params(l, w_in, rwkv_mu, rwkv_w0, rwkv_w_up, rwkv_a0, rwkv_a_up, rwkv_g_up, rwkv_k_k, rwkv_k_a,
                  rwkv_r_k, rwkv_ln_w, rwkv_ln_b, sc_conv_w, w_branch, w_o):
    d = w_in.shape[1]
    q, k, v, qi, ki, wi, rr, rk, rv, wd, ad, gd, cb, cc, cx, gates = _split_w_in(w_in[l])
    assert Z_GATE_COL % d == 0 and gates.shape[1] == 3 * d
    w_z = jnp.concatenate([q, k, v, qi, rr, rk, rv, cb, cc, cx, gates,
                           ki, ki, wi, jnp.zeros((d, LANES - HEADS), F32), wd, ad, gd], axis=1).astype(BF16)
    mu = rwkv_mu[l]
    row = lambda a: a.reshape(1, -1)
    head_id = jnp.arange(QUAD) // HEAD_DIM
    bones_q = (head_id[:, None] == head_id[None, :]).astype(BF16)
    tri = (jnp.arange(CHUNK)[:, None] >= jnp.arange(CHUNK)[None, :]).astype(BF16)
    zl = lambda n: jnp.zeros((n, MIX_W), F32)
    return dict(
        w_z=w_z,
        mu_r=row(mu[:MIX_W]), mu_k=row(mu[MIX_W:2 * MIX_W]), mu_v=row(mu[2 * MIX_W:3 * MIX_W]),
        mu_s2=row(mu[3 * MIX_W:3 * MIX_W + LORA_DECAY + LORA_ICLR]), mu_s3=row(mu[3 * MIX_W + LORA_DECAY + LORA_ICLR:]),
        w0=row(rwkv_w0[l]), w_up=jnp.concatenate([rwkv_w_up[l], zl(LORA_ICLR)], axis=0).astype(BF16),
        a0=row(rwkv_a0[l]), a_up=jnp.concatenate([zl(LORA_DECAY), rwkv_a_up[l]], axis=0).astype(BF16),
        g_up=rwkv_g_up[l].astype(BF16), k_k=row(rwkv_k_k[l]), k_a=row(rwkv_k_a[l]), r_k=row(rwkv_r_k[l]),
        bones_q=bones_q, tri=tri,
        ln_w=row(rwkv_ln_w[l]), ln_b=row(rwkv_ln_b[l]), conv_w=sc_conv_w[l].T,
        w_branch=w_branch[l].astype(BF16), w_o=w_o[l].astype(BF16),
    )


def kernel(x, c, positions, rel_bias, final_norm, ada_w, ada_b, norm_mix, w_in, rwkv_mu, rwkv_w0, rwkv_w_up,
           rwkv_a0, rwkv_a_up, rwkv_g_up, rwkv_k_k, rwkv_k_a, rwkv_r_k, rwkv_ln_w, rwkv_ln_b, sc_conv_w,
           w_branch, w_o, norm_ffn, ffn_w_up, ffn_conv_w, ffn_w_down):
    depth, d = ada_w.shape[0], x.shape[-1]
    mod5 = _ada_mod(c, ada_w, ada_b)
    bias = _bias_tiles(rel_bias)
    for l in range(depth):
        prm = _layer_params(l, w_in, rwkv_mu, rwkv_w0, rwkv_w_up, rwkv_a0, rwkv_a_up, rwkv_g_up, rwkv_k_k,
                            rwkv_k_a, rwkv_r_k, rwkv_ln_w, rwkv_ln_b, sc_conv_w, w_branch, w_o)
        z = _inproj(x, mod5, l, norm_mix[l], prm["w_z"])
        o_attn = _dsa(z, bias, d)
        y, g, bonus = _rwkv(z, prm, d)
        x = _merge(x, o_attn, y, g, bonus, z, mod5, l, prm)
        x = _ffn(x, mod5, l, norm_ffn[l], ffn_w_up[l].astype(BF16), ffn_conv_w[l].T, ffn_w_down[l].astype(BF16),
                 final_norm, final_norm=(l == depth - 1))
    return x
```

```python
import functools
import math

import jax
import jax.numpy as jnp
from jax import lax
from jax.experimental import pallas as pl
from jax.experimental.pallas import tpu as pltpu

F32 = jnp.float32
BF16 = jnp.bfloat16
HIGHEST = lax.Precision.HIGHEST

MIX_W = 512
HEADS = 8
HEAD_DIM = 64
TOPK_MAX = 256
N_BUCKETS = 32
MAX_DISTANCE = 128
LORA_DECAY = 64
LORA_ICLR = 64
LORA_GATE = 128
CONV_W = 3
NORM_EPS = 1e-6
GN_EPS = 64e-5
NEG_INF = -1e30

LANES = 128
SUBLANES = 8
BF16_ROWS = 16
VMEM_LIMIT = 56 * 1024 * 1024

TQ = 256
CHUNK = 64
RW_TM = 256
QUAD = 4 * HEAD_DIM
assert CHUNK == HEAD_DIM

Z_Q, Z_K, Z_V, Z_QI, Z_RR, Z_RK, Z_RV, Z_CB, Z_CC, Z_CX = range(10)
Z_GATE_COL = 10 * MIX_W


def _z_small_block(d_model, i):
    return (Z_GATE_COL + 3 * d_model) // LANES + i


def _cparams(sem):
    return pltpu.CompilerParams(dimension_semantics=sem, vmem_limit_bytes=VMEM_LIMIT)


def _nt(a, b, precision=None):
    return lax.dot_general(a, b, (((1,), (1,)), ((), ())), precision=precision,
                           preferred_element_type=F32)


def _tn(a, b, precision=None):
    return lax.dot_general(a, b, (((0,), (0,)), ((), ())), precision=precision,
                           preferred_element_type=F32)


def _mm(a, b, precision=None):
    return jnp.dot(a, b, precision=precision, preferred_element_type=F32)


def _bmm(a, b):
    return _mm(a.astype(BF16), b.astype(BF16))


def _mod_kernel(c_ref, w_ref, b_ref, o_ref):
    o_ref[...] = _mm(c_ref[...], w_ref[...], HIGHEST) + b_ref[...]


def _ada_mod(c, ada_w, ada_b):
    depth, d, d6 = ada_w.shape
    b = c.shape[0]
    out = pl.pallas_call(
        _mod_kernel,
        grid=(depth, d6 // d),
        in_specs=[pl.BlockSpec((b, d), lambda l, j: (0, 0)),
                  pl.BlockSpec((None, d, d), lambda l, j: (l, 0, j)),
                  pl.BlockSpec((None, 1, d), lambda l, j: (l, 0, j))],
        out_specs=pl.BlockSpec((None, b, d), lambda l, j: (l, 0, j)),
        out_shape=jax.ShapeDtypeStruct((depth, b, d6), F32),
        compiler_params=_cparams(("parallel", "parallel")),
        name="ada_mod",
    )(c, ada_w, ada_b.reshape(depth, 1, d6))
    return out.reshape(depth, b, d6 // d, 1, d)


def _mod_spec(mod5, layer, which, ngrid):
    d = mod5.shape[-1]
    if ngrid == 2:
        return pl.BlockSpec((None, None, None, 1, d), lambda b, i: (layer, b, which, 0, 0))
    return pl.BlockSpec((None, None, None, 1, d), lambda b, i, j: (layer, b, which, 0, 0))


def _norm_mod(x, gain, scale, shift):
    y = x * lax.rsqrt(jnp.mean(x * x, axis=-1, keepdims=True) + NORM_EPS) * gain
    return y * (1.0 + scale) + shift


def _inproj_kernel(x_ref, sh_ref, sc_ref, g_ref, w_ref, o_ref, h_ref):
    @pl.when(pl.program_id(2) == 0)
    def _():
        h_ref[...] = _norm_mod(x_ref[...], g_ref[...], sc_ref[...], sh_ref[...]).astype(h_ref.dtype)

    o_ref[...] = _mm(h_ref[...], w_ref[...]).astype(o_ref.dtype)


def _inproj(x, mod5, layer, gain, w, tm=1024, n_col_tiles=4):
    b, s, d = x.shape
    zc = w.shape[1]
    tm = min(tm, s)
    tn = zc // n_col_tiles
    assert tn * n_col_tiles == zc and tn % LANES == 0
    return pl.pallas_call(
        _inproj_kernel,
        grid=(b, s // tm, zc // tn),
        in_specs=[pl.BlockSpec((None, tm, d), lambda bi, i, j: (bi, i, 0)),
                  _mod_spec(mod5, layer, 0, 3),
                  _mod_spec(mod5, layer, 1, 3),
                  pl.BlockSpec((1, d), lambda bi, i, j: (0, 0)),
                  pl.BlockSpec((d, tn), lambda bi, i, j: (0, j))],
        out_specs=pl.BlockSpec((None, tm, tn), lambda bi, i, j: (bi, i, j)),
        out_shape=jax.ShapeDtypeStruct((b, s, zc), BF16),
        scratch_shapes=[pltpu.VMEM((tm, d), BF16)],
        compiler_params=_cparams(("parallel", "parallel", "arbitrary")),
        name="inproj",
    )(x, mod5, mod5, gain.reshape(1, d), w)


def _bias_kernel(rb_ref, o_ref):
    which = pl.program_id(0)
    h = pl.program_id(1)
    ri = lax.broadcasted_iota(jnp.int32, (TQ, TQ), 0)
    ci = lax.broadcasted_iota(jnp.int32, (TQ, TQ), 1)
    dist = ci - ri + (1 - which) * TQ
    n = jnp.maximum(dist, 0)
    max_exact = N_BUCKETS // 2
    nf = jnp.maximum(n, 1).astype(F32)
    large = max_exact + (jnp.log(nf / max_exact) / math.log(MAX_DISTANCE / max_exact)
                         * (N_BUCKETS - max_exact)).astype(jnp.int32)
    large = jnp.minimum(large, N_BUCKETS - 1)
    bucket = jnp.where(n < max_exact, n, large)
    far = rb_ref[N_BUCKETS - 1, h]
    acc = jnp.zeros((TQ, TQ), F32)
    for bkt in range(N_BUCKETS - 1):
        acc = jnp.where(bucket == bkt, rb_ref[bkt, h] - far, acc)
    o_ref[...] = acc * LOG2E


def _bias_tiles(rel_bias):
    assert TQ >= MAX_DISTANCE
    return pl.pallas_call(
        _bias_kernel,
        grid=(2, HEADS),
        in_specs=[pl.BlockSpec(memory_space=pltpu.SMEM)],
        out_specs=pl.BlockSpec((None, None, TQ, TQ), lambda w, h: (w, h, 0, 0)),
        out_shape=jax.ShapeDtypeStruct((2, HEADS, TQ, TQ), F32),
        compiler_params=_cparams(("parallel", "parallel")),
        name="bias_tiles",
    )(rel_bias)


BISECT_MAX_IT = 300
COARSE_STEPS = 10
FINE_STEPS_UNCHECKED = 5
FINE_STEPS_PER_CHECK = 3
LOG2E = math.log2(math.e)
ONES_ROWS = BF16_ROWS


def _row_groups(t):
    return [t[r * SUBLANES:(r + 1) * SUBLANES, :] for r in range(t.shape[0] // SUBLANES)]


def _dsa_kernel(q_ref, k_ref, v_ref, qi_ref, ki_ref, wi_ref, bias_ref, o_ref,
                sc_ref, scb_ref, qm_ref, qim_ref, wt_ref, vt_ref, m_ref, l_ref, acc_ref, s_ref, *, n_keep, seq):
    j = pl.program_id(1)
    nt = j + 1
    kf = float(n_keep)
    lane = lax.broadcasted_iota(jnp.int32, (TQ, LANES), 1)
    att_scale = HEAD_DIM ** -0.5 * LOG2E
    w_scale = (HEADS ** -0.5) * (HEAD_DIM ** -0.5)

    for h in range(HEADS):
        p, odd = divmod(h, 2)
        hm = (lane >= HEAD_DIM) if odd else (lane < HEAD_DIM)
        qs = q_ref[:, p * LANES:(p + 1) * LANES]
        qm_ref[h] = jnp.where(hm, qs, jnp.zeros_like(qs)) * att_scale
        qis = qi_ref[:, p * LANES:(p + 1) * LANES]
        qim_ref[h] = jnp.where(hm, qis, jnp.zeros_like(qis))
    wt_ref[...] = wi_ref[...].astype(F32).T * w_scale
    v_t = v_ref[...].T
    for h in range(HEADS):
        vt_ref[j, h, 0:HEAD_DIM, :] = v_t[h * HEAD_DIM:(h + 1) * HEAD_DIM, :]
        vt_ref[j, h, HEAD_DIM:HEAD_DIM + ONES_ROWS, :] = jnp.ones((ONES_ROWS, TQ), BF16)

    key = lax.broadcasted_iota(jnp.int32, (TQ, TQ), 0)
    qry = lax.broadcasted_iota(jnp.int32, (TQ, TQ), 1) + j * TQ

    def idx_tile(kt, carry):
        ki_t = ki_ref[pl.ds(pl.multiple_of(kt * TQ, TQ), TQ), :]
        acc = jnp.zeros((TQ, TQ), F32)
        for h in range(HEADS):
            acc = acc + jnp.maximum(_nt(ki_t, qim_ref[h]), 0.0) * wt_ref[h:h + 1, :]
        sc_ref[kt] = jnp.where(key + kt * TQ <= qry, acc, NEG_INF)
        return carry

    lax.fori_loop(0, nt, idx_tile, 0)

    def floor_bf16(x):
        xi = lax.bitcast_convert_type(x, jnp.int32)
        xi = jnp.where(xi < 0, xi + jnp.int32(0xFFFF), xi) & jnp.int32(-0x10000)
        return lax.bitcast_convert_type(xi, F32).astype(BF16)

    def stats_tile(kt, c):
        mx, mn, mp, cp, cn = c
        scb_ref[kt] = floor_bf16(sc_ref[kt])
        for g in _row_groups(sc_ref[kt]):
            pos = g > 0.0
            mx = jnp.maximum(mx, g)
            mn = jnp.minimum(mn, jnp.where(g > 0.5 * NEG_INF, g, -NEG_INF))
            mp = jnp.minimum(mp, jnp.where(pos, g, -NEG_INF))
            cp = cp + jnp.where(pos, 1.0, 0.0)
            cn = cn + jnp.where(g >= 0.0, 1.0, 0.0)
        return mx, mn, mp, cp, cn

    part = lambda v: jnp.full((SUBLANES, TQ), v, F32)
    mx, mn, mp, cp, cn = lax.fori_loop(0, nt, stats_tile,
                                       (part(NEG_INF), part(-NEG_INF), part(-NEG_INF), part(0.0), part(0.0)))
    rmax = jnp.max(mx, axis=0, keepdims=True)
    rmin = jnp.min(mn, axis=0, keepdims=True)
    minpos = jnp.min(mp, axis=0, keepdims=True)
    cpos = jnp.sum(cp, axis=0, keepdims=True)
    cnn = jnp.sum(cn, axis=0, keepdims=True)
    nvalid = (lax.broadcasted_iota(jnp.int32, (1, TQ), 1) + j * TQ + 1).astype(F32)
    small = nvalid <= kf

    def count_ge(thr):
        def body(kt, acc):
            for g in _row_groups(sc_ref[kt]):
                acc = acc + jnp.where(g >= thr, 1.0, 0.0)
            return acc
        acc = lax.fori_loop(0, nt, body, jnp.zeros((SUBLANES, TQ), F32))
        return jnp.sum(acc, axis=0, keepdims=True)

    ztie = jnp.logical_and(cpos < kf, cnn >= kf)
    pos_side = cpos >= kf
    hi_top = rmax + jnp.maximum(jnp.abs(rmax) * 1e-6, 1e-30)
    lo0 = jnp.where(small, 0.5 * NEG_INF, jnp.where(ztie, 0.0, jnp.where(pos_side, minpos, rmin)))
    hi0 = jnp.where(small, -NEG_INF, jnp.where(ztie, minpos, jnp.where(pos_side, hi_top, 0.0)))
    clo0 = jnp.where(jnp.logical_or(small, ztie), jnp.where(small, nvalid, cnn), jnp.where(pos_side, cpos, nvalid))
    chi0 = jnp.where(small, 0.0, jnp.where(ztie, cpos, jnp.where(pos_side, 0.0, cnn)))
    done0 = jnp.where(jnp.logical_or(jnp.logical_or(small, ztie), clo0 == kf), 1.0, 0.0)

    def bisect_step(state, mid, cnt, usable):
        lo, hi, clo, chi, done = state
        act = jnp.logical_and(done < 0.5, usable)
        ge = cnt >= kf
        up_lo = jnp.logical_and(act, ge)
        up_hi = jnp.logical_and(act, jnp.logical_not(ge))
        lo = jnp.where(up_lo, mid, lo)
        clo = jnp.where(up_lo, cnt, clo)
        hi = jnp.where(up_hi, mid, hi)
        chi = jnp.where(up_hi, cnt, chi)
        done = jnp.where(jnp.logical_and(act, cnt == kf), 1.0, done)
        return lo, hi, clo, chi, done

    def count_ge_bf16(thr):
        one, zero = jnp.ones((), BF16), jnp.zeros((), BF16)

        def body(kt, acc):
            t = scb_ref[kt]
            part = jnp.zeros((BF16_ROWS, TQ), BF16)
            for r in range(TQ // BF16_ROWS):
                part = part + jnp.where(t[r * BF16_ROWS:(r + 1) * BF16_ROWS, :] >= thr, one, zero)
            return acc + part.astype(F32)
        acc = lax.fori_loop(0, nt, body, jnp.zeros((BF16_ROWS, TQ), F32))
        return jnp.sum(acc, axis=0, keepdims=True)

    def coarse_step(_, state):
        lo, hi = state[0], state[1]
        mid_b = (lo + 0.5 * (hi - lo)).astype(BF16)
        mid = mid_b.astype(F32)
        usable = jnp.logical_and(mid > lo, mid < hi)
        return bisect_step(state, mid, count_ge_bf16(mid_b), usable)

    state = lax.fori_loop(0, COARSE_STEPS, coarse_step, (lo0, hi0, clo0, chi0, done0))

    def fine_step(_, state):
        lo, hi = state[0], state[1]
        mid = lo + 0.5 * (hi - lo)
        stalled = jnp.logical_or(mid <= lo, mid >= hi)
        state = bisect_step(state, mid, count_ge(mid), jnp.logical_not(stalled))
        return state[:4] + (jnp.where(stalled, 1.0, state[4]),)

    state = lax.fori_loop(0, FINE_STEPS_UNCHECKED, fine_step, state)

    def bis_cond(c):
        return jnp.logical_and(c[0] < BISECT_MAX_IT, c[2] > 0.0)

    def bis_body(c):
        it, state, _ = c
        state = lax.fori_loop(0, FINE_STEPS_PER_CHECK, fine_step, state)
        return it + FINE_STEPS_PER_CHECK, state, jnp.sum(1.0 - state[4])

    _, (lo, hi, clo, chi, _), _ = lax.while_loop(
        bis_cond, bis_body, (jnp.int32(0), state, jnp.sum(1.0 - state[4])))

    tie = jnp.logical_and(clo > kf, jnp.logical_not(small))
    band_quota = jnp.where(tie, kf - chi, float(seq))
    prefix_ones = jnp.where(key >= lax.broadcasted_iota(jnp.int32, (TQ, TQ), 1), 1.0, 0.0).astype(BF16)

    def mask_tiles(kts, before):
        bands = [jnp.where(sc_ref[kt] >= lo, jnp.where(sc_ref[kt] < hi, 1.0, 0.0), 0.0).astype(BF16) for kt in kts]
        ranks = [_mm(prefix_ones, band) for band in bands]
        for kt, rank in zip(kts, ranks):
            t = sc_ref[kt]
            rank = rank + before
            keep_band = jnp.where(rank <= band_quota, 0.0, NEG_INF)
            sc_ref[kt] = jnp.where(t >= lo, jnp.where(t >= hi, 0.0, keep_band), NEG_INF)
            before = rank[TQ - 1:TQ, :]
        return before

    n_pairs = lax.shift_right_logical(nt, 1)
    before = lax.fori_loop(0, n_pairs, lambda i, c: mask_tiles([2 * i, 2 * i + 1], c), jnp.zeros((1, TQ), F32))
    lax.fori_loop(2 * n_pairs, nt, lambda kt, c: mask_tiles([kt], c), before)

    m_ref[...] = jnp.full(m_ref.shape, NEG_INF, F32)
    l_ref[...] = jnp.zeros(l_ref.shape, F32)
    acc_ref[...] = jnp.zeros(acc_ref.shape, F32)

    def attn_tiles(kts, near):
        tile_max = [None] * HEADS
        for i, kt in enumerate(kts):
            rows = pl.ds(pl.multiple_of(kt * TQ, TQ), TQ)
            mask_add = sc_ref[kt]
            for h in range(HEADS):
                p = h // 2
                s = _nt(k_ref[rows, p * LANES:(p + 1) * LANES], qm_ref[h]) + mask_add
                if near:
                    s = s + bias_ref[kt - j + 1, h]
                s_ref[i, h] = s
                mx = jnp.max(s, axis=0, keepdims=True)
                tile_max[h] = mx if i == 0 else jnp.maximum(tile_max[h], mx)
        for h in range(HEADS):
            m_old = m_ref[h]
            m_new = jnp.maximum(m_old, tile_max[h])
            alpha = jnp.exp2(m_old - m_new)
            pv = None
            for i, kt in enumerate(kts):
                part = _mm(vt_ref[kt, h], jnp.exp2(s_ref[i, h] - m_new).astype(BF16))
                pv = part if i == 0 else pv + part
            m_ref[h] = m_new
            l_ref[h] = alpha * l_ref[h] + pv[HEAD_DIM:HEAD_DIM + 1, :]
            acc_ref[h] = alpha * acc_ref[h] + pv[0:HEAD_DIM, :]

    n_far = jnp.maximum(j - 1, 0)
    n_far_pairs = lax.shift_right_logical(n_far, 1)

    def far_pair(i, carry):
        attn_tiles([2 * i, 2 * i + 1], near=False)
        return carry

    def far_single(kt, carry):
        attn_tiles([kt], near=False)
        return carry

    lax.fori_loop(0, n_far_pairs, far_pair, 0)
    lax.fori_loop(2 * n_far_pairs, n_far, far_single, 0)

    @pl.when(j >= 1)
    def _():
        attn_tiles([j - 1, j], near=True)

    @pl.when(j == 0)
    def _():
        attn_tiles([j], near=True)

    for p in range(HEADS // 2):
        pair = jnp.concatenate([acc_ref[2 * p] / l_ref[2 * p], acc_ref[2 * p + 1] / l_ref[2 * p + 1]], axis=0)
        o_ref[:, p * LANES:(p + 1) * LANES] = pair.T.astype(o_ref.dtype)


def _dsa(z, bias, d_model):
    b, s, _ = z.shape
    n_keep = min(TOPK_MAX, s // 4)
    kern = functools.partial(_dsa_kernel, n_keep=n_keep, seq=s)
    ki_blk, wi_blk = _z_small_block(d_model, 0), _z_small_block(d_model, 1)
    return pl.pallas_call(
        kern,
        grid=(b, s // TQ),
        in_specs=[pl.BlockSpec((None, TQ, MIX_W), lambda bi, j: (bi, j, Z_Q)),
                  pl.BlockSpec((None, s, MIX_W), lambda bi, j: (bi, 0, Z_K)),
                  pl.BlockSpec((None, TQ, MIX_W), lambda bi, j: (bi, j, Z_V)),
                  pl.BlockSpec((None, TQ, MIX_W), lambda bi, j: (bi, j, Z_QI)),
                  pl.BlockSpec((None, s, LANES), lambda bi, j: (bi, 0, ki_blk)),
                  pl.BlockSpec((None, TQ, LANES), lambda bi, j: (bi, j, wi_blk)),
                  pl.BlockSpec((2, HEADS, TQ, TQ), lambda bi, j: (0, 0, 0, 0))],
        out_specs=pl.BlockSpec((None, TQ, MIX_W), lambda bi, j: (bi, j, 0)),
        out_shape=jax.ShapeDtypeStruct((b, s, MIX_W), BF16),
        scratch_shapes=[pltpu.VMEM((s // TQ, TQ, TQ), F32),
                        pltpu.VMEM((s // TQ, TQ, TQ), BF16),
                        pltpu.VMEM((HEADS, TQ, LANES), BF16),
                        pltpu.VMEM((HEADS, TQ, LANES), BF16),
                        pltpu.VMEM((LANES, TQ), F32),
                        pltpu.VMEM((s // TQ, HEADS, HEAD_DIM + ONES_ROWS, TQ), BF16),
                        pltpu.VMEM((HEADS, 1, TQ), F32),
                        pltpu.VMEM((HEADS, 1, TQ), F32),
                        pltpu.VMEM((HEADS, HEAD_DIM, TQ), F32),
                        pltpu.VMEM((2, HEADS, TQ, TQ), F32)],
        compiler_params=_cparams(("parallel", "arbitrary")),
        name="dsa",
    )(z, z, z, z, z, z, bias)


def _halo_spec(width, blk, tm, rows):
    step = tm // rows
    return pl.BlockSpec((None, rows, width), lambda bi, i: (bi, jnp.maximum(i * step - 1, 0), blk))


def _shift_lerp(cur_ref, halo_ref, mu, first):
    cur = cur_ref[...].astype(F32)
    nh = halo_ref.shape[0]
    prev_last = jnp.where(first, 0.0, halo_ref[nh - 1:nh, :].astype(F32))
    rolled = pltpu.roll(cur, 1, axis=0)
    rowid = lax.broadcasted_iota(jnp.int32, cur.shape, 0)
    sh = jnp.where(rowid == 0, prev_last, rolled)
    return cur + (sh - cur) * mu


def _head_sum(x, bq_ref):
    xb = x.astype(BF16)
    return jnp.concatenate([_mm(xb[:, q * QUAD:(q + 1) * QUAD], bq_ref[...]) for q in range(MIX_W // QUAD)], axis=1)


def _rwkv_kernel(r_ref, k_ref, v_ref, s2_ref, s3_ref, rh_ref, kh_ref, vh_ref, s2h_ref, s3h_ref,
                 mu_r, mu_k, mu_v, mu_s2, mu_s3, w0_ref, wup_ref, a0_ref, aup_ref, gup_ref,
                 kk_ref, ka_ref, rk_ref, bq_ref, tri_ref,
                 y_ref, g_ref, bonus_ref,
                 st_ref, rs, ls, ks, vs, kks, kbs):
    first = pl.program_id(1) == 0

    @pl.when(first)
    def _():
        st_ref[...] = jnp.zeros_like(st_ref)

    r = _shift_lerp(r_ref, rh_ref, mu_r[...], first)
    k = _shift_lerp(k_ref, kh_ref, mu_k[...], first)
    v = _shift_lerp(v_ref, vh_ref, mu_v[...], first)
    s2 = _shift_lerp(s2_ref, s2h_ref, mu_s2[...], first)
    s3 = _shift_lerp(s3_ref, s3h_ref, mu_s3[...], first)
    xw = w0_ref[...] + _bmm(jnp.tanh(s2), wup_ref[...])
    softplus = jnp.maximum(-xw, 0.0) + jnp.log1p(jnp.exp(-jnp.abs(xw)))
    ld = -jnp.exp(-softplus - 0.5)
    af = jax.nn.sigmoid(a0_ref[...] + _bmm(s2, aup_ref[...]))
    g_ref[...] = _bmm(jax.nn.sigmoid(s3), gup_ref[...]).astype(g_ref.dtype)
    kkr = k * kk_ref[...]
    kkn = kkr / jnp.maximum(jnp.sqrt(_head_sum(kkr * kkr, bq_ref)), 1e-12)
    kmod = k * (1.0 + (af - 1.0) * ka_ref[...])
    bonus_ref[...] = (_head_sum(r * kmod * rk_ref[...], bq_ref) * v).astype(bonus_ref.dtype)
    rs[...] = r
    ls[...] = ld
    ks[...] = kmod
    vs[...] = v
    kks[...] = kkn
    kbs[...] = kkn * af

    ri = lax.broadcasted_iota(jnp.int32, (QUAD, QUAD), 0)
    ci = lax.broadcasted_iota(jnp.int32, (QUAD, QUAD), 1)
    same_head = (ri // HEAD_DIM) == (ci // HEAD_DIM)
    strict = jnp.logical_and(same_head, (ri % CHUNK) > (ci % CHUNK))
    incl = jnp.logical_and(same_head, (ri % CHUNK) >= (ci % CHUNK))
    eye = jnp.where(ri == ci, 1.0, 0.0)

    def stack(x):
        return jnp.where(same_head, jnp.concatenate([x] * 4, axis=0), 0.0).astype(BF16)

    def tile4(x):
        return jnp.concatenate([x] * 4, axis=0).astype(BF16)

    def unstack(x):
        return (x[0:CHUNK] + x[CHUNK:2 * CHUNK]) + (x[2 * CHUNK:3 * CHUNK] + x[3 * CHUNK:4 * CHUNK])

    def chunk_operands(c):
        rows = pl.ds(pl.multiple_of(c * CHUNK, CHUNK), CHUNK)
        ldc = ls[rows, :]
        p1 = ldc.astype(BF16)
        e1 = ldc - p1.astype(F32)
        p2 = e1.astype(BF16)
        p3 = (e1 - p2.astype(F32)).astype(BF16)
        tri = tri_ref[...]
        cl = (_mm(tri, p1) + _mm(tri, p2)) + _mm(tri, p3)
        cl_end = cl[CHUNK - 1:CHUNK, :]
        e_in = jnp.exp(cl)
        e_out = jnp.exp(-cl)
        e_end = jnp.exp(cl_end - cl)
        rt_all = rs[rows, :] * e_in
        at_all = -kks[rows, :] * jnp.exp(cl - ldc)
        bt_all = kbs[rows, :] * e_out
        kt_all = ks[rows, :] * e_out
        bg_all = (kbs[rows, :] * e_end).astype(BF16)
        kg_all = (ks[rows, :] * e_end).astype(BF16)
        v_all = vs[rows, :]
        gam_all = jnp.exp(cl_end)
        units = []
        for q in range(MIX_W // QUAD):
            sl = slice(q * QUAD, (q + 1) * QUAD)
            units.append(dict(rows=rows, sl=sl, q=q, rt=rt_all[:, sl], vv=v_all[:, sl].astype(BF16),
                              a4=stack(at_all[:, sl]), r4=stack(rt_all[:, sl]), v4=stack(v_all[:, sl]),
                              bt4=tile4(bt_all[:, sl]), kt4=tile4(kt_all[:, sl]),
                              bg=bg_all[:, sl], kg=kg_all[:, sl], gam=gam_all[:, sl]))
        return units

    n_doublings = int(math.log2(CHUNK)) - 1

    def chunk_pair(cp, carry):
        us = chunk_operands(2 * cp) + chunk_operands(2 * cp + 1)
        for u in us:
            u["m_ab"] = jnp.where(strict, _nt(u["a4"], u["bt4"]), 0.0)
        for u in us:
            u["m_ak"] = jnp.where(strict, _nt(u["a4"], u["kt4"]), 0.0)
        for u in us:
            u["m_rb"] = jnp.where(incl, _nt(u["r4"], u["bt4"]), 0.0).astype(BF16)
        for u in us:
            u["m_rk"] = jnp.where(incl, _nt(u["r4"], u["kt4"]), 0.0).astype(BF16)
        for u in us:
            u["pw"] = u["m_ab"]
            u["inv"] = eye + u["m_ab"]
        for _ in range(n_doublings):
            for u in us:
                u["pw"] = _bmm(u["pw"], u["pw"])
            for u in us:
                u["inv"] = u["inv"] + _bmm(u["inv"], u["pw"])
        for u in us:
            u["inv"] = u["inv"].astype(BF16)
            u["mv4"] = _bmm(u["m_ak"], u["v4"])
        for u in us:
            u["ah4"] = _mm(u["inv"], u["a4"])
        for u in us:
            u["uh4"] = _bmm(u["inv"], u["mv4"])
        for u in us:
            u["ry"] = (u["rt"] + unstack(_bmm(u["m_rb"], u["ah4"]))).astype(BF16)
        for u in us:
            u["y0"] = unstack(_bmm(u["m_rb"], u["uh4"]) + _mm(u["m_rk"], u["v4"]))
        for u in us:
            ah, uh = unstack(u["ah4"]).astype(BF16), unstack(u["uh4"]).astype(BF16)
            u["g_low"] = jnp.where(same_head, _tn(u["bg"], ah), 0.0).astype(BF16)
            u["h_t"] = jnp.where(same_head, _tn(uh, u["bg"]) + _tn(u["vv"], u["kg"]), 0.0)
        for u in us:
            st = st_ref[u["q"]]
            stb = st.astype(BF16)
            y_ref[u["rows"], u["sl"]] = _nt(u["ry"], stb) + u["y0"]
            st_ref[u["q"]] = st * u["gam"] + _nt(stb, u["g_low"]) + u["h_t"]
        return carry

    lax.fori_loop(0, r_ref.shape[0] // (2 * CHUNK), chunk_pair, 0)


def _rwkv(z, prm, d_model):
    b, s, _ = z.shape
    tm = min(RW_TM, s)
    s2_blk, s3_blk = _z_small_block(d_model, 2), _z_small_block(d_model, 3)
    tok = lambda blk: pl.BlockSpec((None, tm, MIX_W), lambda bi, i: (bi, i, blk))
    tok128 = lambda blk: pl.BlockSpec((None, tm, LANES), lambda bi, i: (bi, i, blk))
    const = lambda a: pl.BlockSpec(a.shape, lambda bi, i: (0,) * a.ndim)
    consts = [prm[n] for n in ("mu_r", "mu_k", "mu_v", "mu_s2", "mu_s3", "w0", "w_up", "a0", "a_up", "g_up",
                               "k_k", "k_a", "r_k", "bones_q", "tri")]
    out_spec = pl.BlockSpec((None, tm, MIX_W), lambda bi, i: (bi, i, 0))
    return pl.pallas_call(
        _rwkv_kernel,
        grid=(b, s // tm),
        in_specs=[tok(Z_RR), tok(Z_RK), tok(Z_RV), tok128(s2_blk), tok128(s3_blk),
                  _halo_spec(MIX_W, Z_RR, tm, BF16_ROWS), _halo_spec(MIX_W, Z_RK, tm, BF16_ROWS),
                  _halo_spec(MIX_W, Z_RV, tm, BF16_ROWS), _halo_spec(LANES, s2_blk, tm, BF16_ROWS),
                  _halo_spec(LANES, s3_blk, tm, BF16_ROWS)] + [const(a) for a in consts],
        out_specs=[out_spec, out_spec, out_spec],
        out_shape=[jax.ShapeDtypeStruct((b, s, MIX_W), F32), jax.ShapeDtypeStruct((b, s, MIX_W), BF16),
                   jax.ShapeDtypeStruct((b, s, MIX_W), BF16)],
        scratch_shapes=[pltpu.VMEM((MIX_W // QUAD, QUAD, QUAD), F32)] + [pltpu.VMEM((tm, MIX_W), F32)] * 6,
        compiler_params=_cparams(("parallel", "arbitrary")),
        name="rwkv",
    )(z, z, z, z, z, z, z, z, z, z, *consts)


def _causal_conv(p, halo, cw_ref):
    nh = halo.shape[0]
    rowid = lax.broadcasted_iota(jnp.int32, (SUBLANES, p.shape[1]), 0)
    conv = p * cw_ref[CONV_W - 1:CONV_W, :]
    for back in range(1, CONV_W):
        rolled = pltpu.roll(p, back, axis=0)
        head = rolled[0:SUBLANES]
        for rr in range(back):
            head = jnp.where(rowid == rr, halo[nh - back + rr:nh - back + rr + 1, :], head)
        rolled = jnp.concatenate([head, rolled[SUBLANES:]], axis=0)
        conv = conv + rolled * cw_ref[CONV_W - 1 - back:CONV_W - back, :]
    return conv


def _merge_kernel(x_ref, oa_ref, y_ref, g_ref, bonus_ref, cb_ref, cc_ref, cx_ref, cch_ref, cxh_ref,
                  gate0_ref, gate1_ref, gate2_ref, g1_ref, lnw_ref, lnb_ref, bq_ref, cw_ref, wb_ref, wo_ref, o_ref):
    first = pl.program_id(1) == 0
    y = y_ref[...]
    inv_n = 1.0 / HEAD_DIM
    p1 = y.astype(BF16)
    mean = (_head_sum(p1, bq_ref) + _head_sum(y - p1.astype(F32), bq_ref)) * inv_n
    yc = y - mean
    var = _head_sum(yc * yc, bq_ref) * inv_n
    o_rwkv = ((yc * lax.rsqrt(var + GN_EPS) * lnw_ref[...] + lnb_ref[...] + bonus_ref[...].astype(F32))
              * g_ref[...].astype(F32))
    p = cc_ref[...].astype(F32) * cx_ref[...].astype(F32)
    ph = jnp.where(first, 0.0, cch_ref[...].astype(F32) * cxh_ref[...].astype(F32))
    o_conv = cb_ref[...].astype(F32) * _causal_conv(p, ph, cw_ref)
    merged = jnp.zeros(x_ref.shape, F32)
    for bi, (o, gate_ref) in enumerate(((oa_ref[...], gate0_ref), (o_rwkv, gate1_ref), (o_conv, gate2_ref))):
        merged = merged + jax.nn.sigmoid(gate_ref[...].astype(F32)) * _mm(o.astype(BF16), wb_ref[bi])
    o_ref[...] = x_ref[...] + g1_ref[...] * _mm(merged.astype(BF16), wo_ref[...])


def _merge(x, o_attn, y, g, bonus, z, mod5, layer, prm, tm=256):
    b, s, d = x.shape
    tm = min(tm, s)
    tok = lambda w, blk: pl.BlockSpec((None, tm, w), lambda bi, i: (bi, i, blk))
    const = lambda a: pl.BlockSpec(a.shape, lambda bi, i: (0,) * a.ndim)
    consts = [prm[n] for n in ("ln_w", "ln_b", "bones_q", "conv_w", "w_branch", "w_o")]
    gate_blk = Z_GATE_COL // d
    return pl.pallas_call(
        _merge_kernel,
        grid=(b, s // tm),
        in_specs=[tok(d, 0), tok(MIX_W, 0), tok(MIX_W, 0), tok(MIX_W, 0), tok(MIX_W, 0),
                  tok(MIX_W, Z_CB), tok(MIX_W, Z_CC), tok(MIX_W, Z_CX),
                  _halo_spec(MIX_W, Z_CC, tm, BF16_ROWS), _halo_spec(MIX_W, Z_CX, tm, BF16_ROWS),
                  tok(d, gate_blk), tok(d, gate_blk + 1), tok(d, gate_blk + 2),
                  _mod_spec(mod5, layer, 2, 2)] + [const(a) for a in consts],
        out_specs=tok(d, 0),
        out_shape=jax.ShapeDtypeStruct((b, s, d), F32),
        compiler_params=_cparams(("parallel", "parallel")),
        name="merge",
    )(x, o_attn, y, g, bonus, z, z, z, z, z, z, z, z, mod5, *consts)


def _ffn_kernel(x_ref, xh_ref, sh_ref, sc_ref, g2_ref, gain_ref, wa_ref, wg_ref, cw_ref, wd_ref, fin_ref, o_ref,
                h_ref, hh_ref, acc_ref, *, final_norm):
    jf = pl.program_id(2)
    first = pl.program_id(1) == 0

    @pl.when(jf == 0)
    def _():
        h_ref[...] = _norm_mod(x_ref[...], gain_ref[...], sc_ref[...], sh_ref[...]).astype(h_ref.dtype)
        hh_ref[...] = _norm_mod(xh_ref[...], gain_ref[...], sc_ref[...], sh_ref[...]).astype(hh_ref.dtype)
        acc_ref[...] = jnp.zeros_like(acc_ref)

    a = _mm(h_ref[...], wa_ref[...])
    ah = jnp.where(first, 0.0, _mm(hh_ref[...], wa_ref[...]))
    gl = _mm(h_ref[...], wg_ref[...])
    conv = _causal_conv(a, ah, cw_ref)
    u = conv * jax.nn.sigmoid(conv) * gl
    acc_ref[...] += _mm(u.astype(BF16), wd_ref[...])

    @pl.when(jf == pl.num_programs(2) - 1)
    def _():
        out = x_ref[...] + g2_ref[...] * acc_ref[...]
        if final_norm:
            out = out * lax.rsqrt(jnp.mean(out * out, axis=-1, keepdims=True) + NORM_EPS) * fin_ref[...]
        o_ref[...] = out


def _ffn(x, mod5, layer, gain, w_up, conv_w, w_down, final_gain, final_norm, tm=512, tf=1408):
    b, s, d = x.shape
    d_ff = w_down.shape[0]
    nf = d_ff // tf
    tm = min(tm, s)
    step = tm // SUBLANES
    return pl.pallas_call(
        functools.partial(_ffn_kernel, final_norm=final_norm),
        grid=(b, s // tm, nf),
        in_specs=[pl.BlockSpec((None, tm, d), lambda bi, i, j: (bi, i, 0)),
                  pl.BlockSpec((None, SUBLANES, d), lambda bi, i, j: (bi, jnp.maximum(i * step - 1, 0), 0)),
                  _mod_spec(mod5, layer, 3, 3), _mod_spec(mod5, layer, 4, 3), _mod_spec(mod5, layer, 5, 3),
                  pl.BlockSpec((1, d), lambda bi, i, j: (0, 0)),
                  pl.BlockSpec((d, tf), lambda bi, i, j: (0, j)),
                  pl.BlockSpec((d, tf), lambda bi, i, j: (0, j + nf)),
                  pl.BlockSpec((CONV_W, tf), lambda bi, i, j: (0, j)),
                  pl.BlockSpec((tf, d), lambda bi, i, j: (j, 0)),
                  pl.BlockSpec((1, d), lambda bi, i, j: (0, 0))],
        out_specs=pl.BlockSpec((None, tm, d), lambda bi, i, j: (bi, i, 0)),
        out_shape=jax.ShapeDtypeStruct((b, s, d), F32),
        scratch_shapes=[pltpu.VMEM((tm, d), BF16), pltpu.VMEM((SUBLANES, d), BF16), pltpu.VMEM((tm, d), F32)],
        compiler_params=_cparams(("parallel", "parallel", "arbitrary")),
        name="ffn",
    )(x, x, mod5, mod5, mod5, gain.reshape(1, d), w_up, w_up, conv_w, w_down, final_gain.reshape(1, d))


def _split_w_in(w):
    sizes = (MIX_W, MIX_W, MIX_W, HEADS * HEAD_DIM, HEAD_DIM, HEADS,
             MIX_W, MIX_W, MIX_W, LORA_DECAY, LORA_ICLR, LORA_GATE,
             MIX_W, MIX_W, MIX_W)
    out, o = [], 0
    for n in sizes:
        out.append(w[:, o:o + n])
        o += n
    out.append(w[:, o:])
    return out


def _layer_params(l, w_in, rwkv_mu, rwkv_w0, rwkv_w_up, rwkv_a0, rwkv_a_up, rwkv_g_up, rwkv_k_k, rwkv_k_a,
                  rwkv_r_k, rwkv_ln_w, rwkv_ln_b, sc_conv_w, w_branch, w_o):
    d = w_in.shape[1]
    q, k, v, qi, ki, wi, rr, rk, rv, wd, ad, gd, cb, cc, cx, gates = _split_w_in(w_in[l])
    assert Z_GATE_COL % d == 0 and gates.shape[1] == 3 * d
    w_z = jnp.concatenate([q, k, v, qi, rr, rk, rv, cb, cc, cx, gates,
                           ki, ki, wi, jnp.zeros((d, LANES - HEADS), F32), wd, ad, gd], axis=1).astype(BF16)
    mu = rwkv_mu[l]
    row = lambda a: a.reshape(1, -1)
    head_id = jnp.arange(QUAD) // HEAD_DIM
    bones_q = (head_id[:, None] == head_id[None, :]).astype(BF16)
    tri = (jnp.arange(CHUNK)[:, None] >= jnp.arange(CHUNK)[None, :]).astype(BF16)
    zl = lambda n: jnp.zeros((n, MIX_W), F32)
    return dict(
        w_z=w_z,
        mu_r=row(mu[:MIX_W]), mu_k=row(mu[MIX_W:2 * MIX_W]), mu_v=row(mu[2 * MIX_W:3 * MIX_W]),
        mu_s2=row(mu[3 * MIX_W:3 * MIX_W + LORA_DECAY + LORA_ICLR]), mu_s3=row(mu[3 * MIX_W + LORA_DECAY + LORA_ICLR:]),
        w0=row(rwkv_w0[l]), w_up=jnp.concatenate([rwkv_w_up[l], zl(LORA_ICLR)], axis=0).astype(BF16),
        a0=row(rwkv_a0[l]), a_up=jnp.concatenate([zl(LORA_DECAY), rwkv_a_up[l]], axis=0).astype(BF16),
        g_up=rwkv_g_up[l].astype(BF16), k_k=row(rwkv_k_k[l]), k_a=row(rwkv_k_a[l]), r_k=row(rwkv_r_k[l]),
        bones_q=bones_q, tri=tri,
        ln_w=row(rwkv_ln_w[l]), ln_b=row(rwkv_ln_b[l]), conv_w=sc_conv_w[l].T,
        w_branch=w_branch[l].astype(BF16), w_o=w_o[l].astype(BF16),
    )


def kernel(x, c, positions, rel_bias, final_norm, ada_w, ada_b, norm_mix, w_in, rwkv_mu, rwkv_w0, rwkv_w_up,
           rwkv_a0, rwkv_a_up, rwkv_g_up, rwkv_k_k, rwkv_k_a, rwkv_r_k, rwkv_ln_w, rwkv_ln_b, sc_conv_w,
           w_branch, w_o, norm_ffn, ffn_w_up, ffn_conv_w, ffn_w_down):
    depth, d = ada_w.shape[0], x.shape[-1]
    mod5 = _ada_mod(c, ada_w, ada_b)
    bias = _bias_tiles(rel_bias)
    for l in range(depth):
        prm = _layer_params(l, w_in, rwkv_mu, rwkv_w0, rwkv_w_up, rwkv_a0, rwkv_a_up, rwkv_g_up, rwkv_k_k,
                            rwkv_k_a, rwkv_r_k, rwkv_ln_w, rwkv_ln_b, sc_conv_w, w_branch, w_o)
        z = _inproj(x, mod5, l, norm_mix[l], prm["w_z"])
        o_attn = _dsa(z, bias, d)
        y, g, bonus = _rwkv(z, prm, d)
        x = _merge(x, o_attn, y, g, bonus, z, mod5, l, prm)
        x = _ffn(x, mod5, l, norm_ffn[l], ffn_w_up[l].astype(BF16), ffn_conv_w[l].T, ffn_w_down[l].astype(BF16),
                 final_norm, final_norm=(l == depth - 1))
    return x
```

```python
import functools
import math

import jax
import jax.numpy as jnp
from jax import lax
from jax.experimental import pallas as pl
from jax.experimental.pallas import tpu as pltpu

F32 = jnp.float32
BF16 = jnp.bfloat16
HIGHEST = lax.Precision.HIGHEST

MIX_W = 512
HEADS = 8
HEAD_DIM = 64
TOPK_MAX = 256
N_BUCKETS = 32
MAX_DISTANCE = 128
LORA_DECAY = 64
LORA_ICLR = 64
LORA_GATE = 128
CONV_W = 3
NORM_EPS = 1e-6
GN_EPS = 64e-5
NEG_INF = -1e30

LANES = 128
SUBLANES = 8
BF16_ROWS = 16
VMEM_LIMIT = 56 * 1024 * 1024

TQ = 256
CHUNK = 64
RW_TM = 256
QUAD = 4 * HEAD_DIM
assert CHUNK == HEAD_DIM

Z_Q, Z_K, Z_V, Z_QI, Z_RR, Z_RK, Z_RV, Z_CB, Z_CC, Z_CX = range(10)
Z_GATE_COL = 10 * MIX_W


def _z_small_block(d_model, i):
    return (Z_GATE_COL + 3 * d_model) // LANES + i


def _cparams(sem):
    return pltpu.CompilerParams(dimension_semantics=sem, vmem_limit_bytes=VMEM_LIMIT)


def _nt(a, b, precision=None):
    return lax.dot_general(a, b, (((1,), (1,)), ((), ())), precision=precision,
                           preferred_element_type=F32)


def _tn(a, b, precision=None):
    return lax.dot_general(a, b, (((0,), (0,)), ((), ())), precision=precision,
                           preferred_element_type=F32)


def _mm(a, b, precision=None):
    return jnp.dot(a, b, precision=precision, preferred_element_type=F32)


def _bmm(a, b):
    return _mm(a.astype(BF16), b.astype(BF16))


def _mod_kernel(c_ref, w_ref, b_ref, o_ref):
    o_ref[...] = _mm(c_ref[...], w_ref[...], HIGHEST) + b_ref[...]


def _ada_mod(c, ada_w, ada_b):
    depth, d, d6 = ada_w.shape
    b = c.shape[0]
    out = pl.pallas_call(
        _mod_kernel,
        grid=(depth, d6 // d),
        in_specs=[pl.BlockSpec((b, d), lambda l, j: (0, 0)),
                  pl.BlockSpec((None, d, d), lambda l, j: (l, 0, j)),
                  pl.BlockSpec((None, 1, d), lambda l, j: (l, 0, j))],
        out_specs=pl.BlockSpec((None, b, d), lambda l, j: (l, 0, j)),
        out_shape=jax.ShapeDtypeStruct((depth, b, d6), F32),
        compiler_params=_cparams(("parallel", "parallel")),
        name="ada_mod",
    )(c, ada_w, ada_b.reshape(depth, 1, d6))
    return out.reshape(depth, b, d6 // d, 1, d)


def _mod_spec(mod5, layer, which, ngrid):
    d = mod5.shape[-1]
    if ngrid == 2:
        return pl.BlockSpec((None, None, None, 1, d), lambda b, i: (layer, b, which, 0, 0))
    return pl.BlockSpec((None, None, None, 1, d), lambda b, i, j: (layer, b, which, 0, 0))


def _norm_mod(x, gain, scale, shift):
    y = x * lax.rsqrt(jnp.mean(x * x, axis=-1, keepdims=True) + NORM_EPS) * gain
    return y * (1.0 + scale) + shift


def _resident(a):
    return pl.BlockSpec(a.shape, lambda bi, i: (0,) * a.ndim, pipeline_mode=pl.Buffered(1))


def _inproj_kernel(x_ref, sh_ref, sc_ref, g_ref, w_ref, o_ref, *, n_col_chunks):
    h = _norm_mod(x_ref[...], g_ref[...], sc_ref[...], sh_ref[...]).astype(BF16)
    tn = w_ref.shape[1] // n_col_chunks
    for c in range(n_col_chunks):
        o_ref[:, c * tn:(c + 1) * tn] = _mm(h, w_ref[:, c * tn:(c + 1) * tn]).astype(o_ref.dtype)


def _inproj(x, mod5, layer, gain, w, tm=512, n_col_chunks=4):
    b, s, d = x.shape
    zc = w.shape[1]
    tm = min(tm, s)
    assert zc % (n_col_chunks * LANES) == 0
    gain = gain.reshape(1, d)
    return pl.pallas_call(
        functools.partial(_inproj_kernel, n_col_chunks=n_col_chunks),
        grid=(b, s // tm),
        in_specs=[pl.BlockSpec((None, tm, d), lambda bi, i: (bi, i, 0)),
                  _mod_spec(mod5, layer, 0, 2),
                  _mod_spec(mod5, layer, 1, 2),
                  _resident(gain), _resident(w)],
        out_specs=pl.BlockSpec((None, tm, zc), lambda bi, i: (bi, i, 0)),
        out_shape=jax.ShapeDtypeStruct((b, s, zc), BF16),
        compiler_params=_cparams(("parallel", "parallel")),
        name="inproj",
    )(x, mod5, mod5, gain, w)


def _bias_kernel(rb_ref, o_ref):
    which = pl.program_id(0)
    h = pl.program_id(1)
    ri = lax.broadcasted_iota(jnp.int32, (TQ, TQ), 0)
    ci = lax.broadcasted_iota(jnp.int32, (TQ, TQ), 1)
    dist = ci - ri + (1 - which) * TQ
    n = jnp.maximum(dist, 0)
    max_exact = N_BUCKETS // 2
    nf = jnp.maximum(n, 1).astype(F32)
    large = max_exact + (jnp.log(nf / max_exact) / math.log(MAX_DISTANCE / max_exact)
                         * (N_BUCKETS - max_exact)).astype(jnp.int32)
    large = jnp.minimum(large, N_BUCKETS - 1)
    bucket = jnp.where(n < max_exact, n, large)
    far = rb_ref[N_BUCKETS - 1, h]
    acc = jnp.zeros((TQ, TQ), F32)
    for bkt in range(N_BUCKETS - 1):
        acc = jnp.where(bucket == bkt, rb_ref[bkt, h] - far, acc)
    o_ref[...] = acc * LOG2E


def _bias_tiles(rel_bias):
    assert TQ >= MAX_DISTANCE
    return pl.pallas_call(
        _bias_kernel,
        grid=(2, HEADS),
        in_specs=[pl.BlockSpec(memory_space=pltpu.SMEM)],
        out_specs=pl.BlockSpec((None, None, TQ, TQ), lambda w, h: (w, h, 0, 0)),
        out_shape=jax.ShapeDtypeStruct((2, HEADS, TQ, TQ), F32),
        compiler_params=_cparams(("parallel", "parallel")),
        name="bias_tiles",
    )(rel_bias)


BISECT_MAX_IT = 300
COARSE_STEPS = 10
FINE_STEPS_UNCHECKED = 5
FINE_STEPS_PER_CHECK = 3
LOG2E = math.log2(math.e)
ONES_ROWS = BF16_ROWS


def _row_groups(t):
    return [t[r * SUBLANES:(r + 1) * SUBLANES, :] for r in range(t.shape[0] // SUBLANES)]


def _dsa_kernel(q_ref, k_ref, v_ref, qi_ref, ki_ref, wi_ref, bias_ref, o_ref,
                sc_ref, scb_ref, qm_ref, qim_ref, wt_ref, vt_ref, m_ref, l_ref, acc_ref, s_ref, *, n_keep, seq):
    j = pl.program_id(1)
    nt = j + 1
    kf = float(n_keep)
    lane = lax.broadcasted_iota(jnp.int32, (TQ, LANES), 1)
    att_scale = HEAD_DIM ** -0.5 * LOG2E
    w_scale = (HEADS ** -0.5) * (HEAD_DIM ** -0.5)

    for h in range(HEADS):
        p, odd = divmod(h, 2)
        hm = (lane >= HEAD_DIM) if odd else (lane < HEAD_DIM)
        qs = q_ref[:, p * LANES:(p + 1) * LANES]
        qm_ref[h] = jnp.where(hm, qs, jnp.zeros_like(qs)) * att_scale
        qis = qi_ref[:, p * LANES:(p + 1) * LANES]
        qim_ref[h] = jnp.where(hm, qis, jnp.zeros_like(qis))
    wt_ref[...] = wi_ref[...].astype(F32).T * w_scale
    v_t = v_ref[...].T
    for h in range(HEADS):
        vt_ref[j, h, 0:HEAD_DIM, :] = v_t[h * HEAD_DIM:(h + 1) * HEAD_DIM, :]
        vt_ref[j, h, HEAD_DIM:HEAD_DIM + ONES_ROWS, :] = jnp.ones((ONES_ROWS, TQ), BF16)

    key = lax.broadcasted_iota(jnp.int32, (TQ, TQ), 0)
    qry = lax.broadcasted_iota(jnp.int32, (TQ, TQ), 1) + j * TQ

    def idx_tile(kt, carry):
        ki_t = ki_ref[pl.ds(pl.multiple_of(kt * TQ, TQ), TQ), :]
        acc = jnp.zeros((TQ, TQ), F32)
        for h in range(HEADS):
            acc = acc + jnp.maximum(_nt(ki_t, qim_ref[h]), 0.0) * wt_ref[h:h + 1, :]
        sc_ref[kt] = jnp.where(key + kt * TQ <= qry, acc, NEG_INF)
        return carry

    lax.fori_loop(0, nt, idx_tile, 0)

    def floor_bf16(x):
        xi = lax.bitcast_convert_type(x, jnp.int32)
        xi = jnp.where(xi < 0, xi + jnp.int32(0xFFFF), xi) & jnp.int32(-0x10000)
        return lax.bitcast_convert_type(xi, F32).astype(BF16)

    def stats_tile(kt, c):
        mx, mn, mp, cp, cn = c
        scb_ref[kt] = floor_bf16(sc_ref[kt])
        for g in _row_groups(sc_ref[kt]):
            pos = g > 0.0
            mx = jnp.maximum(mx, g)
            mn = jnp.minimum(mn, jnp.where(g > 0.5 * NEG_INF, g, -NEG_INF))
            mp = jnp.minimum(mp, jnp.where(pos, g, -NEG_INF))
            cp = cp + jnp.where(pos, 1.0, 0.0)
            cn = cn + jnp.where(g >= 0.0, 1.0, 0.0)
        return mx, mn, mp, cp, cn

    part = lambda v: jnp.full((SUBLANES, TQ), v, F32)
    mx, mn, mp, cp, cn = lax.fori_loop(0, nt, stats_tile,
                                       (part(NEG_INF), part(-NEG_INF), part(-NEG_INF), part(0.0), part(0.0)))
    rmax = jnp.max(mx, axis=0, keepdims=True)
    rmin = jnp.min(mn, axis=0, keepdims=True)
    minpos = jnp.min(mp, axis=0, keepdims=True)
    cpos = jnp.sum(cp, axis=0, keepdims=True)
    cnn = jnp.sum(cn, axis=0, keepdims=True)
    nvalid = (lax.broadcasted_iota(jnp.int32, (1, TQ), 1) + j * TQ + 1).astype(F32)
    small = nvalid <= kf

    def count_ge(thr):
        def body(kt, acc):
            for g in _row_groups(sc_ref[kt]):
                acc = acc + jnp.where(g >= thr, 1.0, 0.0)
            return acc
        acc = lax.fori_loop(0, nt, body, jnp.zeros((SUBLANES, TQ), F32))
        return jnp.sum(acc, axis=0, keepdims=True)

    ztie = jnp.logical_and(cpos < kf, cnn >= kf)
    pos_side = cpos >= kf
    hi_top = rmax + jnp.maximum(jnp.abs(rmax) * 1e-6, 1e-30)
    lo0 = jnp.where(small, 0.5 * NEG_INF, jnp.where(ztie, 0.0, jnp.where(pos_side, minpos, rmin)))
    hi0 = jnp.where(small, -NEG_INF, jnp.where(ztie, minpos, jnp.where(pos_side, hi_top, 0.0)))
    clo0 = jnp.where(jnp.logical_or(small, ztie), jnp.where(small, nvalid, cnn), jnp.where(pos_side, cpos, nvalid))
    chi0 = jnp.where(small, 0.0, jnp.where(ztie, cpos, jnp.where(pos_side, 0.0, cnn)))
    done0 = jnp.where(jnp.logical_or(jnp.logical_or(small, ztie), clo0 == kf), 1.0, 0.0)

    def bisect_step(state, mid, cnt, usable):
        lo, hi, clo, chi, done = state
        act = jnp.logical_and(done < 0.5, usable)
        ge = cnt >= kf
        up_lo = jnp.logical_and(act, ge)
        up_hi = jnp.logical_and(act, jnp.logical_not(ge))
        lo = jnp.where(up_lo, mid, lo)
        clo = jnp.where(up_lo, cnt, clo)
        hi = jnp.where(up_hi, mid, hi)
        chi = jnp.where(up_hi, cnt, chi)
        done = jnp.where(jnp.logical_and(act, cnt == kf), 1.0, done)
        return lo, hi, clo, chi, done

    def count_ge_bf16(thr):
        one, zero = jnp.ones((), BF16), jnp.zeros((), BF16)

        def body(kt, acc):
            t = scb_ref[kt]
            part = jnp.zeros((BF16_ROWS, TQ), BF16)
            for r in range(TQ // BF16_ROWS):
                part = part + jnp.where(t[r * BF16_ROWS:(r + 1) * BF16_ROWS, :] >= thr, one, zero)
            return acc + part.astype(F32)
        acc = lax.fori_loop(0, nt, body, jnp.zeros((BF16_ROWS, TQ), F32))
        return jnp.sum(acc, axis=0, keepdims=True)

    def coarse_step(_, state):
        lo, hi = state[0], state[1]
        mid_b = (lo + 0.5 * (hi - lo)).astype(BF16)
        mid = mid_b.astype(F32)
        usable = jnp.logical_and(mid > lo, mid < hi)
        return bisect_step(state, mid, count_ge_bf16(mid_b), usable)

    state = lax.fori_loop(0, COARSE_STEPS, coarse_step, (lo0, hi0, clo0, chi0, done0))

    def fine_step(_, state):
        lo, hi = state[0], state[1]
        mid = lo + 0.5 * (hi - lo)
        stalled = jnp.logical_or(mid <= lo, mid >= hi)
        state = bisect_step(state, mid, count_ge(mid), jnp.logical_not(stalled))
        return state[:4] + (jnp.where(stalled, 1.0, state[4]),)

    state = lax.fori_loop(0, FINE_STEPS_UNCHECKED, fine_step, state)

    def bis_cond(c):
        return jnp.logical_and(c[0] < BISECT_MAX_IT, c[2] > 0.0)

    def bis_body(c):
        it, state, _ = c
        state = lax.fori_loop(0, FINE_STEPS_PER_CHECK, fine_step, state)
        return it + FINE_STEPS_PER_CHECK, state, jnp.sum(1.0 - state[4])

    _, (lo, hi, clo, chi, _), _ = lax.while_loop(
        bis_cond, bis_body, (jnp.int32(0), state, jnp.sum(1.0 - state[4])))

    tie = jnp.logical_and(clo > kf, jnp.logical_not(small))
    band_quota = jnp.where(tie, kf - chi, float(seq))
    prefix_ones = jnp.where(key >= lax.broadcasted_iota(jnp.int32, (TQ, TQ), 1), 1.0, 0.0).astype(BF16)

    def mask_tiles(kts, before):
        bands = [jnp.where(sc_ref[kt] >= lo, jnp.where(sc_ref[kt] < hi, 1.0, 0.0), 0.0).astype(BF16) for kt in kts]
        ranks = [_mm(prefix_ones, band) for band in bands]
        for kt, rank in zip(kts, ranks):
            t = sc_ref[kt]
            rank = rank + before
            keep_band = jnp.where(rank <= band_quota, 0.0, NEG_INF)
            sc_ref[kt] = jnp.where(t >= lo, jnp.where(t >= hi, 0.0, keep_band), NEG_INF)
            before = rank[TQ - 1:TQ, :]
        return before

    n_pairs = lax.shift_right_logical(nt, 1)
    before = lax.fori_loop(0, n_pairs, lambda i, c: mask_tiles([2 * i, 2 * i + 1], c), jnp.zeros((1, TQ), F32))
    lax.fori_loop(2 * n_pairs, nt, lambda kt, c: mask_tiles([kt], c), before)

    m_ref[...] = jnp.full(m_ref.shape, NEG_INF, F32)
    l_ref[...] = jnp.zeros(l_ref.shape, F32)
    acc_ref[...] = jnp.zeros(acc_ref.shape, F32)

    def attn_tiles(kts, near):
        tile_max = [None] * HEADS
        for i, kt in enumerate(kts):
            rows = pl.ds(pl.multiple_of(kt * TQ, TQ), TQ)
            mask_add = sc_ref[kt]
            for h in range(HEADS):
                p = h // 2
                s = _nt(k_ref[rows, p * LANES:(p + 1) * LANES], qm_ref[h]) + mask_add
                if near:
                    s = s + bias_ref[kt - j + 1, h]
                s_ref[i, h] = s
                mx = jnp.max(s, axis=0, keepdims=True)
                tile_max[h] = mx if i == 0 else jnp.maximum(tile_max[h], mx)
        for h in range(HEADS):
            m_old = m_ref[h]
            m_new = jnp.maximum(m_old, tile_max[h])
            alpha = jnp.exp2(m_old - m_new)
            pv = None
            for i, kt in enumerate(kts):
                part = _mm(vt_ref[kt, h], jnp.exp2(s_ref[i, h] - m_new).astype(BF16))
                pv = part if i == 0 else pv + part
            m_ref[h] = m_new
            l_ref[h] = alpha * l_ref[h] + pv[HEAD_DIM:HEAD_DIM + 1, :]
            acc_ref[h] = alpha * acc_ref[h] + pv[0:HEAD_DIM, :]

    n_far = jnp.maximum(j - 1, 0)
    n_far_pairs = lax.shift_right_logical(n_far, 1)

    def far_pair(i, carry):
        attn_tiles([2 * i, 2 * i + 1], near=False)
        return carry

    def far_single(kt, carry):
        attn_tiles([kt], near=False)
        return carry

    lax.fori_loop(0, n_far_pairs, far_pair, 0)
    lax.fori_loop(2 * n_far_pairs, n_far, far_single, 0)

    @pl.when(j >= 1)
    def _():
        attn_tiles([j - 1, j], near=True)

    @pl.when(j == 0)
    def _():
        attn_tiles([j], near=True)

    for p in range(HEADS // 2):
        pair = jnp.concatenate([acc_ref[2 * p] / l_ref[2 * p], acc_ref[2 * p + 1] / l_ref[2 * p + 1]], axis=0)
        o_ref[:, p * LANES:(p + 1) * LANES] = pair.T.astype(o_ref.dtype)


def _dsa(z, bias, d_model):
    b, s, _ = z.shape
    n_keep = min(TOPK_MAX, s // 4)
    kern = functools.partial(_dsa_kernel, n_keep=n_keep, seq=s)
    ki_blk, wi_blk = _z_small_block(d_model, 0), _z_small_block(d_model, 1)
    return pl.pallas_call(
        kern,
        grid=(b, s // TQ),
        in_specs=[pl.BlockSpec((None, TQ, MIX_W), lambda bi, j: (bi, j, Z_Q)),
                  pl.BlockSpec((None, s, MIX_W), lambda bi, j: (bi, 0, Z_K)),
                  pl.BlockSpec((None, TQ, MIX_W), lambda bi, j: (bi, j, Z_V)),
                  pl.BlockSpec((None, TQ, MIX_W), lambda bi, j: (bi, j, Z_QI)),
                  pl.BlockSpec((None, s, LANES), lambda bi, j: (bi, 0, ki_blk)),
                  pl.BlockSpec((None, TQ, LANES), lambda bi, j: (bi, j, wi_blk)),
                  pl.BlockSpec((2, HEADS, TQ, TQ), lambda bi, j: (0, 0, 0, 0))],
        out_specs=pl.BlockSpec((None, TQ, MIX_W), lambda bi, j: (bi, j, 0)),
        out_shape=jax.ShapeDtypeStruct((b, s, MIX_W), BF16),
        scratch_shapes=[pltpu.VMEM((s // TQ, TQ, TQ), F32),
                        pltpu.VMEM((s // TQ, TQ, TQ), BF16),
                        pltpu.VMEM((HEADS, TQ, LANES), BF16),
                        pltpu.VMEM((HEADS, TQ, LANES), BF16),
                        pltpu.VMEM((LANES, TQ), F32),
                        pltpu.VMEM((s // TQ, HEADS, HEAD_DIM + ONES_ROWS, TQ), BF16),
                        pltpu.VMEM((HEADS, 1, TQ), F32),
                        pltpu.VMEM((HEADS, 1, TQ), F32),
                        pltpu.VMEM((HEADS, HEAD_DIM, TQ), F32),
                        pltpu.VMEM((2, HEADS, TQ, TQ), F32)],
        compiler_params=_cparams(("parallel", "arbitrary")),
        name="dsa",
    )(z, z, z, z, z, z, bias)


def _halo_spec(width, blk, tm, rows):
    step = tm // rows
    return pl.BlockSpec((None, rows, width), lambda bi, i: (bi, jnp.maximum(i * step - 1, 0), blk))


def _shift_lerp(cur_ref, halo_ref, mu, first):
    cur = cur_ref[...].astype(F32)
    nh = halo_ref.shape[0]
    prev_last = jnp.where(first, 0.0, halo_ref[nh - 1:nh, :].astype(F32))
    rolled = pltpu.roll(cur, 1, axis=0)
    rowid = lax.broadcasted_iota(jnp.int32, cur.shape, 0)
    sh = jnp.where(rowid == 0, prev_last, rolled)
    return cur + (sh - cur) * mu


def _head_sum(x, bq_ref):
    xb = x.astype(BF16)
    return jnp.concatenate([_mm(xb[:, q * QUAD:(q + 1) * QUAD], bq_ref[...]) for q in range(MIX_W // QUAD)], axis=1)


def _rwkv_kernel(r_ref, k_ref, v_ref, s2_ref, s3_ref, rh_ref, kh_ref, vh_ref, s2h_ref, s3h_ref,
                 mu_r, mu_k, mu_v, mu_s2, mu_s3, w0_ref, wup_ref, a0_ref, aup_ref, gup_ref,
                 kk_ref, ka_ref, rk_ref, bq_ref, tri_ref,
                 y_ref, g_ref, bonus_ref,
                 st_ref, rs, ls, ks, vs, kks, kbs):
    first = pl.program_id(1) == 0

    @pl.when(first)
    def _():
        st_ref[...] = jnp.zeros_like(st_ref)

    r = _shift_lerp(r_ref, rh_ref, mu_r[...], first)
    k = _shift_lerp(k_ref, kh_ref, mu_k[...], first)
    v = _shift_lerp(v_ref, vh_ref, mu_v[...], first)
    s2 = _shift_lerp(s2_ref, s2h_ref, mu_s2[...], first)
    s3 = _shift_lerp(s3_ref, s3h_ref, mu_s3[...], first)
    xw = w0_ref[...] + _bmm(jnp.tanh(s2), wup_ref[...])
    softplus = jnp.maximum(-xw, 0.0) + jnp.log1p(jnp.exp(-jnp.abs(xw)))
    ld = -jnp.exp(-softplus - 0.5)
    af = jax.nn.sigmoid(a0_ref[...] + _bmm(s2, aup_ref[...]))
    g_ref[...] = _bmm(jax.nn.sigmoid(s3), gup_ref[...]).astype(g_ref.dtype)
    kkr = k * kk_ref[...]
    kkn = kkr / jnp.maximum(jnp.sqrt(_head_sum(kkr * kkr, bq_ref)), 1e-12)
    kmod = k * (1.0 + (af - 1.0) * ka_ref[...])
    bonus_ref[...] = (_head_sum(r * kmod * rk_ref[...], bq_ref) * v).astype(bonus_ref.dtype)
    rs[...] = r
    ls[...] = ld
    ks[...] = kmod
    vs[...] = v
    kks[...] = kkn
    kbs[...] = kkn * af

    ri = lax.broadcasted_iota(jnp.int32, (QUAD, QUAD), 0)
    ci = lax.broadcasted_iota(jnp.int32, (QUAD, QUAD), 1)
    same_head = (ri // HEAD_DIM) == (ci // HEAD_DIM)
    strict = jnp.logical_and(same_head, (ri % CHUNK) > (ci % CHUNK))
    incl = jnp.logical_and(same_head, (ri % CHUNK) >= (ci % CHUNK))
    eye = jnp.where(ri == ci, 1.0, 0.0)

    def stack(x):
        return jnp.where(same_head, jnp.concatenate([x] * 4, axis=0), 0.0).astype(BF16)

    def tile4(x):
        return jnp.concatenate([x] * 4, axis=0).astype(BF16)

    def unstack(x):
        return (x[0:CHUNK] + x[CHUNK:2 * CHUNK]) + (x[2 * CHUNK:3 * CHUNK] + x[3 * CHUNK:4 * CHUNK])

    def chunk_operands(c):
        rows = pl.ds(pl.multiple_of(c * CHUNK, CHUNK), CHUNK)
        ldc = ls[rows, :]
        p1 = ldc.astype(BF16)
        e1 = ldc - p1.astype(F32)
        p2 = e1.astype(BF16)
        p3 = (e1 - p2.astype(F32)).astype(BF16)
        tri = tri_ref[...]
        cl = (_mm(tri, p1) + _mm(tri, p2)) + _mm(tri, p3)
        cl_end = cl[CHUNK - 1:CHUNK, :]
        e_in = jnp.exp(cl)
        e_out = jnp.exp(-cl)
        e_end = jnp.exp(cl_end - cl)
        rt_all = rs[rows, :] * e_in
        at_all = -kks[rows, :] * jnp.exp(cl - ldc)
        bt_all = kbs[rows, :] * e_out
        kt_all = ks[rows, :] * e_out
        bg_all = (kbs[rows, :] * e_end).astype(BF16)
        kg_all = (ks[rows, :] * e_end).astype(BF16)
        v_all = vs[rows, :]
        gam_all = jnp.exp(cl_end)
        units = []
        for q in range(MIX_W // QUAD):
            sl = slice(q * QUAD, (q + 1) * QUAD)
            units.append(dict(rows=rows, sl=sl, q=q, rt=rt_all[:, sl], vv=v_all[:, sl].astype(BF16),
                              a4=stack(at_all[:, sl]), r4=stack(rt_all[:, sl]), v4=stack(v_all[:, sl]),
                              bt4=tile4(bt_all[:, sl]), kt4=tile4(kt_all[:, sl]),
                              bg=bg_all[:, sl], kg=kg_all[:, sl], gam=gam_all[:, sl]))
        return units

    n_doublings = int(math.log2(CHUNK)) - 1

    def chunk_pair(cp, carry):
        us = chunk_operands(2 * cp) + chunk_operands(2 * cp + 1)
        for u in us:
            u["m_ab"] = jnp.where(strict, _nt(u["a4"], u["bt4"]), 0.0)
        for u in us:
            u["m_ak"] = jnp.where(strict, _nt(u["a4"], u["kt4"]), 0.0)
        for u in us:
            u["m_rb"] = jnp.where(incl, _nt(u["r4"], u["bt4"]), 0.0).astype(BF16)
        for u in us:
            u["m_rk"] = jnp.where(incl, _nt(u["r4"], u["kt4"]), 0.0).astype(BF16)
        for u in us:
            u["pw"] = u["m_ab"]
            u["inv"] = eye + u["m_ab"]
        for _ in range(n_doublings):
            for u in us:
                u["pw"] = _bmm(u["pw"], u["pw"])
            for u in us:
                u["inv"] = u["inv"] + _bmm(u["inv"], u["pw"])
        for u in us:
            u["inv"] = u["inv"].astype(BF16)
            u["mv4"] = _bmm(u["m_ak"], u["v4"])
        for u in us:
            u["ah4"] = _mm(u["inv"], u["a4"])
        for u in us:
            u["uh4"] = _bmm(u["inv"], u["mv4"])
        for u in us:
            u["ry"] = (u["rt"] + unstack(_bmm(u["m_rb"], u["ah4"]))).astype(BF16)
        for u in us:
            u["y0"] = unstack(_bmm(u["m_rb"], u["uh4"]) + _mm(u["m_rk"], u["v4"]))
        for u in us:
            ah, uh = unstack(u["ah4"]).astype(BF16), unstack(u["uh4"]).astype(BF16)
            u["g_low"] = jnp.where(same_head, _tn(u["bg"], ah), 0.0).astype(BF16)
            u["h_t"] = jnp.where(same_head, _tn(uh, u["bg"]) + _tn(u["vv"], u["kg"]), 0.0)
        for u in us:
            st = st_ref[u["q"]]
            stb = st.astype(BF16)
            y_ref[u["rows"], u["sl"]] = _nt(u["ry"], stb) + u["y0"]
            st_ref[u["q"]] = st * u["gam"] + _nt(stb, u["g_low"]) + u["h_t"]
        return carry

    lax.fori_loop(0, r_ref.shape[0] // (2 * CHUNK), chunk_pair, 0)


def _rwkv(z, prm, d_model):
    b, s, _ = z.shape
    tm = min(RW_TM, s)
    s2_blk, s3_blk = _z_small_block(d_model, 2), _z_small_block(d_model, 3)
    tok = lambda blk: pl.BlockSpec((None, tm, MIX_W), lambda bi, i: (bi, i, blk))
    tok128 = lambda blk: pl.BlockSpec((None, tm, LANES), lambda bi, i: (bi, i, blk))
    const = lambda a: pl.BlockSpec(a.shape, lambda bi, i: (0,) * a.ndim)
    consts = [prm[n] for n in ("mu_r", "mu_k", "mu_v", "mu_s2", "mu_s3", "w0", "w_up", "a0", "a_up", "g_up",
                               "k_k", "k_a", "r_k", "bones_q", "tri")]
    out_spec = pl.BlockSpec((None, tm, MIX_W), lambda bi, i: (bi, i, 0))
    return pl.pallas_call(
        _rwkv_kernel,
        grid=(b, s // tm),
        in_specs=[tok(Z_RR), tok(Z_RK), tok(Z_RV), tok128(s2_blk), tok128(s3_blk),
                  _halo_spec(MIX_W, Z_RR, tm, BF16_ROWS), _halo_spec(MIX_W, Z_RK, tm, BF16_ROWS),
                  _halo_spec(MIX_W, Z_RV, tm, BF16_ROWS), _halo_spec(LANES, s2_blk, tm, BF16_ROWS),
                  _halo_spec(LANES, s3_blk, tm, BF16_ROWS)] + [const(a) for a in consts],
        out_specs=[out_spec, out_spec, out_spec],
        out_shape=[jax.ShapeDtypeStruct((b, s, MIX_W), F32), jax.ShapeDtypeStruct((b, s, MIX_W), BF16),
                   jax.ShapeDtypeStruct((b, s, MIX_W), BF16)],
        scratch_shapes=[pltpu.VMEM((MIX_W // QUAD, QUAD, QUAD), F32)] + [pltpu.VMEM((tm, MIX_W), F32)] * 6,
        compiler_params=_cparams(("parallel", "arbitrary")),
        name="rwkv",
    )(z, z, z, z, z, z, z, z, z, z, *consts)


def _causal_conv(p, halo, cw_ref):
    nh = halo.shape[0]
    rowid = lax.broadcasted_iota(jnp.int32, (SUBLANES, p.shape[1]), 0)
    conv = p * cw_ref[CONV_W - 1:CONV_W, :]
    for back in range(1, CONV_W):
        rolled = pltpu.roll(p, back, axis=0)
        head = rolled[0:SUBLANES]
        for rr in range(back):
            head = jnp.where(rowid == rr, halo[nh - back + rr:nh - back + rr + 1, :], head)
        rolled = jnp.concatenate([head, rolled[SUBLANES:]], axis=0)
        conv = conv + rolled * cw_ref[CONV_W - 1 - back:CONV_W - back, :]
    return conv


def _merge_kernel(x_ref, oa_ref, y_ref, g_ref, bonus_ref, cb_ref, cc_ref, cx_ref, cch_ref, cxh_ref,
                  gate0_ref, gate1_ref, gate2_ref, g1_ref, lnw_ref, lnb_ref, bq_ref, cw_ref, wb_ref, wo_ref, o_ref):
    first = pl.program_id(1) == 0
    y = y_ref[...]
    inv_n = 1.0 / HEAD_DIM
    p1 = y.astype(BF16)
    mean = (_head_sum(p1, bq_ref) + _head_sum(y - p1.astype(F32), bq_ref)) * inv_n
    yc = y - mean
    var = _head_sum(yc * yc, bq_ref) * inv_n
    o_rwkv = ((yc * lax.rsqrt(var + GN_EPS) * lnw_ref[...] + lnb_ref[...] + bonus_ref[...].astype(F32))
              * g_ref[...].astype(F32))
    p = cc_ref[...].astype(F32) * cx_ref[...].astype(F32)
    ph = jnp.where(first, 0.0, cch_ref[...].astype(F32) * cxh_ref[...].astype(F32))
    o_conv = cb_ref[...].astype(F32) * _causal_conv(p, ph, cw_ref)
    merged = jnp.zeros(x_ref.shape, F32)
    for bi, (o, gate_ref) in enumerate(((oa_ref[...], gate0_ref), (o_rwkv, gate1_ref), (o_conv, gate2_ref))):
        merged = merged + jax.nn.sigmoid(gate_ref[...].astype(F32)) * _mm(o.astype(BF16), wb_ref[bi])
    o_ref[...] = x_ref[...] + g1_ref[...] * _mm(merged.astype(BF16), wo_ref[...])


def _merge(x, o_attn, y, g, bonus, z, mod5, layer, prm, tm=256):
    b, s, d = x.shape
    tm = min(tm, s)
    tok = lambda w, blk: pl.BlockSpec((None, tm, w), lambda bi, i: (bi, i, blk))
    const = lambda a: pl.BlockSpec(a.shape, lambda bi, i: (0,) * a.ndim)
    consts = [prm[n] for n in ("ln_w", "ln_b", "bones_q", "conv_w", "w_branch", "w_o")]
    gate_blk = Z_GATE_COL // d
    return pl.pallas_call(
        _merge_kernel,
        grid=(b, s // tm),
        in_specs=[tok(d, 0), tok(MIX_W, 0), tok(MIX_W, 0), tok(MIX_W, 0), tok(MIX_W, 0),
                  tok(MIX_W, Z_CB), tok(MIX_W, Z_CC), tok(MIX_W, Z_CX),
                  _halo_spec(MIX_W, Z_CC, tm, BF16_ROWS), _halo_spec(MIX_W, Z_CX, tm, BF16_ROWS),
                  tok(d, gate_blk), tok(d, gate_blk + 1), tok(d, gate_blk + 2),
                  _mod_spec(mod5, layer, 2, 2)] + [const(a) for a in consts],
        out_specs=tok(d, 0),
        out_shape=jax.ShapeDtypeStruct((b, s, d), F32),
        compiler_params=_cparams(("parallel", "parallel")),
        name="merge",
    )(x, o_attn, y, g, bonus, z, z, z, z, z, z, z, z, mod5, *consts)


def _ffn_kernel(x_ref, xh_ref, sh_ref, sc_ref, g2_ref, gain_ref, wup_ref, cw_ref, wd_ref, fin_ref, o_ref,
                *, final_norm, tf):
    first = pl.program_id(1) == 0
    x = x_ref[...]
    h = _norm_mod(x, gain_ref[...], sc_ref[...], sh_ref[...]).astype(BF16)
    hh = _norm_mod(xh_ref[...], gain_ref[...], sc_ref[...], sh_ref[...]).astype(BF16)
    d_ff = wd_ref.shape[0]
    acc = jnp.zeros(x.shape, F32)
    for jf in range(d_ff // tf):
        cols = slice(jf * tf, (jf + 1) * tf)
        gate_cols = slice(d_ff + jf * tf, d_ff + (jf + 1) * tf)
        a = _mm(h, wup_ref[:, cols])
        ah = jnp.where(first, 0.0, _mm(hh, wup_ref[:, cols]))
        conv = _causal_conv(a, ah, cw_ref[:, cols])
        u = conv * jax.nn.sigmoid(conv) * _mm(h, wup_ref[:, gate_cols])
        acc = acc + _mm(u.astype(BF16), wd_ref[cols, :])
    out = x + g2_ref[...] * acc
    if final_norm:
        out = out * lax.rsqrt(jnp.mean(out * out, axis=-1, keepdims=True) + NORM_EPS) * fin_ref[...]
    o_ref[...] = out


def _ffn(x, mod5, layer, gain, w_up, conv_w, w_down, final_gain, final_norm, tm=512, tf=1408):
    b, s, d = x.shape
    d_ff = w_down.shape[0]
    assert d_ff % tf == 0 and tf % LANES == 0
    tm = min(tm, s)
    step = tm // SUBLANES
    gain, final_gain = gain.reshape(1, d), final_gain.reshape(1, d)
    return pl.pallas_call(
        functools.partial(_ffn_kernel, final_norm=final_norm, tf=tf),
        grid=(b, s // tm),
        in_specs=[pl.BlockSpec((None, tm, d), lambda bi, i: (bi, i, 0)),
                  pl.BlockSpec((None, SUBLANES, d), lambda bi, i: (bi, jnp.maximum(i * step - 1, 0), 0)),
                  _mod_spec(mod5, layer, 3, 2), _mod_spec(mod5, layer, 4, 2), _mod_spec(mod5, layer, 5, 2),
                  _resident(gain), _resident(w_up), _resident(conv_w), _resident(w_down), _resident(final_gain)],
        out_specs=pl.BlockSpec((None, tm, d), lambda bi, i: (bi, i, 0)),
        out_shape=jax.ShapeDtypeStruct((b, s, d), F32),
        compiler_params=_cparams(("parallel", "parallel")),
        name="ffn",
    )(x, x, mod5, mod5, mod5, gain, w_up, conv_w, w_down, final_gain)


def _split_w_in(w):
    sizes = (MIX_W, MIX_W, MIX_W, HEADS * HEAD_DIM, HEAD_DIM, HEADS,
             MIX_W, MIX_W, MIX_W, LORA_DECAY, LORA_ICLR, LORA_GATE,
             MIX_W, MIX_W, MIX_W)
    out, o = [], 0
    for n in sizes:
        out.append(w[:, o:o + n])
        o += n
    out.append(w[:, o:])
    return out


def _layer_params(l, w_in, rwkv_mu, rwkv_w0, rwkv_w_up, rwkv_a0, rwkv_a_up, rwkv_g_up, rwkv_k_k, rwkv_k_a,
                  rwkv_r_k, rwkv_ln_w, rwkv_ln_b, sc_conv_w, w_branch, w_o):
    d = w_in.shape[1]
    q, k, v, qi, ki, wi, rr, rk, rv, wd, ad, gd, cb, cc, cx, gates = _split_w_in(w_in[l])
    assert Z_GATE_COL % d == 0 and gates.shape[1] == 3 * d
    w_z = jnp.concatenate([q, k, v, qi, rr, rk, rv, cb, cc, cx, gates,
                           ki, ki, wi, jnp.zeros((d, LANES - HEADS), F32), wd, ad, gd], axis=1).astype(BF16)
    mu = rwkv_mu[l]
    row = lambda a: a.reshape(1, -1)
    head_id = jnp.arange(QUAD) // HEAD_DIM
    bones_q = (head_id[:, None] == head_id[None, :]).astype(BF16)
    tri = (jnp.arange(CHUNK)[:, None] >= jnp.arange(CHUNK)[None, :]).astype(BF16)
    zl = lambda n: jnp.zeros((n, MIX_W), F32)
    return dict(
        w_z=w_z,
        mu_r=row(mu[:MIX_W]), mu_k=row(mu[MIX_W:2 * MIX_W]), mu_v=row(mu[2 * MIX_W:3 * MIX_W]),
        mu_s2=row(mu[3 * MIX_W:3 * MIX_W + LORA_DECAY + LORA_ICLR]), mu_s3=row(mu[3 * MIX_W + LORA_DECAY + LORA_ICLR:]),
        w0=row(rwkv_w0[l]), w_up=jnp.concatenate([rwkv_w_up[l], zl(LORA_ICLR)], axis=0).astype(BF16),
        a0=row(rwkv_a0[l]), a_up=jnp.concatenate([zl(LORA_DECAY), rwkv_a_up[l]], axis=0).astype(BF16),
        g_up=rwkv_g_up[l].astype(BF16), k_k=row(rwkv_k_k[l]), k_a=row(rwkv_k_a[l]), r_k=row(rwkv_r_k[l]),
        bones_q=bones_q, tri=tri,
        ln_w=row(rwkv_ln_w[l]), ln_b=row(rwkv_ln_b[l]), conv_w=sc_conv_w[l].T,
        w_branch=w_branch[l].astype(BF16), w_o=w_o[l].astype(BF16),
    )


def kernel(x, c, positions, rel_bias, final_norm, ada_w, ada_b, norm_mix, w_in, rwkv_mu, rwkv_w0, rwkv_w_up,
           rwkv_a0, rwkv_a_up, rwkv_g_up, rwkv_k_k, rwkv_k_a, rwkv_r_k, rwkv_ln_w, rwkv_ln_b, sc_conv_w,
           w_branch, w_o, norm_ffn, ffn_w_up, ffn_conv_w, ffn_w_down):
    depth, d = ada_w.shape[0], x.shape[-1]
    mod5 = _ada_mod(c, ada_w, ada_b)
    bias = _bias_tiles(rel_bias)
    for l in range(depth):
        prm = _layer_params(l, w_in, rwkv_mu, rwkv_w0, rwkv_w_up, rwkv_a0, rwkv_a_up, rwkv_g_up, rwkv_k_k,
                            rwkv_k_a, rwkv_r_k, rwkv_ln_w, rwkv_ln_b, sc_conv_w, w_branch, w_o)
        z = _inproj(x, mod5, l, norm_mix[l], prm["w_z"])
        o_attn = _dsa(z, bias, d)
        y, g, bonus = _rwkv(z, prm, d)
        x = _merge(x, o_attn, y, g, bonus, z, mod5, l, prm)
        x = _ffn(x, mod5, l, norm_ffn[l], ffn_w_up[l].astype(BF16), ffn_conv_w[l].T, ffn_w_down[l].astype(BF16),
                 final_norm, final_norm=(l == depth - 1))
    return x
```

```python
import functools
import math

import jax
import jax.numpy as jnp
from jax import lax
from jax.experimental import pallas as pl
from jax.experimental.pallas import tpu as pltpu

F32 = jnp.float32
BF16 = jnp.bfloat16
HIGHEST = lax.Precision.HIGHEST

MIX_W = 512
HEADS = 8
HEAD_DIM = 64
TOPK_MAX = 256
N_BUCKETS = 32
MAX_DISTANCE = 128
LORA_DECAY = 64
LORA_ICLR = 64
LORA_GATE = 128
CONV_W = 3
NORM_EPS = 1e-6
GN_EPS = 64e-5
NEG_INF = -1e30

LANES = 128
SUBLANES = 8
BF16_ROWS = 16
VMEM_LIMIT = 56 * 1024 * 1024

TQ = 256
CHUNK = 64
RW_TM = 256
QUAD = 4 * HEAD_DIM
assert CHUNK == HEAD_DIM

Z_Q, Z_K, Z_V, Z_QI, Z_RR, Z_RK, Z_RV, Z_CB, Z_CC, Z_CX = range(10)
Z_GATE_COL = 10 * MIX_W


def _z_small_block(d_model, i):
    return (Z_GATE_COL + 3 * d_model) // LANES + i


def _cparams(sem):
    return pltpu.CompilerParams(dimension_semantics=sem, vmem_limit_bytes=VMEM_LIMIT)


def _nt(a, b, precision=None):
    return lax.dot_general(a, b, (((1,), (1,)), ((), ())), precision=precision,
                           preferred_element_type=F32)


def _tn(a, b, precision=None):
    return lax.dot_general(a, b, (((0,), (0,)), ((), ())), precision=precision,
                           preferred_element_type=F32)


def _mm(a, b, precision=None):
    return jnp.dot(a, b, precision=precision, preferred_element_type=F32)


def _bmm(a, b):
    return _mm(a.astype(BF16), b.astype(BF16))


def _mod_kernel(c_ref, w_ref, b_ref, o_ref):
    o_ref[...] = _mm(c_ref[...], w_ref[...], HIGHEST) + b_ref[...]


def _ada_mod(c, ada_w, ada_b):
    depth, d, d6 = ada_w.shape
    b = c.shape[0]
    out = pl.pallas_call(
        _mod_kernel,
        grid=(depth, d6 // d),
        in_specs=[pl.BlockSpec((b, d), lambda l, j: (0, 0)),
                  pl.BlockSpec((None, d, d), lambda l, j: (l, 0, j)),
                  pl.BlockSpec((None, 1, d), lambda l, j: (l, 0, j))],
        out_specs=pl.BlockSpec((None, b, d), lambda l, j: (l, 0, j)),
        out_shape=jax.ShapeDtypeStruct((depth, b, d6), F32),
        compiler_params=_cparams(("parallel", "parallel")),
        name="ada_mod",
    )(c, ada_w, ada_b.reshape(depth, 1, d6))
    return out.reshape(depth, b, d6 // d, 1, d)


def _mod_spec(mod5, layer, which, ngrid):
    d = mod5.shape[-1]
    if ngrid == 2:
        return pl.BlockSpec((None, None, None, 1, d), lambda b, i: (layer, b, which, 0, 0))
    return pl.BlockSpec((None, None, None, 1, d), lambda b, i, j: (layer, b, which, 0, 0))


def _norm_mod(x, gain, scale, shift):
    y = x * lax.rsqrt(jnp.mean(x * x, axis=-1, keepdims=True) + NORM_EPS) * gain
    return y * (1.0 + scale) + shift


def _resident(a):
    return pl.BlockSpec(a.shape, lambda bi, i: (0,) * a.ndim, pipeline_mode=pl.Buffered(1))


def _inproj_kernel(x_ref, sh_ref, sc_ref, g_ref, w_ref, o_ref, *, n_col_chunks):
    h = _norm_mod(x_ref[...], g_ref[...], sc_ref[...], sh_ref[...]).astype(BF16)
    tn = w_ref.shape[1] // n_col_chunks
    for c in range(n_col_chunks):
        o_ref[:, c * tn:(c + 1) * tn] = _mm(h, w_ref[:, c * tn:(c + 1) * tn]).astype(o_ref.dtype)


def _inproj(x, mod5, layer, gain, w, tm=512, n_col_chunks=4):
    b, s, d = x.shape
    zc = w.shape[1]
    tm = min(tm, s)
    assert zc % (n_col_chunks * LANES) == 0
    gain = gain.reshape(1, d)
    return pl.pallas_call(
        functools.partial(_inproj_kernel, n_col_chunks=n_col_chunks),
        grid=(b, s // tm),
        in_specs=[pl.BlockSpec((None, tm, d), lambda bi, i: (bi, i, 0)),
                  _mod_spec(mod5, layer, 0, 2),
                  _mod_spec(mod5, layer, 1, 2),
                  _resident(gain), _resident(w)],
        out_specs=pl.BlockSpec((None, tm, zc), lambda bi, i: (bi, i, 0)),
        out_shape=jax.ShapeDtypeStruct((b, s, zc), BF16),
        compiler_params=_cparams(("parallel", "parallel")),
        name="inproj",
    )(x, mod5, mod5, gain, w)


def _bias_kernel(rb_ref, o_ref):
    which = pl.program_id(0)
    h = pl.program_id(1)
    ri = lax.broadcasted_iota(jnp.int32, (TQ, TQ), 0)
    ci = lax.broadcasted_iota(jnp.int32, (TQ, TQ), 1)
    dist = ci - ri + (1 - which) * TQ
    n = jnp.maximum(dist, 0)
    max_exact = N_BUCKETS // 2
    nf = jnp.maximum(n, 1).astype(F32)
    large = max_exact + (jnp.log(nf / max_exact) / math.log(MAX_DISTANCE / max_exact)
                         * (N_BUCKETS - max_exact)).astype(jnp.int32)
    large = jnp.minimum(large, N_BUCKETS - 1)
    bucket = jnp.where(n < max_exact, n, large)
    far = rb_ref[N_BUCKETS - 1, h]
    acc = jnp.zeros((TQ, TQ), F32)
    for bkt in range(N_BUCKETS - 1):
        acc = jnp.where(bucket == bkt, rb_ref[bkt, h] - far, acc)
    o_ref[...] = acc * LOG2E


def _bias_tiles(rel_bias):
    assert TQ >= MAX_DISTANCE
    return pl.pallas_call(
        _bias_kernel,
        grid=(2, HEADS),
        in_specs=[pl.BlockSpec(memory_space=pltpu.SMEM)],
        out_specs=pl.BlockSpec((None, None, TQ, TQ), lambda w, h: (w, h, 0, 0)),
        out_shape=jax.ShapeDtypeStruct((2, HEADS, TQ, TQ), F32),
        compiler_params=_cparams(("parallel", "parallel")),
        name="bias_tiles",
    )(rel_bias)


BISECT_MAX_IT = 300
COARSE_STEPS = 10
FINE_STEPS_UNCHECKED = 5
FINE_STEPS_PER_CHECK = 3
LOG2E = math.log2(math.e)
ONES_ROWS = BF16_ROWS


def _row_groups(t):
    return [t[r * SUBLANES:(r + 1) * SUBLANES, :] for r in range(t.shape[0] // SUBLANES)]


def _dsa_kernel(q_ref, k_ref, v_ref, qi_ref, ki_ref, wi_ref, bias_ref, o_ref,
                sc_ref, scb_ref, qm_ref, qim_ref, wt_ref, vt_ref, m_ref, l_ref, acc_ref, s_ref, *, n_keep, seq):
    j = pl.program_id(1)
    nt = j + 1
    kf = float(n_keep)
    lane = lax.broadcasted_iota(jnp.int32, (TQ, LANES), 1)
    att_scale = HEAD_DIM ** -0.5 * LOG2E
    w_scale = (HEADS ** -0.5) * (HEAD_DIM ** -0.5)

    for h in range(HEADS):
        p, odd = divmod(h, 2)
        hm = (lane >= HEAD_DIM) if odd else (lane < HEAD_DIM)
        qs = q_ref[:, p * LANES:(p + 1) * LANES]
        qm_ref[h] = jnp.where(hm, qs, jnp.zeros_like(qs)) * att_scale
        qis = qi_ref[:, p * LANES:(p + 1) * LANES]
        qim_ref[h] = jnp.where(hm, qis, jnp.zeros_like(qis))
    wt_ref[...] = wi_ref[...].astype(F32).T * w_scale
    v_t = v_ref[...].T
    for h in range(HEADS):
        vt_ref[j, h, 0:HEAD_DIM, :] = v_t[h * HEAD_DIM:(h + 1) * HEAD_DIM, :]
        vt_ref[j, h, HEAD_DIM:HEAD_DIM + ONES_ROWS, :] = jnp.ones((ONES_ROWS, TQ), BF16)

    key = lax.broadcasted_iota(jnp.int32, (TQ, TQ), 0)
    qry = lax.broadcasted_iota(jnp.int32, (TQ, TQ), 1) + j * TQ

    def floor_bf16(x):
        xi = lax.bitcast_convert_type(x, jnp.int32)
        xi = jnp.where(xi < 0, xi + jnp.int32(0xFFFF), xi) & jnp.int32(-0x10000)
        return lax.bitcast_convert_type(xi, F32).astype(BF16)

    def score_tile(kt):
        ki_t = ki_ref[pl.ds(pl.multiple_of(kt * TQ, TQ), TQ), :]
        acc = jnp.zeros((TQ, TQ), F32)
        for h in range(HEADS):
            acc = acc + jnp.maximum(_nt(ki_t, qim_ref[h]), 0.0) * wt_ref[h:h + 1, :]
        sc_ref[kt] = jnp.where(key + kt * TQ <= qry, acc, NEG_INF)

    def stats_tile(kt, c):
        mx, mn, mp, cp, cn = c
        score = sc_ref[kt]
        scb_ref[kt] = floor_bf16(score)
        for g in _row_groups(score):
            pos = g > 0.0
            mx = jnp.maximum(mx, g)
            mn = jnp.minimum(mn, jnp.where(g > 0.5 * NEG_INF, g, -NEG_INF))
            mp = jnp.minimum(mp, jnp.where(pos, g, -NEG_INF))
            cp = cp + jnp.where(pos, 1.0, 0.0)
            cn = cn + jnp.where(g >= 0.0, 1.0, 0.0)
        return mx, mn, mp, cp, cn

    def idx_tile(kt, c):
        c = stats_tile(kt - 1, c)
        score_tile(kt)
        return c

    part = lambda v: jnp.full((SUBLANES, TQ), v, F32)
    score_tile(0)
    stats = lax.fori_loop(1, nt, idx_tile, (part(NEG_INF), part(-NEG_INF), part(-NEG_INF), part(0.0), part(0.0)))
    mx, mn, mp, cp, cn = stats_tile(nt - 1, stats)

    rmax = jnp.max(mx, axis=0, keepdims=True)
    rmin = jnp.min(mn, axis=0, keepdims=True)
    minpos = jnp.min(mp, axis=0, keepdims=True)
    cpos = jnp.sum(cp, axis=0, keepdims=True)
    cnn = jnp.sum(cn, axis=0, keepdims=True)
    nvalid = (lax.broadcasted_iota(jnp.int32, (1, TQ), 1) + j * TQ + 1).astype(F32)
    small = nvalid <= kf

    def count_ge(thr):
        def body(kt, acc):
            for g in _row_groups(sc_ref[kt]):
                acc = acc + jnp.where(g >= thr, 1.0, 0.0)
            return acc
        acc = lax.fori_loop(0, nt, body, jnp.zeros((SUBLANES, TQ), F32))
        return jnp.sum(acc, axis=0, keepdims=True)

    ztie = jnp.logical_and(cpos < kf, cnn >= kf)
    pos_side = cpos >= kf
    hi_top = rmax + jnp.maximum(jnp.abs(rmax) * 1e-6, 1e-30)
    lo0 = jnp.where(small, 0.5 * NEG_INF, jnp.where(ztie, 0.0, jnp.where(pos_side, minpos, rmin)))
    hi0 = jnp.where(small, -NEG_INF, jnp.where(ztie, minpos, jnp.where(pos_side, hi_top, 0.0)))
    clo0 = jnp.where(jnp.logical_or(small, ztie), jnp.where(small, nvalid, cnn), jnp.where(pos_side, cpos, nvalid))
    chi0 = jnp.where(small, 0.0, jnp.where(ztie, cpos, jnp.where(pos_side, 0.0, cnn)))
    done0 = jnp.where(jnp.logical_or(jnp.logical_or(small, ztie), clo0 == kf), 1.0, 0.0)

    def bisect_step(state, mid, cnt, usable):
        lo, hi, clo, chi, done = state
        act = jnp.logical_and(done < 0.5, usable)
        ge = cnt >= kf
        up_lo = jnp.logical_and(act, ge)
        up_hi = jnp.logical_and(act, jnp.logical_not(ge))
        lo = jnp.where(up_lo, mid, lo)
        clo = jnp.where(up_lo, cnt, clo)
        hi = jnp.where(up_hi, mid, hi)
        chi = jnp.where(up_hi, cnt, chi)
        done = jnp.where(jnp.logical_and(act, cnt == kf), 1.0, done)
        return lo, hi, clo, chi, done

    def count_ge_bf16(thr):
        one, zero = jnp.ones((), BF16), jnp.zeros((), BF16)

        def body(kt, acc):
            t = scb_ref[kt]
            part = jnp.zeros((BF16_ROWS, TQ), BF16)
            for r in range(TQ // BF16_ROWS):
                part = part + jnp.where(t[r * BF16_ROWS:(r + 1) * BF16_ROWS, :] >= thr, one, zero)
            return acc + part.astype(F32)
        acc = lax.fori_loop(0, nt, body, jnp.zeros((BF16_ROWS, TQ), F32))
        return jnp.sum(acc, axis=0, keepdims=True)

    def coarse_step(_, state):
        lo, hi = state[0], state[1]
        mid_b = (lo + 0.5 * (hi - lo)).astype(BF16)
        mid = mid_b.astype(F32)
        usable = jnp.logical_and(mid > lo, mid < hi)
        return bisect_step(state, mid, count_ge_bf16(mid_b), usable)

    state = lax.fori_loop(0, COARSE_STEPS, coarse_step, (lo0, hi0, clo0, chi0, done0))

    def fine_step(_, state):
        lo, hi = state[0], state[1]
        mid = lo + 0.5 * (hi - lo)
        stalled = jnp.logical_or(mid <= lo, mid >= hi)
        state = bisect_step(state, mid, count_ge(mid), jnp.logical_not(stalled))
        return state[:4] + (jnp.where(stalled, 1.0, state[4]),)

    state = lax.fori_loop(0, FINE_STEPS_UNCHECKED, fine_step, state)

    def bis_cond(c):
        return jnp.logical_and(c[0] < BISECT_MAX_IT, c[2] > 0.0)

    def bis_body(c):
        it, state, _ = c
        state = lax.fori_loop(0, FINE_STEPS_PER_CHECK, fine_step, state)
        return it + FINE_STEPS_PER_CHECK, state, jnp.sum(1.0 - state[4])

    _, (lo, hi, clo, chi, _), _ = lax.while_loop(
        bis_cond, bis_body, (jnp.int32(0), state, jnp.sum(1.0 - state[4])))

    tie = jnp.logical_and(clo > kf, jnp.logical_not(small))
    band_quota = jnp.where(tie, kf - chi, float(seq))
    prefix_ones = jnp.where(key >= lax.broadcasted_iota(jnp.int32, (TQ, TQ), 1), 1.0, 0.0).astype(BF16)

    def mask_tiles(kts, before):
        tiles = [sc_ref[kt] for kt in kts]
        bands = [jnp.where(t >= lo, jnp.where(t < hi, 1.0, 0.0), 0.0).astype(BF16) for t in tiles]
        ranks = [_mm(prefix_ones, band) for band in bands]
        masks = []
        for t, rank in zip(tiles, ranks):
            rank = rank + before
            keep_band = jnp.where(rank <= band_quota, 0.0, NEG_INF)
            masks.append(jnp.where(t >= lo, jnp.where(t >= hi, 0.0, keep_band), NEG_INF))
            before = rank[TQ - 1:TQ, :]
        for kt, mask in zip(kts, masks):
            sc_ref[kt] = mask
        return before

    n_pairs = lax.shift_right_logical(nt, 1)
    before = lax.fori_loop(0, n_pairs, lambda i, c: mask_tiles([2 * i, 2 * i + 1], c), jnp.zeros((1, TQ), F32))
    lax.fori_loop(2 * n_pairs, nt, lambda kt, c: mask_tiles([kt], c), before)

    m_ref[...] = jnp.full(m_ref.shape, NEG_INF, F32)
    l_ref[...] = jnp.zeros(l_ref.shape, F32)
    acc_ref[...] = jnp.zeros(acc_ref.shape, F32)

    def attn_tiles(kts, near):
        tile_max = [None] * HEADS
        for i, kt in enumerate(kts):
            rows = pl.ds(pl.multiple_of(kt * TQ, TQ), TQ)
            mask_add = sc_ref[kt]
            for h in range(HEADS):
                p = h // 2
                s = _nt(k_ref[rows, p * LANES:(p + 1) * LANES], qm_ref[h]) + mask_add
                if near:
                    s = s + bias_ref[kt - j + 1, h]
                s_ref[i, h] = s
                mx = jnp.max(s, axis=0, keepdims=True)
                tile_max[h] = mx if i == 0 else jnp.maximum(tile_max[h], mx)
        for h in range(HEADS):
            m_old = m_ref[h]
            m_new = jnp.maximum(m_old, tile_max[h])
            alpha = jnp.exp2(m_old - m_new)
            pv = None
            for i, kt in enumerate(kts):
                part = _mm(vt_ref[kt, h], jnp.exp2(s_ref[i, h] - m_new).astype(BF16))
                pv = part if i == 0 else pv + part
            m_ref[h] = m_new
            l_ref[h] = alpha * l_ref[h] + pv[HEAD_DIM:HEAD_DIM + 1, :]
            acc_ref[h] = alpha * acc_ref[h] + pv[0:HEAD_DIM, :]

    n_far = jnp.maximum(j - 1, 0)
    n_far_pairs = lax.shift_right_logical(n_far, 1)

    def far_pair(i, carry):
        attn_tiles([2 * i, 2 * i + 1], near=False)
        return carry

    def far_single(kt, carry):
        attn_tiles([kt], near=False)
        return carry

    lax.fori_loop(0, n_far_pairs, far_pair, 0)
    lax.fori_loop(2 * n_far_pairs, n_far, far_single, 0)

    @pl.when(j >= 1)
    def _():
        attn_tiles([j - 1, j], near=True)

    @pl.when(j == 0)
    def _():
        attn_tiles([j], near=True)

    for p in range(HEADS // 2):
        pair = jnp.concatenate([acc_ref[2 * p] / l_ref[2 * p], acc_ref[2 * p + 1] / l_ref[2 * p + 1]], axis=0)
        o_ref[:, p * LANES:(p + 1) * LANES] = pair.T.astype(o_ref.dtype)


def _dsa(z, bias, d_model):
    b, s, _ = z.shape
    n_keep = min(TOPK_MAX, s // 4)
    kern = functools.partial(_dsa_kernel, n_keep=n_keep, seq=s)
    ki_blk, wi_blk = _z_small_block(d_model, 0), _z_small_block(d_model, 1)
    return pl.pallas_call(
        kern,
        grid=(b, s // TQ),
        in_specs=[pl.BlockSpec((None, TQ, MIX_W), lambda bi, j: (bi, j, Z_Q)),
                  pl.BlockSpec((None, s, MIX_W), lambda bi, j: (bi, 0, Z_K)),
                  pl.BlockSpec((None, TQ, MIX_W), lambda bi, j: (bi, j, Z_V)),
                  pl.BlockSpec((None, TQ, MIX_W), lambda bi, j: (bi, j, Z_QI)),
                  pl.BlockSpec((None, s, LANES), lambda bi, j: (bi, 0, ki_blk)),
                  pl.BlockSpec((None, TQ, LANES), lambda bi, j: (bi, j, wi_blk)),
                  pl.BlockSpec((2, HEADS, TQ, TQ), lambda bi, j: (0, 0, 0, 0))],
        out_specs=pl.BlockSpec((None, TQ, MIX_W), lambda bi, j: (bi, j, 0)),
        out_shape=jax.ShapeDtypeStruct((b, s, MIX_W), BF16),
        scratch_shapes=[pltpu.VMEM((s // TQ, TQ, TQ), F32),
                        pltpu.VMEM((s // TQ, TQ, TQ), BF16),
                        pltpu.VMEM((HEADS, TQ, LANES), BF16),
                        pltpu.VMEM((HEADS, TQ, LANES), BF16),
                        pltpu.VMEM((LANES, TQ), F32),
                        pltpu.VMEM((s // TQ, HEADS, HEAD_DIM + ONES_ROWS, TQ), BF16),
                        pltpu.VMEM((HEADS, 1, TQ), F32),
                        pltpu.VMEM((HEADS, 1, TQ), F32),
                        pltpu.VMEM((HEADS, HEAD_DIM, TQ), F32),
                        pltpu.VMEM((2, HEADS, TQ, TQ), F32)],
        compiler_params=_cparams(("parallel", "arbitrary")),
        name="dsa",
    )(z, z, z, z, z, z, bias)


def _halo_spec(width, blk, tm, rows):
    step = tm // rows
    return pl.BlockSpec((None, rows, width), lambda bi, i: (bi, jnp.maximum(i * step - 1, 0), blk))


def _shift_lerp(cur_ref, halo_ref, mu, first):
    cur = cur_ref[...].astype(F32)
    nh = halo_ref.shape[0]
    prev_last = jnp.where(first, 0.0, halo_ref[nh - 1:nh, :].astype(F32))
    rolled = pltpu.roll(cur, 1, axis=0)
    rowid = lax.broadcasted_iota(jnp.int32, cur.shape, 0)
    sh = jnp.where(rowid == 0, prev_last, rolled)
    return cur + (sh - cur) * mu


def _head_sum(x, bq_ref):
    xb = x.astype(BF16)
    return jnp.concatenate([_mm(xb[:, q * QUAD:(q + 1) * QUAD], bq_ref[...]) for q in range(MIX_W // QUAD)], axis=1)


def _rwkv_kernel(r_ref, k_ref, v_ref, s2_ref, s3_ref, rh_ref, kh_ref, vh_ref, s2h_ref, s3h_ref,
                 mu_r, mu_k, mu_v, mu_s2, mu_s3, w0_ref, wup_ref, a0_ref, aup_ref, gup_ref,
                 kk_ref, ka_ref, rk_ref, bq_ref, tri_ref,
                 y_ref, g_ref, bonus_ref,
                 st_ref, rs, ls, ks, vs, kks, kbs):
    first = pl.program_id(1) == 0

    @pl.when(first)
    def _():
        st_ref[...] = jnp.zeros_like(st_ref)

    r = _shift_lerp(r_ref, rh_ref, mu_r[...], first)
    k = _shift_lerp(k_ref, kh_ref, mu_k[...], first)
    v = _shift_lerp(v_ref, vh_ref, mu_v[...], first)
    s2 = _shift_lerp(s2_ref, s2h_ref, mu_s2[...], first)
    s3 = _shift_lerp(s3_ref, s3h_ref, mu_s3[...], first)
    xw = w0_ref[...] + _bmm(jnp.tanh(s2), wup_ref[...])
    softplus = jnp.maximum(-xw, 0.0) + jnp.log1p(jnp.exp(-jnp.abs(xw)))
    ld = -jnp.exp(-softplus - 0.5)
    af = jax.nn.sigmoid(a0_ref[...] + _bmm(s2, aup_ref[...]))
    g_ref[...] = _bmm(jax.nn.sigmoid(s3), gup_ref[...]).astype(g_ref.dtype)
    kkr = k * kk_ref[...]
    kkn = kkr / jnp.maximum(jnp.sqrt(_head_sum(kkr * kkr, bq_ref)), 1e-12)
    kmod = k * (1.0 + (af - 1.0) * ka_ref[...])
    bonus_ref[...] = (_head_sum(r * kmod * rk_ref[...], bq_ref) * v).astype(bonus_ref.dtype)
    rs[...] = r
    ls[...] = ld
    ks[...] = kmod
    vs[...] = v
    kks[...] = kkn
    kbs[...] = kkn * af

    ri = lax.broadcasted_iota(jnp.int32, (QUAD, QUAD), 0)
    ci = lax.broadcasted_iota(jnp.int32, (QUAD, QUAD), 1)
    same_head = (ri // HEAD_DIM) == (ci // HEAD_DIM)
    strict = jnp.logical_and(same_head, (ri % CHUNK) > (ci % CHUNK))
    incl = jnp.logical_and(same_head, (ri % CHUNK) >= (ci % CHUNK))
    eye = jnp.where(ri == ci, 1.0, 0.0)

    def stack(x):
        return jnp.where(same_head, jnp.concatenate([x] * 4, axis=0), 0.0).astype(BF16)

    def tile4(x):
        return jnp.concatenate([x] * 4, axis=0).astype(BF16)

    def unstack(x):
        return (x[0:CHUNK] + x[CHUNK:2 * CHUNK]) + (x[2 * CHUNK:3 * CHUNK] + x[3 * CHUNK:4 * CHUNK])

    def chunk_operands(c):
        rows = pl.ds(pl.multiple_of(c * CHUNK, CHUNK), CHUNK)
        ldc = ls[rows, :]
        p1 = ldc.astype(BF16)
        e1 = ldc - p1.astype(F32)
        p2 = e1.astype(BF16)
        p3 = (e1 - p2.astype(F32)).astype(BF16)
        tri = tri_ref[...]
        cl = (_mm(tri, p1) + _mm(tri, p2)) + _mm(tri, p3)
        cl_end = cl[CHUNK - 1:CHUNK, :]
        e_in = jnp.exp(cl)
        e_out = jnp.exp(-cl)
        e_end = jnp.exp(cl_end - cl)
        rt_all = rs[rows, :] * e_in
        at_all = -kks[rows, :] * jnp.exp(cl - ldc)
        bt_all = kbs[rows, :] * e_out
        kt_all = ks[rows, :] * e_out
        bg_all = (kbs[rows, :] * e_end).astype(BF16)
        kg_all = (ks[rows, :] * e_end).astype(BF16)
        v_all = vs[rows, :]
        gam_all = jnp.exp(cl_end)
        units = []
        for q in range(MIX_W // QUAD):
            sl = slice(q * QUAD, (q + 1) * QUAD)
            units.append(dict(rows=rows, sl=sl, q=q, rt=rt_all[:, sl], vv=v_all[:, sl].astype(BF16),
                              a4=stack(at_all[:, sl]), r4=stack(rt_all[:, sl]), v4=stack(v_all[:, sl]),
                              bt4=tile4(bt_all[:, sl]), kt4=tile4(kt_all[:, sl]),
                              bg=bg_all[:, sl], kg=kg_all[:, sl], gam=gam_all[:, sl]))
        return units

    n_doublings = int(math.log2(CHUNK)) - 1

    def chunk_pair(cp, carry):
        us = chunk_operands(2 * cp) + chunk_operands(2 * cp + 1)
        for u in us:
            u["m_ab"] = jnp.where(strict, _nt(u["a4"], u["bt4"]), 0.0)
        for u in us:
            u["m_ak"] = jnp.where(strict, _nt(u["a4"], u["kt4"]), 0.0)
        for u in us:
            u["m_rb"] = jnp.where(incl, _nt(u["r4"], u["bt4"]), 0.0).astype(BF16)
        for u in us:
            u["m_rk"] = jnp.where(incl, _nt(u["r4"], u["kt4"]), 0.0).astype(BF16)
        for u in us:
            u["pw"] = u["m_ab"]
            u["inv"] = eye + u["m_ab"]
        for _ in range(n_doublings):
            for u in us:
                u["pw"] = _bmm(u["pw"], u["pw"])
            for u in us:
                u["inv"] = u["inv"] + _bmm(u["inv"], u["pw"])
        for u in us:
            u["inv"] = u["inv"].astype(BF16)
            u["mv4"] = _bmm(u["m_ak"], u["v4"])
        for u in us:
            u["ah4"] = _mm(u["inv"], u["a4"])
        for u in us:
            u["uh4"] = _bmm(u["inv"], u["mv4"])
        for u in us:
            u["ry"] = (u["rt"] + unstack(_bmm(u["m_rb"], u["ah4"]))).astype(BF16)
        for u in us:
            u["y0"] = unstack(_bmm(u["m_rb"], u["uh4"]) + _mm(u["m_rk"], u["v4"]))
        for u in us:
            ah, uh = unstack(u["ah4"]).astype(BF16), unstack(u["uh4"]).astype(BF16)
            u["g_low"] = jnp.where(same_head, _tn(u["bg"], ah), 0.0).astype(BF16)
            u["h_t"] = jnp.where(same_head, _tn(uh, u["bg"]) + _tn(u["vv"], u["kg"]), 0.0)
        for u in us:
            st = st_ref[u["q"]]
            stb = st.astype(BF16)
            y_ref[u["rows"], u["sl"]] = _nt(u["ry"], stb) + u["y0"]
            st_ref[u["q"]] = st * u["gam"] + _nt(stb, u["g_low"]) + u["h_t"]
        return carry

    lax.fori_loop(0, r_ref.shape[0] // (2 * CHUNK), chunk_pair, 0)


def _rwkv(z, prm, d_model):
    b, s, _ = z.shape
    tm = min(RW_TM, s)
    s2_blk, s3_blk = _z_small_block(d_model, 2), _z_small_block(d_model, 3)
    tok = lambda blk: pl.BlockSpec((None, tm, MIX_W), lambda bi, i: (bi, i, blk))
    tok128 = lambda blk: pl.BlockSpec((None, tm, LANES), lambda bi, i: (bi, i, blk))
    const = lambda a: pl.BlockSpec(a.shape, lambda bi, i: (0,) * a.ndim)
    consts = [prm[n] for n in ("mu_r", "mu_k", "mu_v", "mu_s2", "mu_s3", "w0", "w_up", "a0", "a_up", "g_up",
                               "k_k", "k_a", "r_k", "bones_q", "tri")]
    out_spec = pl.BlockSpec((None, tm, MIX_W), lambda bi, i: (bi, i, 0))
    return pl.pallas_call(
        _rwkv_kernel,
        grid=(b, s // tm),
        in_specs=[tok(Z_RR), tok(Z_RK), tok(Z_RV), tok128(s2_blk), tok128(s3_blk),
                  _halo_spec(MIX_W, Z_RR, tm, BF16_ROWS), _halo_spec(MIX_W, Z_RK, tm, BF16_ROWS),
                  _halo_spec(MIX_W, Z_RV, tm, BF16_ROWS), _halo_spec(LANES, s2_blk, tm, BF16_ROWS),
                  _halo_spec(LANES, s3_blk, tm, BF16_ROWS)] + [const(a) for a in consts],
        out_specs=[out_spec, out_spec, out_spec],
        out_shape=[jax.ShapeDtypeStruct((b, s, MIX_W), F32), jax.ShapeDtypeStruct((b, s, MIX_W), BF16),
                   jax.ShapeDtypeStruct((b, s, MIX_W), BF16)],
        scratch_shapes=[pltpu.VMEM((MIX_W // QUAD, QUAD, QUAD), F32)] + [pltpu.VMEM((tm, MIX_W), F32)] * 6,
        compiler_params=_cparams(("parallel", "arbitrary")),
        name="rwkv",
    )(z, z, z, z, z, z, z, z, z, z, *consts)


def _causal_conv(p, halo, cw_ref):
    nh = halo.shape[0]
    rowid = lax.broadcasted_iota(jnp.int32, (SUBLANES, p.shape[1]), 0)
    conv = p * cw_ref[CONV_W - 1:CONV_W, :]
    for back in range(1, CONV_W):
        rolled = pltpu.roll(p, back, axis=0)
        head = rolled[0:SUBLANES]
        for rr in range(back):
            head = jnp.where(rowid == rr, halo[nh - back + rr:nh - back + rr + 1, :], head)
        rolled = jnp.concatenate([head, rolled[SUBLANES:]], axis=0)
        conv = conv + rolled * cw_ref[CONV_W - 1 - back:CONV_W - back, :]
    return conv


def _merge_kernel(x_ref, oa_ref, y_ref, g_ref, bonus_ref, cb_ref, cc_ref, cx_ref, cch_ref, cxh_ref,
                  gate0_ref, gate1_ref, gate2_ref, g1_ref, lnw_ref, lnb_ref, bq_ref, cw_ref, wb_ref, wo_ref, o_ref):
    first = pl.program_id(1) == 0
    y = y_ref[...]
    inv_n = 1.0 / HEAD_DIM
    p1 = y.astype(BF16)
    mean = (_head_sum(p1, bq_ref) + _head_sum(y - p1.astype(F32), bq_ref)) * inv_n
    yc = y - mean
    var = _head_sum(yc * yc, bq_ref) * inv_n
    o_rwkv = ((yc * lax.rsqrt(var + GN_EPS) * lnw_ref[...] + lnb_ref[...] + bonus_ref[...].astype(F32))
              * g_ref[...].astype(F32))
    p = cc_ref[...].astype(F32) * cx_ref[...].astype(F32)
    ph = jnp.where(first, 0.0, cch_ref[...].astype(F32) * cxh_ref[...].astype(F32))
    o_conv = cb_ref[...].astype(F32) * _causal_conv(p, ph, cw_ref)
    merged = jnp.zeros(x_ref.shape, F32)
    for bi, (o, gate_ref) in enumerate(((oa_ref[...], gate0_ref), (o_rwkv, gate1_ref), (o_conv, gate2_ref))):
        merged = merged + jax.nn.sigmoid(gate_ref[...].astype(F32)) * _mm(o.astype(BF16), wb_ref[bi])
    o_ref[...] = x_ref[...] + g1_ref[...] * _mm(merged.astype(BF16), wo_ref[...])


def _merge(x, o_attn, y, g, bonus, z, mod5, layer, prm, tm=256):
    b, s, d = x.shape
    tm = min(tm, s)
    tok = lambda w, blk: pl.BlockSpec((None, tm, w), lambda bi, i: (bi, i, blk))
    const = lambda a: pl.BlockSpec(a.shape, lambda bi, i: (0,) * a.ndim)
    consts = [prm[n] for n in ("ln_w", "ln_b", "bones_q", "conv_w", "w_branch", "w_o")]
    gate_blk = Z_GATE_COL // d
    return pl.pallas_call(
        _merge_kernel,
        grid=(b, s // tm),
        in_specs=[tok(d, 0), tok(MIX_W, 0), tok(MIX_W, 0), tok(MIX_W, 0), tok(MIX_W, 0),
                  tok(MIX_W, Z_CB), tok(MIX_W, Z_CC), tok(MIX_W, Z_CX),
                  _halo_spec(MIX_W, Z_CC, tm, BF16_ROWS), _halo_spec(MIX_W, Z_CX, tm, BF16_ROWS),
                  tok(d, gate_blk), tok(d, gate_blk + 1), tok(d, gate_blk + 2),
                  _mod_spec(mod5, layer, 2, 2)] + [const(a) for a in consts],
        out_specs=tok(d, 0),
        out_shape=jax.ShapeDtypeStruct((b, s, d), F32),
        compiler_params=_cparams(("parallel", "parallel")),
        name="merge",
    )(x, o_attn, y, g, bonus, z, z, z, z, z, z, z, z, mod5, *consts)


def _ffn_kernel(x_ref, xh_ref, sh_ref, sc_ref, g2_ref, gain_ref, wup_ref, cw_ref, wd_ref, fin_ref, o_ref,
                *, final_norm, tf):
    first = pl.program_id(1) == 0
    x = x_ref[...]
    h = _norm_mod(x, gain_ref[...], sc_ref[...], sh_ref[...]).astype(BF16)
    hh = _norm_mod(xh_ref[...], gain_ref[...], sc_ref[...], sh_ref[...]).astype(BF16)
    d_ff = wd_ref.shape[0]
    acc = jnp.zeros(x.shape, F32)
    for jf in range(d_ff // tf):
        cols = slice(jf * tf, (jf + 1) * tf)
        gate_cols = slice(d_ff + jf * tf, d_ff + (jf + 1) * tf)
        a = _mm(h, wup_ref[:, cols])
        ah = jnp.where(first, 0.0, _mm(hh, wup_ref[:, cols]))
        conv = _causal_conv(a, ah, cw_ref[:, cols])
        u = conv * jax.nn.sigmoid(conv) * _mm(h, wup_ref[:, gate_cols])
        acc = acc + _mm(u.astype(BF16), wd_ref[cols, :])
    out = x + g2_ref[...] * acc
    if final_norm:
        out = out * lax.rsqrt(jnp.mean(out * out, axis=-1, keepdims=True) + NORM_EPS) * fin_ref[...]
    o_ref[...] = out


def _ffn(x, mod5, layer, gain, w_up, conv_w, w_down, final_gain, final_norm, tm=512, tf=1408):
    b, s, d = x.shape
    d_ff = w_down.shape[0]
    assert d_ff % tf == 0 and tf % LANES == 0
    tm = min(tm, s)
    step = tm // SUBLANES
    gain, final_gain = gain.reshape(1, d), final_gain.reshape(1, d)
    return pl.pallas_call(
        functools.partial(_ffn_kernel, final_norm=final_norm, tf=tf),
        grid=(b, s // tm),
        in_specs=[pl.BlockSpec((None, tm, d), lambda bi, i: (bi, i, 0)),
                  pl.BlockSpec((None, SUBLANES, d), lambda bi, i: (bi, jnp.maximum(i * step - 1, 0), 0)),
                  _mod_spec(mod5, layer, 3, 2), _mod_spec(mod5, layer, 4, 2), _mod_spec(mod5, layer, 5, 2),
                  _resident(gain), _resident(w_up), _resident(conv_w), _resident(w_down), _resident(final_gain)],
        out_specs=pl.BlockSpec((None, tm, d), lambda bi, i: (bi, i, 0)),
        out_shape=jax.ShapeDtypeStruct((b, s, d), F32),
        compiler_params=_cparams(("parallel", "parallel")),
        name="ffn",
    )(x, x, mod5, mod5, mod5, gain, w_up, conv_w, w_down, final_gain)


def _split_w_in(w):
    sizes = (MIX_W, MIX_W, MIX_W, HEADS * HEAD_DIM, HEAD_DIM, HEADS,
             MIX_W, MIX_W, MIX_W, LORA_DECAY, LORA_ICLR, LORA_GATE,
             MIX_W, MIX_W, MIX_W)
    out, o = [], 0
    for n in sizes:
        out.append(w[:, o:o + n])
        o += n
    out.append(w[:, o:])
    return out


def _layer_params(l, w_in, rwkv_mu, rwkv_w0, rwkv_w_up, rwkv_a0, rwkv_a_up, rwkv_g_up, rwkv_k_k, rwkv_k_a,
                  rwkv_r_k, rwkv_ln_w, rwkv_ln_b, sc_conv_w, w_branch, w_o):
    d = w_in.shape[1]
    q, k, v, qi, ki, wi, rr, rk, rv, wd, ad, gd, cb, cc, cx, gates = _split_w_in(w_in[l])
    assert Z_GATE_COL % d == 0 and gates.shape[1] == 3 * d
    w_z = jnp.concatenate([q, k, v, qi, rr, rk, rv, cb, cc, cx, gates,
                           ki, ki, wi, jnp.zeros((d, LANES - HEADS), F32), wd, ad, gd], axis=1).astype(BF16)
    mu = rwkv_mu[l]
    row = lambda a: a.reshape(1, -1)
    head_id = jnp.arange(QUAD) // HEAD_DIM
    bones_q = (head_id[:, None] == head_id[None, :]).astype(BF16)
    tri = (jnp.arange(CHUNK)[:, None] >= jnp.arange(CHUNK)[None, :]).astype(BF16)
    zl = lambda n: jnp.zeros((n, MIX_W), F32)
    return dict(
        w_z=w_z,
        mu_r=row(mu[:MIX_W]), mu_k=row(mu[MIX_W:2 * MIX_W]), mu_v=row(mu[2 * MIX_W:3 * MIX_W]),
        mu_s2=row(mu[3 * MIX_W:3 * MIX_W + LORA_DECAY + LORA_ICLR]), mu_s3=row(mu[3 * MIX_W + LORA_DECAY + LORA_ICLR:]),
        w0=row(rwkv_w0[l]), w_up=jnp.concatenate([rwkv_w_up[l], zl(LORA_ICLR)], axis=0).astype(BF16),
        a0=row(rwkv_a0[l]), a_up=jnp.concatenate([zl(LORA_DECAY), rwkv_a_up[l]], axis=0).astype(BF16),
        g_up=rwkv_g_up[l].astype(BF16), k_k=row(rwkv_k_k[l]), k_a=row(rwkv_k_a[l]), r_k=row(rwkv_r_k[l]),
        bones_q=bones_q, tri=tri,
        ln_w=row(rwkv_ln_w[l]), ln_b=row(rwkv_ln_b[l]), conv_w=sc_conv_w[l].T,
        w_branch=w_branch[l].astype(BF16), w_o=w_o[l].astype(BF16),
    )


def kernel(x, c, positions, rel_bias, final_norm, ada_w, ada_b, norm_mix, w_in, rwkv_mu, rwkv_w0, rwkv_w_up,
           rwkv_a0, rwkv_a_up, rwkv_g_up, rwkv_k_k, rwkv_k_a, rwkv_r_k, rwkv_ln_w, rwkv_ln_b, sc_conv_w,
           w_branch, w_o, norm_ffn, ffn_w_up, ffn_conv_w, ffn_w_down):
    depth, d = ada_w.shape[0], x.shape[-1]
    mod5 = _ada_mod(c, ada_w, ada_b)
    bias = _bias_tiles(rel_bias)
    for l in range(depth):
        prm = _layer_params(l, w_in, rwkv_mu, rwkv_w0, rwkv_w_up, rwkv_a0, rwkv_a_up, rwkv_g_up, rwkv_k_k,
                            rwkv_k_a, rwkv_r_k, rwkv_ln_w, rwkv_ln_b, sc_conv_w, w_branch, w_o)
        z = _inproj(x, mod5, l, norm_mix[l], prm["w_z"])
        o_attn = _dsa(z, bias, d)
        y, g, bonus = _rwkv(z, prm, d)
        x = _merge(x, o_attn, y, g, bonus, z, mod5, l, prm)
        x = _ffn(x, mod5, l, norm_ffn[l], ffn_w_up[l].astype(BF16), ffn_conv_w[l].T, ffn_w_down[l].astype(BF16),
                 final_norm, final_norm=(l == depth - 1))
    return x
```

```python
import functools
import math

import jax
import jax.numpy as jnp
from jax import lax
from jax.experimental import pallas as pl
from jax.experimental.pallas import tpu as pltpu

F32 = jnp.float32
BF16 = jnp.bfloat16
HIGHEST = lax.Precision.HIGHEST

MIX_W = 512
HEADS = 8
HEAD_DIM = 64
TOPK_MAX = 256
N_BUCKETS = 32
MAX_DISTANCE = 128
LORA_DECAY = 64
LORA_ICLR = 64
LORA_GATE = 128
CONV_W = 3
NORM_EPS = 1e-6
GN_EPS = 64e-5
NEG_INF = -1e30

LANES = 128
SUBLANES = 8
BF16_ROWS = 16
VMEM_LIMIT = 56 * 1024 * 1024

TQ = 256
CHUNK = 64
RW_TM = 256
QUAD = 4 * HEAD_DIM
assert CHUNK == HEAD_DIM

Z_Q, Z_K, Z_V, Z_QI, Z_RR, Z_RK, Z_RV, Z_CB, Z_CC, Z_CX = range(10)
Z_GATE_COL = 10 * MIX_W


def _z_small_block(d_model, i):
    return (Z_GATE_COL + 3 * d_model) // LANES + i


def _cparams(sem):
    return pltpu.CompilerParams(dimension_semantics=sem, vmem_limit_bytes=VMEM_LIMIT)


def _nt(a, b, precision=None):
    return lax.dot_general(a, b, (((1,), (1,)), ((), ())), precision=precision,
                           preferred_element_type=F32)


def _tn(a, b, precision=None):
    return lax.dot_general(a, b, (((0,), (0,)), ((), ())), precision=precision,
                           preferred_element_type=F32)


def _mm(a, b, precision=None):
    return jnp.dot(a, b, precision=precision, preferred_element_type=F32)


def _bmm(a, b):
    return _mm(a.astype(BF16), b.astype(BF16))


def _mod_kernel(c_ref, w_ref, b_ref, o_ref):
    o_ref[...] = _mm(c_ref[...], w_ref[...], HIGHEST) + b_ref[...]


def _ada_mod(c, ada_w, ada_b):
    depth, d, d6 = ada_w.shape
    b = c.shape[0]
    out = pl.pallas_call(
        _mod_kernel,
        grid=(depth, d6 // d),
        in_specs=[pl.BlockSpec((b, d), lambda l, j: (0, 0)),
                  pl.BlockSpec((None, d, d), lambda l, j: (l, 0, j)),
                  pl.BlockSpec((None, 1, d), lambda l, j: (l, 0, j))],
        out_specs=pl.BlockSpec((None, b, d), lambda l, j: (l, 0, j)),
        out_shape=jax.ShapeDtypeStruct((depth, b, d6), F32),
        compiler_params=_cparams(("parallel", "parallel")),
        name="ada_mod",
    )(c, ada_w, ada_b.reshape(depth, 1, d6))
    return out.reshape(depth, b, d6 // d, 1, d)


def _mod_spec(mod5, layer, which, ngrid):
    d = mod5.shape[-1]
    if ngrid == 2:
        return pl.BlockSpec((None, None, None, 1, d), lambda b, i: (layer, b, which, 0, 0))
    return pl.BlockSpec((None, None, None, 1, d), lambda b, i, j: (layer, b, which, 0, 0))


def _norm_mod(x, gain, scale, shift):
    y = x * lax.rsqrt(jnp.mean(x * x, axis=-1, keepdims=True) + NORM_EPS) * gain
    return y * (1.0 + scale) + shift


def _resident(a):
    return pl.BlockSpec(a.shape, lambda bi, i: (0,) * a.ndim, pipeline_mode=pl.Buffered(1))


def _inproj_kernel(x_ref, sh_ref, sc_ref, g_ref, w_ref, o_ref, *, n_col_chunks):
    h = _norm_mod(x_ref[...], g_ref[...], sc_ref[...], sh_ref[...]).astype(BF16)
    tn = w_ref.shape[1] // n_col_chunks
    for c in range(n_col_chunks):
        o_ref[:, c * tn:(c + 1) * tn] = _mm(h, w_ref[:, c * tn:(c + 1) * tn]).astype(o_ref.dtype)


def _inproj(x, mod5, layer, gain, w, tm=512, n_col_chunks=4):
    b, s, d = x.shape
    zc = w.shape[1]
    tm = min(tm, s)
    assert zc % (n_col_chunks * LANES) == 0
    gain = gain.reshape(1, d)
    return pl.pallas_call(
        functools.partial(_inproj_kernel, n_col_chunks=n_col_chunks),
        grid=(b, s // tm),
        in_specs=[pl.BlockSpec((None, tm, d), lambda bi, i: (bi, i, 0)),
                  _mod_spec(mod5, layer, 0, 2),
                  _mod_spec(mod5, layer, 1, 2),
                  _resident(gain), _resident(w)],
        out_specs=pl.BlockSpec((None, tm, zc), lambda bi, i: (bi, i, 0)),
        out_shape=jax.ShapeDtypeStruct((b, s, zc), BF16),
        compiler_params=_cparams(("parallel", "parallel")),
        name="inproj",
    )(x, mod5, mod5, gain, w)


def _bias_kernel(rb_ref, o_ref):
    which = pl.program_id(0)
    h = pl.program_id(1)
    ri = lax.broadcasted_iota(jnp.int32, (TQ, TQ), 0)
    ci = lax.broadcasted_iota(jnp.int32, (TQ, TQ), 1)
    dist = ci - ri + (1 - which) * TQ
    n = jnp.maximum(dist, 0)
    max_exact = N_BUCKETS // 2
    nf = jnp.maximum(n, 1).astype(F32)
    large = max_exact + (jnp.log(nf / max_exact) / math.log(MAX_DISTANCE / max_exact)
                         * (N_BUCKETS - max_exact)).astype(jnp.int32)
    large = jnp.minimum(large, N_BUCKETS - 1)
    bucket = jnp.where(n < max_exact, n, large)
    far = rb_ref[N_BUCKETS - 1, h]
    acc = jnp.zeros((TQ, TQ), F32)
    for bkt in range(N_BUCKETS - 1):
        acc = jnp.where(bucket == bkt, rb_ref[bkt, h] - far, acc)
    o_ref[...] = acc * LOG2E


def _bias_tiles(rel_bias):
    assert TQ >= MAX_DISTANCE
    return pl.pallas_call(
        _bias_kernel,
        grid=(2, HEADS),
        in_specs=[pl.BlockSpec(memory_space=pltpu.SMEM)],
        out_specs=pl.BlockSpec((None, None, TQ, TQ), lambda w, h: (w, h, 0, 0)),
        out_shape=jax.ShapeDtypeStruct((2, HEADS, TQ, TQ), F32),
        compiler_params=_cparams(("parallel", "parallel")),
        name="bias_tiles",
    )(rel_bias)


BISECT_MAX_IT = 300
COARSE_STEPS = 10
FINE_STEPS_UNCHECKED = 4
FINE_STEPS_PER_CHECK = 2
LOG2E = math.log2(math.e)
ONES_ROWS = BF16_ROWS


def _row_groups(t):
    return [t[r * SUBLANES:(r + 1) * SUBLANES, :] for r in range(t.shape[0] // SUBLANES)]


def _dsa_kernel(q_ref, k_ref, v_ref, qi_ref, ki_ref, wi_ref, bias_ref, o_ref,
                sc_ref, scb_ref, qm_ref, qim_ref, wt_ref, vt_ref, m_ref, l_ref, acc_ref, s_ref, tmax_ref,
                *, n_keep, seq):
    j = pl.program_id(1)
    nt = j + 1
    kf = float(n_keep)
    lane = lax.broadcasted_iota(jnp.int32, (TQ, LANES), 1)
    att_scale = HEAD_DIM ** -0.5 * LOG2E
    w_scale = (HEADS ** -0.5) * (HEAD_DIM ** -0.5)

    for h in range(HEADS):
        p, odd = divmod(h, 2)
        hm = (lane >= HEAD_DIM) if odd else (lane < HEAD_DIM)
        qs = q_ref[:, p * LANES:(p + 1) * LANES]
        qm_ref[h] = jnp.where(hm, qs, jnp.zeros_like(qs)) * att_scale
        qis = qi_ref[:, p * LANES:(p + 1) * LANES]
        qim_ref[h] = jnp.where(hm, qis, jnp.zeros_like(qis))
    wt_ref[...] = wi_ref[...].astype(F32).T * w_scale
    v_t = v_ref[...].T
    for h in range(HEADS):
        vt_ref[j, h, 0:HEAD_DIM, :] = v_t[h * HEAD_DIM:(h + 1) * HEAD_DIM, :]
        vt_ref[j, h, HEAD_DIM:HEAD_DIM + ONES_ROWS, :] = jnp.ones((ONES_ROWS, TQ), BF16)

    key = lax.broadcasted_iota(jnp.int32, (TQ, TQ), 0)
    qry = lax.broadcasted_iota(jnp.int32, (TQ, TQ), 1) + j * TQ

    def floor_bf16(x):
        xi = lax.bitcast_convert_type(x, jnp.int32)
        xi = jnp.where(xi < 0, xi + jnp.int32(0xFFFF), xi) & jnp.int32(-0x10000)
        return lax.bitcast_convert_type(xi, F32).astype(BF16)

    def score_tile(kt):
        ki_t = ki_ref[pl.ds(pl.multiple_of(kt * TQ, TQ), TQ), :]
        acc = jnp.zeros((TQ, TQ), F32)
        for h in range(HEADS):
            acc = acc + jnp.maximum(_nt(ki_t, qim_ref[h]), 0.0) * wt_ref[h:h + 1, :]
        sc_ref[kt] = jnp.where(key + kt * TQ <= qry, acc, NEG_INF)

    def stats_tile(kt, c):
        mx, mn, mp, cp, cn = c
        score = sc_ref[kt]
        scb_ref[kt] = floor_bf16(score)
        for g in _row_groups(score):
            pos = g > 0.0
            mx = jnp.maximum(mx, g)
            mn = jnp.minimum(mn, jnp.where(g > 0.5 * NEG_INF, g, -NEG_INF))
            mp = jnp.minimum(mp, jnp.where(pos, g, -NEG_INF))
            cp = cp + jnp.where(pos, 1.0, 0.0)
            cn = cn + jnp.where(g >= 0.0, 1.0, 0.0)
        return mx, mn, mp, cp, cn

    def idx_tile(kt, c):
        c = stats_tile(kt - 1, c)
        score_tile(kt)
        return c

    part = lambda v: jnp.full((SUBLANES, TQ), v, F32)
    score_tile(0)
    stats = lax.fori_loop(1, nt, idx_tile, (part(NEG_INF), part(-NEG_INF), part(-NEG_INF), part(0.0), part(0.0)))
    mx, mn, mp, cp, cn = stats_tile(nt - 1, stats)

    rmax = jnp.max(mx, axis=0, keepdims=True)
    rmin = jnp.min(mn, axis=0, keepdims=True)
    minpos = jnp.min(mp, axis=0, keepdims=True)
    cpos = jnp.sum(cp, axis=0, keepdims=True)
    cnn = jnp.sum(cn, axis=0, keepdims=True)
    nvalid = (lax.broadcasted_iota(jnp.int32, (1, TQ), 1) + j * TQ + 1).astype(F32)
    small = nvalid <= kf

    def count_ge(thr):
        def body(kt, acc):
            for g in _row_groups(sc_ref[kt]):
                acc = acc + jnp.where(g >= thr, 1.0, 0.0)
            return acc
        acc = lax.fori_loop(0, nt, body, jnp.zeros((SUBLANES, TQ), F32))
        return jnp.sum(acc, axis=0, keepdims=True)

    ztie = jnp.logical_and(cpos < kf, cnn >= kf)
    pos_side = cpos >= kf
    hi_top = rmax + jnp.maximum(jnp.abs(rmax) * 1e-6, 1e-30)
    lo0 = jnp.where(small, 0.5 * NEG_INF, jnp.where(ztie, 0.0, jnp.where(pos_side, minpos, rmin)))
    hi0 = jnp.where(small, -NEG_INF, jnp.where(ztie, minpos, jnp.where(pos_side, hi_top, 0.0)))
    clo0 = jnp.where(jnp.logical_or(small, ztie), jnp.where(small, nvalid, cnn), jnp.where(pos_side, cpos, nvalid))
    chi0 = jnp.where(small, 0.0, jnp.where(ztie, cpos, jnp.where(pos_side, 0.0, cnn)))
    done0 = jnp.where(jnp.logical_or(jnp.logical_or(small, ztie), clo0 == kf), 1.0, 0.0)

    def bisect_step(state, mid, cnt, usable):
        lo, hi, clo, chi, done = state
        act = jnp.logical_and(done < 0.5, usable)
        ge = cnt >= kf
        up_lo = jnp.logical_and(act, ge)
        up_hi = jnp.logical_and(act, jnp.logical_not(ge))
        lo = jnp.where(up_lo, mid, lo)
        clo = jnp.where(up_lo, cnt, clo)
        hi = jnp.where(up_hi, mid, hi)
        chi = jnp.where(up_hi, cnt, chi)
        done = jnp.where(jnp.logical_and(act, cnt == kf), 1.0, done)
        return lo, hi, clo, chi, done

    def count_ge_bf16(thr):
        one, zero = jnp.ones((), BF16), jnp.zeros((), BF16)

        def body(kt, acc):
            t = scb_ref[kt]
            part = jnp.zeros((BF16_ROWS, TQ), BF16)
            for r in range(TQ // BF16_ROWS):
                part = part + jnp.where(t[r * BF16_ROWS:(r + 1) * BF16_ROWS, :] >= thr, one, zero)
            return acc + part.astype(F32)
        acc = lax.fori_loop(0, nt, body, jnp.zeros((BF16_ROWS, TQ), F32))
        return jnp.sum(acc, axis=0, keepdims=True)

    def coarse_step(_, state):
        lo, hi = state[0], state[1]
        mid_b = (lo + 0.5 * (hi - lo)).astype(BF16)
        mid = mid_b.astype(F32)
        usable = jnp.logical_and(mid > lo, mid < hi)
        return bisect_step(state, mid, count_ge_bf16(mid_b), usable)

    state = lax.fori_loop(0, COARSE_STEPS, coarse_step, (lo0, hi0, clo0, chi0, done0))

    def fine_step(_, state):
        lo, hi = state[0], state[1]
        mid = lo + 0.5 * (hi - lo)
        stalled = jnp.logical_or(mid <= lo, mid >= hi)
        state = bisect_step(state, mid, count_ge(mid), jnp.logical_not(stalled))
        return state[:4] + (jnp.where(stalled, 1.0, state[4]),)

    state = lax.fori_loop(0, FINE_STEPS_UNCHECKED, fine_step, state)

    def bis_cond(c):
        return jnp.logical_and(c[0] < BISECT_MAX_IT, c[2] > 0.0)

    def bis_body(c):
        it, state, _ = c
        state = lax.fori_loop(0, FINE_STEPS_PER_CHECK, fine_step, state)
        return it + FINE_STEPS_PER_CHECK, state, jnp.sum(1.0 - state[4])

    _, (lo, hi, clo, chi, _), _ = lax.while_loop(
        bis_cond, bis_body, (jnp.int32(0), state, jnp.sum(1.0 - state[4])))

    tie = jnp.logical_and(clo > kf, jnp.logical_not(small))
    band_quota = jnp.where(tie, kf - chi, float(seq))
    prefix_ones = jnp.where(key >= lax.broadcasted_iota(jnp.int32, (TQ, TQ), 1), 1.0, 0.0).astype(BF16)

    def mask_tiles(kts, before):
        tiles = [sc_ref[kt] for kt in kts]
        bands = [jnp.where(t >= lo, jnp.where(t < hi, 1.0, 0.0), 0.0).astype(BF16) for t in tiles]
        ranks = [_mm(prefix_ones, band) for band in bands]
        masks = []
        for t, rank in zip(tiles, ranks):
            rank = rank + before
            keep_band = jnp.where(rank <= band_quota, 0.0, NEG_INF)
            masks.append(jnp.where(t >= lo, jnp.where(t >= hi, 0.0, keep_band), NEG_INF))
            before = rank[TQ - 1:TQ, :]
        for kt, mask in zip(kts, masks):
            sc_ref[kt] = mask
        return before

    n_pairs = lax.shift_right_logical(nt, 1)
    before = lax.fori_loop(0, n_pairs, lambda i, c: mask_tiles([2 * i, 2 * i + 1], c), jnp.zeros((1, TQ), F32))
    lax.fori_loop(2 * n_pairs, nt, lambda kt, c: mask_tiles([kt], c), before)

    m_ref[...] = jnp.full(m_ref.shape, NEG_INF, F32)
    l_ref[...] = jnp.zeros(l_ref.shape, F32)
    acc_ref[...] = jnp.zeros(acc_ref.shape, F32)

    def logits_phase(kts, near, buf):
        tile_max = [None] * HEADS
        for i, kt in enumerate(kts):
            rows = pl.ds(pl.multiple_of(kt * TQ, TQ), TQ)
            mask_add = sc_ref[kt]
            for h in range(HEADS):
                p = h // 2
                s = _nt(k_ref[rows, p * LANES:(p + 1) * LANES], qm_ref[h]) + mask_add
                if near:
                    s = s + bias_ref[kt - j + 1, h]
                s_ref[buf, i, h] = s
                mx = jnp.max(s, axis=0, keepdims=True)
                tile_max[h] = mx if i == 0 else jnp.maximum(tile_max[h], mx)
        for h in range(HEADS):
            tmax_ref[buf, h] = tile_max[h]

    def softmax_phase(kts, buf):
        for h in range(HEADS):
            m_old = m_ref[h]
            m_new = jnp.maximum(m_old, tmax_ref[buf, h])
            alpha = jnp.exp2(m_old - m_new)
            pv = None
            for i, kt in enumerate(kts):
                part = _mm(vt_ref[kt, h], jnp.exp2(s_ref[buf, i, h] - m_new).astype(BF16))
                pv = part if i == 0 else pv + part
            m_ref[h] = m_new
            l_ref[h] = alpha * l_ref[h] + pv[HEAD_DIM:HEAD_DIM + 1, :]
            acc_ref[h] = alpha * acc_ref[h] + pv[0:HEAD_DIM, :]

    n_far = jnp.maximum(j - 1, 0)
    n_far_pairs = lax.shift_right_logical(n_far, 1)

    def attn_tiles(kts, near):
        logits_phase(kts, near, 0)
        softmax_phase(kts, 0)

    def far_pair(i, carry):
        attn_tiles([2 * i, 2 * i + 1], near=False)
        return carry

    def far_single(kt, carry):
        attn_tiles([kt], near=False)
        return carry

    lax.fori_loop(0, n_far_pairs, far_pair, 0)
    lax.fori_loop(2 * n_far_pairs, n_far, far_single, 0)

    @pl.when(j >= 1)
    def _():
        attn_tiles([j - 1, j], near=True)

    @pl.when(j == 0)
    def _():
        attn_tiles([j], near=True)

    for p in range(HEADS // 2):
        pair = jnp.concatenate([acc_ref[2 * p] / l_ref[2 * p], acc_ref[2 * p + 1] / l_ref[2 * p + 1]], axis=0)
        o_ref[:, p * LANES:(p + 1) * LANES] = pair.T.astype(o_ref.dtype)


def _dsa(z, bias, d_model):
    b, s, _ = z.shape
    n_keep = min(TOPK_MAX, s // 4)
    kern = functools.partial(_dsa_kernel, n_keep=n_keep, seq=s)
    ki_blk, wi_blk = _z_small_block(d_model, 0), _z_small_block(d_model, 1)
    return pl.pallas_call(
        kern,
        grid=(b, s // TQ),
        in_specs=[pl.BlockSpec((None, TQ, MIX_W), lambda bi, j: (bi, j, Z_Q)),
                  pl.BlockSpec((None, s, MIX_W), lambda bi, j: (bi, 0, Z_K)),
                  pl.BlockSpec((None, TQ, MIX_W), lambda bi, j: (bi, j, Z_V)),
                  pl.BlockSpec((None, TQ, MIX_W), lambda bi, j: (bi, j, Z_QI)),
                  pl.BlockSpec((None, s, LANES), lambda bi, j: (bi, 0, ki_blk)),
                  pl.BlockSpec((None, TQ, LANES), lambda bi, j: (bi, j, wi_blk)),
                  pl.BlockSpec((2, HEADS, TQ, TQ), lambda bi, j: (0, 0, 0, 0))],
        out_specs=pl.BlockSpec((None, TQ, MIX_W), lambda bi, j: (bi, j, 0)),
        out_shape=jax.ShapeDtypeStruct((b, s, MIX_W), BF16),
        scratch_shapes=[pltpu.VMEM((s // TQ, TQ, TQ), F32),
                        pltpu.VMEM((s // TQ, TQ, TQ), BF16),
                        pltpu.VMEM((HEADS, TQ, LANES), BF16),
                        pltpu.VMEM((HEADS, TQ, LANES), BF16),
                        pltpu.VMEM((LANES, TQ), F32),
                        pltpu.VMEM((s // TQ, HEADS, HEAD_DIM + ONES_ROWS, TQ), BF16),
                        pltpu.VMEM((HEADS, 1, TQ), F32),
                        pltpu.VMEM((HEADS, 1, TQ), F32),
                        pltpu.VMEM((HEADS, HEAD_DIM, TQ), F32),
                        pltpu.VMEM((1, 2, HEADS, TQ, TQ), F32),
                        pltpu.VMEM((1, HEADS, 1, TQ), F32)],
        compiler_params=_cparams(("parallel", "arbitrary")),
        name="dsa",
    )(z, z, z, z, z, z, bias)


def _halo_spec(width, blk, tm, rows):
    step = tm // rows
    return pl.BlockSpec((None, rows, width), lambda bi, i: (bi, jnp.maximum(i * step - 1, 0), blk))


def _shift_lerp(cur_ref, halo_ref, mu, first):
    cur = cur_ref[...].astype(F32)
    nh = halo_ref.shape[0]
    prev_last = jnp.where(first, 0.0, halo_ref[nh - 1:nh, :].astype(F32))
    rolled = pltpu.roll(cur, 1, axis=0)
    rowid = lax.broadcasted_iota(jnp.int32, cur.shape, 0)
    sh = jnp.where(rowid == 0, prev_last, rolled)
    return cur + (sh - cur) * mu


def _head_sum(x, bq_ref):
    xb = x.astype(BF16)
    return jnp.concatenate([_mm(xb[:, q * QUAD:(q + 1) * QUAD], bq_ref[...]) for q in range(MIX_W // QUAD)], axis=1)


def _rwkv_kernel(r_ref, k_ref, v_ref, s2_ref, s3_ref, rh_ref, kh_ref, vh_ref, s2h_ref, s3h_ref,
                 mu_r, mu_k, mu_v, mu_s2, mu_s3, w0_ref, wup_ref, a0_ref, aup_ref, gup_ref,
                 kk_ref, ka_ref, rk_ref, bq_ref, tri_ref,
                 y_ref, g_ref, bonus_ref,
                 st_ref, rs, ls, ks, vs, kks, kbs):
    first = pl.program_id(1) == 0

    @pl.when(first)
    def _():
        st_ref[...] = jnp.zeros_like(st_ref)

    r = _shift_lerp(r_ref, rh_ref, mu_r[...], first)
    k = _shift_lerp(k_ref, kh_ref, mu_k[...], first)
    v = _shift_lerp(v_ref, vh_ref, mu_v[...], first)
    s2 = _shift_lerp(s2_ref, s2h_ref, mu_s2[...], first)
    s3 = _shift_lerp(s3_ref, s3h_ref, mu_s3[...], first)
    xw = w0_ref[...] + _bmm(jnp.tanh(s2), wup_ref[...])
    softplus = jnp.maximum(-xw, 0.0) + jnp.log(1.0 + jnp.exp(-jnp.abs(xw)))
    ld = -jnp.exp(-softplus - 0.5)
    af = jax.nn.sigmoid(a0_ref[...] + _bmm(s2, aup_ref[...]))
    g_ref[...] = _bmm(jax.nn.sigmoid(s3), gup_ref[...]).astype(g_ref.dtype)
    kkr = k * kk_ref[...]
    kkn = kkr / jnp.maximum(jnp.sqrt(_head_sum(kkr * kkr, bq_ref)), 1e-12)
    kmod = k * (1.0 + (af - 1.0) * ka_ref[...])
    bonus_ref[...] = (_head_sum(r * kmod * rk_ref[...], bq_ref) * v).astype(bonus_ref.dtype)
    rs[...] = r
    ls[...] = ld
    ks[...] = kmod
    vs[...] = v
    kks[...] = kkn
    kbs[...] = kkn * af

    ri = lax.broadcasted_iota(jnp.int32, (QUAD, QUAD), 0)
    ci = lax.broadcasted_iota(jnp.int32, (QUAD, QUAD), 1)
    same_head = (ri // HEAD_DIM) == (ci // HEAD_DIM)
    strict = jnp.logical_and(same_head, (ri % CHUNK) > (ci % CHUNK))
    incl = jnp.logical_and(same_head, (ri % CHUNK) >= (ci % CHUNK))
    eye = jnp.where(ri == ci, 1.0, 0.0)

    def stack(x):
        return jnp.where(same_head, jnp.concatenate([x] * 4, axis=0), 0.0).astype(BF16)

    def tile4(x):
        return jnp.concatenate([x] * 4, axis=0).astype(BF16)

    def unstack(x):
        return (x[0:CHUNK] + x[CHUNK:2 * CHUNK]) + (x[2 * CHUNK:3 * CHUNK] + x[3 * CHUNK:4 * CHUNK])

    def chunk_operands(c):
        rows = pl.ds(pl.multiple_of(c * CHUNK, CHUNK), CHUNK)
        ldc = ls[rows, :]
        p1 = ldc.astype(BF16)
        e1 = ldc - p1.astype(F32)
        p2 = e1.astype(BF16)
        p3 = (e1 - p2.astype(F32)).astype(BF16)
        tri = tri_ref[...]
        cl = (_mm(tri, p1) + _mm(tri, p2)) + _mm(tri, p3)
        cl_end = cl[CHUNK - 1:CHUNK, :]
        e_in = jnp.exp(cl)
        e_out = jnp.exp(-cl)
        e_end = jnp.exp(cl_end - cl)
        rt_all = rs[rows, :] * e_in
        at_all = -kks[rows, :] * jnp.exp(cl - ldc)
        bt_all = kbs[rows, :] * e_out
        kt_all = ks[rows, :] * e_out
        bg_all = (kbs[rows, :] * e_end).astype(BF16)
        kg_all = (ks[rows, :] * e_end).astype(BF16)
        v_all = vs[rows, :]
        gam_all = jnp.exp(cl_end)
        units = []
        for q in range(MIX_W // QUAD):
            sl = slice(q * QUAD, (q + 1) * QUAD)
            units.append(dict(rows=rows, sl=sl, q=q, rt=rt_all[:, sl], vv=v_all[:, sl].astype(BF16),
                              a4=stack(at_all[:, sl]), r4=stack(rt_all[:, sl]), v4=stack(v_all[:, sl]),
                              bt4=tile4(bt_all[:, sl]), kt4=tile4(kt_all[:, sl]),
                              bg=bg_all[:, sl], kg=kg_all[:, sl], gam=gam_all[:, sl]))
        return units

    n_doublings = int(math.log2(CHUNK)) - 1

    def chunk_pair(cp, carry):
        us = chunk_operands(2 * cp) + chunk_operands(2 * cp + 1)
        for u in us:
            u["m_ab"] = jnp.where(strict, _nt(u["a4"], u["bt4"]), 0.0)
        for u in us:
            u["m_ak"] = jnp.where(strict, _nt(u["a4"], u["kt4"]), 0.0)
        for u in us:
            u["m_rb"] = jnp.where(incl, _nt(u["r4"], u["bt4"]), 0.0).astype(BF16)
        for u in us:
            u["m_rk"] = jnp.where(incl, _nt(u["r4"], u["kt4"]), 0.0).astype(BF16)
        for u in us:
            u["pw"] = u["m_ab"]
            u["inv"] = eye + u["m_ab"]
        for _ in range(n_doublings):
            for u in us:
                u["pw"] = _bmm(u["pw"], u["pw"])
            for u in us:
                u["inv"] = u["inv"] + _bmm(u["inv"], u["pw"])
        for u in us:
            u["inv"] = u["inv"].astype(BF16)
            u["mv4"] = _bmm(u["m_ak"], u["v4"])
        for u in us:
            u["ah4"] = _mm(u["inv"], u["a4"])
        for u in us:
            u["uh4"] = _bmm(u["inv"], u["mv4"])
        for u in us:
            u["ry"] = (u["rt"] + unstack(_bmm(u["m_rb"], u["ah4"]))).astype(BF16)
        for u in us:
            u["y0"] = unstack(_bmm(u["m_rb"], u["uh4"]) + _mm(u["m_rk"], u["v4"]))
        for u in us:
            ah, uh = unstack(u["ah4"]).astype(BF16), unstack(u["uh4"]).astype(BF16)
            u["g_low"] = jnp.where(same_head, _tn(u["bg"], ah), 0.0).astype(BF16)
            u["h_t"] = jnp.where(same_head, _tn(uh, u["bg"]) + _tn(u["vv"], u["kg"]), 0.0)
        for u in us:
            st = st_ref[u["q"]]
            stb = st.astype(BF16)
            y_ref[u["rows"], u["sl"]] = _nt(u["ry"], stb) + u["y0"]
            st_ref[u["q"]] = st * u["gam"] + _nt(stb, u["g_low"]) + u["h_t"]
        return carry

    lax.fori_loop(0, r_ref.shape[0] // (2 * CHUNK), chunk_pair, 0)


def _rwkv(z, prm, d_model):
    b, s, _ = z.shape
    tm = min(RW_TM, s)
    s2_blk, s3_blk = _z_small_block(d_model, 2), _z_small_block(d_model, 3)
    tok = lambda blk: pl.BlockSpec((None, tm, MIX_W), lambda bi, i: (bi, i, blk))
    tok128 = lambda blk: pl.BlockSpec((None, tm, LANES), lambda bi, i: (bi, i, blk))
    const = lambda a: pl.BlockSpec(a.shape, lambda bi, i: (0,) * a.ndim)
    consts = [prm[n] for n in ("mu_r", "mu_k", "mu_v", "mu_s2", "mu_s3", "w0", "w_up", "a0", "a_up", "g_up",
                               "k_k", "k_a", "r_k", "bones_q", "tri")]
    out_spec = pl.BlockSpec((None, tm, MIX_W), lambda bi, i: (bi, i, 0))
    return pl.pallas_call(
        _rwkv_kernel,
        grid=(b, s // tm),
        in_specs=[tok(Z_RR), tok(Z_RK), tok(Z_RV), tok128(s2_blk), tok128(s3_blk),
                  _halo_spec(MIX_W, Z_RR, tm, BF16_ROWS), _halo_spec(MIX_W, Z_RK, tm, BF16_ROWS),
                  _halo_spec(MIX_W, Z_RV, tm, BF16_ROWS), _halo_spec(LANES, s2_blk, tm, BF16_ROWS),
                  _halo_spec(LANES, s3_blk, tm, BF16_ROWS)] + [const(a) for a in consts],
        out_specs=[out_spec, out_spec, out_spec],
        out_shape=[jax.ShapeDtypeStruct((b, s, MIX_W), F32), jax.ShapeDtypeStruct((b, s, MIX_W), BF16),
                   jax.ShapeDtypeStruct((b, s, MIX_W), BF16)],
        scratch_shapes=[pltpu.VMEM((MIX_W // QUAD, QUAD, QUAD), F32)] + [pltpu.VMEM((tm, MIX_W), F32)] * 6,
        compiler_params=_cparams(("parallel", "arbitrary")),
        name="rwkv",
    )(z, z, z, z, z, z, z, z, z, z, *consts)


def _causal_conv(p, halo, cw_ref):
    nh = halo.shape[0]
    rowid = lax.broadcasted_iota(jnp.int32, (SUBLANES, p.shape[1]), 0)
    conv = p * cw_ref[CONV_W - 1:CONV_W, :]
    for back in range(1, CONV_W):
        rolled = pltpu.roll(p, back, axis=0)
        head = rolled[0:SUBLANES]
        for rr in range(back):
            head = jnp.where(rowid == rr, halo[nh - back + rr:nh - back + rr + 1, :], head)
        rolled = jnp.concatenate([head, rolled[SUBLANES:]], axis=0)
        conv = conv + rolled * cw_ref[CONV_W - 1 - back:CONV_W - back, :]
    return conv


def _merge_kernel(x_ref, oa_ref, y_ref, g_ref, bonus_ref, cb_ref, cc_ref, cx_ref, cch_ref, cxh_ref,
                  gate0_ref, gate1_ref, gate2_ref, g1_ref, lnw_ref, lnb_ref, bq_ref, cw_ref, wb_ref, wo_ref, o_ref,
                  merged_ref):
    first = pl.program_id(1) == 0
    y = y_ref[...]
    inv_n = 1.0 / HEAD_DIM
    p1 = y.astype(BF16)
    mean = (_head_sum(p1, bq_ref) + _head_sum(y - p1.astype(F32), bq_ref)) * inv_n
    yc = y - mean
    var = _head_sum(yc * yc, bq_ref) * inv_n
    o_rwkv = ((yc * lax.rsqrt(var + GN_EPS) * lnw_ref[...] + lnb_ref[...] + bonus_ref[...].astype(F32))
              * g_ref[...].astype(F32))
    p = cc_ref[...].astype(F32) * cx_ref[...].astype(F32)
    ph = jnp.where(first, 0.0, cch_ref[...].astype(F32) * cxh_ref[...].astype(F32))
    o_conv = cb_ref[...].astype(F32) * _causal_conv(p, ph, cw_ref)
    branches = ((oa_ref[...], gate0_ref), (o_rwkv.astype(BF16), gate1_ref), (o_conv.astype(BF16), gate2_ref))
    d = x_ref.shape[-1]
    for c in range(d // QUAD):
        cols = slice(c * QUAD, (c + 1) * QUAD)
        part = jnp.zeros((x_ref.shape[0], QUAD), F32)
        for bi, (o, gate_ref) in enumerate(branches):
            part = part + jax.nn.sigmoid(gate_ref[:, cols].astype(F32)) * _mm(o, wb_ref[bi, :, cols])
        merged_ref[:, cols] = part.astype(BF16)
    o_ref[...] = x_ref[...] + g1_ref[...] * _mm(merged_ref[...], wo_ref[...])


def _merge(x, o_attn, y, g, bonus, z, mod5, layer, prm, tm=256):
    b, s, d = x.shape
    tm = min(tm, s)
    tok = lambda w, blk: pl.BlockSpec((None, tm, w), lambda bi, i: (bi, i, blk))
    const = lambda a: pl.BlockSpec(a.shape, lambda bi, i: (0,) * a.ndim)
    consts = [prm[n] for n in ("ln_w", "ln_b", "bones_q", "conv_w", "w_branch", "w_o")]
    gate_blk = Z_GATE_COL // d
    return pl.pallas_call(
        _merge_kernel,
        grid=(b, s // tm),
        in_specs=[tok(d, 0), tok(MIX_W, 0), tok(MIX_W, 0), tok(MIX_W, 0), tok(MIX_W, 0),
                  tok(MIX_W, Z_CB), tok(MIX_W, Z_CC), tok(MIX_W, Z_CX),
                  _halo_spec(MIX_W, Z_CC, tm, BF16_ROWS), _halo_spec(MIX_W, Z_CX, tm, BF16_ROWS),
                  tok(d, gate_blk), tok(d, gate_blk + 1), tok(d, gate_blk + 2),
                  _mod_spec(mod5, layer, 2, 2)] + [const(a) for a in consts],
        out_specs=tok(d, 0),
        out_shape=jax.ShapeDtypeStruct((b, s, d), F32),
        scratch_shapes=[pltpu.VMEM((tm, d), BF16)],
        compiler_params=_cparams(("parallel", "parallel")),
        name="merge",
    )(x, o_attn, y, g, bonus, z, z, z, z, z, z, z, z, mod5, *consts)


def _ffn_kernel(x_ref, xh_ref, sh_ref, sc_ref, g2_ref, gain_ref, wup_ref, cw_ref, wd_ref, fin_ref, o_ref,
                *, final_norm, tf):
    first = pl.program_id(1) == 0
    x = x_ref[...]
    h = _norm_mod(x, gain_ref[...], sc_ref[...], sh_ref[...]).astype(BF16)
    hh = _norm_mod(xh_ref[...], gain_ref[...], sc_ref[...], sh_ref[...]).astype(BF16)
    d_ff = wd_ref.shape[0]
    acc = jnp.zeros(x.shape, F32)
    for jf in range(d_ff // tf):
        cols = slice(jf * tf, (jf + 1) * tf)
        gate_cols = slice(d_ff + jf * tf, d_ff + (jf + 1) * tf)
        a = _mm(h, wup_ref[:, cols])
        ah = jnp.where(first, 0.0, _mm(hh, wup_ref[:, cols]))
        conv = _causal_conv(a, ah, cw_ref[:, cols])
        u = conv * jax.nn.sigmoid(conv) * _mm(h, wup_ref[:, gate_cols])
        acc = acc + _mm(u.astype(BF16), wd_ref[cols, :])
    out = x + g2_ref[...] * acc
    if final_norm:
        out = out * lax.rsqrt(jnp.mean(out * out, axis=-1, keepdims=True) + NORM_EPS) * fin_ref[...]
    o_ref[...] = out


def _ffn(x, mod5, layer, gain, w_up, conv_w, w_down, final_gain, final_norm, tm=512, tf=1408):
    b, s, d = x.shape
    d_ff = w_down.shape[0]
    assert d_ff % tf == 0 and tf % LANES == 0
    tm = min(tm, s)
    step = tm // SUBLANES
    gain, final_gain = gain.reshape(1, d), final_gain.reshape(1, d)
    return pl.pallas_call(
        functools.partial(_ffn_kernel, final_norm=final_norm, tf=tf),
        grid=(b, s // tm),
        in_specs=[pl.BlockSpec((None, tm, d), lambda bi, i: (bi, i, 0)),
                  pl.BlockSpec((None, SUBLANES, d), lambda bi, i: (bi, jnp.maximum(i * step - 1, 0), 0)),
                  _mod_spec(mod5, layer, 3, 2), _mod_spec(mod5, layer, 4, 2), _mod_spec(mod5, layer, 5, 2),
                  _resident(gain), _resident(w_up), _resident(conv_w), _resident(w_down), _resident(final_gain)],
        out_specs=pl.BlockSpec((None, tm, d), lambda bi, i: (bi, i, 0)),
        out_shape=jax.ShapeDtypeStruct((b, s, d), F32),
        compiler_params=_cparams(("parallel", "parallel")),
        name="ffn",
    )(x, x, mod5, mod5, mod5, gain, w_up, conv_w, w_down, final_gain)


def _split_w_in(w):
    sizes = (MIX_W, MIX_W, MIX_W, HEADS * HEAD_DIM, HEAD_DIM, HEADS,
             MIX_W, MIX_W, MIX_W, LORA_DECAY, LORA_ICLR, LORA_GATE,
             MIX_W, MIX_W, MIX_W)
    out, o = [], 0
    for n in sizes:
        out.append(w[:, o:o + n])
        o += n
    out.append(w[:, o:])
    return out


def _layer_params(l, w_in, rwkv_mu, rwkv_w0, rwkv_w_up, rwkv_a0, rwkv_a_up, rwkv_g_up, rwkv_k_k, rwkv_k_a,
                  rwkv_r_k, rwkv_ln_w, rwkv_ln_b, sc_conv_w, w_branch, w_o):
    d = w_in.shape[1]
    q, k, v, qi, ki, wi, rr, rk, rv, wd, ad, gd, cb, cc, cx, gates = _split_w_in(w_in[l].astype(BF16))
    assert Z_GATE_COL % d == 0 and gates.shape[1] == 3 * d
    w_z = jnp.concatenate([q, k, v, qi, rr, rk, rv, cb, cc, cx, gates,
                           ki, ki, wi, jnp.zeros((d, LANES - HEADS), BF16), wd, ad, gd], axis=1)
    mu = rwkv_mu[l]
    row = lambda a: a.reshape(1, -1)
    head_id = jnp.arange(QUAD) // HEAD_DIM
    bones_q = (head_id[:, None] == head_id[None, :]).astype(BF16)
    tri = (jnp.arange(CHUNK)[:, None] >= jnp.arange(CHUNK)[None, :]).astype(BF16)
    zl = lambda n: jnp.zeros((n, MIX_W), F32)
    return dict(
        w_z=w_z,
        mu_r=row(mu[:MIX_W]), mu_k=row(mu[MIX_W:2 * MIX_W]), mu_v=row(mu[2 * MIX_W:3 * MIX_W]),
        mu_s2=row(mu[3 * MIX_W:3 * MIX_W + LORA_DECAY + LORA_ICLR]), mu_s3=row(mu[3 * MIX_W + LORA_DECAY + LORA_ICLR:]),
        w0=row(rwkv_w0[l]), w_up=jnp.concatenate([rwkv_w_up[l], zl(LORA_ICLR)], axis=0).astype(BF16),
        a0=row(rwkv_a0[l]), a_up=jnp.concatenate([zl(LORA_DECAY), rwkv_a_up[l]], axis=0).astype(BF16),
        g_up=rwkv_g_up[l].astype(BF16), k_k=row(rwkv_k_k[l]), k_a=row(rwkv_k_a[l]), r_k=row(rwkv_r_k[l]),
        bones_q=bones_q, tri=tri,
        ln_w=row(rwkv_ln_w[l]), ln_b=row(rwkv_ln_b[l]), conv_w=sc_conv_w[l].T,
        w_branch=w_branch[l].astype(BF16), w_o=w_o[l].astype(BF16),
    )


def kernel(x, c, positions, rel_bias, final_norm, ada_w, ada_b, norm_mix, w_in, rwkv_mu, rwkv_w0, rwkv_w_up,
           rwkv_a0, rwkv_a_up, rwkv_g_up, rwkv_k_k, rwkv_k_a, rwkv_r_k, rwkv_ln_w, rwkv_ln_b, sc_conv_w,
           w_branch, w_o, norm_ffn, ffn_w_up, ffn_conv_w, ffn_w_down):
    depth, d = ada_w.shape[0], x.shape[-1]
    mod5 = _ada_mod(c, ada_w, ada_b)
    bias = _bias_tiles(rel_bias)
    for l in range(depth):
        prm = _layer_params(l, w_in, rwkv_mu, rwkv_w0, rwkv_w_up, rwkv_a0, rwkv_a_up, rwkv_g_up, rwkv_k_k,
                            rwkv_k_a, rwkv_r_k, rwkv_ln_w, rwkv_ln_b, sc_conv_w, w_branch, w_o)
        z = _inproj(x, mod5, l, norm_mix[l], prm["w_z"])
        o_attn = _dsa(z, bias, d)
        y, g, bonus = _rwkv(z, prm, d)
        x = _merge(x, o_attn, y, g, bonus, z, mod5, l, prm)
        x = _ffn(x, mod5, l, norm_ffn[l], ffn_w_up[l].astype(BF16), ffn_conv_w[l].T, ffn_w_down[l].astype(BF16),
                 final_norm, final_norm=(l == depth - 1))
    return x
```

```python
import functools
import math

import jax
import jax.numpy as jnp
from jax import lax
from jax.experimental import pallas as pl
from jax.experimental.pallas import tpu as pltpu

F32 = jnp.float32
BF16 = jnp.bfloat16
HIGHEST = lax.Precision.HIGHEST

MIX_W = 512
HEADS = 8
HEAD_DIM = 64
TOPK_MAX = 256
N_BUCKETS = 32
MAX_DISTANCE = 128
LORA_DECAY = 64
LORA_ICLR = 64
LORA_GATE = 128
CONV_W = 3
NORM_EPS = 1e-6
GN_EPS = 64e-5
NEG_INF = -1e30

LANES = 128
SUBLANES = 8
BF16_ROWS = 16
VMEM_LIMIT = 56 * 1024 * 1024

TQ = 256
CHUNK = 64
RW_TM = 512
QUAD = 4 * HEAD_DIM
assert CHUNK == HEAD_DIM

Z_Q, Z_K, Z_V, Z_QI, Z_RR, Z_RK, Z_RV, Z_CB, Z_CC, Z_CX = range(10)
Z_GATE_COL = 10 * MIX_W


def _z_small_block(d_model, i):
    return (Z_GATE_COL + 3 * d_model) // LANES + i


def _cparams(sem):
    return pltpu.CompilerParams(dimension_semantics=sem, vmem_limit_bytes=VMEM_LIMIT)


def _nt(a, b, precision=None):
    return lax.dot_general(a, b, (((1,), (1,)), ((), ())), precision=precision,
                           preferred_element_type=F32)


def _tn(a, b, precision=None):
    return lax.dot_general(a, b, (((0,), (0,)), ((), ())), precision=precision,
                           preferred_element_type=F32)


def _mm(a, b, precision=None):
    return jnp.dot(a, b, precision=precision, preferred_element_type=F32)


def _bmm(a, b):
    return _mm(a.astype(BF16), b.astype(BF16))


def _mod_kernel(c_ref, w_ref, b_ref, o_ref):
    o_ref[...] = _mm(c_ref[...], w_ref[...], HIGHEST) + b_ref[...]


def _ada_mod(c, ada_w, ada_b):
    depth, d, d6 = ada_w.shape
    b = c.shape[0]
    out = pl.pallas_call(
        _mod_kernel,
        grid=(depth, d6 // d),
        in_specs=[pl.BlockSpec((b, d), lambda l, j: (0, 0)),
                  pl.BlockSpec((None, d, d), lambda l, j: (l, 0, j)),
                  pl.BlockSpec((None, 1, d), lambda l, j: (l, 0, j))],
        out_specs=pl.BlockSpec((None, b, d), lambda l, j: (l, 0, j)),
        out_shape=jax.ShapeDtypeStruct((depth, b, d6), F32),
        compiler_params=_cparams(("parallel", "parallel")),
        name="ada_mod",
    )(c, ada_w, ada_b.reshape(depth, 1, d6))
    return out.reshape(depth, b, d6 // d, 1, d)


def _mod_spec(mod5, layer, which, ngrid):
    d = mod5.shape[-1]
    if ngrid == 2:
        return pl.BlockSpec((None, None, None, 1, d), lambda b, i: (layer, b, which, 0, 0))
    return pl.BlockSpec((None, None, None, 1, d), lambda b, i, j: (layer, b, which, 0, 0))


def _norm_mod(x, gain, scale, shift):
    y = x * lax.rsqrt(jnp.mean(x * x, axis=-1, keepdims=True) + NORM_EPS) * gain
    return y * (1.0 + scale) + shift


def _resident(a):
    return pl.BlockSpec(a.shape, lambda bi, i: (0,) * a.ndim, pipeline_mode=pl.Buffered(1))


def _inproj_kernel(x_ref, sh_ref, sc_ref, g_ref, w_ref, o_ref, *, n_col_chunks):
    h = _norm_mod(x_ref[...], g_ref[...], sc_ref[...], sh_ref[...]).astype(BF16)
    tn = w_ref.shape[1] // n_col_chunks
    for c in range(n_col_chunks):
        o_ref[:, c * tn:(c + 1) * tn] = _mm(h, w_ref[:, c * tn:(c + 1) * tn]).astype(o_ref.dtype)


def _inproj(x, mod5, layer, gain, w, tm=512, n_col_chunks=4):
    b, s, d = x.shape
    zc = w.shape[1]
    tm = min(tm, s)
    assert zc % (n_col_chunks * LANES) == 0
    gain = gain.reshape(1, d)
    return pl.pallas_call(
        functools.partial(_inproj_kernel, n_col_chunks=n_col_chunks),
        grid=(b, s // tm),
        in_specs=[pl.BlockSpec((None, tm, d), lambda bi, i: (bi, i, 0)),
                  _mod_spec(mod5, layer, 0, 2),
                  _mod_spec(mod5, layer, 1, 2),
                  _resident(gain), _resident(w)],
        out_specs=pl.BlockSpec((None, tm, zc), lambda bi, i: (bi, i, 0)),
        out_shape=jax.ShapeDtypeStruct((b, s, zc), BF16),
        compiler_params=_cparams(("parallel", "parallel")),
        name="inproj",
    )(x, mod5, mod5, gain, w)


def _bias_kernel(rb_ref, o_ref):
    which = pl.program_id(0)
    h = pl.program_id(1)
    ri = lax.broadcasted_iota(jnp.int32, (TQ, TQ), 0)
    ci = lax.broadcasted_iota(jnp.int32, (TQ, TQ), 1)
    dist = ci - ri + (1 - which) * TQ
    n = jnp.maximum(dist, 0)
    max_exact = N_BUCKETS // 2
    nf = jnp.maximum(n, 1).astype(F32)
    large = max_exact + (jnp.log(nf / max_exact) / math.log(MAX_DISTANCE / max_exact)
                         * (N_BUCKETS - max_exact)).astype(jnp.int32)
    large = jnp.minimum(large, N_BUCKETS - 1)
    bucket = jnp.where(n < max_exact, n, large)
    far = rb_ref[N_BUCKETS - 1, h]
    acc = jnp.zeros((TQ, TQ), F32)
    for bkt in range(N_BUCKETS - 1):
        acc = jnp.where(bucket == bkt, rb_ref[bkt, h] - far, acc)
    o_ref[...] = acc * LOG2E


def _bias_tiles(rel_bias):
    assert TQ >= MAX_DISTANCE
    return pl.pallas_call(
        _bias_kernel,
        grid=(2, HEADS),
        in_specs=[pl.BlockSpec(memory_space=pltpu.SMEM)],
        out_specs=pl.BlockSpec((None, None, TQ, TQ), lambda w, h: (w, h, 0, 0)),
        out_shape=jax.ShapeDtypeStruct((2, HEADS, TQ, TQ), F32),
        compiler_params=_cparams(("parallel", "parallel")),
        name="bias_tiles",
    )(rel_bias)


BISECT_MAX_IT = 300
COARSE_STEPS = 10
FINE_STEPS_UNCHECKED = 4
FINE_STEPS_PER_CHECK = 2
LOG2E = math.log2(math.e)
ONES_ROWS = BF16_ROWS


def _row_groups(t):
    return [t[r * SUBLANES:(r + 1) * SUBLANES, :] for r in range(t.shape[0] // SUBLANES)]


def _dsa_kernel(q_ref, k_ref, v_ref, qi_ref, ki_ref, wi_ref, bias_ref, o_ref,
                sc_ref, scb_ref, qm_ref, qim_ref, wt_ref, vt_ref, m_ref, l_ref, acc_ref, s_ref, tmax_ref,
                *, n_keep, seq):
    j = pl.program_id(1)
    nt = j + 1
    kf = float(n_keep)
    lane = lax.broadcasted_iota(jnp.int32, (TQ, LANES), 1)
    att_scale = HEAD_DIM ** -0.5 * LOG2E
    w_scale = (HEADS ** -0.5) * (HEAD_DIM ** -0.5)

    for h in range(HEADS):
        p, odd = divmod(h, 2)
        hm = (lane >= HEAD_DIM) if odd else (lane < HEAD_DIM)
        qs = q_ref[:, p * LANES:(p + 1) * LANES]
        qm_ref[h] = jnp.where(hm, qs, jnp.zeros_like(qs)) * att_scale
        qis = qi_ref[:, p * LANES:(p + 1) * LANES]
        qim_ref[h] = jnp.where(hm, qis, jnp.zeros_like(qis))
    wt_ref[...] = wi_ref[...].astype(F32).T * w_scale
    v_t = v_ref[...].T
    for h in range(HEADS):
        vt_ref[j, h, 0:HEAD_DIM, :] = v_t[h * HEAD_DIM:(h + 1) * HEAD_DIM, :]
        vt_ref[j, h, HEAD_DIM:HEAD_DIM + ONES_ROWS, :] = jnp.ones((ONES_ROWS, TQ), BF16)

    key = lax.broadcasted_iota(jnp.int32, (TQ, TQ), 0)
    qry = lax.broadcasted_iota(jnp.int32, (TQ, TQ), 1) + j * TQ

    def floor_bf16(x):
        xi = lax.bitcast_convert_type(x, jnp.int32)
        xi = jnp.where(xi < 0, xi + jnp.int32(0xFFFF), xi) & jnp.int32(-0x10000)
        return lax.bitcast_convert_type(xi, F32).astype(BF16)

    def score_tile(kt):
        ki_t = ki_ref[pl.ds(pl.multiple_of(kt * TQ, TQ), TQ), :]
        acc = jnp.zeros((TQ, TQ), F32)
        for h in range(HEADS):
            acc = acc + jnp.maximum(_nt(ki_t, qim_ref[h]), 0.0) * wt_ref[h:h + 1, :]
        sc_ref[kt] = jnp.where(key + kt * TQ <= qry, acc, NEG_INF)

    def stats_tile(kt, c):
        mx, mn, mp, cp, cn = c
        score = sc_ref[kt]
        scb_ref[kt] = floor_bf16(score)
        for g in _row_groups(score):
            pos = g > 0.0
            mx = jnp.maximum(mx, g)
            mn = jnp.minimum(mn, jnp.where(g > 0.5 * NEG_INF, g, -NEG_INF))
            mp = jnp.minimum(mp, jnp.where(pos, g, -NEG_INF))
            cp = cp + jnp.where(pos, 1.0, 0.0)
            cn = cn + jnp.where(g >= 0.0, 1.0, 0.0)
        return mx, mn, mp, cp, cn

    def idx_tile(kt, c):
        c = stats_tile(kt - 1, c)
        score_tile(kt)
        return c

    part = lambda v: jnp.full((SUBLANES, TQ), v, F32)
    score_tile(0)
    stats = lax.fori_loop(1, nt, idx_tile, (part(NEG_INF), part(-NEG_INF), part(-NEG_INF), part(0.0), part(0.0)))
    mx, mn, mp, cp, cn = stats_tile(nt - 1, stats)

    rmax = jnp.max(mx, axis=0, keepdims=True)
    rmin = jnp.min(mn, axis=0, keepdims=True)
    minpos = jnp.min(mp, axis=0, keepdims=True)
    cpos = jnp.sum(cp, axis=0, keepdims=True)
    cnn = jnp.sum(cn, axis=0, keepdims=True)
    nvalid = (lax.broadcasted_iota(jnp.int32, (1, TQ), 1) + j * TQ + 1).astype(F32)
    small = nvalid <= kf

    def count_ge(thr):
        def body(kt, acc):
            for g in _row_groups(sc_ref[kt]):
                acc = acc + jnp.where(g >= thr, 1.0, 0.0)
            return acc
        acc = lax.fori_loop(0, nt, body, jnp.zeros((SUBLANES, TQ), F32))
        return jnp.sum(acc, axis=0, keepdims=True)

    ztie = jnp.logical_and(cpos < kf, cnn >= kf)
    pos_side = cpos >= kf
    hi_top = rmax + jnp.maximum(jnp.abs(rmax) * 1e-6, 1e-30)
    lo0 = jnp.where(small, 0.5 * NEG_INF, jnp.where(ztie, 0.0, jnp.where(pos_side, minpos, rmin)))
    hi0 = jnp.where(small, -NEG_INF, jnp.where(ztie, minpos, jnp.where(pos_side, hi_top, 0.0)))
    clo0 = jnp.where(jnp.logical_or(small, ztie), jnp.where(small, nvalid, cnn), jnp.where(pos_side, cpos, nvalid))
    chi0 = jnp.where(small, 0.0, jnp.where(ztie, cpos, jnp.where(pos_side, 0.0, cnn)))
    done0 = jnp.where(jnp.logical_or(jnp.logical_or(small, ztie), clo0 == kf), 1.0, 0.0)

    def bisect_step(state, mid, cnt, usable):
        lo, hi, clo, chi, done = state
        act = jnp.logical_and(done < 0.5, usable)
        ge = cnt >= kf
        up_lo = jnp.logical_and(act, ge)
        up_hi = jnp.logical_and(act, jnp.logical_not(ge))
        lo = jnp.where(up_lo, mid, lo)
        clo = jnp.where(up_lo, cnt, clo)
        hi = jnp.where(up_hi, mid, hi)
        chi = jnp.where(up_hi, cnt, chi)
        done = jnp.where(jnp.logical_and(act, cnt == kf), 1.0, done)
        return lo, hi, clo, chi, done

    def count_ge_bf16(thr):
        one, zero = jnp.ones((), BF16), jnp.zeros((), BF16)

        def body(kt, acc):
            t = scb_ref[kt]
            part = jnp.zeros((BF16_ROWS, TQ), BF16)
            for r in range(TQ // BF16_ROWS):
                part = part + jnp.where(t[r * BF16_ROWS:(r + 1) * BF16_ROWS, :] >= thr, one, zero)
            return acc + part.astype(F32)
        acc = lax.fori_loop(0, nt, body, jnp.zeros((BF16_ROWS, TQ), F32))
        return jnp.sum(acc, axis=0, keepdims=True)

    def coarse_step(_, state):
        lo, hi = state[0], state[1]
        mid_b = (lo + 0.5 * (hi - lo)).astype(BF16)
        mid = mid_b.astype(F32)
        usable = jnp.logical_and(mid > lo, mid < hi)
        return bisect_step(state, mid, count_ge_bf16(mid_b), usable)

    state = lax.fori_loop(0, COARSE_STEPS, coarse_step, (lo0, hi0, clo0, chi0, done0))

    def fine_step(_, state):
        lo, hi = state[0], state[1]
        mid = lo + 0.5 * (hi - lo)
        stalled = jnp.logical_or(mid <= lo, mid >= hi)
        state = bisect_step(state, mid, count_ge(mid), jnp.logical_not(stalled))
        return state[:4] + (jnp.where(stalled, 1.0, state[4]),)

    state = lax.fori_loop(0, FINE_STEPS_UNCHECKED, fine_step, state)

    def bis_cond(c):
        return jnp.logical_and(c[0] < BISECT_MAX_IT, c[2] > 0.0)

    def bis_body(c):
        it, state, _ = c
        state = lax.fori_loop(0, FINE_STEPS_PER_CHECK, fine_step, state)
        return it + FINE_STEPS_PER_CHECK, state, jnp.sum(1.0 - state[4])

    _, (lo, hi, clo, chi, _), _ = lax.while_loop(
        bis_cond, bis_body, (jnp.int32(0), state, jnp.sum(1.0 - state[4])))

    tie = jnp.logical_and(clo > kf, jnp.logical_not(small))
    band_quota = jnp.where(tie, kf - chi, float(seq))
    prefix_ones = jnp.where(key >= lax.broadcasted_iota(jnp.int32, (TQ, TQ), 1), 1.0, 0.0).astype(BF16)

    def mask_tiles(kts, before):
        tiles = [sc_ref[kt] for kt in kts]
        bands = [jnp.where(t >= lo, jnp.where(t < hi, 1.0, 0.0), 0.0).astype(BF16) for t in tiles]
        ranks = [_mm(prefix_ones, band) for band in bands]
        masks = []
        for t, rank in zip(tiles, ranks):
            rank = rank + before
            keep_band = jnp.where(rank <= band_quota, 0.0, NEG_INF)
            masks.append(jnp.where(t >= lo, jnp.where(t >= hi, 0.0, keep_band), NEG_INF))
            before = rank[TQ - 1:TQ, :]
        for kt, mask in zip(kts, masks):
            sc_ref[kt] = mask
        return before

    n_pairs = lax.shift_right_logical(nt, 1)
    before = lax.fori_loop(0, n_pairs, lambda i, c: mask_tiles([2 * i, 2 * i + 1], c), jnp.zeros((1, TQ), F32))
    lax.fori_loop(2 * n_pairs, nt, lambda kt, c: mask_tiles([kt], c), before)

    m_ref[...] = jnp.full(m_ref.shape, NEG_INF, F32)
    l_ref[...] = jnp.zeros(l_ref.shape, F32)
    acc_ref[...] = jnp.zeros(acc_ref.shape, F32)

    def logits_phase(kts, near, buf):
        tile_max = [None] * HEADS
        for i, kt in enumerate(kts):
            rows = pl.ds(pl.multiple_of(kt * TQ, TQ), TQ)
            mask_add = sc_ref[kt]
            for h in range(HEADS):
                p = h // 2
                s = _nt(k_ref[rows, p * LANES:(p + 1) * LANES], qm_ref[h]) + mask_add
                if near:
                    s = s + bias_ref[kt - j + 1, h]
                s_ref[buf, i, h] = s
                mx = jnp.max(s, axis=0, keepdims=True)
                tile_max[h] = mx if i == 0 else jnp.maximum(tile_max[h], mx)
        for h in range(HEADS):
            tmax_ref[buf, h] = tile_max[h]

    def softmax_phase(kts, buf):
        for h in range(HEADS):
            m_old = m_ref[h]
            m_new = jnp.maximum(m_old, tmax_ref[buf, h])
            alpha = jnp.exp2(m_old - m_new)
            pv = None
            for i, kt in enumerate(kts):
                part = _mm(vt_ref[kt, h], jnp.exp2(s_ref[buf, i, h] - m_new).astype(BF16))
                pv = part if i == 0 else pv + part
            m_ref[h] = m_new
            l_ref[h] = alpha * l_ref[h] + pv[HEAD_DIM:HEAD_DIM + 1, :]
            acc_ref[h] = alpha * acc_ref[h] + pv[0:HEAD_DIM, :]

    n_far = jnp.maximum(j - 1, 0)

    def attn_tiles(kts, near):
        logits_phase(kts, near, 0)
        softmax_phase(kts, 0)

    n_far_pairs = lax.shift_right_logical(n_far, 1)

    def far_pair(i, carry):
        attn_tiles([2 * i, 2 * i + 1], near=False)
        return carry

    def far_single(kt, carry):
        attn_tiles([kt], near=False)
        return carry

    lax.fori_loop(0, n_far_pairs, far_pair, 0)
    lax.fori_loop(2 * n_far_pairs, n_far, far_single, 0)

    @pl.when(j >= 1)
    def _():
        attn_tiles([j - 1, j], near=True)

    @pl.when(j == 0)
    def _():
        attn_tiles([j], near=True)

    for p in range(HEADS // 2):
        pair = jnp.concatenate([acc_ref[2 * p] / l_ref[2 * p], acc_ref[2 * p + 1] / l_ref[2 * p + 1]], axis=0)
        o_ref[:, p * LANES:(p + 1) * LANES] = pair.T.astype(o_ref.dtype)


def _dsa(z, bias, d_model):
    b, s, _ = z.shape
    n_keep = min(TOPK_MAX, s // 4)
    kern = functools.partial(_dsa_kernel, n_keep=n_keep, seq=s)
    ki_blk, wi_blk = _z_small_block(d_model, 0), _z_small_block(d_model, 1)
    return pl.pallas_call(
        kern,
        grid=(b, s // TQ),
        in_specs=[pl.BlockSpec((None, TQ, MIX_W), lambda bi, j: (bi, j, Z_Q)),
                  pl.BlockSpec((None, s, MIX_W), lambda bi, j: (bi, 0, Z_K)),
                  pl.BlockSpec((None, TQ, MIX_W), lambda bi, j: (bi, j, Z_V)),
                  pl.BlockSpec((None, TQ, MIX_W), lambda bi, j: (bi, j, Z_QI)),
                  pl.BlockSpec((None, s, LANES), lambda bi, j: (bi, 0, ki_blk)),
                  pl.BlockSpec((None, TQ, LANES), lambda bi, j: (bi, j, wi_blk)),
                  pl.BlockSpec((2, HEADS, TQ, TQ), lambda bi, j: (0, 0, 0, 0))],
        out_specs=pl.BlockSpec((None, TQ, MIX_W), lambda bi, j: (bi, j, 0)),
        out_shape=jax.ShapeDtypeStruct((b, s, MIX_W), BF16),
        scratch_shapes=[pltpu.VMEM((s // TQ, TQ, TQ), F32),
                        pltpu.VMEM((s // TQ, TQ, TQ), BF16),
                        pltpu.VMEM((HEADS, TQ, LANES), BF16),
                        pltpu.VMEM((HEADS, TQ, LANES), BF16),
                        pltpu.VMEM((LANES, TQ), F32),
                        pltpu.VMEM((s // TQ, HEADS, HEAD_DIM + ONES_ROWS, TQ), BF16),
                        pltpu.VMEM((HEADS, 1, TQ), F32),
                        pltpu.VMEM((HEADS, 1, TQ), F32),
                        pltpu.VMEM((HEADS, HEAD_DIM, TQ), F32),
                        pltpu.VMEM((1, 2, HEADS, TQ, TQ), F32),
                        pltpu.VMEM((1, HEADS, 1, TQ), F32)],
        compiler_params=_cparams(("parallel", "arbitrary")),
        name="dsa",
    )(z, z, z, z, z, z, bias)


def _halo_spec(width, blk, tm, rows):
    step = tm // rows
    return pl.BlockSpec((None, rows, width), lambda bi, i: (bi, jnp.maximum(i * step - 1, 0), blk))


def _shift_lerp(cur_ref, halo_ref, mu, first):
    cur = cur_ref[...].astype(F32)
    nh = halo_ref.shape[0]
    prev_last = jnp.where(first, 0.0, halo_ref[nh - 1:nh, :].astype(F32))
    rolled = pltpu.roll(cur, 1, axis=0)
    rowid = lax.broadcasted_iota(jnp.int32, cur.shape, 0)
    sh = jnp.where(rowid == 0, prev_last, rolled)
    return cur + (sh - cur) * mu


def _head_sum(x, bq_ref):
    xb = x.astype(BF16)
    return jnp.concatenate([_mm(xb[:, q * QUAD:(q + 1) * QUAD], bq_ref[...]) for q in range(MIX_W // QUAD)], axis=1)


def _rwkv_kernel(r_ref, k_ref, v_ref, s2_ref, s3_ref, rh_ref, kh_ref, vh_ref, s2h_ref, s3h_ref,
                 mu_r, mu_k, mu_v, mu_s2, mu_s3, w0_ref, wup_ref, a0_ref, aup_ref, gup_ref,
                 kk_ref, ka_ref, rk_ref, bq_ref, tri_ref,
                 y_ref, g_ref, bonus_ref,
                 st_ref, rs, ls, ks, vs, kks, kbs, s2s, s3s):
    first = pl.program_id(1) == 0

    @pl.when(first)
    def _():
        st_ref[...] = jnp.zeros_like(st_ref)

    rs[...] = _shift_lerp(r_ref, rh_ref, mu_r[...], first)
    ks[...] = _shift_lerp(k_ref, kh_ref, mu_k[...], first)
    vs[...] = _shift_lerp(v_ref, vh_ref, mu_v[...], first)
    s2s[...] = _shift_lerp(s2_ref, s2h_ref, mu_s2[...], first)
    s3s[...] = _shift_lerp(s3_ref, s3h_ref, mu_s3[...], first)

    def prepare(pair):
        rows = slice(pair * 2 * CHUNK, (pair + 1) * 2 * CHUNK)
        r, k, v, s2, s3 = rs[rows, :], ks[rows, :], vs[rows, :], s2s[rows, :], s3s[rows, :]
        xw = w0_ref[...] + _bmm(jnp.tanh(s2), wup_ref[...])
        softplus = jnp.maximum(-xw, 0.0) + jnp.log(1.0 + jnp.exp(-jnp.abs(xw)))
        ld = -jnp.exp(-softplus - 0.5)
        af = jax.nn.sigmoid(a0_ref[...] + _bmm(s2, aup_ref[...]))
        g_ref[rows, :] = _bmm(jax.nn.sigmoid(s3), gup_ref[...]).astype(g_ref.dtype)
        kkr = k * kk_ref[...]
        kkn = kkr / jnp.maximum(jnp.sqrt(_head_sum(kkr * kkr, bq_ref)), 1e-12)
        kmod = k * (1.0 + (af - 1.0) * ka_ref[...])
        bonus_ref[rows, :] = (_head_sum(r * kmod * rk_ref[...], bq_ref) * v).astype(bonus_ref.dtype)
        ls[rows, :] = ld
        ks[rows, :] = kmod
        kks[rows, :] = kkn
        kbs[rows, :] = kkn * af

    ri = lax.broadcasted_iota(jnp.int32, (QUAD, QUAD), 0)
    ci = lax.broadcasted_iota(jnp.int32, (QUAD, QUAD), 1)
    same_head = (ri // HEAD_DIM) == (ci // HEAD_DIM)
    strict = jnp.logical_and(same_head, (ri % CHUNK) > (ci % CHUNK))
    incl = jnp.logical_and(same_head, (ri % CHUNK) >= (ci % CHUNK))
    eye = jnp.where(ri == ci, 1.0, 0.0)

    def stack(x):
        return jnp.where(same_head, jnp.concatenate([x] * 4, axis=0), 0.0).astype(BF16)

    def tile4(x):
        return jnp.concatenate([x] * 4, axis=0).astype(BF16)

    def unstack(x):
        return (x[0:CHUNK] + x[CHUNK:2 * CHUNK]) + (x[2 * CHUNK:3 * CHUNK] + x[3 * CHUNK:4 * CHUNK])

    def chunk_operands(c):
        rows = slice(c * CHUNK, (c + 1) * CHUNK)
        ldc = ls[rows, :]
        p1 = ldc.astype(BF16)
        e1 = ldc - p1.astype(F32)
        p2 = e1.astype(BF16)
        p3 = (e1 - p2.astype(F32)).astype(BF16)
        tri = tri_ref[...]
        cl = (_mm(tri, p1) + _mm(tri, p2)) + _mm(tri, p3)
        cl_end = cl[CHUNK - 1:CHUNK, :]
        e_in = jnp.exp(cl)
        e_out = jnp.exp(-cl)
        e_end = jnp.exp(cl_end - cl)
        rt_all = rs[rows, :] * e_in
        at_all = -kks[rows, :] * jnp.exp(cl - ldc)
        bt_all = kbs[rows, :] * e_out
        kt_all = ks[rows, :] * e_out
        bg_all = (kbs[rows, :] * e_end).astype(BF16)
        kg_all = (ks[rows, :] * e_end).astype(BF16)
        v_all = vs[rows, :]
        gam_all = jnp.exp(cl_end)
        units = []
        for q in range(MIX_W // QUAD):
            sl = slice(q * QUAD, (q + 1) * QUAD)
            units.append(dict(rows=rows, sl=sl, q=q, rt=rt_all[:, sl], vv=v_all[:, sl].astype(BF16),
                              a4=stack(at_all[:, sl]), r4=stack(rt_all[:, sl]), v4=stack(v_all[:, sl]),
                              bt4=tile4(bt_all[:, sl]), kt4=tile4(kt_all[:, sl]),
                              bg=bg_all[:, sl], kg=kg_all[:, sl], gam=gam_all[:, sl]))
        return units

    n_doublings = int(math.log2(CHUNK)) - 1

    def chunk_pair(cp):
        us = chunk_operands(2 * cp) + chunk_operands(2 * cp + 1)
        for u in us:
            u["m_ab"] = jnp.where(strict, _nt(u["a4"], u["bt4"]), 0.0)
        for u in us:
            u["m_ak"] = jnp.where(strict, _nt(u["a4"], u["kt4"]), 0.0)
        for u in us:
            u["m_rb"] = jnp.where(incl, _nt(u["r4"], u["bt4"]), 0.0).astype(BF16)
        for u in us:
            u["m_rk"] = jnp.where(incl, _nt(u["r4"], u["kt4"]), 0.0).astype(BF16)
        for u in us:
            u["pw"] = u["m_ab"]
            u["inv"] = eye + u["m_ab"]
        for _ in range(n_doublings):
            for u in us:
                u["pw"] = _bmm(u["pw"], u["pw"])
            for u in us:
                u["inv"] = u["inv"] + _bmm(u["inv"], u["pw"])
        for u in us:
            u["inv"] = u["inv"].astype(BF16)
            u["mv4"] = _bmm(u["m_ak"], u["v4"])
        for u in us:
            u["ah4"] = _mm(u["inv"], u["a4"])
        for u in us:
            u["uh4"] = _bmm(u["inv"], u["mv4"])
        for u in us:
            u["ry"] = (u["rt"] + unstack(_bmm(u["m_rb"], u["ah4"]))).astype(BF16)
        for u in us:
            u["y0"] = unstack(_bmm(u["m_rb"], u["uh4"]) + _mm(u["m_rk"], u["v4"]))
        for u in us:
            ah, uh = unstack(u["ah4"]).astype(BF16), unstack(u["uh4"]).astype(BF16)
            u["g_low"] = jnp.where(same_head, _tn(u["bg"], ah), 0.0).astype(BF16)
            u["h_t"] = jnp.where(same_head, _tn(uh, u["bg"]) + _tn(u["vv"], u["kg"]), 0.0)
        for u in us:
            st = st_ref[u["q"]]
            stb = st.astype(BF16)
            y_ref[u["rows"], u["sl"]] = _nt(u["ry"], stb) + u["y0"]
            st_ref[u["q"]] = st * u["gam"] + _nt(stb, u["g_low"]) + u["h_t"]

    n_pairs = r_ref.shape[0] // (2 * CHUNK)
    prepare(0)
    for pair in range(n_pairs):
        if pair + 1 < n_pairs:
            prepare(pair + 1)
        chunk_pair(pair)


def _rwkv(z, prm, d_model):
    b, s, _ = z.shape
    tm = min(RW_TM, s)
    s2_blk, s3_blk = _z_small_block(d_model, 2), _z_small_block(d_model, 3)
    tok = lambda blk: pl.BlockSpec((None, tm, MIX_W), lambda bi, i: (bi, i, blk))
    tok128 = lambda blk: pl.BlockSpec((None, tm, LANES), lambda bi, i: (bi, i, blk))
    const = lambda a: pl.BlockSpec(a.shape, lambda bi, i: (0,) * a.ndim)
    consts = [prm[n] for n in ("mu_r", "mu_k", "mu_v", "mu_s2", "mu_s3", "w0", "w_up", "a0", "a_up", "g_up",
                               "k_k", "k_a", "r_k", "bones_q", "tri")]
    out_spec = pl.BlockSpec((None, tm, MIX_W), lambda bi, i: (bi, i, 0))
    return pl.pallas_call(
        _rwkv_kernel,
        grid=(b, s // tm),
        in_specs=[tok(Z_RR), tok(Z_RK), tok(Z_RV), tok128(s2_blk), tok128(s3_blk),
                  _halo_spec(MIX_W, Z_RR, tm, BF16_ROWS), _halo_spec(MIX_W, Z_RK, tm, BF16_ROWS),
                  _halo_spec(MIX_W, Z_RV, tm, BF16_ROWS), _halo_spec(LANES, s2_blk, tm, BF16_ROWS),
                  _halo_spec(LANES, s3_blk, tm, BF16_ROWS)] + [const(a) for a in consts],
        out_specs=[out_spec, out_spec, out_spec],
        out_shape=[jax.ShapeDtypeStruct((b, s, MIX_W), F32), jax.ShapeDtypeStruct((b, s, MIX_W), BF16),
                   jax.ShapeDtypeStruct((b, s, MIX_W), BF16)],
        scratch_shapes=([pltpu.VMEM((MIX_W // QUAD, QUAD, QUAD), F32)] + [pltpu.VMEM((tm, MIX_W), F32)] * 6
                        + [pltpu.VMEM((tm, LANES), F32)] * 2),
        compiler_params=_cparams(("parallel", "arbitrary")),
        name="rwkv",
    )(z, z, z, z, z, z, z, z, z, z, *consts)


def _causal_conv(p, halo, cw_ref):
    nh = halo.shape[0]
    rowid = lax.broadcasted_iota(jnp.int32, (SUBLANES, p.shape[1]), 0)
    conv = p * cw_ref[CONV_W - 1:CONV_W, :]
    for back in range(1, CONV_W):
        rolled = pltpu.roll(p, back, axis=0)
        head = rolled[0:SUBLANES]
        for rr in range(back):
            head = jnp.where(rowid == rr, halo[nh - back + rr:nh - back + rr + 1, :], head)
        rolled = jnp.concatenate([head, rolled[SUBLANES:]], axis=0)
        conv = conv + rolled * cw_ref[CONV_W - 1 - back:CONV_W - back, :]
    return conv


def _merge_kernel(x_ref, oa_ref, y_ref, g_ref, bonus_ref, cb_ref, cc_ref, cx_ref, cch_ref, cxh_ref,
                  gate0_ref, gate1_ref, gate2_ref, g1_ref, lnw_ref, lnb_ref, bq_ref, cw_ref, wb_ref, wo_ref, o_ref,
                  merged_ref):
    first = pl.program_id(1) == 0
    y = y_ref[...]
    inv_n = 1.0 / HEAD_DIM
    p1 = y.astype(BF16)
    mean = (_head_sum(p1, bq_ref) + _head_sum(y - p1.astype(F32), bq_ref)) * inv_n
    yc = y - mean
    var = _head_sum(yc * yc, bq_ref) * inv_n
    o_rwkv = ((yc * lax.rsqrt(var + GN_EPS) * lnw_ref[...] + lnb_ref[...] + bonus_ref[...].astype(F32))
              * g_ref[...].astype(F32))
    p = cc_ref[...].astype(F32) * cx_ref[...].astype(F32)
    ph = jnp.where(first, 0.0, cch_ref[...].astype(F32) * cxh_ref[...].astype(F32))
    o_conv = cb_ref[...].astype(F32) * _causal_conv(p, ph, cw_ref)
    branches = ((oa_ref[...], gate0_ref), (o_rwkv.astype(BF16), gate1_ref), (o_conv.astype(BF16), gate2_ref))
    d = x_ref.shape[-1]
    for c in range(d // QUAD):
        cols = slice(c * QUAD, (c + 1) * QUAD)
        part = jnp.zeros((x_ref.shape[0], QUAD), F32)
        for bi, (o, gate_ref) in enumerate(branches):
            part = part + jax.nn.sigmoid(gate_ref[:, cols].astype(F32)) * _mm(o, wb_ref[bi, :, cols])
        merged_ref[:, cols] = part.astype(BF16)
    o_ref[...] = x_ref[...] + g1_ref[...] * _mm(merged_ref[...], wo_ref[...])


def _merge(x, o_attn, y, g, bonus, z, mod5, layer, prm, tm=256):
    b, s, d = x.shape
    tm = min(tm, s)
    tok = lambda w, blk: pl.BlockSpec((None, tm, w), lambda bi, i: (bi, i, blk))
    const = lambda a: pl.BlockSpec(a.shape, lambda bi, i: (0,) * a.ndim)
    consts = [prm[n] for n in ("ln_w", "ln_b", "bones_q", "conv_w", "w_branch", "w_o")]
    gate_blk = Z_GATE_COL // d
    return pl.pallas_call(
        _merge_kernel,
        grid=(b, s // tm),
        in_specs=[tok(d, 0), tok(MIX_W, 0), tok(MIX_W, 0), tok(MIX_W, 0), tok(MIX_W, 0),
                  tok(MIX_W, Z_CB), tok(MIX_W, Z_CC), tok(MIX_W, Z_CX),
                  _halo_spec(MIX_W, Z_CC, tm, BF16_ROWS), _halo_spec(MIX_W, Z_CX, tm, BF16_ROWS),
                  tok(d, gate_blk), tok(d, gate_blk + 1), tok(d, gate_blk + 2),
                  _mod_spec(mod5, layer, 2, 2)] + [const(a) for a in consts],
        out_specs=tok(d, 0),
        out_shape=jax.ShapeDtypeStruct((b, s, d), F32),
        scratch_shapes=[pltpu.VMEM((tm, d), BF16)],
        compiler_params=_cparams(("parallel", "parallel")),
        name="merge",
    )(x, o_attn, y, g, bonus, z, z, z, z, z, z, z, z, mod5, *consts)


def _ffn_kernel(x_ref, xh_ref, sh_ref, sc_ref, g2_ref, gain_ref, wup_ref, cw_ref, wd_ref, fin_ref, o_ref,
                *, final_norm, tf):
    first = pl.program_id(1) == 0
    x = x_ref[...]
    h = _norm_mod(x, gain_ref[...], sc_ref[...], sh_ref[...]).astype(BF16)
    hh = _norm_mod(xh_ref[...], gain_ref[...], sc_ref[...], sh_ref[...]).astype(BF16)
    d_ff = wd_ref.shape[0]
    acc = jnp.zeros(x.shape, F32)
    for jf in range(d_ff // tf):
        cols = slice(jf * tf, (jf + 1) * tf)
        gate_cols = slice(d_ff + jf * tf, d_ff + (jf + 1) * tf)
        a = _mm(h, wup_ref[:, cols])
        ah = jnp.where(first, 0.0, _mm(hh, wup_ref[:, cols]))
        conv = _causal_conv(a, ah, cw_ref[:, cols])
        u = conv * jax.nn.sigmoid(conv) * _mm(h, wup_ref[:, gate_cols])
        acc = acc + _mm(u.astype(BF16), wd_ref[cols, :])
    out = x + g2_ref[...] * acc
    if final_norm:
        out = out * lax.rsqrt(jnp.mean(out * out, axis=-1, keepdims=True) + NORM_EPS) * fin_ref[...]
    o_ref[...] = out


def _ffn(x, mod5, layer, gain, w_up, conv_w, w_down, final_gain, final_norm, tm=512, tf=1408):
    b, s, d = x.shape
    d_ff = w_down.shape[0]
    assert d_ff % tf == 0 and tf % LANES == 0
    tm = min(tm, s)
    step = tm // SUBLANES
    gain, final_gain = gain.reshape(1, d), final_gain.reshape(1, d)
    return pl.pallas_call(
        functools.partial(_ffn_kernel, final_norm=final_norm, tf=tf),
        grid=(b, s // tm),
        in_specs=[pl.BlockSpec((None, tm, d), lambda bi, i: (bi, i, 0)),
                  pl.BlockSpec((None, SUBLANES, d), lambda bi, i: (bi, jnp.maximum(i * step - 1, 0), 0)),
                  _mod_spec(mod5, layer, 3, 2), _mod_spec(mod5, layer, 4, 2), _mod_spec(mod5, layer, 5, 2),
                  _resident(gain), _resident(w_up), _resident(conv_w), _resident(w_down), _resident(final_gain)],
        out_specs=pl.BlockSpec((None, tm, d), lambda bi, i: (bi, i, 0)),
        out_shape=jax.ShapeDtypeStruct((b, s, d), F32),
        compiler_params=_cparams(("parallel", "parallel")),
        name="ffn",
    )(x, x, mod5, mod5, mod5, gain, w_up, conv_w, w_down, final_gain)


def _split_w_in(w):
    sizes = (MIX_W, MIX_W, MIX_W, HEADS * HEAD_DIM, HEAD_DIM, HEADS,
             MIX_W, MIX_W, MIX_W, LORA_DECAY, LORA_ICLR, LORA_GATE,
             MIX_W, MIX_W, MIX_W)
    out, o = [], 0
    for n in sizes:
        out.append(w[:, o:o + n])
        o += n
    out.append(w[:, o:])
    return out


def _layer_params(l, w_in, rwkv_mu, rwkv_w0, rwkv_w_up, rwkv_a0, rwkv_a_up, rwkv_g_up, rwkv_k_k, rwkv_k_a,
                  rwkv_r_k, rwkv_ln_w, rwkv_ln_b, sc_conv_w, w_branch, w_o):
    d = w_in.shape[1]
    q, k, v, qi, ki, wi, rr, rk, rv, wd, ad, gd, cb, cc, cx, gates = _split_w_in(w_in[l])
    assert Z_GATE_COL % d == 0 and gates.shape[1] == 3 * d
    w_z = jnp.concatenate([q, k, v, qi, rr, rk, rv, cb, cc, cx, gates,
                           ki, ki, wi, jnp.zeros((d, LANES - HEADS), F32), wd, ad, gd], axis=1).astype(BF16)
    mu = rwkv_mu[l]
    row = lambda a: a.reshape(1, -1)
    head_id = jnp.arange(QUAD) // HEAD_DIM
    bones_q = (head_id[:, None] == head_id[None, :]).astype(BF16)
    tri = (jnp.arange(CHUNK)[:, None] >= jnp.arange(CHUNK)[None, :]).astype(BF16)
    zl = lambda n: jnp.zeros((n, MIX_W), F32)
    return dict(
        w_z=w_z,
        mu_r=row(mu[:MIX_W]), mu_k=row(mu[MIX_W:2 * MIX_W]), mu_v=row(mu[2 * MIX_W:3 * MIX_W]),
        mu_s2=row(mu[3 * MIX_W:3 * MIX_W + LORA_DECAY + LORA_ICLR]), mu_s3=row(mu[3 * MIX_W + LORA_DECAY + LORA_ICLR:]),
        w0=row(rwkv_w0[l]), w_up=jnp.concatenate([rwkv_w_up[l], zl(LORA_ICLR)], axis=0).astype(BF16),
        a0=row(rwkv_a0[l]), a_up=jnp.concatenate([zl(LORA_DECAY), rwkv_a_up[l]], axis=0).astype(BF16),
        g_up=rwkv_g_up[l].astype(BF16), k_k=row(rwkv_k_k[l]), k_a=row(rwkv_k_a[l]), r_k=row(rwkv_r_k[l]),
        bones_q=bones_q, tri=tri,
        ln_w=row(rwkv_ln_w[l]), ln_b=row(rwkv_ln_b[l]), conv_w=sc_conv_w[l].T,
        w_branch=w_branch[l].astype(BF16), w_o=w_o[l].astype(BF16),
    )


def kernel(x, c, positions, rel_bias, final_norm, ada_w, ada_b, norm_mix, w_in, rwkv_mu, rwkv_w0, rwkv_w_up,
           rwkv_a0, rwkv_a_up, rwkv_g_up, rwkv_k_k, rwkv_k_a, rwkv_r_k, rwkv_ln_w, rwkv_ln_b, sc_conv_w,
           w_branch, w_o, norm_ffn, ffn_w_up, ffn_conv_w, ffn_w_down):
    depth, d = ada_w.shape[0], x.shape[-1]
    mod5 = _ada_mod(c, ada_w, ada_b)
    bias = _bias_tiles(rel_bias)
    for l in range(depth):
        prm = _layer_params(l, w_in, rwkv_mu, rwkv_w0, rwkv_w_up, rwkv_a0, rwkv_a_up, rwkv_g_up, rwkv_k_k,
                            rwkv_k_a, rwkv_r_k, rwkv_ln_w, rwkv_ln_b, sc_conv_w, w_branch, w_o)
        z = _inproj(x, mod5, l, norm_mix[l], prm["w_z"])
        o_attn = _dsa(z, bias, d)
        y, g, bonus = _rwkv(z, prm, d)
        x = _merge(x, o_attn, y, g, bonus, z, mod5, l, prm)
        x = _ffn(x, mod5, l, norm_ffn[l], ffn_w_up[l].astype(BF16), ffn_conv_w[l].T, ffn_w_down[l].astype(BF16),
                 final_norm, final_norm=(l == depth - 1))
    return x
```

```python
import functools
import math

import jax
import jax.numpy as jnp
from jax import lax
from jax.experimental import pallas as pl
from jax.experimental.pallas import tpu as pltpu

F32 = jnp.float32
BF16 = jnp.bfloat16
HIGHEST = lax.Precision.HIGHEST

MIX_W = 512
HEADS = 8
HEAD_DIM = 64
TOPK_MAX = 256
N_BUCKETS = 32
MAX_DISTANCE = 128
LORA_DECAY = 64
LORA_ICLR = 64
LORA_GATE = 128
CONV_W = 3
NORM_EPS = 1e-6
GN_EPS = 64e-5
NEG_INF = -1e30

LANES = 128
SUBLANES = 8
BF16_ROWS = 16
VMEM_LIMIT = 56 * 1024 * 1024

TQ = 256
CHUNK = 64
RW_TM = 512
QUAD = 4 * HEAD_DIM
assert CHUNK == HEAD_DIM

Z_Q, Z_K, Z_V, Z_QI, Z_RR, Z_RK, Z_RV, Z_CB, Z_CC, Z_CX = range(10)
Z_GATE_COL = 10 * MIX_W


def _z_small_block(d_model, i):
    return (Z_GATE_COL + 3 * d_model) // LANES + i


def _cparams(sem):
    return pltpu.CompilerParams(dimension_semantics=sem, vmem_limit_bytes=VMEM_LIMIT)


def _nt(a, b, precision=None):
    return lax.dot_general(a, b, (((1,), (1,)), ((), ())), precision=precision,
                           preferred_element_type=F32)


def _tn(a, b, precision=None):
    return lax.dot_general(a, b, (((0,), (0,)), ((), ())), precision=precision,
                           preferred_element_type=F32)


def _mm(a, b, precision=None):
    return jnp.dot(a, b, precision=precision, preferred_element_type=F32)


def _bmm(a, b):
    return _mm(a.astype(BF16), b.astype(BF16))


def _mod_kernel(c_ref, w_ref, b_ref, o_ref):
    o_ref[...] = _mm(c_ref[...], w_ref[...], HIGHEST) + b_ref[...]


def _ada_mod(c, ada_w, ada_b):
    depth, d, d6 = ada_w.shape
    b = c.shape[0]
    out = pl.pallas_call(
        _mod_kernel,
        grid=(depth, d6 // d),
        in_specs=[pl.BlockSpec((b, d), lambda l, j: (0, 0)),
                  pl.BlockSpec((None, d, d), lambda l, j: (l, 0, j)),
                  pl.BlockSpec((None, 1, d), lambda l, j: (l, 0, j))],
        out_specs=pl.BlockSpec((None, b, d), lambda l, j: (l, 0, j)),
        out_shape=jax.ShapeDtypeStruct((depth, b, d6), F32),
        compiler_params=_cparams(("parallel", "parallel")),
        name="ada_mod",
    )(c, ada_w, ada_b.reshape(depth, 1, d6))
    return out.reshape(depth, b, d6 // d, 1, d)


def _mod_spec(mod5, layer, which, ngrid):
    d = mod5.shape[-1]
    if ngrid == 2:
        return pl.BlockSpec((None, None, None, 1, d), lambda b, i: (layer, b, which, 0, 0))
    return pl.BlockSpec((None, None, None, 1, d), lambda b, i, j: (layer, b, which, 0, 0))


def _norm_mod(x, gain, scale, shift):
    y = x * lax.rsqrt(jnp.mean(x * x, axis=-1, keepdims=True) + NORM_EPS) * gain
    return y * (1.0 + scale) + shift


def _resident(a):
    return pl.BlockSpec(a.shape, lambda bi, i: (0,) * a.ndim, pipeline_mode=pl.Buffered(1))


def _inproj_kernel(x_ref, sh_ref, sc_ref, g_ref, w_ref, o_ref, *, n_col_chunks):
    h = _norm_mod(x_ref[...], g_ref[...], sc_ref[...], sh_ref[...]).astype(BF16)
    tn = w_ref.shape[1] // n_col_chunks
    for c in range(n_col_chunks):
        o_ref[:, c * tn:(c + 1) * tn] = _mm(h, w_ref[:, c * tn:(c + 1) * tn]).astype(o_ref.dtype)


def _inproj(x, mod5, layer, gain, w, tm=512, n_col_chunks=4):
    b, s, d = x.shape
    zc = w.shape[1]
    tm = min(tm, s)
    assert zc % (n_col_chunks * LANES) == 0
    gain = gain.reshape(1, d)
    return pl.pallas_call(
        functools.partial(_inproj_kernel, n_col_chunks=n_col_chunks),
        grid=(b, s // tm),
        in_specs=[pl.BlockSpec((None, tm, d), lambda bi, i: (bi, i, 0)),
                  _mod_spec(mod5, layer, 0, 2),
                  _mod_spec(mod5, layer, 1, 2),
                  _resident(gain), _resident(w)],
        out_specs=pl.BlockSpec((None, tm, zc), lambda bi, i: (bi, i, 0)),
        out_shape=jax.ShapeDtypeStruct((b, s, zc), BF16),
        compiler_params=_cparams(("parallel", "parallel")),
        name="inproj",
    )(x, mod5, mod5, gain, w)


def _bias_kernel(rb_ref, o_ref):
    which = pl.program_id(0)
    h = pl.program_id(1)
    ri = lax.broadcasted_iota(jnp.int32, (TQ, TQ), 0)
    ci = lax.broadcasted_iota(jnp.int32, (TQ, TQ), 1)
    dist = ci - ri + (1 - which) * TQ
    n = jnp.maximum(dist, 0)
    max_exact = N_BUCKETS // 2
    nf = jnp.maximum(n, 1).astype(F32)
    large = max_exact + (jnp.log(nf / max_exact) / math.log(MAX_DISTANCE / max_exact)
                         * (N_BUCKETS - max_exact)).astype(jnp.int32)
    large = jnp.minimum(large, N_BUCKETS - 1)
    bucket = jnp.where(n < max_exact, n, large)
    far = rb_ref[N_BUCKETS - 1, h]
    acc = jnp.zeros((TQ, TQ), F32)
    for bkt in range(N_BUCKETS - 1):
        acc = jnp.where(bucket == bkt, rb_ref[bkt, h] - far, acc)
    o_ref[...] = acc * LOG2E


def _bias_tiles(rel_bias):
    assert TQ >= MAX_DISTANCE
    return pl.pallas_call(
        _bias_kernel,
        grid=(2, HEADS),
        in_specs=[pl.BlockSpec(memory_space=pltpu.SMEM)],
        out_specs=pl.BlockSpec((None, None, TQ, TQ), lambda w, h: (w, h, 0, 0)),
        out_shape=jax.ShapeDtypeStruct((2, HEADS, TQ, TQ), F32),
        compiler_params=_cparams(("parallel", "parallel")),
        name="bias_tiles",
    )(rel_bias)


BISECT_MAX_IT = 300
COARSE_STEPS = 10
FINE_STEPS_UNCHECKED = 4
FINE_STEPS_PER_CHECK = 2
LOG2E = math.log2(math.e)
ONES_ROWS = BF16_ROWS


def _row_groups(t):
    return [t[r * SUBLANES:(r + 1) * SUBLANES, :] for r in range(t.shape[0] // SUBLANES)]


def _dsa_kernel(q_ref, k_ref, v_ref, qi_ref, ki_ref, wi_ref, bias_ref, o_ref,
                sc_ref, scb_ref, qm_ref, qim_ref, wt_ref, vt_ref, m_ref, l_ref, acc_ref, s_ref, tmax_ref,
                *, n_keep, seq):
    j = pl.program_id(1)
    nt = j + 1
    kf = float(n_keep)
    lane = lax.broadcasted_iota(jnp.int32, (TQ, LANES), 1)
    att_scale = HEAD_DIM ** -0.5 * LOG2E
    w_scale = (HEADS ** -0.5) * (HEAD_DIM ** -0.5)

    for h in range(HEADS):
        p, odd = divmod(h, 2)
        hm = (lane >= HEAD_DIM) if odd else (lane < HEAD_DIM)
        qs = q_ref[:, p * LANES:(p + 1) * LANES]
        qm_ref[h] = jnp.where(hm, qs, jnp.zeros_like(qs)) * att_scale
        qis = qi_ref[:, p * LANES:(p + 1) * LANES]
        qim_ref[h] = jnp.where(hm, qis, jnp.zeros_like(qis))
    wt_ref[...] = wi_ref[...].astype(F32).T * w_scale
    v_t = v_ref[...].T
    for h in range(HEADS):
        vt_ref[j, h, 0:HEAD_DIM, :] = v_t[h * HEAD_DIM:(h + 1) * HEAD_DIM, :]
        vt_ref[j, h, HEAD_DIM:HEAD_DIM + ONES_ROWS, :] = jnp.ones((ONES_ROWS, TQ), BF16)

    key = lax.broadcasted_iota(jnp.int32, (TQ, TQ), 0)
    qry = lax.broadcasted_iota(jnp.int32, (TQ, TQ), 1) + j * TQ

    def floor_bf16(x):
        xi = lax.bitcast_convert_type(x, jnp.int32)
        xi = jnp.where(xi < 0, xi + jnp.int32(0xFFFF), xi) & jnp.int32(-0x10000)
        return lax.bitcast_convert_type(xi, F32).astype(BF16)

    def score_tile(kt):
        ki_t = ki_ref[pl.ds(pl.multiple_of(kt * TQ, TQ), TQ), :]
        acc = jnp.zeros((TQ, TQ), F32)
        for h in range(HEADS):
            acc = acc + jnp.maximum(_nt(ki_t, qim_ref[h]), 0.0) * wt_ref[h:h + 1, :]
        sc_ref[kt] = jnp.where(key + kt * TQ <= qry, acc, NEG_INF)

    def stats_tile(kt, c):
        mx, mn, mp, cp, cn = c
        score = sc_ref[kt]
        scb_ref[kt] = floor_bf16(score)
        for g in _row_groups(score):
            pos = g > 0.0
            mx = jnp.maximum(mx, g)
            mn = jnp.minimum(mn, jnp.where(g > 0.5 * NEG_INF, g, -NEG_INF))
            mp = jnp.minimum(mp, jnp.where(pos, g, -NEG_INF))
            cp = cp + jnp.where(pos, 1.0, 0.0)
            cn = cn + jnp.where(g >= 0.0, 1.0, 0.0)
        return mx, mn, mp, cp, cn

    def idx_tile(kt, c):
        c = stats_tile(kt - 1, c)
        score_tile(kt)
        return c

    part = lambda v: jnp.full((SUBLANES, TQ), v, F32)
    score_tile(0)
    stats = lax.fori_loop(1, nt, idx_tile, (part(NEG_INF), part(-NEG_INF), part(-NEG_INF), part(0.0), part(0.0)))
    mx, mn, mp, cp, cn = stats_tile(nt - 1, stats)

    rmax = jnp.max(mx, axis=0, keepdims=True)
    rmin = jnp.min(mn, axis=0, keepdims=True)
    minpos = jnp.min(mp, axis=0, keepdims=True)
    cpos = jnp.sum(cp, axis=0, keepdims=True)
    cnn = jnp.sum(cn, axis=0, keepdims=True)
    nvalid = (lax.broadcasted_iota(jnp.int32, (1, TQ), 1) + j * TQ + 1).astype(F32)
    small = nvalid <= kf

    def tile_loop(body, init):
        n_pairs = lax.shift_right_logical(nt, 1)
        c = lax.fori_loop(0, n_pairs, lambda i, c: body(2 * i + 1, body(2 * i, c)), init)
        return lax.fori_loop(2 * n_pairs, nt, body, c)

    def count_ge(thr):
        def body(kt, acc):
            for g in _row_groups(sc_ref[kt]):
                acc = acc + jnp.where(g >= thr, 1.0, 0.0)
            return acc
        acc = tile_loop(body, jnp.zeros((SUBLANES, TQ), F32))
        return jnp.sum(acc, axis=0, keepdims=True)

    ztie = jnp.logical_and(cpos < kf, cnn >= kf)
    pos_side = cpos >= kf
    hi_top = rmax + jnp.maximum(jnp.abs(rmax) * 1e-6, 1e-30)
    lo0 = jnp.where(small, 0.5 * NEG_INF, jnp.where(ztie, 0.0, jnp.where(pos_side, minpos, rmin)))
    hi0 = jnp.where(small, -NEG_INF, jnp.where(ztie, minpos, jnp.where(pos_side, hi_top, 0.0)))
    clo0 = jnp.where(jnp.logical_or(small, ztie), jnp.where(small, nvalid, cnn), jnp.where(pos_side, cpos, nvalid))
    chi0 = jnp.where(small, 0.0, jnp.where(ztie, cpos, jnp.where(pos_side, 0.0, cnn)))
    done0 = jnp.where(jnp.logical_or(jnp.logical_or(small, ztie), clo0 == kf), 1.0, 0.0)

    def bisect_step(state, mid, cnt, usable):
        lo, hi, clo, chi, done = state
        act = jnp.logical_and(done < 0.5, usable)
        ge = cnt >= kf
        up_lo = jnp.logical_and(act, ge)
        up_hi = jnp.logical_and(act, jnp.logical_not(ge))
        lo = jnp.where(up_lo, mid, lo)
        clo = jnp.where(up_lo, cnt, clo)
        hi = jnp.where(up_hi, mid, hi)
        chi = jnp.where(up_hi, cnt, chi)
        done = jnp.where(jnp.logical_and(act, cnt == kf), 1.0, done)
        return lo, hi, clo, chi, done

    def count_ge_bf16(thr):
        one, zero = jnp.ones((), BF16), jnp.zeros((), BF16)

        def body(kt, acc):
            t = scb_ref[kt]
            part = jnp.zeros((BF16_ROWS, TQ), BF16)
            for r in range(TQ // BF16_ROWS):
                part = part + jnp.where(t[r * BF16_ROWS:(r + 1) * BF16_ROWS, :] >= thr, one, zero)
            return acc + part.astype(F32)
        acc = tile_loop(body, jnp.zeros((BF16_ROWS, TQ), F32))
        return jnp.sum(acc, axis=0, keepdims=True)

    def coarse_step(_, state):
        lo, hi = state[0], state[1]
        mid_b = (lo + 0.5 * (hi - lo)).astype(BF16)
        mid = mid_b.astype(F32)
        usable = jnp.logical_and(mid > lo, mid < hi)
        return bisect_step(state, mid, count_ge_bf16(mid_b), usable)

    state = lax.fori_loop(0, COARSE_STEPS, coarse_step, (lo0, hi0, clo0, chi0, done0))

    def fine_step(_, state):
        lo, hi = state[0], state[1]
        mid = lo + 0.5 * (hi - lo)
        stalled = jnp.logical_or(mid <= lo, mid >= hi)
        state = bisect_step(state, mid, count_ge(mid), jnp.logical_not(stalled))
        return state[:4] + (jnp.where(stalled, 1.0, state[4]),)

    state = lax.fori_loop(0, FINE_STEPS_UNCHECKED, fine_step, state)

    def bis_cond(c):
        return jnp.logical_and(c[0] < BISECT_MAX_IT, c[2] > 0.0)

    def bis_body(c):
        it, state, _ = c
        state = lax.fori_loop(0, FINE_STEPS_PER_CHECK, fine_step, state)
        return it + FINE_STEPS_PER_CHECK, state, jnp.sum(1.0 - state[4])

    _, (lo, hi, clo, chi, _), _ = lax.while_loop(
        bis_cond, bis_body, (jnp.int32(0), state, jnp.sum(1.0 - state[4])))

    tie = jnp.logical_and(clo > kf, jnp.logical_not(small))
    band_quota = jnp.where(tie, kf - chi, float(seq))
    prefix_ones = jnp.where(key >= lax.broadcasted_iota(jnp.int32, (TQ, TQ), 1), 1.0, 0.0).astype(BF16)

    def mask_tiles(kts, before):
        tiles = [sc_ref[kt] for kt in kts]
        bands = [jnp.where(t >= lo, jnp.where(t < hi, 1.0, 0.0), 0.0).astype(BF16) for t in tiles]
        ranks = [_mm(prefix_ones, band) for band in bands]
        masks = []
        for t, rank in zip(tiles, ranks):
            rank = rank + before
            keep_band = jnp.where(rank <= band_quota, 0.0, NEG_INF)
            masks.append(jnp.where(t >= lo, jnp.where(t >= hi, 0.0, keep_band), NEG_INF))
            before = rank[TQ - 1:TQ, :]
        for kt, mask in zip(kts, masks):
            sc_ref[kt] = mask
        return before

    n_pairs = lax.shift_right_logical(nt, 1)
    before = lax.fori_loop(0, n_pairs, lambda i, c: mask_tiles([2 * i, 2 * i + 1], c), jnp.zeros((1, TQ), F32))
    lax.fori_loop(2 * n_pairs, nt, lambda kt, c: mask_tiles([kt], c), before)

    m_ref[...] = jnp.full(m_ref.shape, NEG_INF, F32)
    l_ref[...] = jnp.zeros(l_ref.shape, F32)
    acc_ref[...] = jnp.zeros(acc_ref.shape, F32)

    def logits_phase(kts, near, buf):
        tile_max = [None] * HEADS
        for i, kt in enumerate(kts):
            rows = pl.ds(pl.multiple_of(kt * TQ, TQ), TQ)
            mask_add = sc_ref[kt]
            for h in range(HEADS):
                p = h // 2
                s = _nt(k_ref[rows, p * LANES:(p + 1) * LANES], qm_ref[h]) + mask_add
                if near:
                    s = s + bias_ref[kt - j + 1, h]
                s_ref[buf, i, h] = s
                mx = jnp.max(s, axis=0, keepdims=True)
                tile_max[h] = mx if i == 0 else jnp.maximum(tile_max[h], mx)
        for h in range(HEADS):
            tmax_ref[buf, h] = tile_max[h]

    def softmax_phase(kts, buf):
        for h in range(HEADS):
            m_old = m_ref[h]
            m_new = jnp.maximum(m_old, tmax_ref[buf, h])
            alpha = jnp.exp2(m_old - m_new)
            pv = None
            for i, kt in enumerate(kts):
                part = _mm(vt_ref[kt, h], jnp.exp2(s_ref[buf, i, h] - m_new).astype(BF16))
                pv = part if i == 0 else pv + part
            m_ref[h] = m_new
            l_ref[h] = alpha * l_ref[h] + pv[HEAD_DIM:HEAD_DIM + 1, :]
            acc_ref[h] = alpha * acc_ref[h] + pv[0:HEAD_DIM, :]

    n_far = jnp.maximum(j - 1, 0)

    def attn_tiles(kts, near):
        logits_phase(kts, near, 0)
        softmax_phase(kts, 0)

    n_far_pairs = lax.shift_right_logical(n_far, 1)

    def far_pair(i, carry):
        attn_tiles([2 * i, 2 * i + 1], near=False)
        return carry

    def far_single(kt, carry):
        attn_tiles([kt], near=False)
        return carry

    lax.fori_loop(0, n_far_pairs, far_pair, 0)
    lax.fori_loop(2 * n_far_pairs, n_far, far_single, 0)

    @pl.when(j >= 1)
    def _():
        attn_tiles([j - 1, j], near=True)

    @pl.when(j == 0)
    def _():
        attn_tiles([j], near=True)

    for p in range(HEADS // 2):
        pair = jnp.concatenate([acc_ref[2 * p] / l_ref[2 * p], acc_ref[2 * p + 1] / l_ref[2 * p + 1]], axis=0)
        o_ref[:, p * LANES:(p + 1) * LANES] = pair.T.astype(o_ref.dtype)


def _dsa(z, bias, d_model):
    b, s, _ = z.shape
    n_keep = min(TOPK_MAX, s // 4)
    kern = functools.partial(_dsa_kernel, n_keep=n_keep, seq=s)
    ki_blk, wi_blk = _z_small_block(d_model, 0), _z_small_block(d_model, 1)
    return pl.pallas_call(
        kern,
        grid=(b, s // TQ),
        in_specs=[pl.BlockSpec((None, TQ, MIX_W), lambda bi, j: (bi, j, Z_Q)),
                  pl.BlockSpec((None, s, MIX_W), lambda bi, j: (bi, 0, Z_K)),
                  pl.BlockSpec((None, TQ, MIX_W), lambda bi, j: (bi, j, Z_V)),
                  pl.BlockSpec((None, TQ, MIX_W), lambda bi, j: (bi, j, Z_QI)),
                  pl.BlockSpec((None, s, LANES), lambda bi, j: (bi, 0, ki_blk)),
                  pl.BlockSpec((None, TQ, LANES), lambda bi, j: (bi, j, wi_blk)),
                  pl.BlockSpec((2, HEADS, TQ, TQ), lambda bi, j: (0, 0, 0, 0))],
        out_specs=pl.BlockSpec((None, TQ, MIX_W), lambda bi, j: (bi, j, 0)),
        out_shape=jax.ShapeDtypeStruct((b, s, MIX_W), BF16),
        scratch_shapes=[pltpu.VMEM((s // TQ, TQ, TQ), F32),
                        pltpu.VMEM((s // TQ, TQ, TQ), BF16),
                        pltpu.VMEM((HEADS, TQ, LANES), BF16),
                        pltpu.VMEM((HEADS, TQ, LANES), BF16),
                        pltpu.VMEM((LANES, TQ), F32),
                        pltpu.VMEM((s // TQ, HEADS, HEAD_DIM + ONES_ROWS, TQ), BF16),
                        pltpu.VMEM((HEADS, 1, TQ), F32),
                        pltpu.VMEM((HEADS, 1, TQ), F32),
                        pltpu.VMEM((HEADS, HEAD_DIM, TQ), F32),
                        pltpu.VMEM((1, 2, HEADS, TQ, TQ), F32),
                        pltpu.VMEM((1, HEADS, 1, TQ), F32)],
        compiler_params=_cparams(("parallel", "arbitrary")),
        name="dsa",
    )(z, z, z, z, z, z, bias)


def _halo_spec(width, blk, tm, rows):
    step = tm // rows
    return pl.BlockSpec((None, rows, width), lambda bi, i: (bi, jnp.maximum(i * step - 1, 0), blk))


def _shift_lerp(cur_ref, halo_ref, mu, first):
    cur = cur_ref[...].astype(F32)
    nh = halo_ref.shape[0]
    prev_last = jnp.where(first, 0.0, halo_ref[nh - 1:nh, :].astype(F32))
    rolled = pltpu.roll(cur, 1, axis=0)
    rowid = lax.broadcasted_iota(jnp.int32, cur.shape, 0)
    sh = jnp.where(rowid == 0, prev_last, rolled)
    return cur + (sh - cur) * mu


def _head_sum(x, bq_ref):
    xb = x.astype(BF16)
    return jnp.concatenate([_mm(xb[:, q * QUAD:(q + 1) * QUAD], bq_ref[...]) for q in range(MIX_W // QUAD)], axis=1)


def _rwkv_kernel(r_ref, k_ref, v_ref, s2_ref, s3_ref, rh_ref, kh_ref, vh_ref, s2h_ref, s3h_ref,
                 mu_r, mu_k, mu_v, mu_s2, mu_s3, w0_ref, wup_ref, a0_ref, aup_ref, gup_ref,
                 kk_ref, ka_ref, rk_ref, bq_ref, tri_ref,
                 y_ref, g_ref, bonus_ref,
                 st_ref, rs, ls, ks, vs, kks, kbs, s2s, s3s):
    first = pl.program_id(1) == 0

    @pl.when(first)
    def _():
        st_ref[...] = jnp.zeros_like(st_ref)

    rs[...] = _shift_lerp(r_ref, rh_ref, mu_r[...], first)
    ks[...] = _shift_lerp(k_ref, kh_ref, mu_k[...], first)
    vs[...] = _shift_lerp(v_ref, vh_ref, mu_v[...], first)
    s2s[...] = _shift_lerp(s2_ref, s2h_ref, mu_s2[...], first)
    s3s[...] = _shift_lerp(s3_ref, s3h_ref, mu_s3[...], first)

    def prepare(pair):
        rows = slice(pair * 2 * CHUNK, (pair + 1) * 2 * CHUNK)
        r, k, v, s2, s3 = rs[rows, :], ks[rows, :], vs[rows, :], s2s[rows, :], s3s[rows, :]
        xw = w0_ref[...] + _bmm(jnp.tanh(s2), wup_ref[...])
        softplus = jnp.maximum(-xw, 0.0) + jnp.log(1.0 + jnp.exp(-jnp.abs(xw)))
        ld = -jnp.exp(-softplus - 0.5)
        af = jax.nn.sigmoid(a0_ref[...] + _bmm(s2, aup_ref[...]))
        g_ref[rows, :] = _bmm(jax.nn.sigmoid(s3), gup_ref[...]).astype(g_ref.dtype)
        kkr = k * kk_ref[...]
        kkn = kkr / jnp.maximum(jnp.sqrt(_head_sum(kkr * kkr, bq_ref)), 1e-12)
        kmod = k * (1.0 + (af - 1.0) * ka_ref[...])
        bonus_ref[rows, :] = (_head_sum(r * kmod * rk_ref[...], bq_ref) * v).astype(bonus_ref.dtype)
        ls[rows, :] = ld
        ks[rows, :] = kmod
        kks[rows, :] = kkn
        kbs[rows, :] = kkn * af

    ri = lax.broadcasted_iota(jnp.int32, (QUAD, QUAD), 0)
    ci = lax.broadcasted_iota(jnp.int32, (QUAD, QUAD), 1)
    same_head = (ri // HEAD_DIM) == (ci // HEAD_DIM)
    strict = jnp.logical_and(same_head, (ri % CHUNK) > (ci % CHUNK))
    incl = jnp.logical_and(same_head, (ri % CHUNK) >= (ci % CHUNK))
    eye = jnp.where(ri == ci, 1.0, 0.0)

    def stack(x):
        return jnp.where(same_head, jnp.concatenate([x] * 4, axis=0), 0.0).astype(BF16)

    def tile4(x):
        return jnp.concatenate([x] * 4, axis=0).astype(BF16)

    def unstack(x):
        return (x[0:CHUNK] + x[CHUNK:2 * CHUNK]) + (x[2 * CHUNK:3 * CHUNK] + x[3 * CHUNK:4 * CHUNK])

    def chunk_operands(c):
        rows = slice(c * CHUNK, (c + 1) * CHUNK)
        ldc = ls[rows, :]
        p1 = ldc.astype(BF16)
        e1 = ldc - p1.astype(F32)
        p2 = e1.astype(BF16)
        p3 = (e1 - p2.astype(F32)).astype(BF16)
        tri = tri_ref[...]
        cl = (_mm(tri, p1) + _mm(tri, p2)) + _mm(tri, p3)
        cl_end = cl[CHUNK - 1:CHUNK, :]
        e_in = jnp.exp(cl)
        e_out = jnp.exp(-cl)
        e_end = jnp.exp(cl_end - cl)
        rt_all = rs[rows, :] * e_in
        at_all = -kks[rows, :] * jnp.exp(cl - ldc)
        bt_all = kbs[rows, :] * e_out
        kt_all = ks[rows, :] * e_out
        bg_all = (kbs[rows, :] * e_end).astype(BF16)
        kg_all = (ks[rows, :] * e_end).astype(BF16)
        v_all = vs[rows, :]
        gam_all = jnp.exp(cl_end)
        units = []
        for q in range(MIX_W // QUAD):
            sl = slice(q * QUAD, (q + 1) * QUAD)
            units.append(dict(rows=rows, sl=sl, q=q, rt=rt_all[:, sl], vv=v_all[:, sl].astype(BF16),
                              a4=stack(at_all[:, sl]), r4=stack(rt_all[:, sl]), v4=stack(v_all[:, sl]),
                              bt4=tile4(bt_all[:, sl]), kt4=tile4(kt_all[:, sl]),
                              bg=bg_all[:, sl], kg=kg_all[:, sl], gam=gam_all[:, sl]))
        return units

    n_doublings = int(math.log2(CHUNK)) - 1

    def chunk_pair(cp):
        us = chunk_operands(2 * cp) + chunk_operands(2 * cp + 1)
        for u in us:
            u["m_ab"] = jnp.where(strict, _nt(u["a4"], u["bt4"]), 0.0)
        for u in us:
            u["m_ak"] = jnp.where(strict, _nt(u["a4"], u["kt4"]), 0.0)
        for u in us:
            u["m_rb"] = jnp.where(incl, _nt(u["r4"], u["bt4"]), 0.0).astype(BF16)
        for u in us:
            u["m_rk"] = jnp.where(incl, _nt(u["r4"], u["kt4"]), 0.0).astype(BF16)
        for u in us:
            u["pw"] = u["m_ab"]
            u["inv"] = eye + u["m_ab"]
        for _ in range(n_doublings):
            for u in us:
                u["pw"] = _bmm(u["pw"], u["pw"])
            for u in us:
                u["inv"] = u["inv"] + _bmm(u["inv"], u["pw"])
        for u in us:
            u["inv"] = u["inv"].astype(BF16)
            u["mv4"] = _bmm(u["m_ak"], u["v4"])
        for u in us:
            u["ah4"] = _mm(u["inv"], u["a4"])
        for u in us:
            u["uh4"] = _bmm(u["inv"], u["mv4"])
        for u in us:
            u["ry"] = (u["rt"] + unstack(_bmm(u["m_rb"], u["ah4"]))).astype(BF16)
        for u in us:
            u["y0"] = unstack(_bmm(u["m_rb"], u["uh4"]) + _mm(u["m_rk"], u["v4"]))
        for u in us:
            ah, uh = unstack(u["ah4"]).astype(BF16), unstack(u["uh4"]).astype(BF16)
            u["g_low"] = jnp.where(same_head, _tn(u["bg"], ah), 0.0).astype(BF16)
            u["h_t"] = jnp.where(same_head, _tn(uh, u["bg"]) + _tn(u["vv"], u["kg"]), 0.0)
        for u in us:
            st = st_ref[u["q"]]
            stb = st.astype(BF16)
            y_ref[u["rows"], u["sl"]] = _nt(u["ry"], stb) + u["y0"]
            st_ref[u["q"]] = st * u["gam"] + _nt(stb, u["g_low"]) + u["h_t"]

    n_pairs = r_ref.shape[0] // (2 * CHUNK)
    prepare(0)
    for pair in range(n_pairs):
        if pair + 1 < n_pairs:
            prepare(pair + 1)
        chunk_pair(pair)


def _rwkv(z, prm, d_model):
    b, s, _ = z.shape
    tm = min(RW_TM, s)
    s2_blk, s3_blk = _z_small_block(d_model, 2), _z_small_block(d_model, 3)
    tok = lambda blk: pl.BlockSpec((None, tm, MIX_W), lambda bi, i: (bi, i, blk))
    tok128 = lambda blk: pl.BlockSpec((None, tm, LANES), lambda bi, i: (bi, i, blk))
    const = lambda a: pl.BlockSpec(a.shape, lambda bi, i: (0,) * a.ndim)
    consts = [prm[n] for n in ("mu_r", "mu_k", "mu_v", "mu_s2", "mu_s3", "w0", "w_up", "a0", "a_up", "g_up",
                               "k_k", "k_a", "r_k", "bones_q", "tri")]
    out_spec = pl.BlockSpec((None, tm, MIX_W), lambda bi, i: (bi, i, 0))
    return pl.pallas_call(
        _rwkv_kernel,
        grid=(b, s // tm),
        in_specs=[tok(Z_RR), tok(Z_RK), tok(Z_RV), tok128(s2_blk), tok128(s3_blk),
                  _halo_spec(MIX_W, Z_RR, tm, BF16_ROWS), _halo_spec(MIX_W, Z_RK, tm, BF16_ROWS),
                  _halo_spec(MIX_W, Z_RV, tm, BF16_ROWS), _halo_spec(LANES, s2_blk, tm, BF16_ROWS),
                  _halo_spec(LANES, s3_blk, tm, BF16_ROWS)] + [const(a) for a in consts],
        out_specs=[out_spec, out_spec, out_spec],
        out_shape=[jax.ShapeDtypeStruct((b, s, MIX_W), F32), jax.ShapeDtypeStruct((b, s, MIX_W), BF16),
                   jax.ShapeDtypeStruct((b, s, MIX_W), BF16)],
        scratch_shapes=([pltpu.VMEM((MIX_W // QUAD, QUAD, QUAD), F32)] + [pltpu.VMEM((tm, MIX_W), F32)] * 6
                        + [pltpu.VMEM((tm, LANES), F32)] * 2),
        compiler_params=_cparams(("parallel", "arbitrary")),
        name="rwkv",
    )(z, z, z, z, z, z, z, z, z, z, *consts)


def _causal_conv(p, halo, cw_ref):
    nh = halo.shape[0]
    rowid = lax.broadcasted_iota(jnp.int32, (SUBLANES, p.shape[1]), 0)
    conv = p * cw_ref[CONV_W - 1:CONV_W, :]
    for back in range(1, CONV_W):
        rolled = pltpu.roll(p, back, axis=0)
        head = rolled[0:SUBLANES]
        for rr in range(back):
            head = jnp.where(rowid == rr, halo[nh - back + rr:nh - back + rr + 1, :], head)
        rolled = jnp.concatenate([head, rolled[SUBLANES:]], axis=0)
        conv = conv + rolled * cw_ref[CONV_W - 1 - back:CONV_W - back, :]
    return conv


def _merge_kernel(x_ref, oa_ref, y_ref, g_ref, bonus_ref, cb_ref, cc_ref, cx_ref, cch_ref, cxh_ref,
                  gate0_ref, gate1_ref, gate2_ref, g1_ref, lnw_ref, lnb_ref, bq_ref, cw_ref, wb_ref, wo_ref, o_ref,
                  merged_ref):
    first = pl.program_id(1) == 0
    y = y_ref[...]
    inv_n = 1.0 / HEAD_DIM
    p1 = y.astype(BF16)
    mean = (_head_sum(p1, bq_ref) + _head_sum(y - p1.astype(F32), bq_ref)) * inv_n
    yc = y - mean
    var = _head_sum(yc * yc, bq_ref) * inv_n
    o_rwkv = ((yc * lax.rsqrt(var + GN_EPS) * lnw_ref[...] + lnb_ref[...] + bonus_ref[...].astype(F32))
              * g_ref[...].astype(F32))
    p = cc_ref[...].astype(F32) * cx_ref[...].astype(F32)
    ph = jnp.where(first, 0.0, cch_ref[...].astype(F32) * cxh_ref[...].astype(F32))
    o_conv = cb_ref[...].astype(F32) * _causal_conv(p, ph, cw_ref)
    branches = ((oa_ref[...], gate0_ref), (o_rwkv.astype(BF16), gate1_ref), (o_conv.astype(BF16), gate2_ref))
    d = x_ref.shape[-1]
    for c in range(d // QUAD):
        cols = slice(c * QUAD, (c + 1) * QUAD)
        part = jnp.zeros((x_ref.shape[0], QUAD), F32)
        for bi, (o, gate_ref) in enumerate(branches):
            part = part + jax.nn.sigmoid(gate_ref[:, cols].astype(F32)) * _mm(o, wb_ref[bi, :, cols])
        merged_ref[:, cols] = part.astype(BF16)
    o_ref[...] = x_ref[...] + g1_ref[...] * _mm(merged_ref[...], wo_ref[...])


def _merge(x, o_attn, y, g, bonus, z, mod5, layer, prm, tm=256):
    b, s, d = x.shape
    tm = min(tm, s)
    tok = lambda w, blk: pl.BlockSpec((None, tm, w), lambda bi, i: (bi, i, blk))
    const = lambda a: pl.BlockSpec(a.shape, lambda bi, i: (0,) * a.ndim)
    consts = [prm[n] for n in ("ln_w", "ln_b", "bones_q", "conv_w", "w_branch", "w_o")]
    gate_blk = Z_GATE_COL // d
    return pl.pallas_call(
        _merge_kernel,
        grid=(b, s // tm),
        in_specs=[tok(d, 0), tok(MIX_W, 0), tok(MIX_W, 0), tok(MIX_W, 0), tok(MIX_W, 0),
                  tok(MIX_W, Z_CB), tok(MIX_W, Z_CC), tok(MIX_W, Z_CX),
                  _halo_spec(MIX_W, Z_CC, tm, BF16_ROWS), _halo_spec(MIX_W, Z_CX, tm, BF16_ROWS),
                  tok(d, gate_blk), tok(d, gate_blk + 1), tok(d, gate_blk + 2),
                  _mod_spec(mod5, layer, 2, 2)] + [const(a) for a in consts],
        out_specs=tok(d, 0),
        out_shape=jax.ShapeDtypeStruct((b, s, d), F32),
        scratch_shapes=[pltpu.VMEM((tm, d), BF16)],
        compiler_params=_cparams(("parallel", "parallel")),
        name="merge",
    )(x, o_attn, y, g, bonus, z, z, z, z, z, z, z, z, mod5, *consts)


def _ffn_kernel(x_ref, xh_ref, sh_ref, sc_ref, g2_ref, gain_ref, wup_ref, cw_ref, wd_ref, fin_ref, o_ref,
                *, final_norm, tf):
    first = pl.program_id(1) == 0
    x = x_ref[...]
    h = _norm_mod(x, gain_ref[...], sc_ref[...], sh_ref[...]).astype(BF16)
    hh = _norm_mod(xh_ref[...], gain_ref[...], sc_ref[...], sh_ref[...]).astype(BF16)
    d_ff = wd_ref.shape[0]
    acc = jnp.zeros(x.shape, F32)
    for jf in range(d_ff // tf):
        cols = slice(jf * tf, (jf + 1) * tf)
        gate_cols = slice(d_ff + jf * tf, d_ff + (jf + 1) * tf)
        a = _mm(h, wup_ref[:, cols])
        ah = jnp.where(first, 0.0, _mm(hh, wup_ref[:, cols]))
        conv = _causal_conv(a, ah, cw_ref[:, cols])
        u = conv * jax.nn.sigmoid(conv) * _mm(h, wup_ref[:, gate_cols])
        acc = acc + _mm(u.astype(BF16), wd_ref[cols, :])
    out = x + g2_ref[...] * acc
    if final_norm:
        out = out * lax.rsqrt(jnp.mean(out * out, axis=-1, keepdims=True) + NORM_EPS) * fin_ref[...]
    o_ref[...] = out


def _ffn(x, mod5, layer, gain, w_up, conv_w, w_down, final_gain, final_norm, tm=512, tf=1408):
    b, s, d = x.shape
    d_ff = w_down.shape[0]
    assert d_ff % tf == 0 and tf % LANES == 0
    tm = min(tm, s)
    step = tm // SUBLANES
    gain, final_gain = gain.reshape(1, d), final_gain.reshape(1, d)
    return pl.pallas_call(
        functools.partial(_ffn_kernel, final_norm=final_norm, tf=tf),
        grid=(b, s // tm),
        in_specs=[pl.BlockSpec((None, tm, d), lambda bi, i: (bi, i, 0)),
                  pl.BlockSpec((None, SUBLANES, d), lambda bi, i: (bi, jnp.maximum(i * step - 1, 0), 0)),
                  _mod_spec(mod5, layer, 3, 2), _mod_spec(mod5, layer, 4, 2), _mod_spec(mod5, layer, 5, 2),
                  _resident(gain), _resident(w_up), _resident(conv_w), _resident(w_down), _resident(final_gain)],
        out_specs=pl.BlockSpec((None, tm, d), lambda bi, i: (bi, i, 0)),
        out_shape=jax.ShapeDtypeStruct((b, s, d), F32),
        compiler_params=_cparams(("parallel", "parallel")),
        name="ffn",
    )(x, x, mod5, mod5, mod5, gain, w_up, conv_w, w_down, final_gain)


def _split_w_in(w):
    sizes = (MIX_W, MIX_W, MIX_W, HEADS * HEAD_DIM, HEAD_DIM, HEADS,
             MIX_W, MIX_W, MIX_W, LORA_DECAY, LORA_ICLR, LORA_GATE,
             MIX_W, MIX_W, MIX_W)
    out, o = [], 0
    for n in sizes:
        out.append(w[:, o:o + n])
        o += n
    out.append(w[:, o:])
    return out


def _layer_params(l, w_in, rwkv_mu, rwkv_w0, rwkv_w_up, rwkv_a0, rwkv_a_up, rwkv_g_up, rwkv_k_k, rwkv_k_a,
                  rwkv_r_k, rwkv_ln_w, rwkv_ln_b, sc_conv_w, w_branch, w_o):
    d = w_in.shape[1]
    q, k, v, qi, ki, wi, rr, rk, rv, wd, ad, gd, cb, cc, cx, gates = _split_w_in(w_in[l])
    assert Z_GATE_COL % d == 0 and gates.shape[1] == 3 * d
    w_z = jnp.concatenate([q, k, v, qi, rr, rk, rv, cb, cc, cx, gates,
                           ki, ki, wi, jnp.zeros((d, LANES - HEADS), F32), wd, ad, gd], axis=1).astype(BF16)
    mu = rwkv_mu[l]
    row = lambda a: a.reshape(1, -1)
    head_id = jnp.arange(QUAD) // HEAD_DIM
    bones_q = (head_id[:, None] == head_id[None, :]).astype(BF16)
    tri = (jnp.arange(CHUNK)[:, None] >= jnp.arange(CHUNK)[None, :]).astype(BF16)
    zl = lambda n: jnp.zeros((n, MIX_W), F32)
    return dict(
        w_z=w_z,
        mu_r=row(mu[:MIX_W]), mu_k=row(mu[MIX_W:2 * MIX_W]), mu_v=row(mu[2 * MIX_W:3 * MIX_W]),
        mu_s2=row(mu[3 * MIX_W:3 * MIX_W + LORA_DECAY + LORA_ICLR]), mu_s3=row(mu[3 * MIX_W + LORA_DECAY + LORA_ICLR:]),
        w0=row(rwkv_w0[l]), w_up=jnp.concatenate([rwkv_w_up[l], zl(LORA_ICLR)], axis=0).astype(BF16),
        a0=row(rwkv_a0[l]), a_up=jnp.concatenate([zl(LORA_DECAY), rwkv_a_up[l]], axis=0).astype(BF16),
        g_up=rwkv_g_up[l].astype(BF16), k_k=row(rwkv_k_k[l]), k_a=row(rwkv_k_a[l]), r_k=row(rwkv_r_k[l]),
        bones_q=bones_q, tri=tri,
        ln_w=row(rwkv_ln_w[l]), ln_b=row(rwkv_ln_b[l]), conv_w=sc_conv_w[l].T,
        w_branch=w_branch[l].astype(BF16), w_o=w_o[l].astype(BF16),
    )


def kernel(x, c, positions, rel_bias, final_norm, ada_w, ada_b, norm_mix, w_in, rwkv_mu, rwkv_w0, rwkv_w_up,
           rwkv_a0, rwkv_a_up, rwkv_g_up, rwkv_k_k, rwkv_k_a, rwkv_r_k, rwkv_ln_w, rwkv_ln_b, sc_conv_w,
           w_branch, w_o, norm_ffn, ffn_w_up, ffn_conv_w, ffn_w_down):
    depth, d = ada_w.shape[0], x.shape[-1]
    mod5 = _ada_mod(c, ada_w, ada_b)
    bias = _bias_tiles(rel_bias)
    for l in range(depth):
        prm = _layer_params(l, w_in, rwkv_mu, rwkv_w0, rwkv_w_up, rwkv_a0, rwkv_a_up, rwkv_g_up, rwkv_k_k,
                            rwkv_k_a, rwkv_r_k, rwkv_ln_w, rwkv_ln_b, sc_conv_w, w_branch, w_o)
        z = _inproj(x, mod5, l, norm_mix[l], prm["w_z"])
        o_attn = _dsa(z, bias, d)
        y, g, bonus = _rwkv(z, prm, d)
        x = _merge(x, o_attn, y, g, bonus, z, mod5, l, prm)
        x = _ffn(x, mod5, l, norm_ffn[l], ffn_w_up[l].astype(BF16), ffn_conv_w[l].T, ffn_w_down[l].astype(BF16),
                 final_norm, final_norm=(l == depth - 1))
    return x
```

```python
import functools
import math

import jax
import jax.numpy as jnp
from jax import lax
from jax.experimental import pallas as pl
from jax.experimental.pallas import tpu as pltpu

F32 = jnp.float32
BF16 = jnp.bfloat16
HIGHEST = lax.Precision.HIGHEST

MIX_W = 512
HEADS = 8
HEAD_DIM = 64
TOPK_MAX = 256
N_BUCKETS = 32
MAX_DISTANCE = 128
LORA_DECAY = 64
LORA_ICLR = 64
LORA_GATE = 128
CONV_W = 3
NORM_EPS = 1e-6
GN_EPS = 64e-5
NEG_INF = -1e30

LANES = 128
SUBLANES = 8
BF16_ROWS = 16
VMEM_LIMIT = 56 * 1024 * 1024

TQ = 256
CHUNK = 64
RW_TM = 512
QUAD = 4 * HEAD_DIM
assert CHUNK == HEAD_DIM

Z_Q, Z_K, Z_V, Z_QI, Z_RR, Z_RK, Z_RV, Z_CB, Z_CC, Z_CX = range(10)
Z_GATE_COL = 10 * MIX_W


def _z_small_block(d_model, i):
    return (Z_GATE_COL + 3 * d_model) // LANES + i


def _cparams(sem):
    return pltpu.CompilerParams(dimension_semantics=sem, vmem_limit_bytes=VMEM_LIMIT)


def _nt(a, b, precision=None):
    return lax.dot_general(a, b, (((1,), (1,)), ((), ())), precision=precision,
                           preferred_element_type=F32)


def _tn(a, b, precision=None):
    return lax.dot_general(a, b, (((0,), (0,)), ((), ())), precision=precision,
                           preferred_element_type=F32)


def _mm(a, b, precision=None):
    return jnp.dot(a, b, precision=precision, preferred_element_type=F32)


def _bmm(a, b):
    return _mm(a.astype(BF16), b.astype(BF16))


def _mod_kernel(c_ref, w_ref, b_ref, o_ref):
    o_ref[...] = _mm(c_ref[...], w_ref[...], HIGHEST) + b_ref[...]


def _ada_mod(c, ada_w, ada_b):
    depth, d, d6 = ada_w.shape
    b = c.shape[0]
    out = pl.pallas_call(
        _mod_kernel,
        grid=(depth, d6 // d),
        in_specs=[pl.BlockSpec((b, d), lambda l, j: (0, 0)),
                  pl.BlockSpec((None, d, d), lambda l, j: (l, 0, j)),
                  pl.BlockSpec((None, 1, d), lambda l, j: (l, 0, j))],
        out_specs=pl.BlockSpec((None, b, d), lambda l, j: (l, 0, j)),
        out_shape=jax.ShapeDtypeStruct((depth, b, d6), F32),
        compiler_params=_cparams(("parallel", "parallel")),
        name="ada_mod",
    )(c, ada_w, ada_b.reshape(depth, 1, d6))
    return out.reshape(depth, b, d6 // d, 1, d)


def _mod_spec(mod5, layer, which, ngrid):
    d = mod5.shape[-1]
    if ngrid == 2:
        return pl.BlockSpec((None, None, None, 1, d), lambda b, i: (layer, b, which, 0, 0))
    return pl.BlockSpec((None, None, None, 1, d), lambda b, i, j: (layer, b, which, 0, 0))


def _norm_mod(x, gain, scale, shift):
    y = x * lax.rsqrt(jnp.mean(x * x, axis=-1, keepdims=True) + NORM_EPS) * gain
    return y * (1.0 + scale) + shift


def _resident(a):
    return pl.BlockSpec(a.shape, lambda bi, i: (0,) * a.ndim, pipeline_mode=pl.Buffered(1))


def _inproj_kernel(x_ref, sh_ref, sc_ref, g_ref, w_ref, o_ref, *, n_col_chunks):
    h = _norm_mod(x_ref[...], g_ref[...], sc_ref[...], sh_ref[...]).astype(BF16)
    tn = w_ref.shape[1] // n_col_chunks
    for c in range(n_col_chunks):
        o_ref[:, c * tn:(c + 1) * tn] = _mm(h, w_ref[:, c * tn:(c + 1) * tn]).astype(o_ref.dtype)


def _inproj(x, mod5, layer, gain, w, tm=512, n_col_chunks=4):
    b, s, d = x.shape
    zc = w.shape[1]
    tm = min(tm, s)
    assert zc % (n_col_chunks * LANES) == 0
    gain = gain.reshape(1, d)
    return pl.pallas_call(
        functools.partial(_inproj_kernel, n_col_chunks=n_col_chunks),
        grid=(b, s // tm),
        in_specs=[pl.BlockSpec((None, tm, d), lambda bi, i: (bi, i, 0)),
                  _mod_spec(mod5, layer, 0, 2),
                  _mod_spec(mod5, layer, 1, 2),
                  _resident(gain), _resident(w)],
        out_specs=pl.BlockSpec((None, tm, zc), lambda bi, i: (bi, i, 0)),
        out_shape=jax.ShapeDtypeStruct((b, s, zc), BF16),
        compiler_params=_cparams(("parallel", "parallel")),
        name="inproj",
    )(x, mod5, mod5, gain, w)


def _bias_kernel(rb_ref, o_ref):
    which = pl.program_id(0)
    h = pl.program_id(1)
    ri = lax.broadcasted_iota(jnp.int32, (TQ, TQ), 0)
    ci = lax.broadcasted_iota(jnp.int32, (TQ, TQ), 1)
    dist = ci - ri + (1 - which) * TQ
    n = jnp.maximum(dist, 0)
    max_exact = N_BUCKETS // 2
    nf = jnp.maximum(n, 1).astype(F32)
    large = max_exact + (jnp.log(nf / max_exact) / math.log(MAX_DISTANCE / max_exact)
                         * (N_BUCKETS - max_exact)).astype(jnp.int32)
    large = jnp.minimum(large, N_BUCKETS - 1)
    bucket = jnp.where(n < max_exact, n, large)
    far = rb_ref[N_BUCKETS - 1, h]
    acc = jnp.zeros((TQ, TQ), F32)
    for bkt in range(N_BUCKETS - 1):
        acc = jnp.where(bucket == bkt, rb_ref[bkt, h] - far, acc)
    o_ref[...] = acc * LOG2E


def _bias_tiles(rel_bias):
    assert TQ >= MAX_DISTANCE
    return pl.pallas_call(
        _bias_kernel,
        grid=(2, HEADS),
        in_specs=[pl.BlockSpec(memory_space=pltpu.SMEM)],
        out_specs=pl.BlockSpec((None, None, TQ, TQ), lambda w, h: (w, h, 0, 0)),
        out_shape=jax.ShapeDtypeStruct((2, HEADS, TQ, TQ), F32),
        compiler_params=_cparams(("parallel", "parallel")),
        name="bias_tiles",
    )(rel_bias)


BISECT_MAX_IT = 300
COARSE_STEPS = 10
FINE_STEPS_UNCHECKED = 4
FINE_STEPS_PER_CHECK = 2
LOG2E = math.log2(math.e)
ONES_ROWS = BF16_ROWS


def _row_groups(t):
    return [t[r * SUBLANES:(r + 1) * SUBLANES, :] for r in range(t.shape[0] // SUBLANES)]


def _dsa_kernel(q_ref, k_ref, v_ref, qi_ref, ki_ref, wi_ref, bias_ref, o_ref,
                sc_ref, scb_ref, qm_ref, qim_ref, wt_ref, vt_ref, m_ref, l_ref, acc_ref, s_ref, tmax_ref,
                *, n_keep, seq):
    j = pl.program_id(1)
    nt = j + 1
    kf = float(n_keep)
    lane = lax.broadcasted_iota(jnp.int32, (TQ, LANES), 1)
    att_scale = HEAD_DIM ** -0.5 * LOG2E
    w_scale = (HEADS ** -0.5) * (HEAD_DIM ** -0.5)

    for h in range(HEADS):
        p, odd = divmod(h, 2)
        hm = (lane >= HEAD_DIM) if odd else (lane < HEAD_DIM)
        qs = q_ref[:, p * LANES:(p + 1) * LANES]
        qm_ref[h] = jnp.where(hm, qs, jnp.zeros_like(qs)) * att_scale
        qis = qi_ref[:, p * LANES:(p + 1) * LANES]
        qim_ref[h] = jnp.where(hm, qis, jnp.zeros_like(qis))
    wt_ref[...] = wi_ref[...].astype(F32).T * w_scale
    v_t = v_ref[...].T
    for h in range(HEADS):
        vt_ref[j, h, 0:HEAD_DIM, :] = v_t[h * HEAD_DIM:(h + 1) * HEAD_DIM, :]
        vt_ref[j, h, HEAD_DIM:HEAD_DIM + ONES_ROWS, :] = jnp.ones((ONES_ROWS, TQ), BF16)

    key = lax.broadcasted_iota(jnp.int32, (TQ, TQ), 0)
    qry = lax.broadcasted_iota(jnp.int32, (TQ, TQ), 1) + j * TQ

    def floor_bf16(x):
        xi = lax.bitcast_convert_type(x, jnp.int32)
        xi = jnp.where(xi < 0, xi + jnp.int32(0xFFFF), xi) & jnp.int32(-0x10000)
        return lax.bitcast_convert_type(xi, F32).astype(BF16)

    def score_tile(kt):
        ki_t = ki_ref[pl.ds(pl.multiple_of(kt * TQ, TQ), TQ), :]
        acc = jnp.zeros((TQ, TQ), F32)
        for h in range(HEADS):
            acc = acc + jnp.maximum(_nt(ki_t, qim_ref[h]), 0.0) * wt_ref[h:h + 1, :]
        sc_ref[kt] = jnp.where(key + kt * TQ <= qry, acc, NEG_INF)

    def stats_tile(kt, c):
        mx, mn, mp, cp, cn = c
        score = sc_ref[kt]
        scb_ref[kt] = floor_bf16(score)
        for g in _row_groups(score):
            pos = g > 0.0
            mx = jnp.maximum(mx, g)
            mn = jnp.minimum(mn, jnp.where(g > 0.5 * NEG_INF, g, -NEG_INF))
            mp = jnp.minimum(mp, jnp.where(pos, g, -NEG_INF))
            cp = cp + jnp.where(pos, 1.0, 0.0)
            cn = cn + jnp.where(g >= 0.0, 1.0, 0.0)
        return mx, mn, mp, cp, cn

    def idx_tile(kt, c):
        c = stats_tile(kt - 1, c)
        score_tile(kt)
        return c

    part = lambda v: jnp.full((SUBLANES, TQ), v, F32)
    score_tile(0)
    stats = lax.fori_loop(1, nt, idx_tile, (part(NEG_INF), part(-NEG_INF), part(-NEG_INF), part(0.0), part(0.0)))
    mx, mn, mp, cp, cn = stats_tile(nt - 1, stats)

    rmax = jnp.max(mx, axis=0, keepdims=True)
    rmin = jnp.min(mn, axis=0, keepdims=True)
    minpos = jnp.min(mp, axis=0, keepdims=True)
    cpos = jnp.sum(cp, axis=0, keepdims=True)
    cnn = jnp.sum(cn, axis=0, keepdims=True)
    nvalid = (lax.broadcasted_iota(jnp.int32, (1, TQ), 1) + j * TQ + 1).astype(F32)
    small = nvalid <= kf

    def tile_loop(body, init):
        n_pairs = lax.shift_right_logical(nt, 1)
        c = lax.fori_loop(0, n_pairs, lambda i, c: body(2 * i + 1, body(2 * i, c)), init)
        return lax.fori_loop(2 * n_pairs, nt, body, c)

    def count_ge(thr):
        def body(kt, acc):
            for g in _row_groups(sc_ref[kt]):
                acc = acc + jnp.where(g >= thr, 1.0, 0.0)
            return acc
        acc = tile_loop(body, jnp.zeros((SUBLANES, TQ), F32))
        return jnp.sum(acc, axis=0, keepdims=True)

    ztie = jnp.logical_and(cpos < kf, cnn >= kf)
    pos_side = cpos >= kf
    hi_top = rmax + jnp.maximum(jnp.abs(rmax) * 1e-6, 1e-30)
    lo0 = jnp.where(small, 0.5 * NEG_INF, jnp.where(ztie, 0.0, jnp.where(pos_side, minpos, rmin)))
    hi0 = jnp.where(small, -NEG_INF, jnp.where(ztie, minpos, jnp.where(pos_side, hi_top, 0.0)))
    clo0 = jnp.where(jnp.logical_or(small, ztie), jnp.where(small, nvalid, cnn), jnp.where(pos_side, cpos, nvalid))
    chi0 = jnp.where(small, 0.0, jnp.where(ztie, cpos, jnp.where(pos_side, 0.0, cnn)))
    done0 = jnp.where(jnp.logical_or(jnp.logical_or(small, ztie), clo0 == kf), 1.0, 0.0)

    def bisect_step(state, mid, cnt, usable):
        lo, hi, clo, chi, done = state
        act = jnp.logical_and(done < 0.5, usable)
        ge = cnt >= kf
        up_lo = jnp.logical_and(act, ge)
        up_hi = jnp.logical_and(act, jnp.logical_not(ge))
        lo = jnp.where(up_lo, mid, lo)
        clo = jnp.where(up_lo, cnt, clo)
        hi = jnp.where(up_hi, mid, hi)
        chi = jnp.where(up_hi, cnt, chi)
        done = jnp.where(jnp.logical_and(act, cnt == kf), 1.0, done)
        return lo, hi, clo, chi, done

    def count_ge_bf16(thr):
        one, zero = jnp.ones((), BF16), jnp.zeros((), BF16)

        def body(kt, acc):
            t = scb_ref[kt]
            part = jnp.zeros((BF16_ROWS, TQ), BF16)
            for r in range(TQ // BF16_ROWS):
                part = part + jnp.where(t[r * BF16_ROWS:(r + 1) * BF16_ROWS, :] >= thr, one, zero)
            return acc + part.astype(F32)
        acc = tile_loop(body, jnp.zeros((BF16_ROWS, TQ), F32))
        return jnp.sum(acc, axis=0, keepdims=True)

    def coarse_step(_, state):
        lo, hi = state[0], state[1]
        mid_b = (lo + 0.5 * (hi - lo)).astype(BF16)
        mid = mid_b.astype(F32)
        usable = jnp.logical_and(mid > lo, mid < hi)
        return bisect_step(state, mid, count_ge_bf16(mid_b), usable)

    state = lax.fori_loop(0, COARSE_STEPS, coarse_step, (lo0, hi0, clo0, chi0, done0))

    def fine_step(_, state):
        lo, hi = state[0], state[1]
        mid = lo + 0.5 * (hi - lo)
        stalled = jnp.logical_or(mid <= lo, mid >= hi)
        state = bisect_step(state, mid, count_ge(mid), jnp.logical_not(stalled))
        return state[:4] + (jnp.where(stalled, 1.0, state[4]),)

    state = lax.fori_loop(0, FINE_STEPS_UNCHECKED, fine_step, state)

    def bis_cond(c):
        return jnp.logical_and(c[0] < BISECT_MAX_IT, c[2] > 0.0)

    def bis_body(c):
        it, state, _ = c
        state = lax.fori_loop(0, FINE_STEPS_PER_CHECK, fine_step, state)
        return it + FINE_STEPS_PER_CHECK, state, jnp.sum(1.0 - state[4])

    _, (lo, hi, clo, chi, _), _ = lax.while_loop(
        bis_cond, bis_body, (jnp.int32(0), state, jnp.sum(1.0 - state[4])))

    tie = jnp.logical_and(clo > kf, jnp.logical_not(small))
    band_quota = jnp.where(tie, kf - chi, float(seq))
    prefix_ones = jnp.where(key >= lax.broadcasted_iota(jnp.int32, (TQ, TQ), 1), 1.0, 0.0).astype(BF16)

    def mask_tiles(kts, before):
        tiles = [sc_ref[kt] for kt in kts]
        bands = [jnp.where(t >= lo, jnp.where(t < hi, 1.0, 0.0), 0.0).astype(BF16) for t in tiles]
        ranks = [_mm(prefix_ones, band) for band in bands]
        masks = []
        for t, rank in zip(tiles, ranks):
            rank = rank + before
            keep_band = jnp.where(rank <= band_quota, 0.0, NEG_INF)
            masks.append(jnp.where(t >= lo, jnp.where(t >= hi, 0.0, keep_band), NEG_INF))
            before = rank[TQ - 1:TQ, :]
        for kt, mask in zip(kts, masks):
            sc_ref[kt] = mask
        return before

    n_pairs = lax.shift_right_logical(nt, 1)
    before = lax.fori_loop(0, n_pairs, lambda i, c: mask_tiles([2 * i, 2 * i + 1], c), jnp.zeros((1, TQ), F32))
    lax.fori_loop(2 * n_pairs, nt, lambda kt, c: mask_tiles([kt], c), before)

    m_ref[...] = jnp.full(m_ref.shape, NEG_INF, F32)
    l_ref[...] = jnp.zeros(l_ref.shape, F32)
    acc_ref[...] = jnp.zeros(acc_ref.shape, F32)

    def logits_phase(kts, near, buf):
        tile_max = [None] * HEADS
        for i, kt in enumerate(kts):
            rows = pl.ds(pl.multiple_of(kt * TQ, TQ), TQ)
            mask_add = sc_ref[kt]
            for h in range(HEADS):
                p = h // 2
                s = _nt(k_ref[rows, p * LANES:(p + 1) * LANES], qm_ref[h]) + mask_add
                if near:
                    s = s + bias_ref[kt - j + 1, h]
                s_ref[buf, i, h] = s
                mx = jnp.max(s, axis=0, keepdims=True)
                tile_max[h] = mx if i == 0 else jnp.maximum(tile_max[h], mx)
        for h in range(HEADS):
            tmax_ref[buf, h] = tile_max[h]

    def softmax_phase(kts, buf):
        for h in range(HEADS):
            m_old = m_ref[h]
            m_new = jnp.maximum(m_old, tmax_ref[buf, h])
            alpha = jnp.exp2(m_old - m_new)
            pv = None
            for i, kt in enumerate(kts):
                part = _mm(vt_ref[kt, h], jnp.exp2(s_ref[buf, i, h] - m_new).astype(BF16))
                pv = part if i == 0 else pv + part
            m_ref[h] = m_new
            l_ref[h] = alpha * l_ref[h] + pv[HEAD_DIM:HEAD_DIM + 1, :]
            acc_ref[h] = alpha * acc_ref[h] + pv[0:HEAD_DIM, :]

    n_far = jnp.maximum(j - 1, 0)

    def attn_tiles(kts, near):
        logits_phase(kts, near, 0)
        softmax_phase(kts, 0)

    n_far_pairs = lax.shift_right_logical(n_far, 1)

    def far_pair(i, carry):
        attn_tiles([2 * i, 2 * i + 1], near=False)
        return carry

    def far_single(kt, carry):
        attn_tiles([kt], near=False)
        return carry

    lax.fori_loop(0, n_far_pairs, far_pair, 0)
    lax.fori_loop(2 * n_far_pairs, n_far, far_single, 0)

    @pl.when(j >= 1)
    def _():
        attn_tiles([j - 1, j], near=True)

    @pl.when(j == 0)
    def _():
        attn_tiles([j], near=True)

    for p in range(HEADS // 2):
        pair = jnp.concatenate([acc_ref[2 * p] / l_ref[2 * p], acc_ref[2 * p + 1] / l_ref[2 * p + 1]], axis=0)
        o_ref[:, p * LANES:(p + 1) * LANES] = pair.T.astype(o_ref.dtype)


def _dsa(z, bias, d_model):
    b, s, _ = z.shape
    n_keep = min(TOPK_MAX, s // 4)
    kern = functools.partial(_dsa_kernel, n_keep=n_keep, seq=s)
    ki_blk, wi_blk = _z_small_block(d_model, 0), _z_small_block(d_model, 1)
    return pl.pallas_call(
        kern,
        grid=(b, s // TQ),
        in_specs=[pl.BlockSpec((None, TQ, MIX_W), lambda bi, j: (bi, j, Z_Q)),
                  pl.BlockSpec((None, s, MIX_W), lambda bi, j: (bi, 0, Z_K)),
                  pl.BlockSpec((None, TQ, MIX_W), lambda bi, j: (bi, j, Z_V)),
                  pl.BlockSpec((None, TQ, MIX_W), lambda bi, j: (bi, j, Z_QI)),
                  pl.BlockSpec((None, s, LANES), lambda bi, j: (bi, 0, ki_blk)),
                  pl.BlockSpec((None, TQ, LANES), lambda bi, j: (bi, j, wi_blk)),
                  pl.BlockSpec((2, HEADS, TQ, TQ), lambda bi, j: (0, 0, 0, 0))],
        out_specs=pl.BlockSpec((None, TQ, MIX_W), lambda bi, j: (bi, j, 0)),
        out_shape=jax.ShapeDtypeStruct((b, s, MIX_W), BF16),
        scratch_shapes=[pltpu.VMEM((s // TQ, TQ, TQ), F32),
                        pltpu.VMEM((s // TQ, TQ, TQ), BF16),
                        pltpu.VMEM((HEADS, TQ, LANES), BF16),
                        pltpu.VMEM((HEADS, TQ, LANES), BF16),
                        pltpu.VMEM((LANES, TQ), F32),
                        pltpu.VMEM((s // TQ, HEADS, HEAD_DIM + ONES_ROWS, TQ), BF16),
                        pltpu.VMEM((HEADS, 1, TQ), F32),
                        pltpu.VMEM((HEADS, 1, TQ), F32),
                        pltpu.VMEM((HEADS, HEAD_DIM, TQ), F32),
                        pltpu.VMEM((1, 2, HEADS, TQ, TQ), F32),
                        pltpu.VMEM((1, HEADS, 1, TQ), F32)],
        compiler_params=_cparams(("parallel", "arbitrary")),
        name="dsa",
    )(z, z, z, z, z, z, bias)


def _halo_spec(width, blk, tm, rows):
    step = tm // rows
    return pl.BlockSpec((None, rows, width), lambda bi, i: (bi, jnp.maximum(i * step - 1, 0), blk))


def _shift_lerp(cur_ref, halo_ref, mu, first):
    cur = cur_ref[...].astype(F32)
    nh = halo_ref.shape[0]
    prev_last = jnp.where(first, 0.0, halo_ref[nh - 1:nh, :].astype(F32))
    rolled = pltpu.roll(cur, 1, axis=0)
    rowid = lax.broadcasted_iota(jnp.int32, cur.shape, 0)
    sh = jnp.where(rowid == 0, prev_last, rolled)
    return cur + (sh - cur) * mu


def _head_sum(x, bq_ref):
    xb = x.astype(BF16)
    return jnp.concatenate([_mm(xb[:, q * QUAD:(q + 1) * QUAD], bq_ref[...]) for q in range(MIX_W // QUAD)], axis=1)


def _rwkv_kernel(r_ref, k_ref, v_ref, s2_ref, s3_ref, rh_ref, kh_ref, vh_ref, s2h_ref, s3h_ref,
                 mu_r, mu_k, mu_v, mu_s2, mu_s3, w0_ref, wup_ref, a0_ref, aup_ref, gup_ref,
                 kk_ref, ka_ref, rk_ref, bq_ref, tri_ref,
                 y_ref, g_ref, bonus_ref,
                 st_ref, rs, ls, ks, vs, kks, kbs, s2s, s3s):
    first = pl.program_id(1) == 0

    @pl.when(first)
    def _():
        st_ref[...] = jnp.zeros_like(st_ref)

    rs[...] = _shift_lerp(r_ref, rh_ref, mu_r[...], first)
    ks[...] = _shift_lerp(k_ref, kh_ref, mu_k[...], first)
    vs[...] = _shift_lerp(v_ref, vh_ref, mu_v[...], first)
    s2s[...] = _shift_lerp(s2_ref, s2h_ref, mu_s2[...], first)
    s3s[...] = _shift_lerp(s3_ref, s3h_ref, mu_s3[...], first)

    def prepare(pair):
        rows = slice(pair * 2 * CHUNK, (pair + 1) * 2 * CHUNK)
        r, k, v, s2, s3 = rs[rows, :], ks[rows, :], vs[rows, :], s2s[rows, :], s3s[rows, :]
        xw = w0_ref[...] + _bmm(jnp.tanh(s2), wup_ref[...])
        softplus = jnp.maximum(-xw, 0.0) + jnp.log(1.0 + jnp.exp(-jnp.abs(xw)))
        ld = -jnp.exp(-softplus - 0.5)
        af = jax.nn.sigmoid(a0_ref[...] + _bmm(s2, aup_ref[...]))
        g_ref[rows, :] = _bmm(jax.nn.sigmoid(s3), gup_ref[...]).astype(g_ref.dtype)
        kkr = k * kk_ref[...]
        kkn = kkr / jnp.maximum(jnp.sqrt(_head_sum(kkr * kkr, bq_ref)), 1e-12)
        kmod = k * (1.0 + (af - 1.0) * ka_ref[...])
        bonus_ref[rows, :] = (_head_sum(r * kmod * rk_ref[...], bq_ref) * v).astype(bonus_ref.dtype)
        ls[rows, :] = ld
        ks[rows, :] = kmod
        kks[rows, :] = kkn
        kbs[rows, :] = kkn * af

    ri = lax.broadcasted_iota(jnp.int32, (QUAD, QUAD), 0)
    ci = lax.broadcasted_iota(jnp.int32, (QUAD, QUAD), 1)
    same_head = (ri // HEAD_DIM) == (ci // HEAD_DIM)
    strict = jnp.logical_and(same_head, (ri % CHUNK) > (ci % CHUNK))
    incl = jnp.logical_and(same_head, (ri % CHUNK) >= (ci % CHUNK))
    eye = jnp.where(ri == ci, 1.0, 0.0)

    def stack(x):
        return jnp.where(same_head, jnp.concatenate([x] * 4, axis=0), 0.0).astype(BF16)

    def tile4(x):
        return jnp.concatenate([x] * 4, axis=0).astype(BF16)

    def unstack(x):
        return (x[0:CHUNK] + x[CHUNK:2 * CHUNK]) + (x[2 * CHUNK:3 * CHUNK] + x[3 * CHUNK:4 * CHUNK])

    def chunk_operands(c):
        rows = slice(c * CHUNK, (c + 1) * CHUNK)
        ldc = ls[rows, :]
        p1 = ldc.astype(BF16)
        e1 = ldc - p1.astype(F32)
        p2 = e1.astype(BF16)
        p3 = (e1 - p2.astype(F32)).astype(BF16)
        tri = tri_ref[...]
        cl = (_mm(tri, p1) + _mm(tri, p2)) + _mm(tri, p3)
        cl_end = cl[CHUNK - 1:CHUNK, :]
        e_in = jnp.exp(cl)
        e_out = jnp.exp(-cl)
        e_end = jnp.exp(cl_end - cl)
        rt_all = rs[rows, :] * e_in
        at_all = -kks[rows, :] * jnp.exp(cl - ldc)
        bt_all = kbs[rows, :] * e_out
        kt_all = ks[rows, :] * e_out
        bg_all = (kbs[rows, :] * e_end).astype(BF16)
        kg_all = (ks[rows, :] * e_end).astype(BF16)
        v_all = vs[rows, :]
        gam_all = jnp.exp(cl_end)
        units = []
        for q in range(MIX_W // QUAD):
            sl = slice(q * QUAD, (q + 1) * QUAD)
            units.append(dict(rows=rows, sl=sl, q=q, rt=rt_all[:, sl], vv=v_all[:, sl].astype(BF16),
                              a4=stack(at_all[:, sl]), r4=stack(rt_all[:, sl]), v4=stack(v_all[:, sl]),
                              bt4=tile4(bt_all[:, sl]), kt4=tile4(kt_all[:, sl]),
                              bg=bg_all[:, sl], kg=kg_all[:, sl], gam=gam_all[:, sl]))
        return units

    n_doublings = int(math.log2(CHUNK)) - 1

    def chunk_pair(cp):
        us = chunk_operands(2 * cp) + chunk_operands(2 * cp + 1)
        for u in us:
            u["m_ab"] = jnp.where(strict, _nt(u["a4"], u["bt4"]), 0.0)
        for u in us:
            u["m_ak"] = jnp.where(strict, _nt(u["a4"], u["kt4"]), 0.0)
        for u in us:
            u["m_rb"] = jnp.where(incl, _nt(u["r4"], u["bt4"]), 0.0).astype(BF16)
        for u in us:
            u["m_rk"] = jnp.where(incl, _nt(u["r4"], u["kt4"]), 0.0).astype(BF16)
        for u in us:
            u["pw"] = u["m_ab"]
            u["inv"] = eye + u["m_ab"]
        for _ in range(n_doublings):
            for u in us:
                u["pw"] = _bmm(u["pw"], u["pw"])
            for u in us:
                u["inv"] = u["inv"] + _bmm(u["inv"], u["pw"])
        for u in us:
            u["inv"] = u["inv"].astype(BF16)
            u["mv4"] = _bmm(u["m_ak"], u["v4"])
        for u in us:
            u["ah4"] = _mm(u["inv"], u["a4"])
        for u in us:
            u["uh4"] = _bmm(u["inv"], u["mv4"])
        for u in us:
            u["ry"] = (u["rt"] + unstack(_bmm(u["m_rb"], u["ah4"]))).astype(BF16)
        for u in us:
            u["y0"] = unstack(_bmm(u["m_rb"], u["uh4"]) + _mm(u["m_rk"], u["v4"]))
        for u in us:
            ah, uh = unstack(u["ah4"]).astype(BF16), unstack(u["uh4"]).astype(BF16)
            u["g_low"] = jnp.where(same_head, _tn(u["bg"], ah), 0.0).astype(BF16)
            u["h_t"] = jnp.where(same_head, _tn(uh, u["bg"]) + _tn(u["vv"], u["kg"]), 0.0)
        for u in us:
            st = st_ref[u["q"]]
            stb = st.astype(BF16)
            y_ref[u["rows"], u["sl"]] = _nt(u["ry"], stb) + u["y0"]
            st_ref[u["q"]] = st * u["gam"] + _nt(stb, u["g_low"]) + u["h_t"]

    n_pairs = r_ref.shape[0] // (2 * CHUNK)
    prepare(0)
    for pair in range(n_pairs):
        if pair + 1 < n_pairs:
            prepare(pair + 1)
        chunk_pair(pair)


def _rwkv(z, prm, d_model):
    b, s, _ = z.shape
    tm = min(RW_TM, s)
    s2_blk, s3_blk = _z_small_block(d_model, 2), _z_small_block(d_model, 3)
    tok = lambda blk: pl.BlockSpec((None, tm, MIX_W), lambda bi, i: (bi, i, blk))
    tok128 = lambda blk: pl.BlockSpec((None, tm, LANES), lambda bi, i: (bi, i, blk))
    const = lambda a: pl.BlockSpec(a.shape, lambda bi, i: (0,) * a.ndim)
    consts = [prm[n] for n in ("mu_r", "mu_k", "mu_v", "mu_s2", "mu_s3", "w0", "w_up", "a0", "a_up", "g_up",
                               "k_k", "k_a", "r_k", "bones_q", "tri")]
    out_spec = pl.BlockSpec((None, tm, MIX_W), lambda bi, i: (bi, i, 0))
    return pl.pallas_call(
        _rwkv_kernel,
        grid=(b, s // tm),
        in_specs=[tok(Z_RR), tok(Z_RK), tok(Z_RV), tok128(s2_blk), tok128(s3_blk),
                  _halo_spec(MIX_W, Z_RR, tm, BF16_ROWS), _halo_spec(MIX_W, Z_RK, tm, BF16_ROWS),
                  _halo_spec(MIX_W, Z_RV, tm, BF16_ROWS), _halo_spec(LANES, s2_blk, tm, BF16_ROWS),
                  _halo_spec(LANES, s3_blk, tm, BF16_ROWS)] + [const(a) for a in consts],
        out_specs=[out_spec, out_spec, out_spec],
        out_shape=[jax.ShapeDtypeStruct((b, s, MIX_W), F32), jax.ShapeDtypeStruct((b, s, MIX_W), BF16),
                   jax.ShapeDtypeStruct((b, s, MIX_W), BF16)],
        scratch_shapes=([pltpu.VMEM((MIX_W // QUAD, QUAD, QUAD), F32)] + [pltpu.VMEM((tm, MIX_W), F32)] * 6
                        + [pltpu.VMEM((tm, LANES), F32)] * 2),
        compiler_params=_cparams(("parallel", "arbitrary")),
        name="rwkv",
    )(z, z, z, z, z, z, z, z, z, z, *consts)


def _causal_conv(p, halo, cw_ref):
    nh = halo.shape[0]
    rowid = lax.broadcasted_iota(jnp.int32, (SUBLANES, p.shape[1]), 0)
    conv = p * cw_ref[CONV_W - 1:CONV_W, :]
    for back in range(1, CONV_W):
        rolled = pltpu.roll(p, back, axis=0)
        head = rolled[0:SUBLANES]
        for rr in range(back):
            head = jnp.where(rowid == rr, halo[nh - back + rr:nh - back + rr + 1, :], head)
        rolled = jnp.concatenate([head, rolled[SUBLANES:]], axis=0)
        conv = conv + rolled * cw_ref[CONV_W - 1 - back:CONV_W - back, :]
    return conv


def _merge_kernel(x_ref, oa_ref, y_ref, g_ref, bonus_ref, cb_ref, cc_ref, cx_ref, cch_ref, cxh_ref,
                  gate0_ref, gate1_ref, gate2_ref, g1_ref, lnw_ref, lnb_ref, bq_ref, cw_ref, wb_ref, wo_ref, o_ref,
                  merged_ref):
    first = pl.program_id(1) == 0
    y = y_ref[...]
    inv_n = 1.0 / HEAD_DIM
    p1 = y.astype(BF16)
    mean = (_head_sum(p1, bq_ref) + _head_sum(y - p1.astype(F32), bq_ref)) * inv_n
    yc = y - mean
    var = _head_sum(yc * yc, bq_ref) * inv_n
    o_rwkv = ((yc * lax.rsqrt(var + GN_EPS) * lnw_ref[...] + lnb_ref[...] + bonus_ref[...].astype(F32))
              * g_ref[...].astype(F32))
    p = cc_ref[...].astype(F32) * cx_ref[...].astype(F32)
    ph = jnp.where(first, 0.0, cch_ref[...].astype(F32) * cxh_ref[...].astype(F32))
    o_conv = cb_ref[...].astype(F32) * _causal_conv(p, ph, cw_ref)
    branches = ((oa_ref[...], gate0_ref), (o_rwkv.astype(BF16), gate1_ref), (o_conv.astype(BF16), gate2_ref))
    d = x_ref.shape[-1]
    for c in range(d // QUAD):
        cols = slice(c * QUAD, (c + 1) * QUAD)
        part = jnp.zeros((x_ref.shape[0], QUAD), F32)
        for bi, (o, gate_ref) in enumerate(branches):
            part = part + jax.nn.sigmoid(gate_ref[:, cols].astype(F32)) * _mm(o, wb_ref[bi, :, cols])
        merged_ref[:, cols] = part.astype(BF16)
    o_ref[...] = x_ref[...] + g1_ref[...] * _mm(merged_ref[...], wo_ref[...])


def _merge(x, o_attn, y, g, bonus, z, mod5, layer, prm, tm=256):
    b, s, d = x.shape
    tm = min(tm, s)
    tok = lambda w, blk: pl.BlockSpec((None, tm, w), lambda bi, i: (bi, i, blk))
    const = lambda a: pl.BlockSpec(a.shape, lambda bi, i: (0,) * a.ndim)
    consts = [prm[n] for n in ("ln_w", "ln_b", "bones_q", "conv_w", "w_branch", "w_o")]
    gate_blk = Z_GATE_COL // d
    return pl.pallas_call(
        _merge_kernel,
        grid=(b, s // tm),
        in_specs=[tok(d, 0), tok(MIX_W, 0), tok(MIX_W, 0), tok(MIX_W, 0), tok(MIX_W, 0),
                  tok(MIX_W, Z_CB), tok(MIX_W, Z_CC), tok(MIX_W, Z_CX),
                  _halo_spec(MIX_W, Z_CC, tm, BF16_ROWS), _halo_spec(MIX_W, Z_CX, tm, BF16_ROWS),
                  tok(d, gate_blk), tok(d, gate_blk + 1), tok(d, gate_blk + 2),
                  _mod_spec(mod5, layer, 2, 2)] + [const(a) for a in consts],
        out_specs=tok(d, 0),
        out_shape=jax.ShapeDtypeStruct((b, s, d), F32),
        scratch_shapes=[pltpu.VMEM((tm, d), BF16)],
        compiler_params=_cparams(("parallel", "parallel")),
        name="merge",
    )(x, o_attn, y, g, bonus, z, z, z, z, z, z, z, z, mod5, *consts)


def _ffn_kernel(x_ref, xh_ref, sh_ref, sc_ref, g2_ref, gain_ref, wup_ref, cw_ref, wd_ref, fin_ref, o_ref,
                *, final_norm, tf):
    first = pl.program_id(1) == 0
    x = x_ref[...]
    h = _norm_mod(x, gain_ref[...], sc_ref[...], sh_ref[...]).astype(BF16)
    hh = _norm_mod(xh_ref[...], gain_ref[...], sc_ref[...], sh_ref[...]).astype(BF16)
    h_ext = jnp.concatenate([hh, h], axis=0)
    d_ff = wd_ref.shape[0]
    acc = jnp.zeros(x.shape, F32)
    for jf in range(d_ff // tf):
        cols = slice(jf * tf, (jf + 1) * tf)
        gate_cols = slice(d_ff + jf * tf, d_ff + (jf + 1) * tf)
        a_ext = _mm(h_ext, wup_ref[:, cols])
        a = a_ext[BF16_ROWS:]
        ah = jnp.where(first, 0.0, a_ext[BF16_ROWS - SUBLANES:BF16_ROWS])
        conv = _causal_conv(a, ah, cw_ref[:, cols])
        u = conv * jax.nn.sigmoid(conv) * _mm(h, wup_ref[:, gate_cols])
        acc = acc + _mm(u.astype(BF16), wd_ref[cols, :])
    out = x + g2_ref[...] * acc
    if final_norm:
        out = out * lax.rsqrt(jnp.mean(out * out, axis=-1, keepdims=True) + NORM_EPS) * fin_ref[...]
    o_ref[...] = out


def _ffn(x, mod5, layer, gain, w_up, conv_w, w_down, final_gain, final_norm, tm=512, tf=1408):
    b, s, d = x.shape
    d_ff = w_down.shape[0]
    assert d_ff % tf == 0 and tf % LANES == 0
    tm = min(tm, s)
    gain, final_gain = gain.reshape(1, d), final_gain.reshape(1, d)
    return pl.pallas_call(
        functools.partial(_ffn_kernel, final_norm=final_norm, tf=tf),
        grid=(b, s // tm),
        in_specs=[pl.BlockSpec((None, tm, d), lambda bi, i: (bi, i, 0)),
                  _halo_spec(d, 0, tm, BF16_ROWS),
                  _mod_spec(mod5, layer, 3, 2), _mod_spec(mod5, layer, 4, 2), _mod_spec(mod5, layer, 5, 2),
                  _resident(gain), _resident(w_up), _resident(conv_w), _resident(w_down), _resident(final_gain)],
        out_specs=pl.BlockSpec((None, tm, d), lambda bi, i: (bi, i, 0)),
        out_shape=jax.ShapeDtypeStruct((b, s, d), F32),
        compiler_params=_cparams(("parallel", "parallel")),
        name="ffn",
    )(x, x, mod5, mod5, mod5, gain, w_up, conv_w, w_down, final_gain)


def _split_w_in(w):
    sizes = (MIX_W, MIX_W, MIX_W, HEADS * HEAD_DIM, HEAD_DIM, HEADS,
             MIX_W, MIX_W, MIX_W, LORA_DECAY, LORA_ICLR, LORA_GATE,
             MIX_W, MIX_W, MIX_W)
    out, o = [], 0
    for n in sizes:
        out.append(w[:, o:o + n])
        o += n
    out.append(w[:, o:])
    return out


def _layer_params(l, w_in, rwkv_mu, rwkv_w0, rwkv_w_up, rwkv_a0, rwkv_a_up, rwkv_g_up, rwkv_k_k, rwkv_k_a,
                  rwkv_r_k, rwkv_ln_w, rwkv_ln_b, sc_conv_w, w_branch, w_o):
    d = w_in.shape[1]
    q, k, v, qi, ki, wi, rr, rk, rv, wd, ad, gd, cb, cc, cx, gates = _split_w_in(w_in[l])
    assert Z_GATE_COL % d == 0 and gates.shape[1] == 3 * d
    w_z = jnp.concatenate([q, k, v, qi, rr, rk, rv, cb, cc, cx, gates,
                           ki, ki, wi, jnp.zeros((d, LANES - HEADS), F32), wd, ad, gd], axis=1).astype(BF16)
    mu = rwkv_mu[l]
    row = lambda a: a.reshape(1, -1)
    head_id = jnp.arange(QUAD) // HEAD_DIM
    bones_q = (head_id[:, None] == head_id[None, :]).astype(BF16)
    tri = (jnp.arange(CHUNK)[:, None] >= jnp.arange(CHUNK)[None, :]).astype(BF16)
    zl = lambda n: jnp.zeros((n, MIX_W), F32)
    return dict(
        w_z=w_z,
        mu_r=row(mu[:MIX_W]), mu_k=row(mu[MIX_W:2 * MIX_W]), mu_v=row(mu[2 * MIX_W:3 * MIX_W]),
        mu_s2=row(mu[3 * MIX_W:3 * MIX_W + LORA_DECAY + LORA_ICLR]), mu_s3=row(mu[3 * MIX_W + LORA_DECAY + LORA_ICLR:]),
        w0=row(rwkv_w0[l]), w_up=jnp.concatenate([rwkv_w_up[l], zl(LORA_ICLR)], axis=0).astype(BF16),
        a0=row(rwkv_a0[l]), a_up=jnp.concatenate([zl(LORA_DECAY), rwkv_a_up[l]], axis=0).astype(BF16),
        g_up=rwkv_g_up[l].astype(BF16), k_k=row(rwkv_k_k[l]), k_a=row(rwkv_k_a[l]), r_k=row(rwkv_r_k[l]),
        bones_q=bones_q, tri=tri,
        ln_w=row(rwkv_ln_w[l]), ln_b=row(rwkv_ln_b[l]), conv_w=sc_conv_w[l].T,
        w_branch=w_branch[l].astype(BF16), w_o=w_o[l].astype(BF16),
    )


def kernel(x, c, positions, rel_bias, final_norm, ada_w, ada_b, norm_mix, w_in, rwkv_mu, rwkv_w0, rwkv_w_up,
           rwkv_a0, rwkv_a_up, rwkv_g_up, rwkv_k_k, rwkv_k_a, rwkv_r_k, rwkv_ln_w, rwkv_ln_b, sc_conv_w,
           w_branch, w_o, norm_ffn, ffn_w_up, ffn_conv_w, ffn_w_down):
    depth, d = ada_w.shape[0], x.shape[-1]
    mod5 = _ada_mod(c, ada_w, ada_b)
    bias = _bias_tiles(rel_bias)
    for l in range(depth):
        prm = _layer_params(l, w_in, rwkv_mu, rwkv_w0, rwkv_w_up, rwkv_a0, rwkv_a_up, rwkv_g_up, rwkv_k_k,
                            rwkv_k_a, rwkv_r_k, rwkv_ln_w, rwkv_ln_b, sc_conv_w, w_branch, w_o)
        z = _inproj(x, mod5, l, norm_mix[l], prm["w_z"])
        o_attn = _dsa(z, bias, d)
        y, g, bonus = _rwkv(z, prm, d)
        x = _merge(x, o_attn, y, g, bonus, z, mod5, l, prm)
        x = _ffn(x, mod5, l, norm_ffn[l], ffn_w_up[l].astype(BF16), ffn_conv_w[l].T, ffn_w_down[l].astype(BF16),
                 final_norm, final_norm=(l == depth - 1))
    return x
```

```python
import functools
import math

import jax
import jax.numpy as jnp
from jax import lax
from jax.experimental import pallas as pl
from jax.experimental.pallas import tpu as pltpu

F32 = jnp.float32
BF16 = jnp.bfloat16
HIGHEST = lax.Precision.HIGHEST

MIX_W = 512
HEADS = 8
HEAD_DIM = 64
TOPK_MAX = 256
N_BUCKETS = 32
MAX_DISTANCE = 128
LORA_DECAY = 64
LORA_ICLR = 64
LORA_GATE = 128
CONV_W = 3
NORM_EPS = 1e-6
GN_EPS = 64e-5
NEG_INF = -1e30

LANES = 128
SUBLANES = 8
BF16_ROWS = 16
VMEM_LIMIT = 56 * 1024 * 1024

TQ = 256
CHUNK = 64
RW_TM = 512
QUAD = 4 * HEAD_DIM
assert CHUNK == HEAD_DIM

Z_Q, Z_K, Z_V, Z_QI, Z_RR, Z_RK, Z_RV, Z_CB, Z_CC, Z_CX = range(10)
Z_GATE_COL = 10 * MIX_W


def _z_small_block(d_model, i):
    return (Z_GATE_COL + 3 * d_model) // LANES + i


def _cparams(sem):
    return pltpu.CompilerParams(dimension_semantics=sem, vmem_limit_bytes=VMEM_LIMIT)


def _nt(a, b, precision=None):
    return lax.dot_general(a, b, (((1,), (1,)), ((), ())), precision=precision,
                           preferred_element_type=F32)


def _tn(a, b, precision=None):
    return lax.dot_general(a, b, (((0,), (0,)), ((), ())), precision=precision,
                           preferred_element_type=F32)


def _mm(a, b, precision=None):
    return jnp.dot(a, b, precision=precision, preferred_element_type=F32)


def _bmm(a, b):
    return _mm(a.astype(BF16), b.astype(BF16))


def _mod_kernel(c_ref, w_ref, b_ref, o_ref):
    o_ref[...] = _mm(c_ref[...], w_ref[...], HIGHEST) + b_ref[...]


def _ada_mod(c, ada_w, ada_b):
    depth, d, d6 = ada_w.shape
    b = c.shape[0]
    out = pl.pallas_call(
        _mod_kernel,
        grid=(depth, d6 // d),
        in_specs=[pl.BlockSpec((b, d), lambda l, j: (0, 0)),
                  pl.BlockSpec((None, d, d), lambda l, j: (l, 0, j)),
                  pl.BlockSpec((None, 1, d), lambda l, j: (l, 0, j))],
        out_specs=pl.BlockSpec((None, b, d), lambda l, j: (l, 0, j)),
        out_shape=jax.ShapeDtypeStruct((depth, b, d6), F32),
        compiler_params=_cparams(("parallel", "parallel")),
        name="ada_mod",
    )(c, ada_w, ada_b.reshape(depth, 1, d6))
    return out.reshape(depth, b, d6 // d, 1, d)


def _mod_spec(mod5, layer, which, ngrid):
    d = mod5.shape[-1]
    if ngrid == 2:
        return pl.BlockSpec((None, None, None, 1, d), lambda b, i: (layer, b, which, 0, 0))
    return pl.BlockSpec((None, None, None, 1, d), lambda b, i, j: (layer, b, which, 0, 0))


def _norm_mod(x, gain, scale, shift):
    y = x * lax.rsqrt(jnp.mean(x * x, axis=-1, keepdims=True) + NORM_EPS) * gain
    return y * (1.0 + scale) + shift


def _resident(a):
    return pl.BlockSpec(a.shape, lambda bi, i: (0,) * a.ndim, pipeline_mode=pl.Buffered(1))


def _inproj_kernel(x_ref, sh_ref, sc_ref, g_ref, w_ref, o_ref, *, n_col_chunks):
    h = _norm_mod(x_ref[...], g_ref[...], sc_ref[...], sh_ref[...]).astype(BF16)
    tn = w_ref.shape[1] // n_col_chunks
    for c in range(n_col_chunks):
        o_ref[:, c * tn:(c + 1) * tn] = _mm(h, w_ref[:, c * tn:(c + 1) * tn]).astype(o_ref.dtype)


def _inproj(x, mod5, layer, gain, w, tm=512, n_col_chunks=4):
    b, s, d = x.shape
    zc = w.shape[1]
    tm = min(tm, s)
    assert zc % (n_col_chunks * LANES) == 0
    gain = gain.reshape(1, d)
    return pl.pallas_call(
        functools.partial(_inproj_kernel, n_col_chunks=n_col_chunks),
        grid=(b, s // tm),
        in_specs=[pl.BlockSpec((None, tm, d), lambda bi, i: (bi, i, 0)),
                  _mod_spec(mod5, layer, 0, 2),
                  _mod_spec(mod5, layer, 1, 2),
                  _resident(gain), _resident(w)],
        out_specs=pl.BlockSpec((None, tm, zc), lambda bi, i: (bi, i, 0)),
        out_shape=jax.ShapeDtypeStruct((b, s, zc), BF16),
        compiler_params=_cparams(("parallel", "parallel")),
        name="inproj",
    )(x, mod5, mod5, gain, w)


def _bias_kernel(rb_ref, o_ref):
    which = pl.program_id(0)
    h = pl.program_id(1)
    ri = lax.broadcasted_iota(jnp.int32, (TQ, TQ), 0)
    ci = lax.broadcasted_iota(jnp.int32, (TQ, TQ), 1)
    dist = ci - ri + (1 - which) * TQ
    n = jnp.maximum(dist, 0)
    max_exact = N_BUCKETS // 2
    nf = jnp.maximum(n, 1).astype(F32)
    large = max_exact + (jnp.log(nf / max_exact) / math.log(MAX_DISTANCE / max_exact)
                         * (N_BUCKETS - max_exact)).astype(jnp.int32)
    large = jnp.minimum(large, N_BUCKETS - 1)
    bucket = jnp.where(n < max_exact, n, large)
    far = rb_ref[N_BUCKETS - 1, h]
    acc = jnp.zeros((TQ, TQ), F32)
    for bkt in range(N_BUCKETS - 1):
        acc = jnp.where(bucket == bkt, rb_ref[bkt, h] - far, acc)
    o_ref[...] = acc * LOG2E


def _bias_tiles(rel_bias):
    assert TQ >= MAX_DISTANCE
    return pl.pallas_call(
        _bias_kernel,
        grid=(2, HEADS),
        in_specs=[pl.BlockSpec(memory_space=pltpu.SMEM)],
        out_specs=pl.BlockSpec((None, None, TQ, TQ), lambda w, h: (w, h, 0, 0)),
        out_shape=jax.ShapeDtypeStruct((2, HEADS, TQ, TQ), F32),
        compiler_params=_cparams(("parallel", "parallel")),
        name="bias_tiles",
    )(rel_bias)


BISECT_MAX_IT = 300
COARSE_STEPS = 10
FINE_STEPS_UNCHECKED = 4
FINE_STEPS_PER_CHECK = 2
LOG2E = math.log2(math.e)
ONES_ROWS = BF16_ROWS


def _row_groups(t):
    return [t[r * SUBLANES:(r + 1) * SUBLANES, :] for r in range(t.shape[0] // SUBLANES)]


def _dsa_kernel(q_ref, k_ref, v_ref, qi_ref, ki_ref, wi_ref, bias_ref, o_ref,
                sc_ref, scb_ref, qm_ref, qim_ref, wt_ref, vt_ref, m_ref, l_ref, acc_ref, s_ref, tmax_ref,
                *, n_keep, seq):
    j = pl.program_id(1)
    nt = j + 1
    kf = float(n_keep)
    lane = lax.broadcasted_iota(jnp.int32, (TQ, LANES), 1)
    att_scale = HEAD_DIM ** -0.5 * LOG2E
    w_scale = (HEADS ** -0.5) * (HEAD_DIM ** -0.5)

    for h in range(HEADS):
        p, odd = divmod(h, 2)
        hm = (lane >= HEAD_DIM) if odd else (lane < HEAD_DIM)
        qs = q_ref[:, p * LANES:(p + 1) * LANES]
        qm_ref[h] = jnp.where(hm, qs, jnp.zeros_like(qs)) * att_scale
        qis = qi_ref[:, p * LANES:(p + 1) * LANES]
        qim_ref[h] = jnp.where(hm, qis, jnp.zeros_like(qis))
    wt_ref[...] = wi_ref[...].astype(F32).T * w_scale
    v_t = v_ref[...].T
    for h in range(HEADS):
        vt_ref[j, h, 0:HEAD_DIM, :] = v_t[h * HEAD_DIM:(h + 1) * HEAD_DIM, :]
        vt_ref[j, h, HEAD_DIM:HEAD_DIM + ONES_ROWS, :] = jnp.ones((ONES_ROWS, TQ), BF16)

    key = lax.broadcasted_iota(jnp.int32, (TQ, TQ), 0)
    qry = lax.broadcasted_iota(jnp.int32, (TQ, TQ), 1) + j * TQ

    def floor_bf16(x):
        xi = lax.bitcast_convert_type(x, jnp.int32)
        xi = jnp.where(xi < 0, xi + jnp.int32(0xFFFF), xi) & jnp.int32(-0x10000)
        return lax.bitcast_convert_type(xi, F32).astype(BF16)

    def score_tiles(kt0, n):
        ki_t = ki_ref[pl.ds(pl.multiple_of(kt0 * TQ, TQ), n * TQ), :]
        acc = jnp.zeros((n * TQ, TQ), F32)
        for h in range(HEADS):
            acc = acc + jnp.maximum(_nt(ki_t, qim_ref[h]), 0.0) * wt_ref[h:h + 1, :]
        for i in range(n):
            sc_ref[kt0 + i] = jnp.where(key + (kt0 + i) * TQ <= qry, acc[i * TQ:(i + 1) * TQ], NEG_INF)

    def stats_tile(kt, c):
        mx, mn, mp, cp, cn = c
        score = sc_ref[kt]
        scb_ref[kt] = floor_bf16(score)
        for g in _row_groups(score):
            pos = g > 0.0
            mx = jnp.maximum(mx, g)
            mn = jnp.minimum(mn, jnp.where(g > 0.5 * NEG_INF, g, -NEG_INF))
            mp = jnp.minimum(mp, jnp.where(pos, g, -NEG_INF))
            cp = cp + jnp.where(pos, 1.0, 0.0)
            cn = cn + jnp.where(g >= 0.0, 1.0, 0.0)
        return mx, mn, mp, cp, cn

    def idx_tile(kt, c):
        c = stats_tile(kt - 1, c)
        score_tiles(kt, 1)
        return c

    part = lambda v: jnp.full((SUBLANES, TQ), v, F32)
    score_tiles(0, 1)
    stats = lax.fori_loop(1, nt, idx_tile, (part(NEG_INF), part(-NEG_INF), part(-NEG_INF), part(0.0), part(0.0)))
    mx, mn, mp, cp, cn = stats_tile(nt - 1, stats)

    rmax = jnp.max(mx, axis=0, keepdims=True)
    rmin = jnp.min(mn, axis=0, keepdims=True)
    minpos = jnp.min(mp, axis=0, keepdims=True)
    cpos = jnp.sum(cp, axis=0, keepdims=True)
    cnn = jnp.sum(cn, axis=0, keepdims=True)
    nvalid = (lax.broadcasted_iota(jnp.int32, (1, TQ), 1) + j * TQ + 1).astype(F32)
    small = nvalid <= kf

    def tile_loop(body, init):
        n_pairs = lax.shift_right_logical(nt, 1)
        c = lax.fori_loop(0, n_pairs, lambda i, c: body(2 * i + 1, body(2 * i, c)), init)
        return lax.fori_loop(2 * n_pairs, nt, body, c)

    def count_ge(thr):
        def body(kt, acc):
            for g in _row_groups(sc_ref[kt]):
                acc = acc + jnp.where(g >= thr, 1.0, 0.0)
            return acc
        acc = tile_loop(body, jnp.zeros((SUBLANES, TQ), F32))
        return jnp.sum(acc, axis=0, keepdims=True)

    ztie = jnp.logical_and(cpos < kf, cnn >= kf)
    pos_side = cpos >= kf
    hi_top = rmax + jnp.maximum(jnp.abs(rmax) * 1e-6, 1e-30)
    lo0 = jnp.where(small, 0.5 * NEG_INF, jnp.where(ztie, 0.0, jnp.where(pos_side, minpos, rmin)))
    hi0 = jnp.where(small, -NEG_INF, jnp.where(ztie, minpos, jnp.where(pos_side, hi_top, 0.0)))
    clo0 = jnp.where(jnp.logical_or(small, ztie), jnp.where(small, nvalid, cnn), jnp.where(pos_side, cpos, nvalid))
    chi0 = jnp.where(small, 0.0, jnp.where(ztie, cpos, jnp.where(pos_side, 0.0, cnn)))
    done0 = jnp.where(jnp.logical_or(jnp.logical_or(small, ztie), clo0 == kf), 1.0, 0.0)

    def bisect_step(state, mid, cnt, usable):
        lo, hi, clo, chi, done = state
        act = jnp.logical_and(done < 0.5, usable)
        ge = cnt >= kf
        up_lo = jnp.logical_and(act, ge)
        up_hi = jnp.logical_and(act, jnp.logical_not(ge))
        lo = jnp.where(up_lo, mid, lo)
        clo = jnp.where(up_lo, cnt, clo)
        hi = jnp.where(up_hi, mid, hi)
        chi = jnp.where(up_hi, cnt, chi)
        done = jnp.where(jnp.logical_and(act, cnt == kf), 1.0, done)
        return lo, hi, clo, chi, done

    def count_ge_bf16(thr):
        one, zero = jnp.ones((), BF16), jnp.zeros((), BF16)

        def body(kt, acc):
            t = scb_ref[kt]
            part = jnp.zeros((BF16_ROWS, TQ), BF16)
            for r in range(TQ // BF16_ROWS):
                part = part + jnp.where(t[r * BF16_ROWS:(r + 1) * BF16_ROWS, :] >= thr, one, zero)
            return acc + part.astype(F32)
        acc = tile_loop(body, jnp.zeros((BF16_ROWS, TQ), F32))
        return jnp.sum(acc, axis=0, keepdims=True)

    def coarse_step(_, state):
        lo, hi = state[0], state[1]
        mid_b = (lo + 0.5 * (hi - lo)).astype(BF16)
        mid = mid_b.astype(F32)
        usable = jnp.logical_and(mid > lo, mid < hi)
        return bisect_step(state, mid, count_ge_bf16(mid_b), usable)

    state = lax.fori_loop(0, COARSE_STEPS, coarse_step, (lo0, hi0, clo0, chi0, done0))

    def fine_step(_, state):
        lo, hi = state[0], state[1]
        mid = lo + 0.5 * (hi - lo)
        stalled = jnp.logical_or(mid <= lo, mid >= hi)
        state = bisect_step(state, mid, count_ge(mid), jnp.logical_not(stalled))
        return state[:4] + (jnp.where(stalled, 1.0, state[4]),)

    state = lax.fori_loop(0, FINE_STEPS_UNCHECKED, fine_step, state)

    def bis_cond(c):
        return jnp.logical_and(c[0] < BISECT_MAX_IT, c[2] > 0.0)

    def bis_body(c):
        it, state, _ = c
        state = lax.fori_loop(0, FINE_STEPS_PER_CHECK, fine_step, state)
        return it + FINE_STEPS_PER_CHECK, state, jnp.sum(1.0 - state[4])

    _, (lo, hi, clo, chi, _), _ = lax.while_loop(
        bis_cond, bis_body, (jnp.int32(0), state, jnp.sum(1.0 - state[4])))

    tie = jnp.logical_and(clo > kf, jnp.logical_not(small))
    band_quota = jnp.where(tie, kf - chi, float(seq))
    prefix_ones = jnp.where(key >= lax.broadcasted_iota(jnp.int32, (TQ, TQ), 1), 1.0, 0.0).astype(BF16)

    def mask_tiles(kts, before):
        tiles = [sc_ref[kt] for kt in kts]
        bands = [jnp.where(t >= lo, jnp.where(t < hi, 1.0, 0.0), 0.0).astype(BF16) for t in tiles]
        ranks = [_mm(prefix_ones, band) for band in bands]
        masks = []
        for t, rank in zip(tiles, ranks):
            rank = rank + before
            keep_band = jnp.where(rank <= band_quota, 0.0, NEG_INF)
            masks.append(jnp.where(t >= lo, jnp.where(t >= hi, 0.0, keep_band), NEG_INF))
            before = rank[TQ - 1:TQ, :]
        for kt, mask in zip(kts, masks):
            sc_ref[kt] = mask
        return before

    n_pairs = lax.shift_right_logical(nt, 1)
    before = lax.fori_loop(0, n_pairs, lambda i, c: mask_tiles([2 * i, 2 * i + 1], c), jnp.zeros((1, TQ), F32))
    lax.fori_loop(2 * n_pairs, nt, lambda kt, c: mask_tiles([kt], c), before)

    m_ref[...] = jnp.full(m_ref.shape, NEG_INF, F32)
    l_ref[...] = jnp.zeros(l_ref.shape, F32)
    acc_ref[...] = jnp.zeros(acc_ref.shape, F32)

    def logits_phase(kts, near, buf):
        n = len(kts)
        rows = pl.ds(pl.multiple_of(kts[0] * TQ, TQ), n * TQ)
        masks = [sc_ref[kt] for kt in kts]
        for h in range(HEADS):
            p = h // 2
            qk = _nt(k_ref[rows, p * LANES:(p + 1) * LANES], qm_ref[h])
            tile_max = None
            for i, kt in enumerate(kts):
                s = qk[i * TQ:(i + 1) * TQ] + masks[i]
                if near:
                    s = s + bias_ref[kt - j + 1, h]
                s_ref[buf, i, h] = s
                mx = jnp.max(s, axis=0, keepdims=True)
                tile_max = mx if i == 0 else jnp.maximum(tile_max, mx)
            tmax_ref[buf, h] = tile_max

    def softmax_phase(kts, buf):
        for h in range(HEADS):
            m_old = m_ref[h]
            m_new = jnp.maximum(m_old, tmax_ref[buf, h])
            alpha = jnp.exp2(m_old - m_new)
            pv = None
            for i, kt in enumerate(kts):
                part = _mm(vt_ref[kt, h], jnp.exp2(s_ref[buf, i, h] - m_new).astype(BF16))
                pv = part if i == 0 else pv + part
            m_ref[h] = m_new
            l_ref[h] = alpha * l_ref[h] + pv[HEAD_DIM:HEAD_DIM + 1, :]
            acc_ref[h] = alpha * acc_ref[h] + pv[0:HEAD_DIM, :]

    n_far = jnp.maximum(j - 1, 0)

    def attn_tiles(kts, near):
        logits_phase(kts, near, 0)
        softmax_phase(kts, 0)

    n_far_pairs = lax.shift_right_logical(n_far, 1)

    def far_pair(i, carry):
        attn_tiles([2 * i, 2 * i + 1], near=False)
        return carry

    def far_single(kt, carry):
        attn_tiles([kt], near=False)
        return carry

    lax.fori_loop(0, n_far_pairs, far_pair, 0)
    lax.fori_loop(2 * n_far_pairs, n_far, far_single, 0)

    @pl.when(j >= 1)
    def _():
        attn_tiles([j - 1, j], near=True)

    @pl.when(j == 0)
    def _():
        attn_tiles([j], near=True)

    for p in range(HEADS // 2):
        pair = jnp.concatenate([acc_ref[2 * p] / l_ref[2 * p], acc_ref[2 * p + 1] / l_ref[2 * p + 1]], axis=0)
        o_ref[:, p * LANES:(p + 1) * LANES] = pair.T.astype(o_ref.dtype)


def _dsa(z, bias, d_model):
    b, s, _ = z.shape
    n_keep = min(TOPK_MAX, s // 4)
    kern = functools.partial(_dsa_kernel, n_keep=n_keep, seq=s)
    ki_blk, wi_blk = _z_small_block(d_model, 0), _z_small_block(d_model, 1)
    return pl.pallas_call(
        kern,
        grid=(b, s // TQ),
        in_specs=[pl.BlockSpec((None, TQ, MIX_W), lambda bi, j: (bi, j, Z_Q)),
                  pl.BlockSpec((None, s, MIX_W), lambda bi, j: (bi, 0, Z_K)),
                  pl.BlockSpec((None, TQ, MIX_W), lambda bi, j: (bi, j, Z_V)),
                  pl.BlockSpec((None, TQ, MIX_W), lambda bi, j: (bi, j, Z_QI)),
                  pl.BlockSpec((None, s, LANES), lambda bi, j: (bi, 0, ki_blk)),
                  pl.BlockSpec((None, TQ, LANES), lambda bi, j: (bi, j, wi_blk)),
                  pl.BlockSpec((2, HEADS, TQ, TQ), lambda bi, j: (0, 0, 0, 0))],
        out_specs=pl.BlockSpec((None, TQ, MIX_W), lambda bi, j: (bi, j, 0)),
        out_shape=jax.ShapeDtypeStruct((b, s, MIX_W), BF16),
        scratch_shapes=[pltpu.VMEM((s // TQ, TQ, TQ), F32),
                        pltpu.VMEM((s // TQ, TQ, TQ), BF16),
                        pltpu.VMEM((HEADS, TQ, LANES), BF16),
                        pltpu.VMEM((HEADS, TQ, LANES), BF16),
                        pltpu.VMEM((LANES, TQ), F32),
                        pltpu.VMEM((s // TQ, HEADS, HEAD_DIM + ONES_ROWS, TQ), BF16),
                        pltpu.VMEM((HEADS, 1, TQ), F32),
                        pltpu.VMEM((HEADS, 1, TQ), F32),
                        pltpu.VMEM((HEADS, HEAD_DIM, TQ), F32),
                        pltpu.VMEM((1, 2, HEADS, TQ, TQ), F32),
                        pltpu.VMEM((1, HEADS, 1, TQ), F32)],
        compiler_params=_cparams(("parallel", "arbitrary")),
        name="dsa",
    )(z, z, z, z, z, z, bias)


def _halo_spec(width, blk, tm, rows):
    step = tm // rows
    return pl.BlockSpec((None, rows, width), lambda bi, i: (bi, jnp.maximum(i * step - 1, 0), blk))


def _shift_lerp(cur_ref, halo_ref, mu, first):
    cur = cur_ref[...].astype(F32)
    nh = halo_ref.shape[0]
    prev_last = jnp.where(first, 0.0, halo_ref[nh - 1:nh, :].astype(F32))
    rolled = pltpu.roll(cur, 1, axis=0)
    rowid = lax.broadcasted_iota(jnp.int32, cur.shape, 0)
    sh = jnp.where(rowid == 0, prev_last, rolled)
    return cur + (sh - cur) * mu


def _head_sum(x, bq_ref):
    xb = x.astype(BF16)
    return jnp.concatenate([_mm(xb[:, q * QUAD:(q + 1) * QUAD], bq_ref[...]) for q in range(MIX_W // QUAD)], axis=1)


def _rwkv_kernel(r_ref, k_ref, v_ref, s2_ref, s3_ref, rh_ref, kh_ref, vh_ref, s2h_ref, s3h_ref,
                 mu_r, mu_k, mu_v, mu_s2, mu_s3, w0_ref, wup_ref, a0_ref, aup_ref, gup_ref,
                 kk_ref, ka_ref, rk_ref, bq_ref, tri_ref,
                 y_ref, g_ref, bonus_ref,
                 st_ref, rs, ls, ks, vs, kks, kbs, s2s, s3s):
    first = pl.program_id(1) == 0

    @pl.when(first)
    def _():
        st_ref[...] = jnp.zeros_like(st_ref)

    rs[...] = _shift_lerp(r_ref, rh_ref, mu_r[...], first)
    ks[...] = _shift_lerp(k_ref, kh_ref, mu_k[...], first)
    vs[...] = _shift_lerp(v_ref, vh_ref, mu_v[...], first)
    s2s[...] = _shift_lerp(s2_ref, s2h_ref, mu_s2[...], first)
    s3s[...] = _shift_lerp(s3_ref, s3h_ref, mu_s3[...], first)

    def prepare(pair):
        rows = slice(pair * 2 * CHUNK, (pair + 1) * 2 * CHUNK)
        r, k, v, s2, s3 = rs[rows, :], ks[rows, :], vs[rows, :], s2s[rows, :], s3s[rows, :]
        xw = w0_ref[...] + _bmm(jnp.tanh(s2), wup_ref[...])
        softplus = jnp.maximum(-xw, 0.0) + jnp.log(1.0 + jnp.exp(-jnp.abs(xw)))
        ld = -jnp.exp(-softplus - 0.5)
        af = jax.nn.sigmoid(a0_ref[...] + _bmm(s2, aup_ref[...]))
        g_ref[rows, :] = _bmm(jax.nn.sigmoid(s3), gup_ref[...]).astype(g_ref.dtype)
        kkr = k * kk_ref[...]
        kkn = kkr / jnp.maximum(jnp.sqrt(_head_sum(kkr * kkr, bq_ref)), 1e-12)
        kmod = k * (1.0 + (af - 1.0) * ka_ref[...])
        bonus_ref[rows, :] = (_head_sum(r * kmod * rk_ref[...], bq_ref) * v).astype(bonus_ref.dtype)
        ls[rows, :] = ld
        ks[rows, :] = kmod
        kks[rows, :] = kkn
        kbs[rows, :] = kkn * af

    ri = lax.broadcasted_iota(jnp.int32, (QUAD, QUAD), 0)
    ci = lax.broadcasted_iota(jnp.int32, (QUAD, QUAD), 1)
    same_head = (ri // HEAD_DIM) == (ci // HEAD_DIM)
    strict = jnp.logical_and(same_head, (ri % CHUNK) > (ci % CHUNK))
    incl = jnp.logical_and(same_head, (ri % CHUNK) >= (ci % CHUNK))
    eye = jnp.where(ri == ci, 1.0, 0.0)

    def stack(x):
        return jnp.where(same_head, jnp.concatenate([x] * 4, axis=0), 0.0).astype(BF16)

    def tile4(x):
        return jnp.concatenate([x] * 4, axis=0).astype(BF16)

    def unstack(x):
        return (x[0:CHUNK] + x[CHUNK:2 * CHUNK]) + (x[2 * CHUNK:3 * CHUNK] + x[3 * CHUNK:4 * CHUNK])

    def chunk_operands(c):
        rows = slice(c * CHUNK, (c + 1) * CHUNK)
        ldc = ls[rows, :]
        p1 = ldc.astype(BF16)
        e1 = ldc - p1.astype(F32)
        p2 = e1.astype(BF16)
        p3 = (e1 - p2.astype(F32)).astype(BF16)
        tri = tri_ref[...]
        cl = (_mm(tri, p1) + _mm(tri, p2)) + _mm(tri, p3)
        cl_end = cl[CHUNK - 1:CHUNK, :]
        e_in = jnp.exp(cl)
        e_out = jnp.exp(-cl)
        e_end = jnp.exp(cl_end - cl)
        rt_all = rs[rows, :] * e_in
        at_all = -kks[rows, :] * jnp.exp(cl - ldc)
        bt_all = kbs[rows, :] * e_out
        kt_all = ks[rows, :] * e_out
        bg_all = (kbs[rows, :] * e_end).astype(BF16)
        kg_all = (ks[rows, :] * e_end).astype(BF16)
        v_all = vs[rows, :]
        gam_all = jnp.exp(cl_end)
        units = []
        for q in range(MIX_W // QUAD):
            sl = slice(q * QUAD, (q + 1) * QUAD)
            units.append(dict(rows=rows, sl=sl, q=q, rt=rt_all[:, sl], vv=v_all[:, sl].astype(BF16),
                              a4=stack(at_all[:, sl]), r4=stack(rt_all[:, sl]), v4=stack(v_all[:, sl]),
                              bt4=tile4(bt_all[:, sl]), kt4=tile4(kt_all[:, sl]),
                              bg=bg_all[:, sl], kg=kg_all[:, sl], gam=gam_all[:, sl]))
        return units

    n_doublings = int(math.log2(CHUNK)) - 1

    def chunk_pair(cp):
        us = chunk_operands(2 * cp) + chunk_operands(2 * cp + 1)
        for u in us:
            u["m_ab"] = jnp.where(strict, _nt(u["a4"], u["bt4"]), 0.0)
        for u in us:
            u["m_ak"] = jnp.where(strict, _nt(u["a4"], u["kt4"]), 0.0)
        for u in us:
            u["m_rb"] = jnp.where(incl, _nt(u["r4"], u["bt4"]), 0.0).astype(BF16)
        for u in us:
            u["m_rk"] = jnp.where(incl, _nt(u["r4"], u["kt4"]), 0.0).astype(BF16)
        for u in us:
            u["pw"] = u["m_ab"]
            u["inv"] = eye + u["m_ab"]
        for _ in range(n_doublings):
            for u in us:
                u["pw"] = _bmm(u["pw"], u["pw"])
            for u in us:
                u["inv"] = u["inv"] + _bmm(u["inv"], u["pw"])
        for u in us:
            u["inv"] = u["inv"].astype(BF16)
            u["mv4"] = _bmm(u["m_ak"], u["v4"])
        for u in us:
            u["ah4"] = _mm(u["inv"], u["a4"])
        for u in us:
            u["uh4"] = _bmm(u["inv"], u["mv4"])
        for u in us:
            u["ry"] = (u["rt"] + unstack(_bmm(u["m_rb"], u["ah4"]))).astype(BF16)
        for u in us:
            u["y0"] = unstack(_bmm(u["m_rb"], u["uh4"]) + _mm(u["m_rk"], u["v4"]))
        for u in us:
            ah, uh = unstack(u["ah4"]).astype(BF16), unstack(u["uh4"]).astype(BF16)
            u["g_low"] = jnp.where(same_head, _tn(u["bg"], ah), 0.0).astype(BF16)
            u["h_t"] = jnp.where(same_head, _tn(uh, u["bg"]) + _tn(u["vv"], u["kg"]), 0.0)
        for u in us:
            st = st_ref[u["q"]]
            stb = st.astype(BF16)
            y_ref[u["rows"], u["sl"]] = _nt(u["ry"], stb) + u["y0"]
            st_ref[u["q"]] = st * u["gam"] + _nt(stb, u["g_low"]) + u["h_t"]

    n_pairs = r_ref.shape[0] // (2 * CHUNK)
    prepare(0)
    for pair in range(n_pairs):
        if pair + 1 < n_pairs:
            prepare(pair + 1)
        chunk_pair(pair)


def _rwkv(z, prm, d_model):
    b, s, _ = z.shape
    tm = min(RW_TM, s)
    s2_blk, s3_blk = _z_small_block(d_model, 2), _z_small_block(d_model, 3)
    tok = lambda blk: pl.BlockSpec((None, tm, MIX_W), lambda bi, i: (bi, i, blk))
    tok128 = lambda blk: pl.BlockSpec((None, tm, LANES), lambda bi, i: (bi, i, blk))
    const = lambda a: pl.BlockSpec(a.shape, lambda bi, i: (0,) * a.ndim)
    consts = [prm[n] for n in ("mu_r", "mu_k", "mu_v", "mu_s2", "mu_s3", "w0", "w_up", "a0", "a_up", "g_up",
                               "k_k", "k_a", "r_k", "bones_q", "tri")]
    out_spec = pl.BlockSpec((None, tm, MIX_W), lambda bi, i: (bi, i, 0))
    return pl.pallas_call(
        _rwkv_kernel,
        grid=(b, s // tm),
        in_specs=[tok(Z_RR), tok(Z_RK), tok(Z_RV), tok128(s2_blk), tok128(s3_blk),
                  _halo_spec(MIX_W, Z_RR, tm, BF16_ROWS), _halo_spec(MIX_W, Z_RK, tm, BF16_ROWS),
                  _halo_spec(MIX_W, Z_RV, tm, BF16_ROWS), _halo_spec(LANES, s2_blk, tm, BF16_ROWS),
                  _halo_spec(LANES, s3_blk, tm, BF16_ROWS)] + [const(a) for a in consts],
        out_specs=[out_spec, out_spec, out_spec],
        out_shape=[jax.ShapeDtypeStruct((b, s, MIX_W), F32), jax.ShapeDtypeStruct((b, s, MIX_W), BF16),
                   jax.ShapeDtypeStruct((b, s, MIX_W), BF16)],
        scratch_shapes=([pltpu.VMEM((MIX_W // QUAD, QUAD, QUAD), F32)] + [pltpu.VMEM((tm, MIX_W), F32)] * 6
                        + [pltpu.VMEM((tm, LANES), F32)] * 2),
        compiler_params=_cparams(("parallel", "arbitrary")),
        name="rwkv",
    )(z, z, z, z, z, z, z, z, z, z, *consts)


def _causal_conv(p, halo, cw_ref):
    nh = halo.shape[0]
    rowid = lax.broadcasted_iota(jnp.int32, (SUBLANES, p.shape[1]), 0)
    conv = p * cw_ref[CONV_W - 1:CONV_W, :]
    for back in range(1, CONV_W):
        rolled = pltpu.roll(p, back, axis=0)
        head = rolled[0:SUBLANES]
        for rr in range(back):
            head = jnp.where(rowid == rr, halo[nh - back + rr:nh - back + rr + 1, :], head)
        rolled = jnp.concatenate([head, rolled[SUBLANES:]], axis=0)
        conv = conv + rolled * cw_ref[CONV_W - 1 - back:CONV_W - back, :]
    return conv


def _merge_kernel(x_ref, oa_ref, y_ref, g_ref, bonus_ref, cb_ref, cc_ref, cx_ref, cch_ref, cxh_ref,
                  gate0_ref, gate1_ref, gate2_ref, g1_ref, lnw_ref, lnb_ref, bq_ref, cw_ref, wb_ref, wo_ref, o_ref,
                  merged_ref):
    first = pl.program_id(1) == 0
    y = y_ref[...]
    inv_n = 1.0 / HEAD_DIM
    p1 = y.astype(BF16)
    mean = (_head_sum(p1, bq_ref) + _head_sum(y - p1.astype(F32), bq_ref)) * inv_n
    yc = y - mean
    var = _head_sum(yc * yc, bq_ref) * inv_n
    o_rwkv = ((yc * lax.rsqrt(var + GN_EPS) * lnw_ref[...] + lnb_ref[...] + bonus_ref[...].astype(F32))
              * g_ref[...].astype(F32))
    p = cc_ref[...].astype(F32) * cx_ref[...].astype(F32)
    ph = jnp.where(first, 0.0, cch_ref[...].astype(F32) * cxh_ref[...].astype(F32))
    o_conv = cb_ref[...].astype(F32) * _causal_conv(p, ph, cw_ref)
    branches = ((oa_ref[...], gate0_ref), (o_rwkv.astype(BF16), gate1_ref), (o_conv.astype(BF16), gate2_ref))
    d = x_ref.shape[-1]
    for c in range(d // QUAD):
        cols = slice(c * QUAD, (c + 1) * QUAD)
        part = jnp.zeros((x_ref.shape[0], QUAD), F32)
        for bi, (o, gate_ref) in enumerate(branches):
            part = part + jax.nn.sigmoid(gate_ref[:, cols].astype(F32)) * _mm(o, wb_ref[bi, :, cols])
        merged_ref[:, cols] = part.astype(BF16)
    o_ref[...] = x_ref[...] + g1_ref[...] * _mm(merged_ref[...], wo_ref[...])


def _merge(x, o_attn, y, g, bonus, z, mod5, layer, prm, tm=256):
    b, s, d = x.shape
    tm = min(tm, s)
    tok = lambda w, blk: pl.BlockSpec((None, tm, w), lambda bi, i: (bi, i, blk))
    const = lambda a: pl.BlockSpec(a.shape, lambda bi, i: (0,) * a.ndim)
    consts = [prm[n] for n in ("ln_w", "ln_b", "bones_q", "conv_w", "w_branch", "w_o")]
    gate_blk = Z_GATE_COL // d
    return pl.pallas_call(
        _merge_kernel,
        grid=(b, s // tm),
        in_specs=[tok(d, 0), tok(MIX_W, 0), tok(MIX_W, 0), tok(MIX_W, 0), tok(MIX_W, 0),
                  tok(MIX_W, Z_CB), tok(MIX_W, Z_CC), tok(MIX_W, Z_CX),
                  _halo_spec(MIX_W, Z_CC, tm, BF16_ROWS), _halo_spec(MIX_W, Z_CX, tm, BF16_ROWS),
                  tok(d, gate_blk), tok(d, gate_blk + 1), tok(d, gate_blk + 2),
                  _mod_spec(mod5, layer, 2, 2)] + [const(a) for a in consts],
        out_specs=tok(d, 0),
        out_shape=jax.ShapeDtypeStruct((b, s, d), F32),
        scratch_shapes=[pltpu.VMEM((tm, d), BF16)],
        compiler_params=_cparams(("parallel", "parallel")),
        name="merge",
    )(x, o_attn, y, g, bonus, z, z, z, z, z, z, z, z, mod5, *consts)


def _ffn_kernel(x_ref, xh_ref, sh_ref, sc_ref, g2_ref, gain_ref, wup_ref, cw_ref, wd_ref, fin_ref, o_ref,
                *, final_norm, tf):
    first = pl.program_id(1) == 0
    x = x_ref[...]
    h = _norm_mod(x, gain_ref[...], sc_ref[...], sh_ref[...]).astype(BF16)
    hh = _norm_mod(xh_ref[...], gain_ref[...], sc_ref[...], sh_ref[...]).astype(BF16)
    h_ext = jnp.concatenate([hh, h], axis=0)
    d_ff = wd_ref.shape[0]
    acc = jnp.zeros(x.shape, F32)
    for jf in range(d_ff // tf):
        cols = slice(jf * tf, (jf + 1) * tf)
        gate_cols = slice(d_ff + jf * tf, d_ff + (jf + 1) * tf)
        a_ext = _mm(h_ext, wup_ref[:, cols])
        a = a_ext[BF16_ROWS:]
        ah = jnp.where(first, 0.0, a_ext[BF16_ROWS - SUBLANES:BF16_ROWS])
        conv = _causal_conv(a, ah, cw_ref[:, cols])
        u = conv * jax.nn.sigmoid(conv) * _mm(h, wup_ref[:, gate_cols])
        acc = acc + _mm(u.astype(BF16), wd_ref[cols, :])
    out = x + g2_ref[...] * acc
    if final_norm:
        out = out * lax.rsqrt(jnp.mean(out * out, axis=-1, keepdims=True) + NORM_EPS) * fin_ref[...]
    o_ref[...] = out


def _ffn(x, mod5, layer, gain, w_up, conv_w, w_down, final_gain, final_norm, tm=512, tf=1408):
    b, s, d = x.shape
    d_ff = w_down.shape[0]
    assert d_ff % tf == 0 and tf % LANES == 0
    tm = min(tm, s)
    gain, final_gain = gain.reshape(1, d), final_gain.reshape(1, d)
    return pl.pallas_call(
        functools.partial(_ffn_kernel, final_norm=final_norm, tf=tf),
        grid=(b, s // tm),
        in_specs=[pl.BlockSpec((None, tm, d), lambda bi, i: (bi, i, 0)),
                  _halo_spec(d, 0, tm, BF16_ROWS),
                  _mod_spec(mod5, layer, 3, 2), _mod_spec(mod5, layer, 4, 2), _mod_spec(mod5, layer, 5, 2),
                  _resident(gain), _resident(w_up), _resident(conv_w), _resident(w_down), _resident(final_gain)],
        out_specs=pl.BlockSpec((None, tm, d), lambda bi, i: (bi, i, 0)),
        out_shape=jax.ShapeDtypeStruct((b, s, d), F32),
        compiler_params=_cparams(("parallel", "parallel")),
        name="ffn",
    )(x, x, mod5, mod5, mod5, gain, w_up, conv_w, w_down, final_gain)


def _split_w_in(w):
    sizes = (MIX_W, MIX_W, MIX_W, HEADS * HEAD_DIM, HEAD_DIM, HEADS,
             MIX_W, MIX_W, MIX_W, LORA_DECAY, LORA_ICLR, LORA_GATE,
             MIX_W, MIX_W, MIX_W)
    out, o = [], 0
    for n in sizes:
        out.append(w[:, o:o + n])
        o += n
    out.append(w[:, o:])
    return out


def _layer_params(l, w_in, rwkv_mu, rwkv_w0, rwkv_w_up, rwkv_a0, rwkv_a_up, rwkv_g_up, rwkv_k_k, rwkv_k_a,
                  rwkv_r_k, rwkv_ln_w, rwkv_ln_b, sc_conv_w, w_branch, w_o):
    d = w_in.shape[1]
    q, k, v, qi, ki, wi, rr, rk, rv, wd, ad, gd, cb, cc, cx, gates = _split_w_in(w_in[l])
    assert Z_GATE_COL % d == 0 and gates.shape[1] == 3 * d
    w_z = jnp.concatenate([q, k, v, qi, rr, rk, rv, cb, cc, cx, gates,
                           ki, ki, wi, jnp.zeros((d, LANES - HEADS), F32), wd, ad, gd], axis=1).astype(BF16)
    mu = rwkv_mu[l]
    row = lambda a: a.reshape(1, -1)
    head_id = jnp.arange(QUAD) // HEAD_DIM
    bones_q = (head_id[:, None] == head_id[None, :]).astype(BF16)
    tri = (jnp.arange(CHUNK)[:, None] >= jnp.arange(CHUNK)[None, :]).astype(BF16)
    zl = lambda n: jnp.zeros((n, MIX_W), F32)
    return dict(
        w_z=w_z,
        mu_r=row(mu[:MIX_W]), mu_k=row(mu[MIX_W:2 * MIX_W]), mu_v=row(mu[2 * MIX_W:3 * MIX_W]),
        mu_s2=row(mu[3 * MIX_W:3 * MIX_W + LORA_DECAY + LORA_ICLR]), mu_s3=row(mu[3 * MIX_W + LORA_DECAY + LORA_ICLR:]),
        w0=row(rwkv_w0[l]), w_up=jnp.concatenate([rwkv_w_up[l], zl(LORA_ICLR)], axis=0).astype(BF16),
        a0=row(rwkv_a0[l]), a_up=jnp.concatenate([zl(LORA_DECAY), rwkv_a_up[l]], axis=0).astype(BF16),
        g_up=rwkv_g_up[l].astype(BF16), k_k=row(rwkv_k_k[l]), k_a=row(rwkv_k_a[l]), r_k=row(rwkv_r_k[l]),
        bones_q=bones_q, tri=tri,
        ln_w=row(rwkv_ln_w[l]), ln_b=row(rwkv_ln_b[l]), conv_w=sc_conv_w[l].T,
        w_branch=w_branch[l].astype(BF16), w_o=w_o[l].astype(BF16),
    )


def kernel(x, c, positions, rel_bias, final_norm, ada_w, ada_b, norm_mix, w_in, rwkv_mu, rwkv_w0, rwkv_w_up,
           rwkv_a0, rwkv_a_up, rwkv_g_up, rwkv_k_k, rwkv_k_a, rwkv_r_k, rwkv_ln_w, rwkv_ln_b, sc_conv_w,
           w_branch, w_o, norm_ffn, ffn_w_up, ffn_conv_w, ffn_w_down):
    depth, d = ada_w.shape[0], x.shape[-1]
    mod5 = _ada_mod(c, ada_w, ada_b)
    bias = _bias_tiles(rel_bias)
    for l in range(depth):
        prm = _layer_params(l, w_in, rwkv_mu, rwkv_w0, rwkv_w_up, rwkv_a0, rwkv_a_up, rwkv_g_up, rwkv_k_k,
                            rwkv_k_a, rwkv_r_k, rwkv_ln_w, rwkv_ln_b, sc_conv_w, w_branch, w_o)
        z = _inproj(x, mod5, l, norm_mix[l], prm["w_z"])
        o_attn = _dsa(z, bias, d)
        y, g, bonus = _rwkv(z, prm, d)
        x = _merge(x, o_attn, y, g, bonus, z, mod5, l, prm)
        x = _ffn(x, mod5, l, norm_ffn[l], ffn_w_up[l].astype(BF16), ffn_conv_w[l].T, ffn_w_down[l].astype(BF16),
                 final_norm, final_norm=(l == depth - 1))
    return x
```

```python
import functools
import math

import jax
import jax.numpy as jnp
from jax import lax
from jax.experimental import pallas as pl
from jax.experimental.pallas import tpu as pltpu

F32 = jnp.float32
BF16 = jnp.bfloat16
HIGHEST = lax.Precision.HIGHEST

MIX_W = 512
HEADS = 8
HEAD_DIM = 64
TOPK_MAX = 256
N_BUCKETS = 32
MAX_DISTANCE = 128
LORA_DECAY = 64
LORA_ICLR = 64
LORA_GATE = 128
CONV_W = 3
NORM_EPS = 1e-6
GN_EPS = 64e-5
NEG_INF = -1e30

LANES = 128
SUBLANES = 8
BF16_ROWS = 16
VMEM_LIMIT = 56 * 1024 * 1024

TQ = 256
CHUNK = 64
RW_TM = 512
QUAD = 4 * HEAD_DIM
assert CHUNK == HEAD_DIM

Z_Q, Z_K, Z_V, Z_QI, Z_RR, Z_RK, Z_RV, Z_CB, Z_CC, Z_CX = range(10)
Z_GATE_COL = 10 * MIX_W


def _z_small_block(d_model, i):
    return (Z_GATE_COL + 3 * d_model) // LANES + i


def _cparams(sem):
    return pltpu.CompilerParams(dimension_semantics=sem, vmem_limit_bytes=VMEM_LIMIT)


def _nt(a, b, precision=None):
    return lax.dot_general(a, b, (((1,), (1,)), ((), ())), precision=precision,
                           preferred_element_type=F32)


def _tn(a, b, precision=None):
    return lax.dot_general(a, b, (((0,), (0,)), ((), ())), precision=precision,
                           preferred_element_type=F32)


def _mm(a, b, precision=None):
    return jnp.dot(a, b, precision=precision, preferred_element_type=F32)


def _bmm(a, b):
    return _mm(a.astype(BF16), b.astype(BF16))


def _mod_kernel(c_ref, w_ref, b_ref, o_ref):
    o_ref[...] = _mm(c_ref[...], w_ref[...], HIGHEST) + b_ref[...]


def _ada_mod(c, ada_w, ada_b):
    depth, d, d6 = ada_w.shape
    b = c.shape[0]
    out = pl.pallas_call(
        _mod_kernel,
        grid=(depth, d6 // d),
        in_specs=[pl.BlockSpec((b, d), lambda l, j: (0, 0)),
                  pl.BlockSpec((None, d, d), lambda l, j: (l, 0, j)),
                  pl.BlockSpec((None, 1, d), lambda l, j: (l, 0, j))],
        out_specs=pl.BlockSpec((None, b, d), lambda l, j: (l, 0, j)),
        out_shape=jax.ShapeDtypeStruct((depth, b, d6), F32),
        compiler_params=_cparams(("parallel", "parallel")),
        name="ada_mod",
    )(c, ada_w, ada_b.reshape(depth, 1, d6))
    return out.reshape(depth, b, d6 // d, 1, d)


def _mod_spec(mod5, layer, which, ngrid):
    d = mod5.shape[-1]
    if ngrid == 2:
        return pl.BlockSpec((None, None, None, 1, d), lambda b, i: (layer, b, which, 0, 0))
    return pl.BlockSpec((None, None, None, 1, d), lambda b, i, j: (layer, b, which, 0, 0))


def _norm_mod(x, gain, scale, shift):
    y = x * lax.rsqrt(jnp.mean(x * x, axis=-1, keepdims=True) + NORM_EPS) * gain
    return y * (1.0 + scale) + shift


def _resident(a):
    return pl.BlockSpec(a.shape, lambda bi, i: (0,) * a.ndim, pipeline_mode=pl.Buffered(1))


def _inproj_kernel(x_ref, sh_ref, sc_ref, g_ref, w_ref, o_ref, *, n_col_chunks):
    h = _norm_mod(x_ref[...], g_ref[...], sc_ref[...], sh_ref[...]).astype(BF16)
    tn = w_ref.shape[1] // n_col_chunks
    for c in range(n_col_chunks):
        o_ref[:, c * tn:(c + 1) * tn] = _mm(h, w_ref[:, c * tn:(c + 1) * tn]).astype(o_ref.dtype)


def _inproj(x, mod5, layer, gain, w, tm=512, n_col_chunks=4):
    b, s, d = x.shape
    zc = w.shape[1]
    tm = min(tm, s)
    assert zc % (n_col_chunks * LANES) == 0
    gain = gain.reshape(1, d)
    return pl.pallas_call(
        functools.partial(_inproj_kernel, n_col_chunks=n_col_chunks),
        grid=(b, s // tm),
        in_specs=[pl.BlockSpec((None, tm, d), lambda bi, i: (bi, i, 0)),
                  _mod_spec(mod5, layer, 0, 2),
                  _mod_spec(mod5, layer, 1, 2),
                  _resident(gain), _resident(w)],
        out_specs=pl.BlockSpec((None, tm, zc), lambda bi, i: (bi, i, 0)),
        out_shape=jax.ShapeDtypeStruct((b, s, zc), BF16),
        compiler_params=_cparams(("parallel", "parallel")),
        name="inproj",
    )(x, mod5, mod5, gain, w)


def _bias_kernel(rb_ref, o_ref):
    which = pl.program_id(0)
    h = pl.program_id(1)
    ri = lax.broadcasted_iota(jnp.int32, (TQ, TQ), 0)
    ci = lax.broadcasted_iota(jnp.int32, (TQ, TQ), 1)
    dist = ci - ri + (1 - which) * TQ
    n = jnp.maximum(dist, 0)
    max_exact = N_BUCKETS // 2
    nf = jnp.maximum(n, 1).astype(F32)
    large = max_exact + (jnp.log(nf / max_exact) / math.log(MAX_DISTANCE / max_exact)
                         * (N_BUCKETS - max_exact)).astype(jnp.int32)
    large = jnp.minimum(large, N_BUCKETS - 1)
    bucket = jnp.where(n < max_exact, n, large)
    far = rb_ref[N_BUCKETS - 1, h]
    acc = jnp.zeros((TQ, TQ), F32)
    for bkt in range(N_BUCKETS - 1):
        acc = jnp.where(bucket == bkt, rb_ref[bkt, h] - far, acc)
    o_ref[...] = acc * LOG2E


def _bias_tiles(rel_bias):
    assert TQ >= MAX_DISTANCE
    return pl.pallas_call(
        _bias_kernel,
        grid=(2, HEADS),
        in_specs=[pl.BlockSpec(memory_space=pltpu.SMEM)],
        out_specs=pl.BlockSpec((None, None, TQ, TQ), lambda w, h: (w, h, 0, 0)),
        out_shape=jax.ShapeDtypeStruct((2, HEADS, TQ, TQ), F32),
        compiler_params=_cparams(("parallel", "parallel")),
        name="bias_tiles",
    )(rel_bias)


BISECT_MAX_IT = 300
COARSE_STEPS = 10
FINE_STEPS_UNCHECKED = 4
FINE_STEPS_PER_CHECK = 2
LOG2E = math.log2(math.e)
ONES_ROWS = BF16_ROWS


def _row_groups(t):
    return [t[r * SUBLANES:(r + 1) * SUBLANES, :] for r in range(t.shape[0] // SUBLANES)]


def _dsa_kernel(q_ref, k_ref, v_ref, qi_ref, ki_ref, wi_ref, bias_ref, o_ref,
                sc_ref, scb_ref, qm_ref, qim_ref, wt_ref, vt_ref, m_ref, l_ref, acc_ref, s_ref, tmax_ref,
                *, n_keep, seq):
    j = pl.program_id(1)
    nt = j + 1
    kf = float(n_keep)
    lane = lax.broadcasted_iota(jnp.int32, (TQ, LANES), 1)
    att_scale = HEAD_DIM ** -0.5 * LOG2E
    w_scale = (HEADS ** -0.5) * (HEAD_DIM ** -0.5)

    for h in range(HEADS):
        p, odd = divmod(h, 2)
        hm = (lane >= HEAD_DIM) if odd else (lane < HEAD_DIM)
        qs = q_ref[:, p * LANES:(p + 1) * LANES]
        qm_ref[h] = jnp.where(hm, qs, jnp.zeros_like(qs)) * att_scale
        qis = qi_ref[:, p * LANES:(p + 1) * LANES]
        qim_ref[h] = jnp.where(hm, qis, jnp.zeros_like(qis))
    wt_ref[...] = wi_ref[...].astype(F32).T * w_scale
    v_t = v_ref[...].T
    for h in range(HEADS):
        vt_ref[j, h, 0:HEAD_DIM, :] = v_t[h * HEAD_DIM:(h + 1) * HEAD_DIM, :]
        vt_ref[j, h, HEAD_DIM:HEAD_DIM + ONES_ROWS, :] = jnp.ones((ONES_ROWS, TQ), BF16)

    key = lax.broadcasted_iota(jnp.int32, (TQ, TQ), 0)
    qry = lax.broadcasted_iota(jnp.int32, (TQ, TQ), 1) + j * TQ

    def floor_bf16(x):
        xi = lax.bitcast_convert_type(x, jnp.int32)
        xi = jnp.where(xi < 0, xi + jnp.int32(0xFFFF), xi) & jnp.int32(-0x10000)
        return lax.bitcast_convert_type(xi, F32).astype(BF16)

    def score_tiles(kt0, n):
        ki_t = ki_ref[pl.ds(pl.multiple_of(kt0 * TQ, TQ), n * TQ), :]
        acc = jnp.zeros((n * TQ, TQ), F32)
        for h in range(HEADS):
            acc = acc + jnp.maximum(_nt(ki_t, qim_ref[h]), 0.0) * wt_ref[h:h + 1, :]
        for i in range(n):
            sc_ref[kt0 + i] = jnp.where(key + (kt0 + i) * TQ <= qry, acc[i * TQ:(i + 1) * TQ], NEG_INF)

    def stats_tile(kt, c):
        mx, mn, mp, cp, cn = c
        score = sc_ref[kt]
        scb_ref[kt] = floor_bf16(score)
        for g in _row_groups(score):
            pos = g > 0.0
            mx = jnp.maximum(mx, g)
            mn = jnp.minimum(mn, jnp.where(g > 0.5 * NEG_INF, g, -NEG_INF))
            mp = jnp.minimum(mp, jnp.where(pos, g, -NEG_INF))
            cp = cp + jnp.where(pos, 1.0, 0.0)
            cn = cn + jnp.where(g >= 0.0, 1.0, 0.0)
        return mx, mn, mp, cp, cn

    def idx_tile(kt, c):
        c = stats_tile(kt - 1, c)
        score_tiles(kt, 1)
        return c

    part = lambda v: jnp.full((SUBLANES, TQ), v, F32)
    score_tiles(0, 1)
    stats = lax.fori_loop(1, nt, idx_tile, (part(NEG_INF), part(-NEG_INF), part(-NEG_INF), part(0.0), part(0.0)))
    mx, mn, mp, cp, cn = stats_tile(nt - 1, stats)

    rmax = jnp.max(mx, axis=0, keepdims=True)
    rmin = jnp.min(mn, axis=0, keepdims=True)
    minpos = jnp.min(mp, axis=0, keepdims=True)
    cpos = jnp.sum(cp, axis=0, keepdims=True)
    cnn = jnp.sum(cn, axis=0, keepdims=True)
    nvalid = (lax.broadcasted_iota(jnp.int32, (1, TQ), 1) + j * TQ + 1).astype(F32)
    small = nvalid <= kf

    def tile_loop(body, init):
        n_pairs = lax.shift_right_logical(nt, 1)
        c = lax.fori_loop(0, n_pairs, lambda i, c: body(2 * i + 1, body(2 * i, c)), init)
        return lax.fori_loop(2 * n_pairs, nt, body, c)

    def count_ge(thr):
        def body(kt, acc):
            for g in _row_groups(sc_ref[kt]):
                acc = acc + jnp.where(g >= thr, 1.0, 0.0)
            return acc
        acc = tile_loop(body, jnp.zeros((SUBLANES, TQ), F32))
        return jnp.sum(acc, axis=0, keepdims=True)

    ztie = jnp.logical_and(cpos < kf, cnn >= kf)
    pos_side = cpos >= kf
    hi_top = rmax + jnp.maximum(jnp.abs(rmax) * 1e-6, 1e-30)
    lo0 = jnp.where(small, 0.5 * NEG_INF, jnp.where(ztie, 0.0, jnp.where(pos_side, minpos, rmin)))
    hi0 = jnp.where(small, -NEG_INF, jnp.where(ztie, minpos, jnp.where(pos_side, hi_top, 0.0)))
    clo0 = jnp.where(jnp.logical_or(small, ztie), jnp.where(small, nvalid, cnn), jnp.where(pos_side, cpos, nvalid))
    chi0 = jnp.where(small, 0.0, jnp.where(ztie, cpos, jnp.where(pos_side, 0.0, cnn)))
    done0 = jnp.where(jnp.logical_or(jnp.logical_or(small, ztie), clo0 == kf), 1.0, 0.0)

    def bisect_step(state, mid, cnt, usable):
        lo, hi, clo, chi, done = state
        act = jnp.logical_and(done < 0.5, usable)
        ge = cnt >= kf
        up_lo = jnp.logical_and(act, ge)
        up_hi = jnp.logical_and(act, jnp.logical_not(ge))
        lo = jnp.where(up_lo, mid, lo)
        clo = jnp.where(up_lo, cnt, clo)
        hi = jnp.where(up_hi, mid, hi)
        chi = jnp.where(up_hi, cnt, chi)
        done = jnp.where(jnp.logical_and(act, cnt == kf), 1.0, done)
        return lo, hi, clo, chi, done

    def count_ge_bf16(thr):
        one, zero = jnp.ones((), BF16), jnp.zeros((), BF16)

        def body(kt, acc):
            t = scb_ref[kt]
            part = jnp.zeros((BF16_ROWS, TQ), BF16)
            for r in range(TQ // BF16_ROWS):
                part = part + jnp.where(t[r * BF16_ROWS:(r + 1) * BF16_ROWS, :] >= thr, one, zero)
            return acc + part.astype(F32)
        acc = tile_loop(body, jnp.zeros((BF16_ROWS, TQ), F32))
        return jnp.sum(acc, axis=0, keepdims=True)

    def coarse_step(_, state):
        lo, hi = state[0], state[1]
        mid_b = (lo + 0.5 * (hi - lo)).astype(BF16)
        mid = mid_b.astype(F32)
        usable = jnp.logical_and(mid > lo, mid < hi)
        return bisect_step(state, mid, count_ge_bf16(mid_b), usable)

    state = lax.fori_loop(0, COARSE_STEPS, coarse_step, (lo0, hi0, clo0, chi0, done0))

    def fine_step(_, state):
        lo, hi = state[0], state[1]
        mid = lo + 0.5 * (hi - lo)
        stalled = jnp.logical_or(mid <= lo, mid >= hi)
        state = bisect_step(state, mid, count_ge(mid), jnp.logical_not(stalled))
        return state[:4] + (jnp.where(stalled, 1.0, state[4]),)

    state = lax.fori_loop(0, FINE_STEPS_UNCHECKED, fine_step, state)

    def bis_cond(c):
        return jnp.logical_and(c[0] < BISECT_MAX_IT, c[2] > 0.0)

    def bis_body(c):
        it, state, _ = c
        state = lax.fori_loop(0, FINE_STEPS_PER_CHECK, fine_step, state)
        return it + FINE_STEPS_PER_CHECK, state, jnp.sum(1.0 - state[4])

    _, (lo, hi, clo, chi, _), _ = lax.while_loop(
        bis_cond, bis_body, (jnp.int32(0), state, jnp.sum(1.0 - state[4])))

    tie = jnp.logical_and(clo > kf, jnp.logical_not(small))
    band_quota = jnp.where(tie, kf - chi, float(seq))
    prefix_ones = jnp.where(key >= lax.broadcasted_iota(jnp.int32, (TQ, TQ), 1), 1.0, 0.0).astype(BF16)

    def mask_tiles(kts, before):
        tiles = [sc_ref[kt] for kt in kts]
        bands = [jnp.where(t >= lo, jnp.where(t < hi, 1.0, 0.0), 0.0).astype(BF16) for t in tiles]
        ranks = [_mm(prefix_ones, band) for band in bands]
        masks = []
        for t, rank in zip(tiles, ranks):
            rank = rank + before
            keep_band = jnp.where(rank <= band_quota, 0.0, NEG_INF)
            masks.append(jnp.where(t >= lo, jnp.where(t >= hi, 0.0, keep_band), NEG_INF))
            before = rank[TQ - 1:TQ, :]
        for kt, mask in zip(kts, masks):
            sc_ref[kt] = mask
        return before

    n_pairs = lax.shift_right_logical(nt, 1)
    before = lax.fori_loop(0, n_pairs, lambda i, c: mask_tiles([2 * i, 2 * i + 1], c), jnp.zeros((1, TQ), F32))
    lax.fori_loop(2 * n_pairs, nt, lambda kt, c: mask_tiles([kt], c), before)

    m_ref[...] = jnp.full(m_ref.shape, NEG_INF, F32)
    l_ref[...] = jnp.zeros(l_ref.shape, F32)
    acc_ref[...] = jnp.zeros(acc_ref.shape, F32)

    def logits_phase(kts, near, buf):
        n = len(kts)
        rows = pl.ds(pl.multiple_of(kts[0] * TQ, TQ), n * TQ)
        masks = [sc_ref[kt] for kt in kts]
        for h in range(HEADS):
            p = h // 2
            qk = _nt(k_ref[rows, p * LANES:(p + 1) * LANES], qm_ref[h])
            tile_max = None
            for i, kt in enumerate(kts):
                s = qk[i * TQ:(i + 1) * TQ] + masks[i]
                if near:
                    s = s + bias_ref[kt - j + 1, h]
                s_ref[buf, i, h] = s
                mx = jnp.max(s, axis=0, keepdims=True)
                tile_max = mx if i == 0 else jnp.maximum(tile_max, mx)
            tmax_ref[buf, h] = tile_max

    def softmax_phase(kts, buf):
        for h in range(HEADS):
            m_old = m_ref[h]
            m_new = jnp.maximum(m_old, tmax_ref[buf, h])
            alpha = jnp.exp2(m_old - m_new)
            pv = None
            for i, kt in enumerate(kts):
                part = _mm(vt_ref[kt, h], jnp.exp2(s_ref[buf, i, h] - m_new).astype(BF16))
                pv = part if i == 0 else pv + part
            m_ref[h] = m_new
            l_ref[h] = alpha * l_ref[h] + pv[HEAD_DIM:HEAD_DIM + 1, :]
            acc_ref[h] = alpha * acc_ref[h] + pv[0:HEAD_DIM, :]

    n_far = jnp.maximum(j - 1, 0)

    def attn_tiles(kts, near):
        logits_phase(kts, near, 0)
        softmax_phase(kts, 0)

    n_far_pairs = lax.shift_right_logical(n_far, 1)

    def far_pair(i, carry):
        attn_tiles([2 * i, 2 * i + 1], near=False)
        return carry

    def far_single(kt, carry):
        attn_tiles([kt], near=False)
        return carry

    lax.fori_loop(0, n_far_pairs, far_pair, 0)
    lax.fori_loop(2 * n_far_pairs, n_far, far_single, 0)

    @pl.when(j >= 1)
    def _():
        attn_tiles([j - 1, j], near=True)

    @pl.when(j == 0)
    def _():
        attn_tiles([j], near=True)

    for p in range(HEADS // 2):
        pair = jnp.concatenate([acc_ref[2 * p] / l_ref[2 * p], acc_ref[2 * p + 1] / l_ref[2 * p + 1]], axis=0)
        o_ref[:, p * LANES:(p + 1) * LANES] = pair.T.astype(o_ref.dtype)


def _dsa(z, bias, d_model):
    b, s, _ = z.shape
    n_keep = min(TOPK_MAX, s // 4)
    kern = functools.partial(_dsa_kernel, n_keep=n_keep, seq=s)
    ki_blk, wi_blk = _z_small_block(d_model, 0), _z_small_block(d_model, 1)
    return pl.pallas_call(
        kern,
        grid=(b, s // TQ),
        in_specs=[pl.BlockSpec((None, TQ, MIX_W), lambda bi, j: (bi, j, Z_Q)),
                  pl.BlockSpec((None, s, MIX_W), lambda bi, j: (bi, 0, Z_K)),
                  pl.BlockSpec((None, TQ, MIX_W), lambda bi, j: (bi, j, Z_V)),
                  pl.BlockSpec((None, TQ, MIX_W), lambda bi, j: (bi, j, Z_QI)),
                  pl.BlockSpec((None, s, LANES), lambda bi, j: (bi, 0, ki_blk)),
                  pl.BlockSpec((None, TQ, LANES), lambda bi, j: (bi, j, wi_blk)),
                  pl.BlockSpec((2, HEADS, TQ, TQ), lambda bi, j: (0, 0, 0, 0))],
        out_specs=pl.BlockSpec((None, TQ, MIX_W), lambda bi, j: (bi, j, 0)),
        out_shape=jax.ShapeDtypeStruct((b, s, MIX_W), BF16),
        scratch_shapes=[pltpu.VMEM((s // TQ, TQ, TQ), F32),
                        pltpu.VMEM((s // TQ, TQ, TQ), BF16),
                        pltpu.VMEM((HEADS, TQ, LANES), BF16),
                        pltpu.VMEM((HEADS, TQ, LANES), BF16),
                        pltpu.VMEM((LANES, TQ), F32),
                        pltpu.VMEM((s // TQ, HEADS, HEAD_DIM + ONES_ROWS, TQ), BF16),
                        pltpu.VMEM((HEADS, 1, TQ), F32),
                        pltpu.VMEM((HEADS, 1, TQ), F32),
                        pltpu.VMEM((HEADS, HEAD_DIM, TQ), F32),
                        pltpu.VMEM((1, 2, HEADS, TQ, TQ), F32),
                        pltpu.VMEM((1, HEADS, 1, TQ), F32)],
        compiler_params=_cparams(("parallel", "arbitrary")),
        name="dsa",
    )(z, z, z, z, z, z, bias)


def _halo_spec(width, blk, tm, rows):
    step = tm // rows
    return pl.BlockSpec((None, rows, width), lambda bi, i: (bi, jnp.maximum(i * step - 1, 0), blk))


def _shift_lerp(cur_ref, halo_ref, mu, first):
    cur = cur_ref[...].astype(F32)
    nh = halo_ref.shape[0]
    prev_last = jnp.where(first, 0.0, halo_ref[nh - 1:nh, :].astype(F32))
    rolled = pltpu.roll(cur, 1, axis=0)
    rowid = lax.broadcasted_iota(jnp.int32, cur.shape, 0)
    sh = jnp.where(rowid == 0, prev_last, rolled)
    return cur + (sh - cur) * mu


def _head_sum(x, bq_ref):
    xb = x.astype(BF16)
    return jnp.concatenate([_mm(xb[:, q * QUAD:(q + 1) * QUAD], bq_ref[...]) for q in range(MIX_W // QUAD)], axis=1)


def _rwkv_kernel(r_ref, k_ref, v_ref, s2_ref, s3_ref, rh_ref, kh_ref, vh_ref, s2h_ref, s3h_ref,
                 mu_r, mu_k, mu_v, mu_s2, mu_s3, w0_ref, wup_ref, a0_ref, aup_ref, gup_ref,
                 kk_ref, ka_ref, rk_ref, bq_ref, tri_ref,
                 y_ref, g_ref, bonus_ref,
                 st_ref, rs, ls, ks, vs, kks, kbs, s2s, s3s):
    first = pl.program_id(1) == 0

    @pl.when(first)
    def _():
        st_ref[...] = jnp.zeros_like(st_ref)

    rs[...] = _shift_lerp(r_ref, rh_ref, mu_r[...], first)
    ks[...] = _shift_lerp(k_ref, kh_ref, mu_k[...], first)
    vs[...] = _shift_lerp(v_ref, vh_ref, mu_v[...], first)
    s2s[...] = _shift_lerp(s2_ref, s2h_ref, mu_s2[...], first)
    s3s[...] = _shift_lerp(s3_ref, s3h_ref, mu_s3[...], first)

    def prepare(pair):
        rows = slice(pair * 2 * CHUNK, (pair + 1) * 2 * CHUNK)
        r, k, v, s2, s3 = rs[rows, :], ks[rows, :], vs[rows, :], s2s[rows, :], s3s[rows, :]
        xw = w0_ref[...] + _bmm(jnp.tanh(s2), wup_ref[...])
        softplus = jnp.maximum(-xw, 0.0) + jnp.log(1.0 + jnp.exp(-jnp.abs(xw)))
        ld = -jnp.exp(-softplus - 0.5)
        af = jax.nn.sigmoid(a0_ref[...] + _bmm(s2, aup_ref[...]))
        g_ref[rows, :] = _bmm(jax.nn.sigmoid(s3), gup_ref[...]).astype(g_ref.dtype)
        kkr = k * kk_ref[...]
        kkn = kkr / jnp.maximum(jnp.sqrt(_head_sum(kkr * kkr, bq_ref)), 1e-12)
        kmod = k * (1.0 + (af - 1.0) * ka_ref[...])
        bonus_ref[rows, :] = (_head_sum(r * kmod * rk_ref[...], bq_ref) * v).astype(bonus_ref.dtype)
        ls[rows, :] = ld
        ks[rows, :] = kmod
        kks[rows, :] = kkn
        kbs[rows, :] = kkn * af

    ri = lax.broadcasted_iota(jnp.int32, (QUAD, QUAD), 0)
    ci = lax.broadcasted_iota(jnp.int32, (QUAD, QUAD), 1)
    same_head = (ri // HEAD_DIM) == (ci // HEAD_DIM)
    strict = jnp.logical_and(same_head, (ri % CHUNK) > (ci % CHUNK))
    incl = jnp.logical_and(same_head, (ri % CHUNK) >= (ci % CHUNK))
    eye = jnp.where(ri == ci, 1.0, 0.0)

    def stack(x):
        return jnp.where(same_head, jnp.concatenate([x] * 4, axis=0), 0.0).astype(BF16)

    def tile4(x):
        return jnp.concatenate([x] * 4, axis=0).astype(BF16)

    def unstack(x):
        return (x[0:CHUNK] + x[CHUNK:2 * CHUNK]) + (x[2 * CHUNK:3 * CHUNK] + x[3 * CHUNK:4 * CHUNK])

    def chunk_operands(c):
        rows = slice(c * CHUNK, (c + 1) * CHUNK)
        ldc = ls[rows, :]
        p1 = ldc.astype(BF16)
        e1 = ldc - p1.astype(F32)
        p2 = e1.astype(BF16)
        p3 = (e1 - p2.astype(F32)).astype(BF16)
        tri = tri_ref[...]
        cl = (_mm(tri, p1) + _mm(tri, p2)) + _mm(tri, p3)
        cl_end = cl[CHUNK - 1:CHUNK, :]
        e_in = jnp.exp(cl)
        e_out = jnp.exp(-cl)
        e_end = jnp.exp(cl_end - cl)
        rt_all = rs[rows, :] * e_in
        at_all = -kks[rows, :] * jnp.exp(cl - ldc)
        bt_all = kbs[rows, :] * e_out
        kt_all = ks[rows, :] * e_out
        bg_all = (kbs[rows, :] * e_end).astype(BF16)
        kg_all = (ks[rows, :] * e_end).astype(BF16)
        v_all = vs[rows, :]
        gam_all = jnp.exp(cl_end)
        units = []
        for q in range(MIX_W // QUAD):
            sl = slice(q * QUAD, (q + 1) * QUAD)
            units.append(dict(rows=rows, sl=sl, q=q, rt=rt_all[:, sl], vv=v_all[:, sl].astype(BF16),
                              a4=stack(at_all[:, sl]), r4=stack(rt_all[:, sl]), v4=stack(v_all[:, sl]),
                              bt4=tile4(bt_all[:, sl]), kt4=tile4(kt_all[:, sl]),
                              bg=bg_all[:, sl], kg=kg_all[:, sl], gam=gam_all[:, sl]))
        return units

    n_doublings = int(math.log2(CHUNK)) - 1

    def chunk_pair(cp):
        us = chunk_operands(2 * cp) + chunk_operands(2 * cp + 1)
        for u in us:
            u["m_ab"] = jnp.where(strict, _nt(u["a4"], u["bt4"]), 0.0)
        for u in us:
            u["m_ak"] = jnp.where(strict, _nt(u["a4"], u["kt4"]), 0.0)
        for u in us:
            u["m_rb"] = jnp.where(incl, _nt(u["r4"], u["bt4"]), 0.0).astype(BF16)
        for u in us:
            u["m_rk"] = jnp.where(incl, _nt(u["r4"], u["kt4"]), 0.0).astype(BF16)
        for u in us:
            u["pw"] = u["m_ab"]
            u["inv"] = eye + u["m_ab"]
        for _ in range(n_doublings):
            for u in us:
                u["pw"] = _bmm(u["pw"], u["pw"])
            for u in us:
                u["inv"] = u["inv"] + _bmm(u["inv"], u["pw"])
        for u in us:
            u["inv"] = u["inv"].astype(BF16)
            u["mv4"] = _bmm(u["m_ak"], u["v4"])
        for u in us:
            u["ah4"] = _mm(u["inv"], u["a4"])
        for u in us:
            u["uh4"] = _bmm(u["inv"], u["mv4"])
        for u in us:
            u["ry"] = (u["rt"] + unstack(_bmm(u["m_rb"], u["ah4"]))).astype(BF16)
        for u in us:
            u["y0"] = unstack(_bmm(u["m_rb"], u["uh4"]) + _mm(u["m_rk"], u["v4"]))
        for u in us:
            ah, uh = unstack(u["ah4"]).astype(BF16), unstack(u["uh4"]).astype(BF16)
            u["g_low"] = jnp.where(same_head, _tn(u["bg"], ah), 0.0).astype(BF16)
            u["h_t"] = jnp.where(same_head, _tn(uh, u["bg"]) + _tn(u["vv"], u["kg"]), 0.0)
        for u in us:
            st = st_ref[u["q"]]
            stb = st.astype(BF16)
            y_ref[u["rows"], u["sl"]] = _nt(u["ry"], stb) + u["y0"]
            st_ref[u["q"]] = st * u["gam"] + _nt(stb, u["g_low"]) + u["h_t"]

    n_pairs = r_ref.shape[0] // (2 * CHUNK)
    prepare(0)
    for pair in range(n_pairs):
        if pair + 1 < n_pairs:
            prepare(pair + 1)
        chunk_pair(pair)


def _rwkv(z, prm, d_model):
    b, s, _ = z.shape
    tm = min(RW_TM, s)
    s2_blk, s3_blk = _z_small_block(d_model, 2), _z_small_block(d_model, 3)
    tok = lambda blk: pl.BlockSpec((None, tm, MIX_W), lambda bi, i: (bi, i, blk))
    tok128 = lambda blk: pl.BlockSpec((None, tm, LANES), lambda bi, i: (bi, i, blk))
    const = lambda a: pl.BlockSpec(a.shape, lambda bi, i: (0,) * a.ndim)
    consts = [prm[n] for n in ("mu_r", "mu_k", "mu_v", "mu_s2", "mu_s3", "w0", "w_up", "a0", "a_up", "g_up",
                               "k_k", "k_a", "r_k", "bones_q", "tri")]
    out_spec = pl.BlockSpec((None, tm, MIX_W), lambda bi, i: (bi, i, 0))
    return pl.pallas_call(
        _rwkv_kernel,
        grid=(b, s // tm),
        in_specs=[tok(Z_RR), tok(Z_RK), tok(Z_RV), tok128(s2_blk), tok128(s3_blk),
                  _halo_spec(MIX_W, Z_RR, tm, BF16_ROWS), _halo_spec(MIX_W, Z_RK, tm, BF16_ROWS),
                  _halo_spec(MIX_W, Z_RV, tm, BF16_ROWS), _halo_spec(LANES, s2_blk, tm, BF16_ROWS),
                  _halo_spec(LANES, s3_blk, tm, BF16_ROWS)] + [const(a) for a in consts],
        out_specs=[out_spec, out_spec, out_spec],
        out_shape=[jax.ShapeDtypeStruct((b, s, MIX_W), F32), jax.ShapeDtypeStruct((b, s, MIX_W), BF16),
                   jax.ShapeDtypeStruct((b, s, MIX_W), BF16)],
        scratch_shapes=([pltpu.VMEM((MIX_W // QUAD, QUAD, QUAD), F32)] + [pltpu.VMEM((tm, MIX_W), F32)] * 6
                        + [pltpu.VMEM((tm, LANES), F32)] * 2),
        compiler_params=_cparams(("parallel", "arbitrary")),
        name="rwkv",
    )(z, z, z, z, z, z, z, z, z, z, *consts)


def _causal_conv(p, halo, cw_ref):
    nh = halo.shape[0]
    rowid = lax.broadcasted_iota(jnp.int32, (SUBLANES, p.shape[1]), 0)
    conv = p * cw_ref[CONV_W - 1:CONV_W, :]
    for back in range(1, CONV_W):
        rolled = pltpu.roll(p, back, axis=0)
        head = rolled[0:SUBLANES]
        for rr in range(back):
            head = jnp.where(rowid == rr, halo[nh - back + rr:nh - back + rr + 1, :], head)
        rolled = jnp.concatenate([head, rolled[SUBLANES:]], axis=0)
        conv = conv + rolled * cw_ref[CONV_W - 1 - back:CONV_W - back, :]
    return conv


def _merge_kernel(x_ref, oa_ref, y_ref, g_ref, bonus_ref, cb_ref, cc_ref, cx_ref, cch_ref, cxh_ref,
                  gate0_ref, gate1_ref, gate2_ref, g1_ref, lnw_ref, lnb_ref, bq_ref, cw_ref, wb_ref, wo_ref, o_ref,
                  merged_ref):
    first = pl.program_id(1) == 0
    y = y_ref[...]
    inv_n = 1.0 / HEAD_DIM
    p1 = y.astype(BF16)
    mean = (_head_sum(p1, bq_ref) + _head_sum(y - p1.astype(F32), bq_ref)) * inv_n
    yc = y - mean
    var = _head_sum(yc * yc, bq_ref) * inv_n
    o_rwkv = ((yc * lax.rsqrt(var + GN_EPS) * lnw_ref[...] + lnb_ref[...] + bonus_ref[...].astype(F32))
              * g_ref[...].astype(F32))
    p = cc_ref[...].astype(F32) * cx_ref[...].astype(F32)
    ph = jnp.where(first, 0.0, cch_ref[...].astype(F32) * cxh_ref[...].astype(F32))
    o_conv = cb_ref[...].astype(F32) * _causal_conv(p, ph, cw_ref)
    branches = ((oa_ref[...], gate0_ref), (o_rwkv.astype(BF16), gate1_ref), (o_conv.astype(BF16), gate2_ref))
    d = x_ref.shape[-1]
    for c in range(d // QUAD):
        cols = slice(c * QUAD, (c + 1) * QUAD)
        part = jnp.zeros((x_ref.shape[0], QUAD), F32)
        for bi, (o, gate_ref) in enumerate(branches):
            part = part + jax.nn.sigmoid(gate_ref[:, cols].astype(F32)) * _mm(o, wb_ref[bi, :, cols])
        merged_ref[:, cols] = part.astype(BF16)
    o_ref[...] = x_ref[...] + g1_ref[...] * _mm(merged_ref[...], wo_ref[...])


def _merge(x, o_attn, y, g, bonus, z, mod5, layer, prm, tm=512):
    b, s, d = x.shape
    tm = min(tm, s)
    tok = lambda w, blk: pl.BlockSpec((None, tm, w), lambda bi, i: (bi, i, blk))
    const = lambda a: pl.BlockSpec(a.shape, lambda bi, i: (0,) * a.ndim)
    consts = [prm[n] for n in ("ln_w", "ln_b", "bones_q", "conv_w", "w_branch", "w_o")]
    gate_blk = Z_GATE_COL // d
    return pl.pallas_call(
        _merge_kernel,
        grid=(b, s // tm),
        in_specs=[tok(d, 0), tok(MIX_W, 0), tok(MIX_W, 0), tok(MIX_W, 0), tok(MIX_W, 0),
                  tok(MIX_W, Z_CB), tok(MIX_W, Z_CC), tok(MIX_W, Z_CX),
                  _halo_spec(MIX_W, Z_CC, tm, BF16_ROWS), _halo_spec(MIX_W, Z_CX, tm, BF16_ROWS),
                  tok(d, gate_blk), tok(d, gate_blk + 1), tok(d, gate_blk + 2),
                  _mod_spec(mod5, layer, 2, 2)] + [const(a) for a in consts],
        out_specs=tok(d, 0),
        out_shape=jax.ShapeDtypeStruct((b, s, d), F32),
        scratch_shapes=[pltpu.VMEM((tm, d), BF16)],
        compiler_params=_cparams(("parallel", "parallel")),
        name="merge",
    )(x, o_attn, y, g, bonus, z, z, z, z, z, z, z, z, mod5, *consts)


def _ffn_kernel(x_ref, xh_ref, sh_ref, sc_ref, g2_ref, gain_ref, wup_ref, cw_ref, wd_ref, fin_ref, o_ref,
                *, final_norm, tf):
    first = pl.program_id(1) == 0
    x = x_ref[...]
    h = _norm_mod(x, gain_ref[...], sc_ref[...], sh_ref[...]).astype(BF16)
    hh = _norm_mod(xh_ref[...], gain_ref[...], sc_ref[...], sh_ref[...]).astype(BF16)
    h_ext = jnp.concatenate([hh, h], axis=0)
    d_ff = wd_ref.shape[0]
    acc = jnp.zeros(x.shape, F32)
    for jf in range(d_ff // tf):
        cols = slice(jf * tf, (jf + 1) * tf)
        gate_cols = slice(d_ff + jf * tf, d_ff + (jf + 1) * tf)
        a_ext = _mm(h_ext, wup_ref[:, cols])
        a = a_ext[BF16_ROWS:]
        ah = jnp.where(first, 0.0, a_ext[BF16_ROWS - SUBLANES:BF16_ROWS])
        conv = _causal_conv(a, ah, cw_ref[:, cols])
        u = conv * jax.nn.sigmoid(conv) * _mm(h, wup_ref[:, gate_cols])
        acc = acc + _mm(u.astype(BF16), wd_ref[cols, :])
    out = x + g2_ref[...] * acc
    if final_norm:
        out = out * lax.rsqrt(jnp.mean(out * out, axis=-1, keepdims=True) + NORM_EPS) * fin_ref[...]
    o_ref[...] = out


def _ffn(x, mod5, layer, gain, w_up, conv_w, w_down, final_gain, final_norm, tm=512, tf=1408):
    b, s, d = x.shape
    d_ff = w_down.shape[0]
    assert d_ff % tf == 0 and tf % LANES == 0
    tm = min(tm, s)
    gain, final_gain = gain.reshape(1, d), final_gain.reshape(1, d)
    return pl.pallas_call(
        functools.partial(_ffn_kernel, final_norm=final_norm, tf=tf),
        grid=(b, s // tm),
        in_specs=[pl.BlockSpec((None, tm, d), lambda bi, i: (bi, i, 0)),
                  _halo_spec(d, 0, tm, BF16_ROWS),
                  _mod_spec(mod5, layer, 3, 2), _mod_spec(mod5, layer, 4, 2), _mod_spec(mod5, layer, 5, 2),
                  _resident(gain), _resident(w_up), _resident(conv_w), _resident(w_down), _resident(final_gain)],
        out_specs=pl.BlockSpec((None, tm, d), lambda bi, i: (bi, i, 0)),
        out_shape=jax.ShapeDtypeStruct((b, s, d), F32),
        compiler_params=_cparams(("parallel", "parallel")),
        name="ffn",
    )(x, x, mod5, mod5, mod5, gain, w_up, conv_w, w_down, final_gain)


def _split_w_in(w):
    sizes = (MIX_W, MIX_W, MIX_W, HEADS * HEAD_DIM, HEAD_DIM, HEADS,
             MIX_W, MIX_W, MIX_W, LORA_DECAY, LORA_ICLR, LORA_GATE,
             MIX_W, MIX_W, MIX_W)
    out, o = [], 0
    for n in sizes:
        out.append(w[:, o:o + n])
        o += n
    out.append(w[:, o:])
    return out


def _layer_params(l, w_in, rwkv_mu, rwkv_w0, rwkv_w_up, rwkv_a0, rwkv_a_up, rwkv_g_up, rwkv_k_k, rwkv_k_a,
                  rwkv_r_k, rwkv_ln_w, rwkv_ln_b, sc_conv_w, w_branch, w_o):
    d = w_in.shape[1]
    q, k, v, qi, ki, wi, rr, rk, rv, wd, ad, gd, cb, cc, cx, gates = _split_w_in(w_in[l])
    assert Z_GATE_COL % d == 0 and gates.shape[1] == 3 * d
    w_z = jnp.concatenate([q, k, v, qi, rr, rk, rv, cb, cc, cx, gates,
                           ki, ki, wi, jnp.zeros((d, LANES - HEADS), F32), wd, ad, gd], axis=1).astype(BF16)
    mu = rwkv_mu[l]
    row = lambda a: a.reshape(1, -1)
    head_id = jnp.arange(QUAD) // HEAD_DIM
    bones_q = (head_id[:, None] == head_id[None, :]).astype(BF16)
    tri = (jnp.arange(CHUNK)[:, None] >= jnp.arange(CHUNK)[None, :]).astype(BF16)
    zl = lambda n: jnp.zeros((n, MIX_W), F32)
    return dict(
        w_z=w_z,
        mu_r=row(mu[:MIX_W]), mu_k=row(mu[MIX_W:2 * MIX_W]), mu_v=row(mu[2 * MIX_W:3 * MIX_W]),
        mu_s2=row(mu[3 * MIX_W:3 * MIX_W + LORA_DECAY + LORA_ICLR]), mu_s3=row(mu[3 * MIX_W + LORA_DECAY + LORA_ICLR:]),
        w0=row(rwkv_w0[l]), w_up=jnp.concatenate([rwkv_w_up[l], zl(LORA_ICLR)], axis=0).astype(BF16),
        a0=row(rwkv_a0[l]), a_up=jnp.concatenate([zl(LORA_DECAY), rwkv_a_up[l]], axis=0).astype(BF16),
        g_up=rwkv_g_up[l].astype(BF16), k_k=row(rwkv_k_k[l]), k_a=row(rwkv_k_a[l]), r_k=row(rwkv_r_k[l]),
        bones_q=bones_q, tri=tri,
        ln_w=row(rwkv_ln_w[l]), ln_b=row(rwkv_ln_b[l]), conv_w=sc_conv_w[l].T,
        w_branch=w_branch[l].astype(BF16), w_o=w_o[l].astype(BF16),
    )


def kernel(x, c, positions, rel_bias, final_norm, ada_w, ada_b, norm_mix, w_in, rwkv_mu, rwkv_w0, rwkv_w_up,
           rwkv_a0, rwkv_a_up, rwkv_g_up, rwkv_k_k, rwkv_k_a, rwkv_r_k, rwkv_ln_w, rwkv_ln_b, sc_conv_w,
           w_branch, w_o, norm_ffn, ffn_w_up, ffn_conv_w, ffn_w_down):
    depth, d = ada_w.shape[0], x.shape[-1]
    mod5 = _ada_mod(c, ada_w, ada_b)
    bias = _bias_tiles(rel_bias)
    for l in range(depth):
        prm = _layer_params(l, w_in, rwkv_mu, rwkv_w0, rwkv_w_up, rwkv_a0, rwkv_a_up, rwkv_g_up, rwkv_k_k,
                            rwkv_k_a, rwkv_r_k, rwkv_ln_w, rwkv_ln_b, sc_conv_w, w_branch, w_o)
        z = _inproj(x, mod5, l, norm_mix[l], prm["w_z"])
        o_attn = _dsa(z, bias, d)
        y, g, bonus = _rwkv(z, prm, d)
        x = _merge(x, o_attn, y, g, bonus, z, mod5, l, prm)
        x = _ffn(x, mod5, l, norm_ffn[l], ffn_w_up[l].astype(BF16), ffn_conv_w[l].T, ffn_w_down[l].astype(BF16),
                 final_norm, final_norm=(l == depth - 1))
    return x
```

```python
import functools
import math

import jax
import jax.numpy as jnp
from jax import lax
from jax.experimental import pallas as pl
from jax.experimental.pallas import tpu as pltpu

F32 = jnp.float32
BF16 = jnp.bfloat16
HIGHEST = lax.Precision.HIGHEST

MIX_W = 512
HEADS = 8
HEAD_DIM = 64
TOPK_MAX = 256
N_BUCKETS = 32
MAX_DISTANCE = 128
LORA_DECAY = 64
LORA_ICLR = 64
LORA_GATE = 128
CONV_W = 3
NORM_EPS = 1e-6
GN_EPS = 64e-5
NEG_INF = -1e30

LANES = 128
SUBLANES = 8
BF16_ROWS = 16
VMEM_LIMIT = 56 * 1024 * 1024

TQ = 256
CHUNK = 64
RW_TM = 512
QUAD = 4 * HEAD_DIM
assert CHUNK == HEAD_DIM

Z_Q, Z_K, Z_V, Z_QI, Z_RR, Z_RK, Z_RV, Z_CB, Z_CC, Z_CX = range(10)
Z_GATE_COL = 10 * MIX_W


def _z_small_block(d_model, i):
    return (Z_GATE_COL + 3 * d_model) // LANES + i


def _cparams(sem):
    return pltpu.CompilerParams(dimension_semantics=sem, vmem_limit_bytes=VMEM_LIMIT)


def _nt(a, b, precision=None):
    return lax.dot_general(a, b, (((1,), (1,)), ((), ())), precision=precision,
                           preferred_element_type=F32)


def _tn(a, b, precision=None):
    return lax.dot_general(a, b, (((0,), (0,)), ((), ())), precision=precision,
                           preferred_element_type=F32)


def _mm(a, b, precision=None):
    return jnp.dot(a, b, precision=precision, preferred_element_type=F32)


def _bmm(a, b):
    return _mm(a.astype(BF16), b.astype(BF16))


def _mod_kernel(c_ref, w_ref, b_ref, o_ref):
    o_ref[...] = _mm(c_ref[...], w_ref[...], HIGHEST) + b_ref[...]


def _ada_mod(c, ada_w, ada_b):
    depth, d, d6 = ada_w.shape
    b = c.shape[0]
    out = pl.pallas_call(
        _mod_kernel,
        grid=(depth, d6 // d),
        in_specs=[pl.BlockSpec((b, d), lambda l, j: (0, 0)),
                  pl.BlockSpec((None, d, d), lambda l, j: (l, 0, j)),
                  pl.BlockSpec((None, 1, d), lambda l, j: (l, 0, j))],
        out_specs=pl.BlockSpec((None, b, d), lambda l, j: (l, 0, j)),
        out_shape=jax.ShapeDtypeStruct((depth, b, d6), F32),
        compiler_params=_cparams(("parallel", "parallel")),
        name="ada_mod",
    )(c, ada_w, ada_b.reshape(depth, 1, d6))
    return out.reshape(depth, b, d6 // d, 1, d)


def _mod_spec(mod5, layer, which, ngrid):
    d = mod5.shape[-1]
    if ngrid == 2:
        return pl.BlockSpec((None, None, None, 1, d), lambda b, i: (layer, b, which, 0, 0))
    return pl.BlockSpec((None, None, None, 1, d), lambda b, i, j: (layer, b, which, 0, 0))


def _norm_mod(x, gain, scale, shift):
    y = x * lax.rsqrt(jnp.mean(x * x, axis=-1, keepdims=True) + NORM_EPS) * gain
    return y * (1.0 + scale) + shift


def _resident(a):
    return pl.BlockSpec(a.shape, lambda bi, i: (0,) * a.ndim, pipeline_mode=pl.Buffered(1))


def _inproj_kernel(x_ref, sh_ref, sc_ref, g_ref, w_ref, o_ref, *, n_col_chunks):
    h = _norm_mod(x_ref[...], g_ref[...], sc_ref[...], sh_ref[...]).astype(BF16)
    tn = w_ref.shape[1] // n_col_chunks
    for c in range(n_col_chunks):
        o_ref[:, c * tn:(c + 1) * tn] = _mm(h, w_ref[:, c * tn:(c + 1) * tn]).astype(o_ref.dtype)


def _inproj(x, mod5, layer, gain, w, tm=512, n_col_chunks=4):
    b, s, d = x.shape
    zc = w.shape[1]
    tm = min(tm, s)
    assert zc % (n_col_chunks * LANES) == 0
    gain = gain.reshape(1, d)
    return pl.pallas_call(
        functools.partial(_inproj_kernel, n_col_chunks=n_col_chunks),
        grid=(b, s // tm),
        in_specs=[pl.BlockSpec((None, tm, d), lambda bi, i: (bi, i, 0)),
                  _mod_spec(mod5, layer, 0, 2),
                  _mod_spec(mod5, layer, 1, 2),
                  _resident(gain), _resident(w)],
        out_specs=pl.BlockSpec((None, tm, zc), lambda bi, i: (bi, i, 0)),
        out_shape=jax.ShapeDtypeStruct((b, s, zc), BF16),
        compiler_params=_cparams(("parallel", "parallel")),
        name="inproj",
    )(x, mod5, mod5, gain, w)


def _bias_kernel(rb_ref, o_ref):
    which = pl.program_id(0)
    h = pl.program_id(1)
    ri = lax.broadcasted_iota(jnp.int32, (TQ, TQ), 0)
    ci = lax.broadcasted_iota(jnp.int32, (TQ, TQ), 1)
    dist = ci - ri + (1 - which) * TQ
    n = jnp.maximum(dist, 0)
    max_exact = N_BUCKETS // 2
    nf = jnp.maximum(n, 1).astype(F32)
    large = max_exact + (jnp.log(nf / max_exact) / math.log(MAX_DISTANCE / max_exact)
                         * (N_BUCKETS - max_exact)).astype(jnp.int32)
    large = jnp.minimum(large, N_BUCKETS - 1)
    bucket = jnp.where(n < max_exact, n, large)
    far = rb_ref[N_BUCKETS - 1, h]
    acc = jnp.zeros((TQ, TQ), F32)
    for bkt in range(N_BUCKETS - 1):
        acc = jnp.where(bucket == bkt, rb_ref[bkt, h] - far, acc)
    o_ref[...] = acc * LOG2E


def _bias_tiles(rel_bias):
    assert TQ >= MAX_DISTANCE
    return pl.pallas_call(
        _bias_kernel,
        grid=(2, HEADS),
        in_specs=[pl.BlockSpec(memory_space=pltpu.SMEM)],
        out_specs=pl.BlockSpec((None, None, TQ, TQ), lambda w, h: (w, h, 0, 0)),
        out_shape=jax.ShapeDtypeStruct((2, HEADS, TQ, TQ), F32),
        compiler_params=_cparams(("parallel", "parallel")),
        name="bias_tiles",
    )(rel_bias)


BISECT_MAX_IT = 300
COARSE_STEPS = 10
FINE_STEPS_UNCHECKED = 4
FINE_STEPS_PER_CHECK = 2
LOG2E = math.log2(math.e)
ONES_ROWS = BF16_ROWS


def _row_groups(t):
    return [t[r * SUBLANES:(r + 1) * SUBLANES, :] for r in range(t.shape[0] // SUBLANES)]


def _dsa_kernel(q_ref, k_ref, v_ref, qi_ref, ki_ref, wi_ref, bias_ref, o_ref,
                sc_ref, scb_ref, qm_ref, qim_ref, wt_ref, vt_ref, m_ref, l_ref, acc_ref, s_ref, tmax_ref,
                *, n_keep, seq):
    j = pl.program_id(1)
    nt = j + 1
    kf = float(n_keep)
    lane = lax.broadcasted_iota(jnp.int32, (TQ, LANES), 1)
    att_scale = HEAD_DIM ** -0.5 * LOG2E
    w_scale = (HEADS ** -0.5) * (HEAD_DIM ** -0.5)

    for h in range(HEADS):
        p, odd = divmod(h, 2)
        hm = (lane >= HEAD_DIM) if odd else (lane < HEAD_DIM)
        qs = q_ref[:, p * LANES:(p + 1) * LANES]
        qm_ref[h] = jnp.where(hm, qs, jnp.zeros_like(qs)) * att_scale
        qis = qi_ref[:, p * LANES:(p + 1) * LANES]
        qim_ref[h] = jnp.where(hm, qis, jnp.zeros_like(qis))
    wt_ref[...] = wi_ref[...].astype(F32).T * w_scale
    v_t = v_ref[...].T
    for h in range(HEADS):
        vt_ref[j, h, 0:HEAD_DIM, :] = v_t[h * HEAD_DIM:(h + 1) * HEAD_DIM, :]
        vt_ref[j, h, HEAD_DIM:HEAD_DIM + ONES_ROWS, :] = jnp.ones((ONES_ROWS, TQ), BF16)

    key = lax.broadcasted_iota(jnp.int32, (TQ, TQ), 0)
    qry = lax.broadcasted_iota(jnp.int32, (TQ, TQ), 1) + j * TQ

    def floor_bf16(x):
        xi = lax.bitcast_convert_type(x, jnp.int32)
        xi = jnp.where(xi < 0, xi + jnp.int32(0xFFFF), xi) & jnp.int32(-0x10000)
        return lax.bitcast_convert_type(xi, F32).astype(BF16)

    def score_tiles(kt0, n):
        ki_t = ki_ref[pl.ds(pl.multiple_of(kt0 * TQ, TQ), n * TQ), :]
        acc = jnp.zeros((n * TQ, TQ), F32)
        for h in range(HEADS):
            acc = acc + jnp.maximum(_nt(ki_t, qim_ref[h]), 0.0) * wt_ref[h:h + 1, :]
        for i in range(n):
            sc_ref[kt0 + i] = jnp.where(key + (kt0 + i) * TQ <= qry, acc[i * TQ:(i + 1) * TQ], NEG_INF)

    def stats_tile(kt, c):
        mx, mn, mp, cp, cn = c
        score = sc_ref[kt]
        scb_ref[kt] = floor_bf16(score)
        for g in _row_groups(score):
            pos = g > 0.0
            mx = jnp.maximum(mx, g)
            mn = jnp.minimum(mn, jnp.where(g > 0.5 * NEG_INF, g, -NEG_INF))
            mp = jnp.minimum(mp, jnp.where(pos, g, -NEG_INF))
            cp = cp + jnp.where(pos, 1.0, 0.0)
            cn = cn + jnp.where(g >= 0.0, 1.0, 0.0)
        return mx, mn, mp, cp, cn

    def idx_tile(kt, c):
        c = stats_tile(kt - 1, c)
        score_tiles(kt, 1)
        return c

    part = lambda v: jnp.full((SUBLANES, TQ), v, F32)
    score_tiles(0, 1)
    stats = lax.fori_loop(1, nt, idx_tile, (part(NEG_INF), part(-NEG_INF), part(-NEG_INF), part(0.0), part(0.0)))
    mx, mn, mp, cp, cn = stats_tile(nt - 1, stats)

    rmax = jnp.max(mx, axis=0, keepdims=True)
    rmin = jnp.min(mn, axis=0, keepdims=True)
    minpos = jnp.min(mp, axis=0, keepdims=True)
    cpos = jnp.sum(cp, axis=0, keepdims=True)
    cnn = jnp.sum(cn, axis=0, keepdims=True)
    nvalid = (lax.broadcasted_iota(jnp.int32, (1, TQ), 1) + j * TQ + 1).astype(F32)
    small = nvalid <= kf

    def tile_loop(body, init):
        n_pairs = lax.shift_right_logical(nt, 1)
        c = lax.fori_loop(0, n_pairs, lambda i, c: body(2 * i + 1, body(2 * i, c)), init)
        return lax.fori_loop(2 * n_pairs, nt, body, c)

    def count_ge(thr):
        def body(kt, acc):
            for g in _row_groups(sc_ref[kt]):
                acc = acc + jnp.where(g >= thr, 1.0, 0.0)
            return acc
        acc = tile_loop(body, jnp.zeros((SUBLANES, TQ), F32))
        return jnp.sum(acc, axis=0, keepdims=True)

    ztie = jnp.logical_and(cpos < kf, cnn >= kf)
    pos_side = cpos >= kf
    hi_top = rmax + jnp.maximum(jnp.abs(rmax) * 1e-6, 1e-30)
    lo0 = jnp.where(small, 0.5 * NEG_INF, jnp.where(ztie, 0.0, jnp.where(pos_side, minpos, rmin)))
    hi0 = jnp.where(small, -NEG_INF, jnp.where(ztie, minpos, jnp.where(pos_side, hi_top, 0.0)))
    clo0 = jnp.where(jnp.logical_or(small, ztie), jnp.where(small, nvalid, cnn), jnp.where(pos_side, cpos, nvalid))
    chi0 = jnp.where(small, 0.0, jnp.where(ztie, cpos, jnp.where(pos_side, 0.0, cnn)))
    done0 = jnp.where(jnp.logical_or(jnp.logical_or(small, ztie), clo0 == kf), 1.0, 0.0)

    def bisect_step(state, mid, cnt, usable):
        lo, hi, clo, chi, done = state
        act = jnp.logical_and(done < 0.5, usable)
        ge = cnt >= kf
        up_lo = jnp.logical_and(act, ge)
        up_hi = jnp.logical_and(act, jnp.logical_not(ge))
        lo = jnp.where(up_lo, mid, lo)
        clo = jnp.where(up_lo, cnt, clo)
        hi = jnp.where(up_hi, mid, hi)
        chi = jnp.where(up_hi, cnt, chi)
        done = jnp.where(jnp.logical_and(act, cnt == kf), 1.0, done)
        return lo, hi, clo, chi, done

    def count_ge_bf16(thr):
        one, zero = jnp.ones((), BF16), jnp.zeros((), BF16)

        def body(kt, acc):
            t = scb_ref[kt]
            part = jnp.zeros((BF16_ROWS, TQ), BF16)
            for r in range(TQ // BF16_ROWS):
                part = part + jnp.where(t[r * BF16_ROWS:(r + 1) * BF16_ROWS, :] >= thr, one, zero)
            return acc + part.astype(F32)
        acc = tile_loop(body, jnp.zeros((BF16_ROWS, TQ), F32))
        return jnp.sum(acc, axis=0, keepdims=True)

    def coarse_step(_, state):
        lo, hi = state[0], state[1]
        mid_b = (lo + 0.5 * (hi - lo)).astype(BF16)
        mid = mid_b.astype(F32)
        usable = jnp.logical_and(mid > lo, mid < hi)
        return bisect_step(state, mid, count_ge_bf16(mid_b), usable)

    state = lax.fori_loop(0, COARSE_STEPS, coarse_step, (lo0, hi0, clo0, chi0, done0))

    def fine_step(_, state):
        lo, hi = state[0], state[1]
        mid = lo + 0.5 * (hi - lo)
        stalled = jnp.logical_or(mid <= lo, mid >= hi)
        state = bisect_step(state, mid, count_ge(mid), jnp.logical_not(stalled))
        return state[:4] + (jnp.where(stalled, 1.0, state[4]),)

    state = lax.fori_loop(0, FINE_STEPS_UNCHECKED, fine_step, state)

    def bis_cond(c):
        return jnp.logical_and(c[0] < BISECT_MAX_IT, c[2] > 0.0)

    def bis_body(c):
        it, state, _ = c
        state = lax.fori_loop(0, FINE_STEPS_PER_CHECK, fine_step, state)
        return it + FINE_STEPS_PER_CHECK, state, jnp.sum(1.0 - state[4])

    _, (lo, hi, clo, chi, _), _ = lax.while_loop(
        bis_cond, bis_body, (jnp.int32(0), state, jnp.sum(1.0 - state[4])))

    tie = jnp.logical_and(clo > kf, jnp.logical_not(small))
    band_quota = jnp.where(tie, kf - chi, float(seq))
    prefix_ones = jnp.where(key >= lax.broadcasted_iota(jnp.int32, (TQ, TQ), 1), 1.0, 0.0).astype(BF16)

    def mask_tiles(kts, before):
        tiles = [sc_ref[kt] for kt in kts]
        bands = [jnp.where(t >= lo, jnp.where(t < hi, 1.0, 0.0), 0.0).astype(BF16) for t in tiles]
        ranks = [_mm(prefix_ones, band) for band in bands]
        masks = []
        for t, rank in zip(tiles, ranks):
            rank = rank + before
            keep_band = jnp.where(rank <= band_quota, 0.0, NEG_INF)
            masks.append(jnp.where(t >= lo, jnp.where(t >= hi, 0.0, keep_band), NEG_INF))
            before = rank[TQ - 1:TQ, :]
        for kt, mask in zip(kts, masks):
            sc_ref[kt] = mask
        return before

    n_pairs = lax.shift_right_logical(nt, 1)
    before = lax.fori_loop(0, n_pairs, lambda i, c: mask_tiles([2 * i, 2 * i + 1], c), jnp.zeros((1, TQ), F32))
    lax.fori_loop(2 * n_pairs, nt, lambda kt, c: mask_tiles([kt], c), before)

    m_ref[...] = jnp.full(m_ref.shape, NEG_INF, F32)
    l_ref[...] = jnp.zeros(l_ref.shape, F32)
    acc_ref[...] = jnp.zeros(acc_ref.shape, F32)

    def logits_phase(kts, near, buf):
        n = len(kts)
        rows = pl.ds(pl.multiple_of(kts[0] * TQ, TQ), n * TQ)
        masks = [sc_ref[kt] for kt in kts]
        for h in range(HEADS):
            p = h // 2
            qk = _nt(k_ref[rows, p * LANES:(p + 1) * LANES], qm_ref[h])
            tile_max = None
            for i, kt in enumerate(kts):
                s = qk[i * TQ:(i + 1) * TQ] + masks[i]
                if near:
                    s = s + bias_ref[kt - j + 1, h]
                s_ref[buf, i, h] = s
                mx = jnp.max(s, axis=0, keepdims=True)
                tile_max = mx if i == 0 else jnp.maximum(tile_max, mx)
            tmax_ref[buf, h] = tile_max

    def softmax_phase(kts, buf):
        for h in range(HEADS):
            m_old = m_ref[h]
            m_new = jnp.maximum(m_old, tmax_ref[buf, h])
            alpha = jnp.exp2(m_old - m_new)
            pv = None
            for i, kt in enumerate(kts):
                part = _mm(vt_ref[kt, h], jnp.exp2(s_ref[buf, i, h] - m_new).astype(BF16))
                pv = part if i == 0 else pv + part
            m_ref[h] = m_new
            l_ref[h] = alpha * l_ref[h] + pv[HEAD_DIM:HEAD_DIM + 1, :]
            acc_ref[h] = alpha * acc_ref[h] + pv[0:HEAD_DIM, :]

    n_far = jnp.maximum(j - 1, 0)

    def attn_tiles(kts, near):
        logits_phase(kts, near, 0)
        softmax_phase(kts, 0)

    n_far_pairs = lax.shift_right_logical(n_far, 1)

    def far_pair(i, carry):
        attn_tiles([2 * i, 2 * i + 1], near=False)
        return carry

    def far_single(kt, carry):
        attn_tiles([kt], near=False)
        return carry

    lax.fori_loop(0, n_far_pairs, far_pair, 0)
    lax.fori_loop(2 * n_far_pairs, n_far, far_single, 0)

    @pl.when(j >= 1)
    def _():
        attn_tiles([j - 1, j], near=True)

    @pl.when(j == 0)
    def _():
        attn_tiles([j], near=True)

    for p in range(HEADS // 2):
        pair = jnp.concatenate([acc_ref[2 * p] / l_ref[2 * p], acc_ref[2 * p + 1] / l_ref[2 * p + 1]], axis=0)
        o_ref[:, p * LANES:(p + 1) * LANES] = pair.T.astype(o_ref.dtype)


def _dsa(z, bias, d_model):
    b, s, _ = z.shape
    n_keep = min(TOPK_MAX, s // 4)
    kern = functools.partial(_dsa_kernel, n_keep=n_keep, seq=s)
    ki_blk, wi_blk = _z_small_block(d_model, 0), _z_small_block(d_model, 1)
    return pl.pallas_call(
        kern,
        grid=(b, s // TQ),
        in_specs=[pl.BlockSpec((None, TQ, MIX_W), lambda bi, j: (bi, j, Z_Q)),
                  pl.BlockSpec((None, s, MIX_W), lambda bi, j: (bi, 0, Z_K)),
                  pl.BlockSpec((None, TQ, MIX_W), lambda bi, j: (bi, j, Z_V)),
                  pl.BlockSpec((None, TQ, MIX_W), lambda bi, j: (bi, j, Z_QI)),
                  pl.BlockSpec((None, s, LANES), lambda bi, j: (bi, 0, ki_blk)),
                  pl.BlockSpec((None, TQ, LANES), lambda bi, j: (bi, j, wi_blk)),
                  pl.BlockSpec((2, HEADS, TQ, TQ), lambda bi, j: (0, 0, 0, 0))],
        out_specs=pl.BlockSpec((None, TQ, MIX_W), lambda bi, j: (bi, j, 0)),
        out_shape=jax.ShapeDtypeStruct((b, s, MIX_W), BF16),
        scratch_shapes=[pltpu.VMEM((s // TQ, TQ, TQ), F32),
                        pltpu.VMEM((s // TQ, TQ, TQ), BF16),
                        pltpu.VMEM((HEADS, TQ, LANES), BF16),
                        pltpu.VMEM((HEADS, TQ, LANES), BF16),
                        pltpu.VMEM((LANES, TQ), F32),
                        pltpu.VMEM((s // TQ, HEADS, HEAD_DIM + ONES_ROWS, TQ), BF16),
                        pltpu.VMEM((HEADS, 1, TQ), F32),
                        pltpu.VMEM((HEADS, 1, TQ), F32),
                        pltpu.VMEM((HEADS, HEAD_DIM, TQ), F32),
                        pltpu.VMEM((1, 2, HEADS, TQ, TQ), F32),
                        pltpu.VMEM((1, HEADS, 1, TQ), F32)],
        compiler_params=_cparams(("parallel", "arbitrary")),
        name="dsa",
    )(z, z, z, z, z, z, bias)


def _halo_spec(width, blk, tm, rows):
    step = tm // rows
    return pl.BlockSpec((None, rows, width), lambda bi, i: (bi, jnp.maximum(i * step - 1, 0), blk))


def _shift_lerp(cur_ref, halo_ref, mu, first):
    cur = cur_ref[...].astype(F32)
    nh = halo_ref.shape[0]
    prev_last = jnp.where(first, 0.0, halo_ref[nh - 1:nh, :].astype(F32))
    rolled = pltpu.roll(cur, 1, axis=0)
    rowid = lax.broadcasted_iota(jnp.int32, cur.shape, 0)
    sh = jnp.where(rowid == 0, prev_last, rolled)
    return cur + (sh - cur) * mu


def _head_sum(x, bq_ref):
    xb = x.astype(BF16)
    return jnp.concatenate([_mm(xb[:, q * QUAD:(q + 1) * QUAD], bq_ref[...]) for q in range(MIX_W // QUAD)], axis=1)


def _rwkv_kernel(r_ref, k_ref, v_ref, s2_ref, s3_ref, rh_ref, kh_ref, vh_ref, s2h_ref, s3h_ref,
                 mu_r, mu_k, mu_v, mu_s2, mu_s3, w0_ref, wup_ref, a0_ref, aup_ref, gup_ref,
                 kk_ref, ka_ref, rk_ref, bq_ref, tri_ref,
                 y_ref, g_ref, bonus_ref,
                 st_ref, rs, ls, ks, vs, kks, kbs, s2s, s3s):
    first = pl.program_id(1) == 0

    @pl.when(first)
    def _():
        st_ref[...] = jnp.zeros_like(st_ref)

    rs[...] = _shift_lerp(r_ref, rh_ref, mu_r[...], first)
    ks[...] = _shift_lerp(k_ref, kh_ref, mu_k[...], first)
    vs[...] = _shift_lerp(v_ref, vh_ref, mu_v[...], first)
    s2s[...] = _shift_lerp(s2_ref, s2h_ref, mu_s2[...], first)
    s3s[...] = _shift_lerp(s3_ref, s3h_ref, mu_s3[...], first)

    def prepare(pair):
        rows = slice(pair * 2 * CHUNK, (pair + 1) * 2 * CHUNK)
        r, k, v, s2, s3 = rs[rows, :], ks[rows, :], vs[rows, :], s2s[rows, :], s3s[rows, :]
        xw = w0_ref[...] + _bmm(jnp.tanh(s2), wup_ref[...])
        softplus = jnp.maximum(-xw, 0.0) + jnp.log(1.0 + jnp.exp(-jnp.abs(xw)))
        ld = -jnp.exp(-softplus - 0.5)
        af = jax.nn.sigmoid(a0_ref[...] + _bmm(s2, aup_ref[...]))
        g_ref[rows, :] = _bmm(jax.nn.sigmoid(s3), gup_ref[...]).astype(g_ref.dtype)
        kkr = k * kk_ref[...]
        kkn = kkr / jnp.maximum(jnp.sqrt(_head_sum(kkr * kkr, bq_ref)), 1e-12)
        kmod = k * (1.0 + (af - 1.0) * ka_ref[...])
        bonus_ref[rows, :] = (_head_sum(r * kmod * rk_ref[...], bq_ref) * v).astype(bonus_ref.dtype)
        ls[rows, :] = ld
        ks[rows, :] = kmod
        kks[rows, :] = kkn
        kbs[rows, :] = kkn * af

    ri = lax.broadcasted_iota(jnp.int32, (QUAD, QUAD), 0)
    ci = lax.broadcasted_iota(jnp.int32, (QUAD, QUAD), 1)
    same_head = (ri // HEAD_DIM) == (ci // HEAD_DIM)
    strict = jnp.logical_and(same_head, (ri % CHUNK) > (ci % CHUNK))
    incl = jnp.logical_and(same_head, (ri % CHUNK) >= (ci % CHUNK))
    eye = jnp.where(ri == ci, 1.0, 0.0)

    def stack(x):
        return jnp.where(same_head, jnp.concatenate([x] * 4, axis=0), 0.0).astype(BF16)

    def tile4(x):
        return jnp.concatenate([x] * 4, axis=0).astype(BF16)

    def unstack(x):
        return (x[0:CHUNK] + x[CHUNK:2 * CHUNK]) + (x[2 * CHUNK:3 * CHUNK] + x[3 * CHUNK:4 * CHUNK])

    def chunk_operands(c):
        rows = slice(c * CHUNK, (c + 1) * CHUNK)
        ldc = ls[rows, :]
        p1 = ldc.astype(BF16)
        e1 = ldc - p1.astype(F32)
        p2 = e1.astype(BF16)
        p3 = (e1 - p2.astype(F32)).astype(BF16)
        tri = tri_ref[...]
        cl = (_mm(tri, p1) + _mm(tri, p2)) + _mm(tri, p3)
        cl_end = cl[CHUNK - 1:CHUNK, :]
        e_in = jnp.exp(cl)
        e_out = jnp.exp(-cl)
        e_end = jnp.exp(cl_end - cl)
        rt_all = rs[rows, :] * e_in
        at_all = -kks[rows, :] * jnp.exp(cl - ldc)
        bt_all = kbs[rows, :] * e_out
        kt_all = ks[rows, :] * e_out
        bg_all = (kbs[rows, :] * e_end).astype(BF16)
        kg_all = (ks[rows, :] * e_end).astype(BF16)
        v_all = vs[rows, :]
        gam_all = jnp.exp(cl_end)
        units = []
        for q in range(MIX_W // QUAD):
            sl = slice(q * QUAD, (q + 1) * QUAD)
            units.append(dict(rows=rows, sl=sl, q=q, rt=rt_all[:, sl], vv=v_all[:, sl].astype(BF16),
                              a4=stack(at_all[:, sl]), r4=stack(rt_all[:, sl]), v4=stack(v_all[:, sl]),
                              bt4=tile4(bt_all[:, sl]), kt4=tile4(kt_all[:, sl]),
                              bg=bg_all[:, sl], kg=kg_all[:, sl], gam=gam_all[:, sl]))
        return units

    n_doublings = int(math.log2(CHUNK)) - 1

    def chunk_pair(cp):
        us = chunk_operands(2 * cp) + chunk_operands(2 * cp + 1)
        for u in us:
            u["m_ab"] = jnp.where(strict, _nt(u["a4"], u["bt4"]), 0.0)
        for u in us:
            u["m_ak"] = jnp.where(strict, _nt(u["a4"], u["kt4"]), 0.0)
        for u in us:
            u["m_rb"] = jnp.where(incl, _nt(u["r4"], u["bt4"]), 0.0).astype(BF16)
        for u in us:
            u["m_rk"] = jnp.where(incl, _nt(u["r4"], u["kt4"]), 0.0).astype(BF16)
        for u in us:
            u["pw"] = u["m_ab"]
            u["inv"] = eye + u["m_ab"]
        for _ in range(n_doublings):
            for u in us:
                u["pw"] = _bmm(u["pw"], u["pw"])
            for u in us:
                u["inv"] = u["inv"] + _bmm(u["inv"], u["pw"])
        for u in us:
            u["inv"] = u["inv"].astype(BF16)
            u["mv4"] = _bmm(u["m_ak"], u["v4"])
        for u in us:
            u["ah4"] = _mm(u["inv"], u["a4"])
        for u in us:
            u["uh4"] = _bmm(u["inv"], u["mv4"])
        for u in us:
            u["ry"] = (u["rt"] + unstack(_bmm(u["m_rb"], u["ah4"]))).astype(BF16)
        for u in us:
            u["y0"] = unstack(_bmm(u["m_rb"], u["uh4"]) + _mm(u["m_rk"], u["v4"]))
        for u in us:
            ah, uh = unstack(u["ah4"]).astype(BF16), unstack(u["uh4"]).astype(BF16)
            u["g_low"] = jnp.where(same_head, _tn(u["bg"], ah), 0.0).astype(BF16)
            u["h_t"] = jnp.where(same_head, _tn(uh, u["bg"]) + _tn(u["vv"], u["kg"]), 0.0)
        for u in us:
            st = st_ref[u["q"]]
            stb = st.astype(BF16)
            y_ref[u["rows"], u["sl"]] = _nt(u["ry"], stb) + u["y0"]
            st_ref[u["q"]] = st * u["gam"] + _nt(stb, u["g_low"]) + u["h_t"]

    n_pairs = r_ref.shape[0] // (2 * CHUNK)
    prepare(0)
    for pair in range(n_pairs):
        if pair + 1 < n_pairs:
            prepare(pair + 1)
        chunk_pair(pair)


def _rwkv(z, prm, d_model):
    b, s, _ = z.shape
    tm = min(RW_TM, s)
    s2_blk, s3_blk = _z_small_block(d_model, 2), _z_small_block(d_model, 3)
    tok = lambda blk: pl.BlockSpec((None, tm, MIX_W), lambda bi, i: (bi, i, blk))
    tok128 = lambda blk: pl.BlockSpec((None, tm, LANES), lambda bi, i: (bi, i, blk))
    const = lambda a: pl.BlockSpec(a.shape, lambda bi, i: (0,) * a.ndim)
    consts = [prm[n] for n in ("mu_r", "mu_k", "mu_v", "mu_s2", "mu_s3", "w0", "w_up", "a0", "a_up", "g_up",
                               "k_k", "k_a", "r_k", "bones_q", "tri")]
    out_spec = pl.BlockSpec((None, tm, MIX_W), lambda bi, i: (bi, i, 0))
    return pl.pallas_call(
        _rwkv_kernel,
        grid=(b, s // tm),
        in_specs=[tok(Z_RR), tok(Z_RK), tok(Z_RV), tok128(s2_blk), tok128(s3_blk),
                  _halo_spec(MIX_W, Z_RR, tm, BF16_ROWS), _halo_spec(MIX_W, Z_RK, tm, BF16_ROWS),
                  _halo_spec(MIX_W, Z_RV, tm, BF16_ROWS), _halo_spec(LANES, s2_blk, tm, BF16_ROWS),
                  _halo_spec(LANES, s3_blk, tm, BF16_ROWS)] + [const(a) for a in consts],
        out_specs=[out_spec, out_spec, out_spec],
        out_shape=[jax.ShapeDtypeStruct((b, s, MIX_W), F32), jax.ShapeDtypeStruct((b, s, MIX_W), BF16),
                   jax.ShapeDtypeStruct((b, s, MIX_W), BF16)],
        scratch_shapes=([pltpu.VMEM((MIX_W // QUAD, QUAD, QUAD), F32)] + [pltpu.VMEM((tm, MIX_W), F32)] * 6
                        + [pltpu.VMEM((tm, LANES), F32)] * 2),
        compiler_params=_cparams(("parallel", "arbitrary")),
        name="rwkv",
    )(z, z, z, z, z, z, z, z, z, z, *consts)


def _causal_conv(p, halo, cw_ref):
    nh = halo.shape[0]
    rowid = lax.broadcasted_iota(jnp.int32, (SUBLANES, p.shape[1]), 0)
    conv = p * cw_ref[CONV_W - 1:CONV_W, :]
    for back in range(1, CONV_W):
        rolled = pltpu.roll(p, back, axis=0)
        head = rolled[0:SUBLANES]
        for rr in range(back):
            head = jnp.where(rowid == rr, halo[nh - back + rr:nh - back + rr + 1, :], head)
        rolled = jnp.concatenate([head, rolled[SUBLANES:]], axis=0)
        conv = conv + rolled * cw_ref[CONV_W - 1 - back:CONV_W - back, :]
    return conv


def _merge_kernel(x_ref, oa_ref, y_ref, g_ref, bonus_ref, cb_ref, cc_ref, cx_ref, cch_ref, cxh_ref,
                  gate0_ref, gate1_ref, gate2_ref, g1_ref, lnw_ref, lnb_ref, bq_ref, cw_ref, wb_ref, wo_ref, o_ref,
                  merged_ref):
    first = pl.program_id(1) == 0
    y = y_ref[...]
    inv_n = 1.0 / HEAD_DIM
    p1 = y.astype(BF16)
    mean = (_head_sum(p1, bq_ref) + _head_sum(y - p1.astype(F32), bq_ref)) * inv_n
    yc = y - mean
    var = _head_sum(yc * yc, bq_ref) * inv_n
    o_rwkv = ((yc * lax.rsqrt(var + GN_EPS) * lnw_ref[...] + lnb_ref[...] + bonus_ref[...].astype(F32))
              * g_ref[...].astype(F32))
    p = cc_ref[...].astype(F32) * cx_ref[...].astype(F32)
    ph = jnp.where(first, 0.0, cch_ref[...].astype(F32) * cxh_ref[...].astype(F32))
    o_conv = cb_ref[...].astype(F32) * _causal_conv(p, ph, cw_ref)
    branches = ((oa_ref[...], gate0_ref), (o_rwkv.astype(BF16), gate1_ref), (o_conv.astype(BF16), gate2_ref))
    d = x_ref.shape[-1]
    for c in range(d // QUAD):
        cols = slice(c * QUAD, (c + 1) * QUAD)
        part = jnp.zeros((x_ref.shape[0], QUAD), F32)
        for bi, (o, gate_ref) in enumerate(branches):
            part = part + jax.nn.sigmoid(gate_ref[:, cols].astype(F32)) * _mm(o, wb_ref[bi, :, cols])
        merged_ref[:, cols] = part.astype(BF16)
    o_ref[...] = x_ref[...] + g1_ref[...] * _mm(merged_ref[...], wo_ref[...])


def _merge(x, o_attn, y, g, bonus, z, mod5, layer, prm, tm=512):
    b, s, d = x.shape
    tm = min(tm, s)
    tok = lambda w, blk: pl.BlockSpec((None, tm, w), lambda bi, i: (bi, i, blk))
    const = lambda a: pl.BlockSpec(a.shape, lambda bi, i: (0,) * a.ndim)
    consts = [prm[n] for n in ("ln_w", "ln_b", "bones_q", "conv_w", "w_branch", "w_o")]
    gate_blk = Z_GATE_COL // d
    return pl.pallas_call(
        _merge_kernel,
        grid=(b, s // tm),
        in_specs=[tok(d, 0), tok(MIX_W, 0), tok(MIX_W, 0), tok(MIX_W, 0), tok(MIX_W, 0),
                  tok(MIX_W, Z_CB), tok(MIX_W, Z_CC), tok(MIX_W, Z_CX),
                  _halo_spec(MIX_W, Z_CC, tm, BF16_ROWS), _halo_spec(MIX_W, Z_CX, tm, BF16_ROWS),
                  tok(d, gate_blk), tok(d, gate_blk + 1), tok(d, gate_blk + 2),
                  _mod_spec(mod5, layer, 2, 2)] + [const(a) for a in consts],
        out_specs=tok(d, 0),
        out_shape=jax.ShapeDtypeStruct((b, s, d), F32),
        scratch_shapes=[pltpu.VMEM((tm, d), BF16)],
        compiler_params=_cparams(("parallel", "parallel")),
        name="merge",
    )(x, o_attn, y, g, bonus, z, z, z, z, z, z, z, z, mod5, *consts)


def _ffn_kernel(x_ref, xh_ref, sh_ref, sc_ref, g2_ref, gain_ref, wup_ref, cw_ref, wd_ref, fin_ref, o_ref,
                *, final_norm, tf):
    first = pl.program_id(1) == 0
    x = x_ref[...]
    h = _norm_mod(x, gain_ref[...], sc_ref[...], sh_ref[...]).astype(BF16)
    hh = _norm_mod(xh_ref[...], gain_ref[...], sc_ref[...], sh_ref[...]).astype(BF16)
    h_ext = jnp.concatenate([hh, h], axis=0)
    d_ff = wd_ref.shape[0]
    acc = jnp.zeros(x.shape, F32)
    for jf in range(d_ff // tf):
        cols = slice(jf * tf, (jf + 1) * tf)
        gate_cols = slice(d_ff + jf * tf, d_ff + (jf + 1) * tf)
        a_ext = _mm(h_ext, wup_ref[:, cols])
        a = a_ext[BF16_ROWS:]
        ah = jnp.where(first, 0.0, a_ext[BF16_ROWS - SUBLANES:BF16_ROWS])
        conv = _causal_conv(a, ah, cw_ref[:, cols])
        u = conv * jax.nn.sigmoid(conv) * _mm(h, wup_ref[:, gate_cols])
        acc = acc + _mm(u.astype(BF16), wd_ref[cols, :])
    out = x + g2_ref[...] * acc
    if final_norm:
        out = out * lax.rsqrt(jnp.mean(out * out, axis=-1, keepdims=True) + NORM_EPS) * fin_ref[...]
    o_ref[...] = out


def _ffn(x, mod5, layer, gain, w_up, conv_w, w_down, final_gain, final_norm, tm=1024, tf=1408):
    b, s, d = x.shape
    d_ff = w_down.shape[0]
    assert d_ff % tf == 0 and tf % LANES == 0
    tm = min(tm, s)
    gain, final_gain = gain.reshape(1, d), final_gain.reshape(1, d)
    return pl.pallas_call(
        functools.partial(_ffn_kernel, final_norm=final_norm, tf=tf),
        grid=(b, s // tm),
        in_specs=[pl.BlockSpec((None, tm, d), lambda bi, i: (bi, i, 0)),
                  _halo_spec(d, 0, tm, BF16_ROWS),
                  _mod_spec(mod5, layer, 3, 2), _mod_spec(mod5, layer, 4, 2), _mod_spec(mod5, layer, 5, 2),
                  _resident(gain), _resident(w_up), _resident(conv_w), _resident(w_down), _resident(final_gain)],
        out_specs=pl.BlockSpec((None, tm, d), lambda bi, i: (bi, i, 0)),
        out_shape=jax.ShapeDtypeStruct((b, s, d), F32),
        compiler_params=_cparams(("parallel", "parallel")),
        name="ffn",
    )(x, x, mod5, mod5, mod5, gain, w_up, conv_w, w_down, final_gain)


def _split_w_in(w):
    sizes = (MIX_W, MIX_W, MIX_W, HEADS * HEAD_DIM, HEAD_DIM, HEADS,
             MIX_W, MIX_W, MIX_W, LORA_DECAY, LORA_ICLR, LORA_GATE,
             MIX_W, MIX_W, MIX_W)
    out, o = [], 0
    for n in sizes:
        out.append(w[:, o:o + n])
        o += n
    out.append(w[:, o:])
    return out


def _layer_params(l, w_in, rwkv_mu, rwkv_w0, rwkv_w_up, rwkv_a0, rwkv_a_up, rwkv_g_up, rwkv_k_k, rwkv_k_a,
                  rwkv_r_k, rwkv_ln_w, rwkv_ln_b, sc_conv_w, w_branch, w_o):
    d = w_in.shape[1]
    q, k, v, qi, ki, wi, rr, rk, rv, wd, ad, gd, cb, cc, cx, gates = _split_w_in(w_in[l])
    assert Z_GATE_COL % d == 0 and gates.shape[1] == 3 * d
    w_z = jnp.concatenate([q, k, v, qi, rr, rk, rv, cb, cc, cx, gates,
                           ki, ki, wi, jnp.zeros((d, LANES - HEADS), F32), wd, ad, gd], axis=1).astype(BF16)
    mu = rwkv_mu[l]
    row = lambda a: a.reshape(1, -1)
    head_id = jnp.arange(QUAD) // HEAD_DIM
    bones_q = (head_id[:, None] == head_id[None, :]).astype(BF16)
    tri = (jnp.arange(CHUNK)[:, None] >= jnp.arange(CHUNK)[None, :]).astype(BF16)
    zl = lambda n: jnp.zeros((n, MIX_W), F32)
    return dict(
        w_z=w_z,
        mu_r=row(mu[:MIX_W]), mu_k=row(mu[MIX_W:2 * MIX_W]), mu_v=row(mu[2 * MIX_W:3 * MIX_W]),
        mu_s2=row(mu[3 * MIX_W:3 * MIX_W + LORA_DECAY + LORA_ICLR]), mu_s3=row(mu[3 * MIX_W + LORA_DECAY + LORA_ICLR:]),
        w0=row(rwkv_w0[l]), w_up=jnp.concatenate([rwkv_w_up[l], zl(LORA_ICLR)], axis=0).astype(BF16),
        a0=row(rwkv_a0[l]), a_up=jnp.concatenate([zl(LORA_DECAY), rwkv_a_up[l]], axis=0).astype(BF16),
        g_up=rwkv_g_up[l].astype(BF16), k_k=row(rwkv_k_k[l]), k_a=row(rwkv_k_a[l]), r_k=row(rwkv_r_k[l]),
        bones_q=bones_q, tri=tri,
        ln_w=row(rwkv_ln_w[l]), ln_b=row(rwkv_ln_b[l]), conv_w=sc_conv_w[l].T,
        w_branch=w_branch[l].astype(BF16), w_o=w_o[l].astype(BF16),
    )


def kernel(x, c, positions, rel_bias, final_norm, ada_w, ada_b, norm_mix, w_in, rwkv_mu, rwkv_w0, rwkv_w_up,
           rwkv_a0, rwkv_a_up, rwkv_g_up, rwkv_k_k, rwkv_k_a, rwkv_r_k, rwkv_ln_w, rwkv_ln_b, sc_conv_w,
           w_branch, w_o, norm_ffn, ffn_w_up, ffn_conv_w, ffn_w_down):
    depth, d = ada_w.shape[0], x.shape[-1]
    mod5 = _ada_mod(c, ada_w, ada_b)
    bias = _bias_tiles(rel_bias)
    for l in range(depth):
        prm = _layer_params(l, w_in, rwkv_mu, rwkv_w0, rwkv_w_up, rwkv_a0, rwkv_a_up, rwkv_g_up, rwkv_k_k,
                            rwkv_k_a, rwkv_r_k, rwkv_ln_w, rwkv_ln_b, sc_conv_w, w_branch, w_o)
        z = _inproj(x, mod5, l, norm_mix[l], prm["w_z"])
        o_attn = _dsa(z, bias, d)
        y, g, bonus = _rwkv(z, prm, d)
        x = _merge(x, o_attn, y, g, bonus, z, mod5, l, prm)
        x = _ffn(x, mod5, l, norm_ffn[l], ffn_w_up[l].astype(BF16), ffn_conv_w[l].T, ffn_w_down[l].astype(BF16),
                 final_norm, final_norm=(l == depth - 1))
    return x
```
